```python
import numpy as np
import jax
import jax.numpy as jnp
from jax import lax

D_MODEL = 1024
BATCH = 8
SEQ = 4096
DEPTH = 1

HEAD_DIM = 64
ATTN_HEADS = 6
A_WIDTH = ATTN_HEADS * HEAD_DIM
KV_RANK = 128
IDX_HEADS = 4
IDX_DIM = 64
INDEX_TOPK = 256
Q_BLOCK = 128
GDN_HEADS = 6
GDN_WIDTH = GDN_HEADS * HEAD_DIM
GDN_CONV_CH = 3 * GDN_WIDTH
CONV_WIDTH = 4
CHUNK = 64
MEM_HEADS = 4
MEM_WIDTH = MEM_HEADS * HEAD_DIM
N_MEM = 256
MIX_WIDTH = A_WIDTH + GDN_WIDTH + MEM_WIDTH
SPLIT_SIZES = (A_WIDTH, KV_RANK, IDX_HEADS * IDX_DIM, IDX_DIM, IDX_HEADS,
               GDN_WIDTH, GDN_WIDTH, GDN_WIDTH, GDN_WIDTH, GDN_HEADS, GDN_HEADS,
               MEM_WIDTH)
IN_WIDTH = sum(SPLIT_SIZES)
ROPE_THETA = 10000.0
N_GROUPS = 8
EXPERTS_PER_GROUP = 8
N_EXPERTS = N_GROUPS * EXPERTS_PER_GROUP
TOP_K = 2
D_EXPERT = 256
MOE_BLOCK = 128
DEEPNORM_ALPHA = (2 * DEPTH) ** 0.25
DEEPNORM_BETA = (8 * DEPTH) ** -0.25
LN_EPS = 1e-5
RMS_EPS = 1e-6
NEG_INF = -1e30
F32 = jnp.float32

kernel_name = 'hybrid_dsa_gdn_memxattn_hmoe_deepnorm'


def _split_cols(t, sizes):
    offs = np.cumsum(sizes)[:-1].tolist()
    return jnp.split(t, offs, axis=-1)


def _rmsnorm(t, g):
    tf = t.astype(F32)
    y = tf * lax.rsqrt(jnp.mean(tf * tf, axis=-1, keepdims=True) + RMS_EPS)
    return (y * g.astype(F32)).astype(t.dtype)


def _layernorm(t, g, b):
    tf = t.astype(F32)
    mu = jnp.mean(tf, axis=-1, keepdims=True)
    var = jnp.mean(jnp.square(tf - mu), axis=-1, keepdims=True)
    return ((tf - mu) * lax.rsqrt(var + LN_EPS) * g.astype(F32) + b.astype(F32)).astype(t.dtype)


def _l2norm(t):
    tf = t.astype(F32)
    return tf * lax.rsqrt(jnp.sum(tf * tf, axis=-1, keepdims=True) + RMS_EPS)


def _rope_tables(seq):
    inv_freq = 1.0 / (ROPE_THETA ** (jnp.arange(0, HEAD_DIM, 2, dtype=F32) / HEAD_DIM))
    ang = jnp.arange(seq, dtype=F32)[:, None] * inv_freq[None, :]
    return jnp.cos(ang), jnp.sin(ang)


def _rope(t, cos, sin):
    half = t.shape[-1] // 2
    tf = t.astype(F32)
    t1, t2 = tf[..., :half], tf[..., half:]
    return jnp.concatenate([t1 * cos - t2 * sin, t2 * cos + t1 * sin], axis=-1).astype(t.dtype)


def _causal_dwconv(t, w):
    k = w.shape[0]
    return lax.conv_general_dilated(t, w[:, None, :].astype(t.dtype), window_strides=(1,),
                                    padding=[(k - 1, 0)],
                                    dimension_numbers=('NWC', 'WIO', 'NWC'),
                                    feature_group_count=t.shape[-1])


def _dsa_attention(q, k, v, iq, ik, iw):
    bsz, seq, n_h, d_h = q.shape
    n_blk = seq // Q_BLOCK
    top = min(INDEX_TOPK, seq // 4)
    key_pos = jnp.arange(seq)
    scale = d_h ** -0.5
    gather = jax.vmap(lambda t, i: t[i])

    def blocks(t):
        return jnp.moveaxis(t.reshape((bsz, n_blk, Q_BLOCK) + t.shape[2:]), 1, 0)

    def one_block(args):
        qb, iqb, iwb, start = args
        q_pos = start + jnp.arange(Q_BLOCK)
        causal = key_pos[None, :] <= q_pos[:, None]
        idx_dot = jnp.einsum('bqhd,bsd->bqhs', iqb, ik)
        score = jnp.einsum('bqh,bqhs->bqs', iwb, jax.nn.relu(idx_dot)).astype(F32)
        score = jnp.where(causal[None], score, NEG_INF)
        _, sel = lax.top_k(score, top)
        valid = sel <= q_pos[None, :, None]
        k_sel = gather(k, sel)
        v_sel = gather(v, sel)
        logits = jnp.einsum('bqhd,bqkd->bqhk', qb, k_sel).astype(F32) * scale
        logits = jnp.where(valid[:, :, None, :], logits, NEG_INF)
        p = jax.nn.softmax(logits, axis=-1).astype(v.dtype)
        return jnp.einsum('bqhk,bqkd->bqhd', p, v_sel)

    out = lax.map(one_block, (blocks(q), blocks(iq), blocks(iw), jnp.arange(n_blk) * Q_BLOCK))
    return jnp.moveaxis(out, 0, 1).reshape(bsz, seq, n_h, d_h)


def _gated_delta_rule(q, k, v, g, beta):
    bsz, seq, n_h, d_k = q.shape
    d_v = v.shape[-1]
    n_c = seq // CHUNK

    def chunks(t):
        return jnp.moveaxis(t.reshape(bsz, n_c, CHUNK, n_h, -1), 3, 1)

    q, k, v = chunks(q), chunks(k), chunks(v)
    g = chunks(g[..., None])[..., 0]
    beta = chunks(beta[..., None])[..., 0]
    gc = jnp.cumsum(g, axis=-1)
    pos = jnp.arange(CHUNK)
    incl = pos[:, None] >= pos[None, :]
    strict = pos[:, None] > pos[None, :]
    diff = gc[..., :, None] - gc[..., None, :]
    decay = jnp.where(incl, jnp.exp(jnp.where(incl, diff, 0.0)), 0.0)
    kb = k * beta[..., None]
    lower = jnp.where(strict, jnp.einsum('bhnid,bhnjd->bhnij', kb, k) * decay, 0.0)
    t_mat = lower + jnp.eye(CHUNK, dtype=F32)
    u = lax.linalg.triangular_solve(t_mat, v * beta[..., None], left_side=True,
                                    lower=True, unit_diagonal=True)
    w = lax.linalg.triangular_solve(t_mat, kb * jnp.exp(gc)[..., None], left_side=True,
                                    lower=True, unit_diagonal=True)
    a_intra = jnp.where(incl, jnp.einsum('bhnid,bhnjd->bhnij', q, k) * decay, 0.0)
    q_dec = q * jnp.exp(gc)[..., None]
    k_dec = k * jnp.exp(gc[..., -1:] - gc)[..., None]
    g_last = jnp.exp(gc[..., -1])

    def step(state, inp):
        u_n, w_n, qd_n, kd_n, a_n, gl_n = inp
        v_new = u_n - jnp.einsum('bhck,bhkv->bhcv', w_n, state)
        o_n = jnp.einsum('bhck,bhkv->bhcv', qd_n, state) + jnp.einsum('bhcs,bhsv->bhcv', a_n, v_new)
        state = state * gl_n[..., None, None] + jnp.einsum('bhck,bhcv->bhkv', kd_n, v_new)
        return state, o_n

    xs = tuple(jnp.moveaxis(t, 2, 0) for t in (u, w, q_dec, k_dec, a_intra, g_last))
    s0 = jnp.zeros((bsz, n_h, d_k, d_v), F32)
    _, o = lax.scan(step, s0, xs)
    return jnp.transpose(o, (1, 0, 3, 2, 4)).reshape(bsz, seq, n_h, d_v)


def _token_mixer(h, mem, cos, sin, w_in, kv_norm_g, w_k_up, w_v_up, conv_w, A_log, dt_bias,
                 gdn_norm_g, attn_norm_g, mem_norm_g, w_mem_kv, w_out):
    bsz, seq, _ = h.shape
    proj = h @ w_in
    (q_a, c_kv, iq, ik, iw, q_b, k_b, v_b, z_b, a_b, b_b, q_c) = _split_cols(proj, SPLIT_SIZES)
    cos_h, sin_h = cos[None, :, None, :], sin[None, :, None, :]
    cos_t, sin_t = cos[None], sin[None]

    q_a = _rope(q_a.reshape(bsz, seq, ATTN_HEADS, HEAD_DIM), cos_h, sin_h)
    c_kv = _rmsnorm(c_kv, kv_norm_g)
    k_a = _rope(c_kv @ w_k_up, cos_t, sin_t)
    v_a = c_kv @ w_v_up
    iq = _rope(iq.reshape(bsz, seq, IDX_HEADS, IDX_DIM), cos_h, sin_h)
    ik = _rope(ik, cos_t, sin_t)
    iw = iw * (IDX_HEADS ** -0.5 * IDX_DIM ** -0.5)
    o_a = _dsa_attention(q_a, k_a, v_a, iq, ik, iw).reshape(bsz, seq, A_WIDTH)
    o_a = _rmsnorm(o_a, attn_norm_g)

    qkv = jax.nn.silu(_causal_dwconv(jnp.concatenate([q_b, k_b, v_b], axis=-1), conv_w))
    q_b, k_b, v_b = jnp.split(qkv, 3, axis=-1)
    q_b = _l2norm(q_b.reshape(bsz, seq, GDN_HEADS, HEAD_DIM)) * (HEAD_DIM ** -0.5)
    k_b = _l2norm(k_b.reshape(bsz, seq, GDN_HEADS, HEAD_DIM))
    v_b = v_b.reshape(bsz, seq, GDN_HEADS, HEAD_DIM).astype(F32)
    beta = jax.nn.sigmoid(b_b.astype(F32))
    g = -jnp.exp(A_log.astype(F32)) * jax.nn.softplus(a_b.astype(F32) + dt_bias.astype(F32))
    o_b = _gated_delta_rule(q_b, k_b, v_b, g, beta).astype(h.dtype)
    o_b = _rmsnorm(o_b, gdn_norm_g) * jax.nn.silu(z_b.reshape(bsz, seq, GDN_HEADS, HEAD_DIM))
    o_b = o_b.reshape(bsz, seq, GDN_WIDTH)

    m_k, m_v = jnp.split(mem @ w_mem_kv, 2, axis=-1)
    m_k = m_k.reshape(bsz, -1, MEM_HEADS, HEAD_DIM)
    m_v = m_v.reshape(bsz, -1, MEM_HEADS, HEAD_DIM)
    q_c = q_c.reshape(bsz, seq, MEM_HEADS, HEAD_DIM)
    logits = jnp.einsum('bshd,bmhd->bhsm', q_c, m_k).astype(F32) * (HEAD_DIM ** -0.5)
    p = jax.nn.softmax(logits, axis=-1).astype(h.dtype)
    o_c = jnp.einsum('bhsm,bmhd->bshd', p, m_v).reshape(bsz, seq, MEM_WIDTH)
    o_c = _rmsnorm(o_c, mem_norm_g)

    return jnp.concatenate([o_a, o_b, o_c], axis=-1) @ w_out


def _hier_moe(h, w_group, b_group, w_router, b_router, w_gate, w_up, w_down):
    bsz, seq, d = h.shape
    xf = h.reshape(-1, d)
    n_tok = xf.shape[0]
    g_prob = jax.nn.softmax((xf @ w_group + b_group).astype(F32), axis=-1)
    p_grp, grp = lax.top_k(g_prob, 1)
    e_logit = (xf @ w_router + b_router).astype(F32).reshape(n_tok, N_GROUPS, EXPERTS_PER_GROUP)
    e_logit = jnp.take_along_axis(e_logit, grp[:, :, None], axis=1)[:, 0]
    p_loc, loc = lax.top_k(jax.nn.softmax(e_logit, axis=-1), TOP_K)
    gate = p_grp * p_loc / jnp.sum(p_loc, axis=-1, keepdims=True)
    expert = grp * EXPERTS_PER_GROUP + loc

    n_asg = n_tok * TOP_K
    e_flat = expert.reshape(-1)
    tok_flat = jnp.arange(n_asg, dtype=jnp.int32) // TOP_K
    order = jnp.argsort(e_flat)
    e_s = e_flat[order]
    tok_s = tok_flat[order]
    g_s = gate.reshape(-1)[order]
    counts = jnp.zeros((N_EXPERTS,), jnp.int32).at[e_flat].add(1)
    padded = (counts + MOE_BLOCK - 1) // MOE_BLOCK * MOE_BLOCK
    starts = jnp.cumsum(counts) - counts
    pad_ends = jnp.cumsum(padded)
    dest = pad_ends[e_s] - padded[e_s] + jnp.arange(n_asg, dtype=jnp.int32) - starts[e_s]
    cap = (n_asg + MOE_BLOCK - 1) // MOE_BLOCK * MOE_BLOCK + N_EXPERTS * MOE_BLOCK
    n_blk = cap // MOE_BLOCK
    x_buf = jnp.zeros((cap, d), xf.dtype).at[dest].set(xf[tok_s])
    tok_buf = jnp.zeros((cap,), jnp.int32).at[dest].set(tok_s)
    g_buf = jnp.zeros((cap,), F32).at[dest].set(g_s)
    blk_expert = jnp.minimum(
        jnp.searchsorted(pad_ends, jnp.arange(n_blk, dtype=jnp.int32) * MOE_BLOCK, side='right'),
        N_EXPERTS - 1)

    def expert_block(args):
        xb, e = args
        return (jax.nn.silu(xb @ w_gate[e]) * (xb @ w_up[e])) @ w_down[e]

    y_buf = lax.map(expert_block, (x_buf.reshape(n_blk, MOE_BLOCK, d), blk_expert))
    y_buf = (y_buf.reshape(cap, d).astype(F32) * g_buf[:, None]).astype(xf.dtype)
    return jnp.zeros_like(xf).at[tok_buf].add(y_buf).reshape(bsz, seq, d)


def setup_inputs(seed: int = 0) -> dict:
    key = jax.random.key(seed)
    ks = jax.random.split(key, 32)

    def normal(i, shape, scale):
        return jax.random.normal(ks[i], shape, F32) * scale

    def gain(i, shape):
        return 1.0 + normal(i, shape, 0.02)

    dt = jnp.exp(jax.random.uniform(ks[7], (DEPTH, GDN_HEADS), F32,
                                    minval=float(np.log(1e-3)), maxval=float(np.log(1e-1))))
    return {
        'x': normal(0, (BATCH, SEQ, D_MODEL), 1.0),
        'mem': normal(1, (BATCH, N_MEM, D_MODEL), 1.0),
        'w_in': normal(2, (DEPTH, D_MODEL, IN_WIDTH), D_MODEL ** -0.5),
        'kv_norm_g': gain(3, (DEPTH, KV_RANK)),
        'w_k_up': normal(4, (DEPTH, KV_RANK, HEAD_DIM), KV_RANK ** -0.5),
        'w_v_up': normal(5, (DEPTH, KV_RANK, HEAD_DIM), KV_RANK ** -0.5),
        'conv_w': normal(6, (DEPTH, CONV_WIDTH, GDN_CONV_CH), CONV_WIDTH ** -0.5),
        'A_log': jnp.log(jax.random.uniform(ks[8], (DEPTH, GDN_HEADS), F32, minval=1.0, maxval=16.0)),
        'dt_bias': dt + jnp.log(-jnp.expm1(-dt)),
        'gdn_norm_g': gain(9, (DEPTH, HEAD_DIM)),
        'attn_norm_g': gain(10, (DEPTH, A_WIDTH)),
        'mem_norm_g': gain(11, (DEPTH, MEM_WIDTH)),
        'w_mem_kv': normal(12, (DEPTH, D_MODEL, 2 * MEM_WIDTH), D_MODEL ** -0.5),
        'w_out': normal(13, (DEPTH, MIX_WIDTH, D_MODEL), MIX_WIDTH ** -0.5 * DEEPNORM_BETA),
        'ln1_g': gain(14, (DEPTH, D_MODEL)),
        'ln1_b': normal(15, (DEPTH, D_MODEL), 0.02),
        'w_group': normal(16, (DEPTH, D_MODEL, N_GROUPS), D_MODEL ** -0.5),
        'b_group': normal(17, (DEPTH, N_GROUPS), 0.01),
        'w_router': normal(18, (DEPTH, D_MODEL, N_EXPERTS), D_MODEL ** -0.5),
        'b_router': normal(19, (DEPTH, N_EXPERTS), 0.01),
        'w_gate': normal(20, (DEPTH, N_EXPERTS, D_MODEL, D_EXPERT), D_MODEL ** -0.5),
        'w_up': normal(21, (DEPTH, N_EXPERTS, D_MODEL, D_EXPERT), D_MODEL ** -0.5),
        'w_down': normal(22, (DEPTH, N_EXPERTS, D_EXPERT, D_MODEL), D_EXPERT ** -0.5 * DEEPNORM_BETA),
        'ln2_g': gain(23, (DEPTH, D_MODEL)),
        'ln2_b': normal(24, (DEPTH, D_MODEL), 0.02),
    }


def reference(x, mem, w_in, kv_norm_g, w_k_up, w_v_up, conv_w, A_log, dt_bias, gdn_norm_g,
              attn_norm_g, mem_norm_g, w_mem_kv, w_out, ln1_g, ln1_b, w_group, b_group,
              w_router, b_router, w_gate, w_up, w_down, ln2_g, ln2_b):
    cos, sin = _rope_tables(x.shape[1])
    for layer in range(DEPTH):
        mix = _token_mixer(x, mem, cos, sin, w_in[layer], kv_norm_g[layer], w_k_up[layer],
                           w_v_up[layer], conv_w[layer], A_log[layer], dt_bias[layer],
                           gdn_norm_g[layer], attn_norm_g[layer], mem_norm_g[layer],
                           w_mem_kv[layer], w_out[layer])
        x = _layernorm(DEEPNORM_ALPHA * x + mix, ln1_g[layer], ln1_b[layer])
        ffn = _hier_moe(x, w_group[layer], b_group[layer], w_router[layer], b_router[layer],
                        w_gate[layer], w_up[layer], w_down[layer])
        x = _layernorm(DEEPNORM_ALPHA * x + ffn, ln2_g[layer], ln2_b[layer])
    return x
```

```python
import functools

import numpy as np
import jax
import jax.numpy as jnp
from jax import lax
from jax.experimental import pallas as pl
from jax.experimental.pallas import tpu as pltpu

F32 = jnp.float32
BF16 = jnp.bfloat16
I32 = jnp.int32

HEAD_DIM = 64
ATTN_HEADS = 6
A_WIDTH = ATTN_HEADS * HEAD_DIM
KV_RANK = 128
IDX_HEADS = 4
IDX_DIM = 64
INDEX_TOPK = 256
GDN_HEADS = 6
GDN_WIDTH = GDN_HEADS * HEAD_DIM
CONV_WIDTH = 4
CHUNK = 64
MEM_HEADS = 4
MEM_WIDTH = MEM_HEADS * HEAD_DIM
SPLIT_SIZES = (A_WIDTH, KV_RANK, IDX_HEADS * IDX_DIM, IDX_DIM, IDX_HEADS,
               GDN_WIDTH, GDN_WIDTH, GDN_WIDTH, GDN_WIDTH, GDN_HEADS, GDN_HEADS,
               MEM_WIDTH)
ROPE_THETA = 10000.0
N_GROUPS = 8
EXPERTS_PER_GROUP = 8
N_EXPERTS = N_GROUPS * EXPERTS_PER_GROUP
TOP_K = 2
MOE_BLOCK = 128
LN_EPS = 1e-5
RMS_EPS = 1e-6
NEG_INF = -1e30
INT_MIN = -2 ** 31

LANES = 128
VMEM_LIMIT = 56 * 1024 * 1024


def _cparams(sem):
    return pltpu.CompilerParams(dimension_semantics=sem, vmem_limit_bytes=VMEM_LIMIT)


def _dot(a, b):
    return jnp.dot(a.astype(BF16), b.astype(BF16), preferred_element_type=F32)


def _dot_nt(a, b):
    return lax.dot_general(a.astype(BF16), b.astype(BF16), (((1,), (1,)), ((), ())),
                           preferred_element_type=F32)


def _silu(t):
    return t * (1.0 / (1.0 + jnp.exp(-t)))


_P_QA = (0, 384)
_P_QAR = (384, 768)
_P_IQ = (768, 1024)
_P_IQR = (1024, 1280)
_P_CKV = (1280, 1408)
_P_IK = (1408, 1536)
_P_IW = (1536, 1664)
_P_QKVB = (1664, 2816)
_P_ZB = (2816, 3200)
_P_AB = (3200, 3328)
_P_QC = (3328, 3584)
_P_TOTAL = 3584


def _rot_cols(w, n_heads):
    k = w.shape[0]
    w4 = w.reshape(k, n_heads, 2, HEAD_DIM // 2)
    return jnp.concatenate([-w4[:, :, 1:2], w4[:, :, 0:1]], axis=2).reshape(k, n_heads * HEAD_DIM)


def _pad_cols(w, width):
    return jnp.pad(w, ((0, 0), (0, width - w.shape[1])))


def _proj_kernel(x_ref, w_ref, w2_ref, cos_ref, sin_ref, kvg_ref,
                 qa_ref, iq_ref, ik_ref, ka_ref, va_ref, iw_ref, qkvb_ref, zb_ref, ab_ref, qc_ref):
    xb = x_ref[...].astype(BF16)

    def mm(slab):
        return jnp.dot(xb, w_ref[:, slab[0]:slab[1]], preferred_element_type=F32)

    cos = cos_ref[...]
    sin = sin_ref[...]
    cos3 = jnp.concatenate([cos] * 3, axis=1)
    sin3 = jnp.concatenate([sin] * 3, axis=1)
    cos2 = jnp.concatenate([cos] * 2, axis=1)
    sin2 = jnp.concatenate([sin] * 2, axis=1)
    cos64 = cos[:, :HEAD_DIM]
    sin64 = sin[:, :HEAD_DIM]

    qa_ref[...] = (mm(_P_QA) * cos3 + mm(_P_QAR) * sin3).astype(BF16)
    iq_ref[...] = (mm(_P_IQ) * cos2 + mm(_P_IQR) * sin2).astype(BF16)

    ckv = mm(_P_CKV)
    cn = ckv * lax.rsqrt(jnp.mean(ckv * ckv, axis=-1, keepdims=True) + RMS_EPS) * kvg_ref[...]
    r = jnp.dot(cn.astype(BF16), w2_ref[...], preferred_element_type=F32)
    ka_ref[...] = (r[:, 0:64] * cos64 + r[:, 64:128] * sin64).astype(BF16)
    va_ref[...] = r[:, 128:192].astype(BF16)

    ikk = mm(_P_IK)
    ik_ref[...] = (ikk[:, 0:64] * cos64 + ikk[:, 64:128] * sin64).astype(BF16)
    iw_ref[...] = mm(_P_IW) * (IDX_HEADS ** -0.5 * IDX_DIM ** -0.5)
    qkvb_ref[...] = mm(_P_QKVB)
    zb_ref[...] = mm(_P_ZB)
    ab_ref[...] = mm(_P_AB)
    qc_ref[...] = mm(_P_QC).astype(BF16)


def _proj(x2, w_in, w_k_up, w_v_up, kv_norm_g, cos128, sin128, seq, tm=256):
    n_tok, d_model = x2.shape
    offs = np.cumsum(SPLIT_SIZES)[:-1].tolist()
    (w_qa, w_ckv, w_iq, w_ik, w_iw, w_qb, w_kb, w_vb, w_zb, w_a, w_b, w_qc) = jnp.split(w_in, offs, axis=1)
    w1 = jnp.concatenate([
        w_qa, _rot_cols(w_qa, ATTN_HEADS), w_iq, _rot_cols(w_iq, IDX_HEADS), w_ckv,
        w_ik, _rot_cols(w_ik, 1), _pad_cols(w_iw, LANES),
        w_qb, w_kb, w_vb, w_zb, _pad_cols(jnp.concatenate([w_a, w_b], axis=1), LANES), w_qc,
    ], axis=1).astype(BF16)
    assert w1.shape[1] == _P_TOTAL
    w2 = _pad_cols(jnp.concatenate([w_k_up, _rot_cols(w_k_up, 1), w_v_up], axis=1), 2 * LANES).astype(BF16)
    n_pos = seq // tm
    row = lambda i: (i, 0)
    const = lambda i: (0, 0)
    pos = lambda i: (i % n_pos, 0)
    outs = [(A_WIDTH, BF16), (IDX_HEADS * IDX_DIM, BF16), (IDX_DIM, BF16), (HEAD_DIM, BF16),
            (HEAD_DIM, BF16), (LANES, F32), (3 * GDN_WIDTH, F32), (GDN_WIDTH, F32), (LANES, F32),
            (MEM_WIDTH, BF16)]
    return pl.pallas_call(
        _proj_kernel,
        grid=(n_tok // tm,),
        in_specs=[pl.BlockSpec((tm, d_model), row),
                  pl.BlockSpec(w1.shape, const),
                  pl.BlockSpec(w2.shape, const),
                  pl.BlockSpec((tm, LANES), pos),
                  pl.BlockSpec((tm, LANES), pos),
                  pl.BlockSpec((1, KV_RANK), const)],
        out_specs=[pl.BlockSpec((tm, w), row) for w, _ in outs],
        out_shape=[jax.ShapeDtypeStruct((n_tok, w), dt) for w, dt in outs],
        compiler_params=_cparams(("parallel",)),
        name="proj",
    )(x2, w1, w2, cos128, sin128, kv_norm_g.reshape(1, KV_RANK).astype(F32))


def _dsa_kernel(qa_ref, iq_ref, iw_ref, ik_ref, ka_ref, va_ref, o_ref,
                key_ref, jcut_ref, m_ref, l_ref, acc_ref, *, tq, kc, top, seq):
    qi = pl.program_id(1)
    row0 = qi * tq
    n_kc = (row0 + tq + kc - 1) // kc
    rowpos = row0 + lax.broadcasted_iota(I32, (tq, 1), 0)
    lane_slabs = kc // LANES

    iq = iq_ref[...]
    iw = iw_ref[...]

    def score_body(c, carry):
        k0 = pl.multiple_of(c * kc, kc)
        ikc = ik_ref[pl.ds(k0, kc), :]
        s = jnp.zeros((tq, kc), F32)
        for h in range(IDX_HEADS):
            d = _dot_nt(iq[:, h * IDX_DIM:(h + 1) * IDX_DIM], ikc)
            s = s + iw[:, h:h + 1] * jnp.maximum(d, 0.0)
        col = k0 + lax.broadcasted_iota(I32, (tq, kc), 1)
        s = jnp.where(s == 0.0, 0.0, s)
        s = jnp.where(col <= rowpos, s, NEG_INF)
        bits = pltpu.bitcast(s, I32)
        key_ref[c] = jnp.where(bits >= 0, bits, bits ^ 0x7FFFFFFF)
        return carry

    lax.fori_loop(0, n_kc, score_body, 0)

    k_eff = jnp.minimum(top, rowpos + 1).astype(F32)

    def lane_fold(v):
        acc = v[:, 0:LANES]
        for j in range(1, lane_slabs):
            acc = acc + v[:, j * LANES:(j + 1) * LANES]
        return acc

    def count_ge(cand):
        def body(c, acc):
            return acc + lane_fold(jnp.where(key_ref[c] >= cand, 1.0, 0.0))
        acc = lax.fori_loop(0, n_kc, body, jnp.zeros((tq, LANES), F32))
        return jnp.sum(acc, axis=1, keepdims=True)

    t0 = jnp.where(count_ge(jnp.zeros((tq, 1), I32)) >= k_eff, 0, INT_MIN).astype(I32)

    def bit_body(i, t):
        cand = t + lax.shift_left(jnp.int32(1), 30 - i)
        return jnp.where(count_ge(cand) >= k_eff, cand, t)

    thr = lax.fori_loop(0, 31, bit_body, t0)

    jcut_ref[...] = jnp.full((tq, LANES), 2 ** 30, I32)
    c_ge = count_ge(thr)

    @pl.when(jnp.max(c_ge - k_eff) > 0.0)
    def _():
        def count_sel(j):
            def body(c, acc):
                col = c * kc + lax.broadcasted_iota(I32, (tq, kc), 1)
                kk = key_ref[c] - jnp.where(col >= j, 1, 0)
                return acc + lane_fold(jnp.where(kk >= thr, 1.0, 0.0))
            acc = lax.fori_loop(0, n_kc, body, jnp.zeros((tq, LANES), F32))
            return jnp.sum(acc, axis=1, keepdims=True)

        n_bits = int(seq).bit_length()

        def jbit_body(i, j):
            cand = j + lax.shift_left(jnp.int32(1), n_bits - 1 - i)
            return jnp.where(count_sel(cand) <= k_eff, cand, j)

        j = lax.fori_loop(0, n_bits, jbit_body, jnp.zeros((tq, 1), I32))
        jcut_ref[...] = jnp.broadcast_to(j, (tq, LANES))

    q = (qa_ref[...].astype(F32) * (HEAD_DIM ** -0.5)).astype(BF16)
    m_ref[...] = jnp.full(m_ref.shape, NEG_INF, F32)
    l_ref[...] = jnp.zeros(l_ref.shape, F32)
    acc_ref[...] = jnp.zeros(acc_ref.shape, F32)
    jcut = jcut_ref[:, 0:1]

    def attn_body(c, carry):
        k0 = pl.multiple_of(c * kc, kc)
        kch = ka_ref[pl.ds(k0, kc), :]
        vch = va_ref[pl.ds(k0, kc), :]
        col = k0 + lax.broadcasted_iota(I32, (tq, kc), 1)
        kk = key_ref[c] - jnp.where(col >= jcut, 1, 0)
        bias = jnp.where((kk >= thr) & (col <= rowpos), 0.0, NEG_INF)
        for h in range(ATTN_HEADS):
            lg = _dot_nt(q[:, h * HEAD_DIM:(h + 1) * HEAD_DIM], kch) + bias
            m_prev = m_ref[h]
            m_new = jnp.maximum(m_prev, jnp.max(lg, axis=1, keepdims=True))
            p = jnp.exp(lg - jnp.concatenate([m_new] * lane_slabs, axis=1))
            alpha = jnp.exp(m_prev - m_new)
            l_ref[h] = alpha * l_ref[h] + jnp.sum(p, axis=1, keepdims=True)
            acc_ref[h] = alpha[:, :HEAD_DIM] * acc_ref[h] + jnp.dot(
                p.astype(BF16), vch, preferred_element_type=F32)
            m_ref[h] = m_new
        return carry

    lax.fori_loop(0, n_kc, attn_body, 0)
    o_ref[...] = jnp.concatenate(
        [acc_ref[h] / l_ref[h][:, :HEAD_DIM] for h in range(ATTN_HEADS)], axis=1)


def _dsa(qa, iq, iw, ik, ka, va, bsz, seq, tq=128):
    kc = min(512, seq)
    top = min(INDEX_TOPK, seq // 4)
    n_q = seq // tq
    row = lambda b, i: (b * n_q + i, 0)
    per_b = lambda b, i: (b, 0, 0)
    kern = functools.partial(_dsa_kernel, tq=tq, kc=kc, top=top, seq=seq)
    return pl.pallas_call(
        kern,
        grid=(bsz, n_q),
        in_specs=[pl.BlockSpec((tq, A_WIDTH), row),
                  pl.BlockSpec((tq, IDX_HEADS * IDX_DIM), row),
                  pl.BlockSpec((tq, LANES), row),
                  pl.BlockSpec((None, seq, IDX_DIM), per_b),
                  pl.BlockSpec((None, seq, HEAD_DIM), per_b),
                  pl.BlockSpec((None, seq, HEAD_DIM), per_b)],
        out_specs=pl.BlockSpec((tq, A_WIDTH), row),
        out_shape=jax.ShapeDtypeStruct((bsz * seq, A_WIDTH), F32),
        scratch_shapes=[pltpu.VMEM((seq // kc, tq, kc), I32),
                        pltpu.VMEM((tq, LANES), I32),
                        pltpu.VMEM((ATTN_HEADS, tq, LANES), F32),
                        pltpu.VMEM((ATTN_HEADS, tq, LANES), F32),
                        pltpu.VMEM((ATTN_HEADS, tq, HEAD_DIM), F32)],
        compiler_params=_cparams(("parallel", "arbitrary")),
        name="dsa",
    )(qa, iq, iw, ik.reshape(bsz, seq, IDX_DIM), ka.reshape(bsz, seq, HEAD_DIM),
      va.reshape(bsz, seq, HEAD_DIM))


_HALO = 8


def _gdn_kernel(qkv_ref, ab_ref, z_ref, cw_ref, alog_ref, dtb_ref, gn_ref, o_ref,
                ext_ref, state_ref, *, tr):
    ti = pl.program_id(1)
    width = 3 * GDN_WIDTH

    @pl.when(ti == 0)
    def _():
        ext_ref[0:_HALO, :] = jnp.zeros((_HALO, width), F32)
        state_ref[...] = jnp.zeros(state_ref.shape, F32)

    ext_ref[_HALO:_HALO + tr, :] = qkv_ref[...]
    conv = jnp.zeros((tr, width), F32)
    for j in range(CONV_WIDTH):
        off = _HALO - (CONV_WIDTH - 1) + j
        conv = conv + cw_ref[j:j + 1, :] * ext_ref[off:off + tr, :]
    ext_ref[0:_HALO, :] = ext_ref[tr:tr + _HALO, :]
    qkv = _silu(conv)

    ab = ab_ref[...]
    sp = jnp.maximum(ab + dtb_ref[...], 0.0) + jnp.log1p(jnp.exp(-jnp.abs(ab + dtb_ref[...])))
    g_all = -jnp.exp(alog_ref[...]) * sp
    beta_all = 1.0 / (1.0 + jnp.exp(-ab))

    ri = lax.broadcasted_iota(I32, (CHUNK, CHUNK), 0)
    ci = lax.broadcasted_iota(I32, (CHUNK, CHUNK), 1)
    incl = ri >= ci
    strict = ri > ci
    tri = jnp.where(incl, 1.0, 0.0).astype(BF16)
    gn = gn_ref[...]

    for c in range(tr // CHUNK):
        r0 = c * CHUNK
        g_c = g_all[r0:r0 + CHUNK, :]
        g_hi = g_c.astype(BF16)
        g_r1 = g_c - g_hi.astype(F32)
        g_mid = g_r1.astype(BF16)
        g_lo = (g_r1 - g_mid.astype(F32)).astype(BF16)
        gc_col = (jnp.dot(tri, g_hi, preferred_element_type=F32)
                  + jnp.dot(tri, g_mid, preferred_element_type=F32)
                  + jnp.dot(tri, g_lo, preferred_element_type=F32))
        gc_row = gc_col.T
        outs = []
        for h in range(GDN_HEADS):
            lo, hi = h * HEAD_DIM, (h + 1) * HEAD_DIM
            qh = qkv[r0:r0 + CHUNK, lo:hi]
            kh = qkv[r0:r0 + CHUNK, GDN_WIDTH + lo:GDN_WIDTH + hi]
            vh = qkv[r0:r0 + CHUNK, 2 * GDN_WIDTH + lo:2 * GDN_WIDTH + hi]
            qh = qh * lax.rsqrt(jnp.sum(qh * qh, axis=-1, keepdims=True) + RMS_EPS) * (HEAD_DIM ** -0.5)
            kh = kh * lax.rsqrt(jnp.sum(kh * kh, axis=-1, keepdims=True) + RMS_EPS)
            beta = beta_all[r0:r0 + CHUNK, GDN_HEADS + h:GDN_HEADS + h + 1]
            gcc = gc_col[:, h:h + 1]
            gcr = gc_row[h:h + 1, :]
            decay = jnp.where(incl, jnp.exp(jnp.where(incl, gcc - gcr, 0.0)), 0.0)
            kb = kh * beta
            low = jnp.where(strict, _dot_nt(kb, kh) * decay, 0.0)
            e_gc = jnp.exp(gcc)
            rhs = jnp.concatenate([vh * beta, kb * e_gc], axis=1)
            rhs = rhs - _dot(low, rhs)
            pw = low
            for _ in range(5):
                pw = _dot(pw, pw)
                rhs = rhs + _dot(pw, rhs)
            u = rhs[:, :HEAD_DIM]
            w = rhs[:, HEAD_DIM:]
            a_intra = jnp.where(incl, _dot_nt(qh, kh) * decay, 0.0)
            q_dec = qh * e_gc
            gc_last = gcc[CHUNK - 1:CHUNK, :]
            k_dec = kh * jnp.exp(gc_last - gcc)
            st = state_ref[h]
            v_new = u - _dot(w, st)
            o = _dot(q_dec, st) + _dot(a_intra, v_new)
            state_ref[h] = st * jnp.exp(gc_last) + _dot(k_dec.T, v_new)
            zh = z_ref[r0:r0 + CHUNK, lo:hi]
            y = o * lax.rsqrt(jnp.mean(o * o, axis=-1, keepdims=True) + RMS_EPS) * gn
            outs.append(y * _silu(zh))
        o_ref[r0:r0 + CHUNK, :] = jnp.concatenate(outs, axis=1)


def _gdn(qkvb, ab, zb, conv_w, a_log, dt_bias, gdn_norm_g, bsz, seq, tr=128):
    n_t = seq // tr
    row = lambda b, i: (b * n_t + i, 0)
    const = lambda b, i: (0, 0)
    alog = _pad_cols(a_log.reshape(1, GDN_HEADS).astype(F32), LANES)
    dtb = _pad_cols(dt_bias.reshape(1, GDN_HEADS).astype(F32), LANES)
    return pl.pallas_call(
        functools.partial(_gdn_kernel, tr=tr),
        grid=(bsz, n_t),
        in_specs=[pl.BlockSpec((tr, 3 * GDN_WIDTH), row),
                  pl.BlockSpec((tr, LANES), row),
                  pl.BlockSpec((tr, GDN_WIDTH), row),
                  pl.BlockSpec((CONV_WIDTH, 3 * GDN_WIDTH), const),
                  pl.BlockSpec((1, LANES), const),
                  pl.BlockSpec((1, LANES), const),
                  pl.BlockSpec((1, HEAD_DIM), const)],
        out_specs=pl.BlockSpec((tr, GDN_WIDTH), row),
        out_shape=jax.ShapeDtypeStruct((bsz * seq, GDN_WIDTH), F32),
        scratch_shapes=[pltpu.VMEM((tr + _HALO, 3 * GDN_WIDTH), F32),
                        pltpu.VMEM((GDN_HEADS, HEAD_DIM, HEAD_DIM), F32)],
        compiler_params=_cparams(("parallel", "arbitrary")),
        name="gdn",
    )(qkvb, ab, zb, conv_w.astype(F32), alog, dtb, gdn_norm_g.reshape(1, HEAD_DIM).astype(F32))


def _memkv_kernel(m_ref, w_ref, k_ref, v_ref):
    r = jnp.dot(m_ref[...].astype(BF16), w_ref[...], preferred_element_type=F32)
    k_ref[...] = r[:, :MEM_WIDTH].astype(BF16)
    v_ref[...] = r[:, MEM_WIDTH:].astype(BF16)


def _memkv(mem2, w_mem_kv, n_mem):
    n_rows, d_model = mem2.shape
    row = lambda i: (i, 0)
    return pl.pallas_call(
        _memkv_kernel,
        grid=(n_rows // n_mem,),
        in_specs=[pl.BlockSpec((n_mem, d_model), row),
                  pl.BlockSpec((d_model, 2 * MEM_WIDTH), lambda i: (0, 0))],
        out_specs=[pl.BlockSpec((n_mem, MEM_WIDTH), row)] * 2,
        out_shape=[jax.ShapeDtypeStruct((n_rows, MEM_WIDTH), BF16)] * 2,
        compiler_params=_cparams(("parallel",)),
        name="memkv",
    )(mem2, w_mem_kv.astype(BF16))


_ROUTE_ROWS = 128


def _post_kernel(x_ref, oa_ref, ob_ref, qc_ref, mk_ref, mv_ref, wout_ref, ag_ref, mg_ref,
                 lg_ref, lb_ref, wr_ref, br_ref, x1_ref, ids_ref, gate_ref, *, alpha):
    tm = x_ref.shape[0]
    oa = oa_ref[...]
    oa = oa * lax.rsqrt(jnp.mean(oa * oa, axis=-1, keepdims=True) + RMS_EPS) * ag_ref[...]

    qc = qc_ref[...]
    mk = mk_ref[...]
    mv = mv_ref[...]
    heads = []
    for h in range(MEM_HEADS):
        lo, hi = h * HEAD_DIM, (h + 1) * HEAD_DIM
        lg = _dot_nt(qc[:, lo:hi], mk[:, lo:hi]) * (HEAD_DIM ** -0.5)
        e = jnp.exp(lg - jnp.max(lg, axis=-1, keepdims=True))
        p = e / jnp.sum(e, axis=-1, keepdims=True)
        heads.append(jnp.dot(p.astype(BF16), mv[:, lo:hi], preferred_element_type=F32))
    oc = jnp.concatenate(heads, axis=1)
    oc = oc * lax.rsqrt(jnp.mean(oc * oc, axis=-1, keepdims=True) + RMS_EPS) * mg_ref[...]

    mix = (jnp.dot(oa.astype(BF16), wout_ref[0:A_WIDTH, :], preferred_element_type=F32)
           + jnp.dot(ob_ref[...].astype(BF16), wout_ref[A_WIDTH:A_WIDTH + GDN_WIDTH, :],
                     preferred_element_type=F32)
           + jnp.dot(oc.astype(BF16), wout_ref[A_WIDTH + GDN_WIDTH:, :], preferred_element_type=F32))
    hres = alpha * x_ref[...] + mix
    mu = jnp.mean(hres, axis=-1, keepdims=True)
    var = jnp.mean(jnp.square(hres - mu), axis=-1, keepdims=True)
    x1 = (hres - mu) * lax.rsqrt(var + LN_EPS) * lg_ref[...] + lb_ref[...]
    x1_ref[...] = x1

    lt = _dot_nt(wr_ref[...], x1) + br_ref[...]
    sub = lax.broadcasted_iota(I32, (EXPERTS_PER_GROUP, tm), 0)
    gl = lt[0:N_GROUPS, :]
    gmax = jnp.max(gl, axis=0, keepdims=True)
    gprob = jnp.exp(gl - gmax) / jnp.sum(jnp.exp(gl - gmax), axis=0, keepdims=True)
    p_grp = jnp.max(gprob, axis=0, keepdims=True)
    grp = jnp.min(jnp.where(gprob == p_grp, sub, N_GROUPS), axis=0, keepdims=True)
    el = jnp.zeros((EXPERTS_PER_GROUP, tm), F32)
    for g in range(N_GROUPS):
        r0 = N_GROUPS + g * EXPERTS_PER_GROUP
        el = el + jnp.where(grp == g, lt[r0:r0 + EXPERTS_PER_GROUP, :], 0.0)
    ee = jnp.exp(el - jnp.max(el, axis=0, keepdims=True))
    pe = ee / jnp.sum(ee, axis=0, keepdims=True)
    p1 = jnp.max(pe, axis=0, keepdims=True)
    i1 = jnp.min(jnp.where(pe == p1, sub, EXPERTS_PER_GROUP), axis=0, keepdims=True)
    rest = jnp.where(sub == i1, -1.0, pe)
    p2 = jnp.max(rest, axis=0, keepdims=True)
    i2 = jnp.min(jnp.where(rest == p2, sub, EXPERTS_PER_GROUP), axis=0, keepdims=True)
    psum = p1 + p2
    g1 = p_grp * p1 / psum
    g2 = p_grp * p2 / psum
    e1 = grp * EXPERTS_PER_GROUP + i1
    e2 = grp * EXPERTS_PER_GROUP + i2
    ids_ref[...] = jnp.where(sub == 0, e1, jnp.where(sub == 1, e2, 0))
    gate_ref[...] = jnp.where(sub == 0, g1, jnp.where(sub == 1, g2, 0.0))


def _post(x2, oa, ob, qc, mk, mv, w_out, attn_g, mem_g, ln_g, ln_b, w_group, b_group,
          w_router, b_router, seq, n_mem, alpha, tm=256):
    n_tok, d_model = x2.shape
    n_t = seq // tm
    row = lambda i: (i, 0)
    const = lambda i: (0, 0)
    per_b = lambda i: (i // n_t, 0, 0)
    lane = lambda i: (0, i)
    bsz = n_tok // seq
    wr = jnp.pad(jnp.concatenate([w_group, w_router], axis=1).T,
                 ((0, _ROUTE_ROWS - N_GROUPS - N_EXPERTS), (0, 0))).astype(BF16)
    br = jnp.pad(jnp.concatenate([b_group, b_router]),
                 (0, _ROUTE_ROWS - N_GROUPS - N_EXPERTS)).reshape(_ROUTE_ROWS, 1).astype(F32)
    vec = lambda v: v.reshape(1, -1).astype(F32)
    return pl.pallas_call(
        functools.partial(_post_kernel, alpha=alpha),
        grid=(n_tok // tm,),
        in_specs=[pl.BlockSpec((tm, d_model), row),
                  pl.BlockSpec((tm, A_WIDTH), row),
                  pl.BlockSpec((tm, GDN_WIDTH), row),
                  pl.BlockSpec((tm, MEM_WIDTH), row),
                  pl.BlockSpec((None, n_mem, MEM_WIDTH), per_b),
                  pl.BlockSpec((None, n_mem, MEM_WIDTH), per_b),
                  pl.BlockSpec(w_out.shape, const),
                  pl.BlockSpec((1, A_WIDTH), const),
                  pl.BlockSpec((1, MEM_WIDTH), const),
                  pl.BlockSpec((1, d_model), const),
                  pl.BlockSpec((1, d_model), const),
                  pl.BlockSpec((_ROUTE_ROWS, d_model), const),
                  pl.BlockSpec((_ROUTE_ROWS, 1), const)],
        out_specs=[pl.BlockSpec((tm, d_model), row),
                   pl.BlockSpec((EXPERTS_PER_GROUP, tm), lane),
                   pl.BlockSpec((EXPERTS_PER_GROUP, tm), lane)],
        out_shape=[jax.ShapeDtypeStruct((n_tok, d_model), F32),
                   jax.ShapeDtypeStruct((EXPERTS_PER_GROUP, n_tok), I32),
                   jax.ShapeDtypeStruct((EXPERTS_PER_GROUP, n_tok), F32)],
        compiler_params=_cparams(("parallel",)),
        name="post",
    )(x2, oa, ob, qc, mk.reshape(bsz, n_mem, MEM_WIDTH), mv.reshape(bsz, n_mem, MEM_WIDTH),
      w_out.astype(BF16), vec(attn_g), vec(mem_g), vec(ln_g), vec(ln_b), wr, br)


def _rank_kernel(ids_ref, rank_ref, cnt_ref, carry_ref):
    i = pl.program_id(0)
    tm = ids_ref.shape[1]

    @pl.when(i == 0)
    def _():
        carry_ref[...] = jnp.zeros(carry_ref.shape, F32)

    ids = ids_ref[...]
    eio = lax.broadcasted_iota(I32, (N_EXPERTS, tm), 0)
    oh0 = jnp.where(eio == ids[0:1, :], 1.0, 0.0)
    oh1 = jnp.where(eio == ids[1:2, :], 1.0, 0.0)
    cnt = oh0 + oh1
    before = (lax.broadcasted_iota(I32, (tm, tm), 0) < lax.broadcasted_iota(I32, (tm, tm), 1))
    prefix = jnp.dot(cnt.astype(BF16), jnp.where(before, 1.0, 0.0).astype(BF16),
                     preferred_element_type=F32) + carry_ref[:, 0:1]
    r0 = jnp.sum(oh0 * prefix, axis=0, keepdims=True)
    r1 = jnp.sum(oh1 * prefix, axis=0, keepdims=True)
    sub = lax.broadcasted_iota(I32, (EXPERTS_PER_GROUP, tm), 0)
    rank_ref[...] = jnp.where(sub == 0, r0, jnp.where(sub == 1, r1, 0.0)).astype(I32)
    carry_ref[...] = carry_ref[...] + jnp.sum(cnt, axis=1, keepdims=True)
    cnt_ref[...] = carry_ref[...].astype(I32)


def _rank(ids, tm=512):
    n_tok = ids.shape[1]
    lane = lambda i: (0, i)
    return pl.pallas_call(
        _rank_kernel,
        grid=(n_tok // tm,),
        in_specs=[pl.BlockSpec((EXPERTS_PER_GROUP, tm), lane)],
        out_specs=[pl.BlockSpec((EXPERTS_PER_GROUP, tm), lane),
                   pl.BlockSpec((N_EXPERTS, LANES), lambda i: (0, 0))],
        out_shape=[jax.ShapeDtypeStruct((EXPERTS_PER_GROUP, n_tok), I32),
                   jax.ShapeDtypeStruct((N_EXPERTS, LANES), I32)],
        scratch_shapes=[pltpu.VMEM((N_EXPERTS, LANES), F32)],
        compiler_params=_cparams(("arbitrary",)),
        name="rank",
    )(ids)


def _row_copy(src_ref, src_row, dst_ref, dst_row, sem):
    return pltpu.make_async_copy(src_ref.at[pl.ds(src_row, 1)], dst_ref.at[pl.ds(dst_row, 1)], sem)


def _dispatch_kernel(pstart_ref, meta_ref, x_ref, zeros_ref, xbuf_ref, sem, *, td):
    del zeros_ref
    t0 = pl.program_id(0) * td

    def issue(t, carry):
        for j in range(TOP_K):
            dest = pstart_ref[meta_ref[0, j, t]] + meta_ref[0, TOP_K + j, t]
            _row_copy(x_ref, t0 + t, xbuf_ref, dest, sem).start()
        return carry

    lax.fori_loop(0, td, issue, 0)

    def drain(t, carry):
        for j in range(TOP_K):
            _row_copy(x_ref, 0, xbuf_ref, 0, sem).wait()
        return carry

    lax.fori_loop(0, td, drain, 0)


def _dispatch(pad_start, meta, x1, cap, td):
    n_tok, d_model = x1.shape
    zeros = jnp.zeros((cap, d_model), F32)
    return pl.pallas_call(
        functools.partial(_dispatch_kernel, td=td),
        grid_spec=pltpu.PrefetchScalarGridSpec(
            num_scalar_prefetch=1,
            grid=(n_tok // td,),
            in_specs=[pl.BlockSpec((1, 2 * TOP_K, td), lambda i, ps: (i, 0, 0),
                                   memory_space=pltpu.SMEM),
                      pl.BlockSpec(memory_space=pl.ANY),
                      pl.BlockSpec(memory_space=pl.ANY)],
            out_specs=pl.BlockSpec(memory_space=pl.ANY),
            scratch_shapes=[pltpu.SemaphoreType.DMA(())]),
        out_shape=jax.ShapeDtypeStruct((cap, d_model), F32),
        input_output_aliases={3: 0},
        compiler_params=_cparams(("arbitrary",)),
        name="dispatch",
    )(pad_start, meta, x1, zeros)


def _mlp_kernel(be_ref, nused_ref, x_ref, wg_ref, wu_ref, wd_ref, y_ref):
    i = pl.program_id(0)

    @pl.when(i < nused_ref[0])
    def _():
        xb = x_ref[...].astype(BF16)
        hg = jnp.dot(xb, wg_ref[...].astype(BF16), preferred_element_type=F32)
        hu = jnp.dot(xb, wu_ref[...].astype(BF16), preferred_element_type=F32)
        y_ref[...] = jnp.dot((_silu(hg) * hu).astype(BF16), wd_ref[...].astype(BF16),
                             preferred_element_type=F32)

    @pl.when(i >= nused_ref[0])
    def _():
        y_ref[...] = jnp.zeros(y_ref.shape, F32)


def _mlp(blk_expert, n_used, xbuf, w_gate, w_up, w_down):
    cap, d_model = xbuf.shape
    d_exp = w_gate.shape[2]
    blk = lambda i, be, nu: (i, 0)
    wsel = lambda i, be, nu: (be[i], 0, 0)
    return pl.pallas_call(
        _mlp_kernel,
        grid_spec=pltpu.PrefetchScalarGridSpec(
            num_scalar_prefetch=2,
            grid=(cap // MOE_BLOCK,),
            in_specs=[pl.BlockSpec((MOE_BLOCK, d_model), blk),
                      pl.BlockSpec((None, d_model, d_exp), wsel),
                      pl.BlockSpec((None, d_model, d_exp), wsel),
                      pl.BlockSpec((None, d_exp, d_model), wsel)],
            out_specs=pl.BlockSpec((MOE_BLOCK, d_model), blk)),
        out_shape=jax.ShapeDtypeStruct((cap, d_model), F32),
        compiler_params=_cparams(("arbitrary",)),
        name="mlp",
    )(blk_expert, n_used, xbuf, w_gate, w_up, w_down)


def _combine_kernel(pstart_ref, meta_ref, x1_ref, gt_ref, lg_ref, lb_ref, ybuf_ref, o_ref,
                    rows_ref, sem, *, tc, alpha):
    def issue(t, carry):
        for j in range(TOP_K):
            src = pstart_ref[meta_ref[0, j, t]] + meta_ref[0, TOP_K + j, t]
            _row_copy(ybuf_ref, src, rows_ref.at[j], t, sem).start()
        return carry

    lax.fori_loop(0, tc, issue, 0)

    def drain(t, carry):
        for j in range(TOP_K):
            _row_copy(ybuf_ref, 0, rows_ref.at[j], 0, sem).wait()
        return carry

    lax.fori_loop(0, tc, drain, 0)

    gt = gt_ref[...]
    ffn = rows_ref[0] * gt[:, 0:1] + rows_ref[1] * gt[:, 1:2]
    hres = alpha * x1_ref[...] + ffn
    mu = jnp.mean(hres, axis=-1, keepdims=True)
    var = jnp.mean(jnp.square(hres - mu), axis=-1, keepdims=True)
    o_ref[...] = (hres - mu) * lax.rsqrt(var + LN_EPS) * lg_ref[...] + lb_ref[...]


def _combine(pad_start, meta, x1, gates_t, ln_g, ln_b, ybuf, alpha, tc):
    n_tok, d_model = x1.shape
    row = lambda i, ps: (i, 0)
    const = lambda i, ps: (0, 0)
    vec = lambda v: v.reshape(1, -1).astype(F32)
    return pl.pallas_call(
        functools.partial(_combine_kernel, tc=tc, alpha=alpha),
        grid_spec=pltpu.PrefetchScalarGridSpec(
            num_scalar_prefetch=1,
            grid=(n_tok // tc,),
            in_specs=[pl.BlockSpec((1, 2 * TOP_K, tc), lambda i, ps: (i, 0, 0),
                                   memory_space=pltpu.SMEM),
                      pl.BlockSpec((tc, d_model), row),
                      pl.BlockSpec((tc, EXPERTS_PER_GROUP), row),
                      pl.BlockSpec((1, d_model), const),
                      pl.BlockSpec((1, d_model), const),
                      pl.BlockSpec(memory_space=pl.ANY)],
            out_specs=pl.BlockSpec((tc, d_model), row),
            scratch_shapes=[pltpu.VMEM((TOP_K, tc, d_model), F32),
                            pltpu.SemaphoreType.DMA(())]),
        out_shape=jax.ShapeDtypeStruct((n_tok, d_model), F32),
        compiler_params=_cparams(("arbitrary",)),
        name="combine",
    )(pad_start, meta, x1, gates_t, vec(ln_g), vec(ln_b), ybuf)


def _moe(x1, ids, gates, w_gate, w_up, w_down, ln_g, ln_b, alpha, tile=256):
    n_tok, d_model = x1.shape
    rank, counts = _rank(ids)
    counts = counts[:, 0]
    padded = (counts + MOE_BLOCK - 1) // MOE_BLOCK * MOE_BLOCK
    pad_ends = jnp.cumsum(padded)
    pad_start = (pad_ends - padded).astype(I32)
    n_asg = n_tok * TOP_K
    cap = (n_asg + MOE_BLOCK - 1) // MOE_BLOCK * MOE_BLOCK + N_EXPERTS * MOE_BLOCK
    n_blk = cap // MOE_BLOCK
    blk_expert = jnp.minimum(
        jnp.searchsorted(pad_ends, jnp.arange(n_blk, dtype=I32) * MOE_BLOCK, side='right'),
        N_EXPERTS - 1).astype(I32)
    n_used = (pad_ends[-1:] // MOE_BLOCK).astype(I32)
    meta = jnp.concatenate([ids[0:TOP_K], rank[0:TOP_K]], axis=0)
    meta = meta.reshape(2 * TOP_K, n_tok // tile, tile).transpose(1, 0, 2)
    xbuf = _dispatch(pad_start, meta, x1, cap, tile)
    ybuf = _mlp(blk_expert, n_used, xbuf, w_gate, w_up, w_down)
    return _combine(pad_start, meta, x1, gates.T, ln_g, ln_b, ybuf, alpha, tile)


def kernel(x, mem, w_in, kv_norm_g, w_k_up, w_v_up, conv_w, A_log, dt_bias, gdn_norm_g, attn_norm_g, mem_norm_g, w_mem_kv, w_out, ln1_g, ln1_b, w_group, b_group, w_router, b_router, w_gate, w_up, w_down, ln2_g, ln2_b):
    bsz, seq, d_model = x.shape
    n_mem = mem.shape[1]
    depth = w_in.shape[0]
    alpha = (2 * depth) ** 0.25
    inv_freq = 1.0 / (ROPE_THETA ** (jnp.arange(0, HEAD_DIM, 2, dtype=F32) / HEAD_DIM))
    ang = jnp.arange(seq, dtype=F32)[:, None] * inv_freq[None, :]
    cos128 = jnp.tile(jnp.cos(ang), (1, LANES // (HEAD_DIM // 2)))
    sin128 = jnp.tile(jnp.sin(ang), (1, LANES // (HEAD_DIM // 2)))
    x2 = x.reshape(bsz * seq, d_model)
    mem2 = mem.reshape(bsz * n_mem, d_model)
    for l in range(depth):
        qa, iq, ik, ka, va, iw, qkvb, zb, ab, qc = _proj(
            x2, w_in[l], w_k_up[l], w_v_up[l], kv_norm_g[l], cos128, sin128, seq)
        oa = _dsa(qa, iq, iw, ik, ka, va, bsz, seq)
        ob = _gdn(qkvb, ab, zb, conv_w[l], A_log[l], dt_bias[l], gdn_norm_g[l], bsz, seq)
        mk, mv = _memkv(mem2, w_mem_kv[l], n_mem)
        x1, ids, gates = _post(x2, oa, ob, qc, mk, mv, w_out[l], attn_norm_g[l], mem_norm_g[l],
                               ln1_g[l], ln1_b[l], w_group[l], b_group[l], w_router[l],
                               b_router[l], seq, n_mem, alpha)
        x2 = _moe(x1, ids, gates, w_gate[l], w_up[l], w_down[l], ln2_g[l], ln2_b[l], alpha)
    return x2.reshape(bsz, seq, d_model)
```

```python
import functools

import numpy as np
import jax
import jax.numpy as jnp
from jax import lax
from jax.experimental import pallas as pl
from jax.experimental.pallas import tpu as pltpu
from jax.experimental.pallas import tpu_sc as plsc

F32 = jnp.float32
BF16 = jnp.bfloat16
I32 = jnp.int32
I16 = jnp.int16

HEAD_DIM = 64
ATTN_HEADS = 6
A_WIDTH = ATTN_HEADS * HEAD_DIM
KV_RANK = 128
IDX_HEADS = 4
IDX_DIM = 64
INDEX_TOPK = 256
GDN_HEADS = 6
GDN_WIDTH = GDN_HEADS * HEAD_DIM
CONV_WIDTH = 4
CHUNK = 64
MEM_HEADS = 4
MEM_WIDTH = MEM_HEADS * HEAD_DIM
SPLIT_SIZES = (A_WIDTH, KV_RANK, IDX_HEADS * IDX_DIM, IDX_DIM, IDX_HEADS,
               GDN_WIDTH, GDN_WIDTH, GDN_WIDTH, GDN_WIDTH, GDN_HEADS, GDN_HEADS,
               MEM_WIDTH)
ROPE_THETA = 10000.0
N_GROUPS = 8
EXPERTS_PER_GROUP = 8
N_EXPERTS = N_GROUPS * EXPERTS_PER_GROUP
TOP_K = 2
MOE_BLOCK = 256
LN_EPS = 1e-5
RMS_EPS = 1e-6
NEG_INF = -1e30
INT_MIN = -2 ** 31
I16_MIN = -2 ** 15

LANES = 128
VMEM_LIMIT = 56 * 1024 * 1024


def _cparams(sem):
    return pltpu.CompilerParams(dimension_semantics=sem, vmem_limit_bytes=VMEM_LIMIT)


def _dot(a, b):
    return jnp.dot(a.astype(BF16), b.astype(BF16), preferred_element_type=F32)


def _dot_nt(a, b):
    return lax.dot_general(a.astype(BF16), b.astype(BF16), (((1,), (1,)), ((), ())),
                           preferred_element_type=F32)


def _silu(t):
    return t * (1.0 / (1.0 + jnp.exp(-t)))


_P_QA = (0, 384)
_P_IQ = (384, 640)
_P_CKV = (640, 768)
_P_IK = (768, 896)
_P_IW = (896, 1024)
_P_QKVB = (1024, 2176)
_P_ZB = (2176, 2560)
_P_AB = (2560, 2688)
_P_QC = (2688, 2944)
_P_TOTAL = 2944
_V_ROWS = HEAD_DIM + 16


def _rot_cols(w, n_heads):
    k = w.shape[0]
    w4 = w.reshape(k, n_heads, 2, HEAD_DIM // 2)
    return jnp.concatenate([-w4[:, :, 1:2], w4[:, :, 0:1]], axis=2).reshape(k, n_heads * HEAD_DIM)


def _pad_cols(w, width):
    return jnp.pad(w, ((0, 0), (0, width - w.shape[1])))


def _proj_kernel(x_ref, w_ref, w2_ref, cos_ref, sin_ref, kvg_ref,
                 qa_ref, iq_ref, ik_ref, ka_ref, va_ref, iw_ref, qkvb_ref, zb_ref, ab_ref, qc_ref):
    xb = x_ref[...].astype(BF16)

    def mm(slab):
        return jnp.dot(xb, w_ref[:, slab[0]:slab[1]], preferred_element_type=F32)

    cos = cos_ref[...]
    sin = sin_ref[...]
    cos3 = jnp.concatenate([cos] * 3, axis=1)
    sin3 = jnp.concatenate([sin] * 3, axis=1)
    cos2 = jnp.concatenate([cos] * 2, axis=1)
    sin2 = jnp.concatenate([sin] * 2, axis=1)
    cos64 = cos[:, :HEAD_DIM]
    sin64 = sin[:, :HEAD_DIM]

    def rotate_half(t):
        slabs = []
        for j in range(t.shape[1] // LANES):
            ts = t[:, j * LANES:(j + 1) * LANES]
            first = lax.broadcasted_iota(I32, ts.shape, 1) % HEAD_DIM < HEAD_DIM // 2
            slabs.append(jnp.where(first, -pltpu.roll(ts, LANES - HEAD_DIM // 2, 1),
                                   pltpu.roll(ts, HEAD_DIM // 2, 1)))
        return jnp.concatenate(slabs, axis=1)

    qa = mm(_P_QA)
    qa_ref[...] = (qa * cos3 + rotate_half(qa) * sin3).astype(BF16)
    iq = mm(_P_IQ)
    iq_ref[...] = (iq * cos2 + rotate_half(iq) * sin2).astype(BF16)

    ckv = mm(_P_CKV)
    cn = ckv * lax.rsqrt(jnp.mean(ckv * ckv, axis=-1, keepdims=True) + RMS_EPS) * kvg_ref[...]
    r = jnp.dot(cn.astype(BF16), w2_ref[...], preferred_element_type=F32)
    ka_ref[...] = (r[:, 0:64] * cos64 + r[:, 64:128] * sin64).astype(BF16)
    va_ref[...] = jnp.concatenate(
        [r[:, 128:256].T[0:HEAD_DIM, :], jnp.ones((_V_ROWS - HEAD_DIM, r.shape[0]), F32)],
        axis=0).astype(BF16)

    ikk = mm(_P_IK)
    ik_ref[...] = (ikk[:, 0:64] * cos64 + ikk[:, 64:128] * sin64).astype(BF16)
    iw_ref[...] = mm(_P_IW) * (IDX_HEADS ** -0.5 * IDX_DIM ** -0.5)
    qkvb_ref[...] = mm(_P_QKVB)
    zb_ref[...] = mm(_P_ZB)
    ab_ref[...] = mm(_P_AB)
    qc_ref[...] = mm(_P_QC).astype(BF16)


def _proj(x2, w_in, w_k_up, w_v_up, kv_norm_g, cos128, sin128, seq, tm=256):
    n_tok, d_model = x2.shape
    offs = np.cumsum(SPLIT_SIZES)[:-1].tolist()
    (w_qa, w_ckv, w_iq, w_ik, w_iw, w_qb, w_kb, w_vb, w_zb, w_a, w_b, w_qc) = jnp.split(w_in, offs, axis=1)
    w1 = jnp.concatenate([
        w_qa, w_iq, w_ckv,
        w_ik, _rot_cols(w_ik, 1), _pad_cols(w_iw, LANES),
        w_qb, w_kb, w_vb, w_zb, _pad_cols(jnp.concatenate([w_a, w_b], axis=1), LANES), w_qc,
    ], axis=1).astype(BF16)
    assert w1.shape[1] == _P_TOTAL
    w2 = _pad_cols(jnp.concatenate([w_k_up, _rot_cols(w_k_up, 1), w_v_up], axis=1), 2 * LANES).astype(BF16)
    n_pos = seq // tm
    row = lambda i: (i, 0)
    const = lambda i: (0, 0)
    pos = lambda i: (i % n_pos, 0)
    outs = [(A_WIDTH, BF16), (IDX_HEADS * IDX_DIM, BF16), (IDX_DIM, BF16), (HEAD_DIM, BF16),
            None, (LANES, F32), (3 * GDN_WIDTH, F32), (GDN_WIDTH, F32), (LANES, F32),
            (MEM_WIDTH, BF16)]
    out_specs = [pl.BlockSpec((tm, o[0]), row) if o else pl.BlockSpec((_V_ROWS, tm), lambda i: (0, i))
                 for o in outs]
    out_shape = [jax.ShapeDtypeStruct((n_tok, o[0]), o[1]) if o
                 else jax.ShapeDtypeStruct((_V_ROWS, n_tok), BF16) for o in outs]
    return pl.pallas_call(
        _proj_kernel,
        grid=(n_tok // tm,),
        in_specs=[pl.BlockSpec((tm, d_model), row),
                  pl.BlockSpec(w1.shape, const),
                  pl.BlockSpec(w2.shape, const),
                  pl.BlockSpec((tm, LANES), pos),
                  pl.BlockSpec((tm, LANES), pos),
                  pl.BlockSpec((1, KV_RANK), const)],
        out_specs=out_specs,
        out_shape=out_shape,
        compiler_params=_cparams(("parallel",)),
        name="proj",
    )(x2, w1, w2, cos128, sin128, kv_norm_g.reshape(1, KV_RANK).astype(F32))


def _dsa_kernel(qa_ref, iq_ref, iw_ref, ik_ref, ka_ref, vat_ref, o_ref,
                key_ref, hi_ref, lo_ref, m_ref, acc_ref, *, tq, kc, top):
    qi = pl.program_id(1)
    row0 = qi * tq
    n_kc = (row0 + tq + kc - 1) // kc
    qpos = row0 + lax.broadcasted_iota(I32, (1, tq), 1)

    def key_fold(v):
        return jnp.sum(v.reshape(kc // 8, 8, tq), axis=0)

    def head_rows(x, n_heads, width):
        return jnp.concatenate([x[:, h * width:(h + 1) * width] for h in range(n_heads)], axis=0)

    iq_rows = head_rows(iq_ref[...], IDX_HEADS, IDX_DIM)
    iw_t = iw_ref[...].T

    def score_body(c, carry):
        k0 = pl.multiple_of(c * kc, kc)
        d = _dot_nt(ik_ref[pl.ds(k0, kc), :], iq_rows)
        s = jnp.zeros((kc, tq), F32)
        for h in range(IDX_HEADS):
            s = s + iw_t[h:h + 1, :] * jnp.maximum(d[:, h * tq:(h + 1) * tq], 0.0)
        kidx = k0 + lax.broadcasted_iota(I32, (kc, tq), 0)
        s = jnp.where(s == 0.0, 0.0, s)
        s = jnp.where(kidx <= qpos, s, NEG_INF)
        bits = pltpu.bitcast(s, I32)
        key = jnp.where(bits >= 0, bits, bits ^ 0x7FFFFFFF)
        key_ref[pl.ds(k0, kc), :] = key
        hi_ref[pl.ds(k0, kc), :] = lax.shift_right_arithmetic(key, 16).astype(I16)
        lo_ref[pl.ds(k0, kc), :] = ((key & 0xFFFF) + I16_MIN).astype(I16)
        return carry

    lax.fori_loop(0, n_kc, score_body, 0)

    k_eff = jnp.minimum(top, qpos + 1).astype(F32)

    @pl.when(n_kc % 2 == 1)
    def _():
        k_pad = pl.multiple_of(n_kc * kc, kc)
        hi_ref[pl.ds(k_pad, kc), :] = jnp.full((kc, tq), I16_MIN, I16)
        lo_ref[pl.ds(k_pad, kc), :] = jnp.full((kc, tq), I16_MIN, I16)

    n_pairs = (n_kc + 1) // 2
    one_b = jnp.ones((), BF16)
    zero_b = jnp.zeros((), BF16)

    def count16(ref, pred):
        def body(c, acc):
            k0 = pl.multiple_of(c * (2 * kc), 2 * kc)
            for j in range(2):
                hit = jnp.where(pred(ref[pl.ds(k0 + j * kc, kc), :]), one_b, zero_b)
                hit = hit.reshape(kc // 16, 16, tq)
                parts = [hit[r] for r in range(kc // 16)]
                while len(parts) > 1:
                    parts = [a + b for a, b in zip(parts[0::2], parts[1::2])]
                acc = acc + parts[0]
            return acc
        acc = lax.fori_loop(0, n_pairs, body, jnp.zeros((16, tq), BF16))
        return jnp.sum(acc.astype(F32), axis=0, keepdims=True)

    def search16(ref, start, n_bits, k_want):
        def bit_body(i, t):
            cand = t + lax.shift_left(jnp.int32(1), n_bits - 1 - i)
            c16 = cand.astype(I16)
            return jnp.where(count16(ref, lambda v: v >= c16) >= k_want, cand, t)
        return lax.fori_loop(0, n_bits, bit_body, start)

    hi0 = jnp.where(count16(hi_ref, lambda v: v >= 0) >= k_eff, 0, I16_MIN).astype(I32)
    thr_hi = search16(hi_ref, hi0, 15, k_eff)
    thr_hi16 = thr_hi.astype(I16)
    k_low = k_eff - count16(hi_ref, lambda v: v > thr_hi16)

    def band_body(c, carry):
        k0 = pl.multiple_of(c * kc, kc)
        lo_ref[pl.ds(k0, kc), :] = jnp.where(hi_ref[pl.ds(k0, kc), :] == thr_hi16,
                                             lo_ref[pl.ds(k0, kc), :], I16_MIN)
        return carry

    lax.fori_loop(0, n_kc, band_body, 0)
    thr_lo = search16(lo_ref, jnp.full((1, tq), I16_MIN, I32), 16, k_low)
    thr_lo16 = thr_lo.astype(I16)
    thr = thr_hi * 65536 + (thr_lo - I16_MIN)
    n_tie = k_low - count16(lo_ref, lambda v: v > thr_lo16)

    q_rows = head_rows((qa_ref[...].astype(F32) * (HEAD_DIM ** -0.5)).astype(BF16),
                       ATTN_HEADS, HEAD_DIM)
    m_ref[...] = jnp.full(m_ref.shape, NEG_INF, F32)
    acc_ref[...] = jnp.zeros(acc_ref.shape, F32)
    earlier = jnp.where(lax.broadcasted_iota(I32, (kc, kc), 1) < lax.broadcasted_iota(I32, (kc, kc), 0),
                        1.0, 0.0).astype(BF16)

    def attn_body(c, tie_seen):
        k0 = pl.multiple_of(c * kc, kc)
        kk = key_ref[pl.ds(k0, kc), :]
        kidx = k0 + lax.broadcasted_iota(I32, (kc, tq), 0)
        tie = kk == thr
        tie_f = jnp.where(tie, 1.0, 0.0)
        tie_rank = jnp.dot(earlier, tie_f.astype(BF16), preferred_element_type=F32) + tie_seen
        bias = jnp.where(kk > thr, 0.0,
                         jnp.where(tie, jnp.where(tie_rank < n_tie, 0.0, NEG_INF), NEG_INF))
        bias = jnp.where(kidx <= qpos, bias, NEG_INF)
        lg_all = _dot_nt(ka_ref[pl.ds(k0, kc), :], q_rows)
        lgs = [lg_all[:, h * tq:(h + 1) * tq] + bias for h in range(ATTN_HEADS)]
        m_prev = m_ref[...]
        m_new = jnp.maximum(m_prev, jnp.concatenate(
            [jnp.max(lg, axis=0, keepdims=True) for lg in lgs], axis=1))
        p = jnp.concatenate([jnp.exp(lgs[h] - m_new[0:1, h * tq:(h + 1) * tq])
                             for h in range(ATTN_HEADS)], axis=1)
        alpha = jnp.exp(m_prev - m_new)
        acc_ref[...] = alpha[0:1, :] * acc_ref[...] + jnp.dot(
            vat_ref[:, pl.ds(k0, kc)], p.astype(BF16), preferred_element_type=F32)
        m_ref[...] = m_new
        return tie_seen + jnp.sum(key_fold(tie_f), axis=0, keepdims=True)

    lax.fori_loop(0, n_kc, attn_body, jnp.zeros((1, tq), F32))
    o_t = acc_ref[0:HEAD_DIM, :] / acc_ref[HEAD_DIM:HEAD_DIM + 1, :]
    o_ref[...] = jnp.concatenate(
        [o_t[:, h * tq:(h + 1) * tq].T for h in range(ATTN_HEADS)], axis=1)


def _dsa(qa, iq, iw, ik, ka, va_t, bsz, seq, tq=256):
    tq = min(tq, seq)
    kc = tq
    top = min(INDEX_TOPK, seq // 4)
    n_q = seq // tq
    assert seq // 16 <= 256 and (seq // kc) % 2 == 0
    row = lambda b, i: (b * n_q + i, 0)
    per_b = lambda b, i: (b, 0, 0)
    kern = functools.partial(_dsa_kernel, tq=tq, kc=kc, top=top)
    return pl.pallas_call(
        kern,
        grid=(bsz, n_q),
        in_specs=[pl.BlockSpec((tq, A_WIDTH), row),
                  pl.BlockSpec((tq, IDX_HEADS * IDX_DIM), row),
                  pl.BlockSpec((tq, LANES), row),
                  pl.BlockSpec((None, seq, IDX_DIM), per_b),
                  pl.BlockSpec((None, seq, HEAD_DIM), per_b),
                  pl.BlockSpec((_V_ROWS, seq), lambda b, i: (0, b))],
        out_specs=pl.BlockSpec((tq, A_WIDTH), row),
        out_shape=jax.ShapeDtypeStruct((bsz * seq, A_WIDTH), F32),
        scratch_shapes=[pltpu.VMEM((seq, tq), I32),
                        pltpu.VMEM((seq, tq), I16),
                        pltpu.VMEM((seq, tq), I16),
                        pltpu.VMEM((8, ATTN_HEADS * tq), F32),
                        pltpu.VMEM((_V_ROWS, ATTN_HEADS * tq), F32)],
        compiler_params=_cparams(("parallel", "arbitrary")),
        name="dsa",
    )(qa, iq, iw, ik.reshape(bsz, seq, IDX_DIM), ka.reshape(bsz, seq, HEAD_DIM), va_t)


_HALO = 8


def _bmm(a, b):
    return lax.dot_general(a.astype(BF16), b.astype(BF16), (((2,), (1,)), ((0,), (0,))),
                           preferred_element_type=F32)


def _bmm_nt(a, b):
    return lax.dot_general(a.astype(BF16), b.astype(BF16), (((2,), (2,)), ((0,), (0,))),
                           preferred_element_type=F32)


def _gdn_kernel(qkv_ref, ab_ref, z_ref, cw_ref, alog_ref, dtb_ref, gn_ref, o_ref,
                ext_ref, state_ref, *, tr):
    ti = pl.program_id(1)
    width = 3 * GDN_WIDTH
    n_c = tr // CHUNK

    @pl.when(ti == 0)
    def _():
        ext_ref[0:_HALO, :] = jnp.zeros((_HALO, width), F32)
        state_ref[...] = jnp.zeros(state_ref.shape, F32)

    ext_ref[_HALO:_HALO + tr, :] = qkv_ref[...]
    conv = cw_ref[CONV_WIDTH - 1:CONV_WIDTH, :] * ext_ref[_HALO:_HALO + tr, :]
    for j in range(CONV_WIDTH - 1):
        off = _HALO - (CONV_WIDTH - 1) + j
        conv = conv + cw_ref[j:j + 1, :] * ext_ref[off:off + tr, :]
    ext_ref[0:_HALO, :] = ext_ref[tr:tr + _HALO, :]
    qkv = _silu(conv)

    ab = ab_ref[...]
    sp = jnp.maximum(ab + dtb_ref[...], 0.0) + jnp.log1p(jnp.exp(-jnp.abs(ab + dtb_ref[...])))
    g_all = -jnp.exp(alog_ref[...]) * sp
    beta_all = 1.0 / (1.0 + jnp.exp(-ab))

    lane = lax.broadcasted_iota(I32, (CHUNK, LANES), 1)
    row_i = lax.broadcasted_iota(I32, (CHUNK, LANES), 0)
    left = lane < HEAD_DIM
    incl = (row_i >= lane % HEAD_DIM)[None]
    strict = (row_i > lane % HEAD_DIM)[None]
    left_t = (lax.broadcasted_iota(I32, (tr, LANES), 1) < HEAD_DIM)
    tri = jnp.where(lax.broadcasted_iota(I32, (CHUNK, CHUNK), 0) >= lax.broadcasted_iota(I32, (CHUNK, CHUNK), 1),
                    1.0, 0.0).astype(BF16)

    def blk(x):
        return jnp.concatenate([jnp.where(left[None], x, 0.0), jnp.where(left[None], 0.0, x)], axis=1)

    def head_scale(x, eps, scale):
        sq = x * x
        keep = left_t if x.shape[0] == tr else left
        s_l = jnp.sum(jnp.where(keep, sq, 0.0), axis=-1, keepdims=True)
        s_r = jnp.sum(jnp.where(keep, 0.0, sq), axis=-1, keepdims=True)
        return x * jnp.where(keep, lax.rsqrt(s_l * scale + eps), lax.rsqrt(s_r * scale + eps))

    gc_cols, gc_rows = [], []
    for c in range(n_c):
        g_c = g_all[c * CHUNK:(c + 1) * CHUNK, :]
        g_hi = g_c.astype(BF16)
        g_r1 = g_c - g_hi.astype(F32)
        g_mid = g_r1.astype(BF16)
        g_lo = (g_r1 - g_mid.astype(F32)).astype(BF16)
        gc = (jnp.dot(tri, g_hi, preferred_element_type=F32)
              + jnp.dot(tri, g_mid, preferred_element_type=F32)
              + jnp.dot(tri, g_lo, preferred_element_type=F32))
        gc_cols.append(gc)
        gc_rows.append(gc.T)

    n_p = GDN_HEADS // 2
    q_n = [head_scale(qkv[:, p * LANES:(p + 1) * LANES], RMS_EPS, 1.0) * (HEAD_DIM ** -0.5)
           for p in range(n_p)]
    k_n = [head_scale(qkv[:, GDN_WIDTH + p * LANES:GDN_WIDTH + (p + 1) * LANES], RMS_EPS, 1.0)
           for p in range(n_p)]

    def stack(fn):
        return jnp.stack([fn(c, p) for c in range(n_c) for p in range(n_p)], axis=0)

    def rows(c):
        return slice(c * CHUNK, (c + 1) * CHUNK)

    def pair_cols(x, c, p, base):
        return jnp.where(left, x[rows(c), base + 2 * p:base + 2 * p + 1],
                         x[rows(c), base + 2 * p + 1:base + 2 * p + 2])

    qs = stack(lambda c, p: q_n[p][rows(c), :])
    ks = stack(lambda c, p: k_n[p][rows(c), :])
    vs = stack(lambda c, p: qkv[rows(c), 2 * GDN_WIDTH + p * LANES:2 * GDN_WIDTH + (p + 1) * LANES])
    beta = stack(lambda c, p: pair_cols(beta_all, c, p, GDN_HEADS))
    gcc = stack(lambda c, p: pair_cols(gc_cols[c], 0, p, 0))
    gcr = stack(lambda c, p: jnp.concatenate(
        [gc_rows[c][2 * p:2 * p + 1, :], gc_rows[c][2 * p + 1:2 * p + 2, :]], axis=1))
    decay = jnp.where(incl, jnp.exp(jnp.where(incl, gcc - gcr, 0.0)), 0.0)
    kb = ks * beta
    k_blk = blk(ks)
    low = jnp.where(strict, _bmm_nt(kb, k_blk) * decay, 0.0)
    e_gc = jnp.exp(gcc)
    u = vs * beta
    w = kb * e_gc
    u = u - _bmm(low, blk(u))
    w = w - _bmm(low, blk(w))
    pw = low
    for _ in range(5):
        pw = _bmm(pw, blk(pw))
        u = u + _bmm(pw, blk(u))
        w = w + _bmm(pw, blk(w))
    a_intra = jnp.where(incl, _bmm_nt(qs, k_blk) * decay, 0.0)
    q_dec = qs * e_gc
    gc_last = gcc[:, CHUNK - 1:CHUNK, :]
    kd_blk = blk(ks * jnp.exp(gc_last - gcc))
    kd_blk_t = jnp.stack([kd_blk[i].T for i in range(n_c * n_p)], axis=0)
    g_last = jnp.exp(gc_last)

    gn = gn_ref[...]
    st = state_ref[...]
    for c in range(n_c):
        sl = slice(c * n_p, (c + 1) * n_p)
        v_new = blk(u[sl] - _bmm(w[sl], st))
        o = _bmm(q_dec[sl], st) + _bmm(a_intra[sl], v_new)
        st = st * g_last[sl] + _bmm(kd_blk_t[sl], v_new)
        for p in range(n_p):
            y = head_scale(o[p], RMS_EPS, 1.0 / HEAD_DIM) * gn
            o_ref[rows(c), p * LANES:(p + 1) * LANES] = y * _silu(z_ref[rows(c), p * LANES:(p + 1) * LANES])
    state_ref[...] = st


def _gdn(qkvb, ab, zb, conv_w, a_log, dt_bias, gdn_norm_g, bsz, seq, tr=256):
    tr = min(tr, seq)
    n_t = seq // tr
    row = lambda b, i: (b * n_t + i, 0)
    const = lambda b, i: (0, 0)
    alog = _pad_cols(a_log.reshape(1, GDN_HEADS).astype(F32), LANES)
    dtb = _pad_cols(dt_bias.reshape(1, GDN_HEADS).astype(F32), LANES)
    return pl.pallas_call(
        functools.partial(_gdn_kernel, tr=tr),
        grid=(bsz, n_t),
        in_specs=[pl.BlockSpec((tr, 3 * GDN_WIDTH), row),
                  pl.BlockSpec((tr, LANES), row),
                  pl.BlockSpec((tr, GDN_WIDTH), row),
                  pl.BlockSpec((CONV_WIDTH, 3 * GDN_WIDTH), const),
                  pl.BlockSpec((1, LANES), const),
                  pl.BlockSpec((1, LANES), const),
                  pl.BlockSpec((1, LANES), const)],
        out_specs=pl.BlockSpec((tr, GDN_WIDTH), row),
        out_shape=jax.ShapeDtypeStruct((bsz * seq, GDN_WIDTH), F32),
        scratch_shapes=[pltpu.VMEM((tr + _HALO, 3 * GDN_WIDTH), F32),
                        pltpu.VMEM((GDN_HEADS // 2, LANES, LANES), F32)],
        compiler_params=_cparams(("parallel", "arbitrary")),
        name="gdn",
    )(qkvb, ab, zb, conv_w.astype(F32), alog, dtb,
      jnp.tile(gdn_norm_g.reshape(1, HEAD_DIM).astype(F32), (1, LANES // HEAD_DIM)))


def _memkv_kernel(m_ref, w_ref, k_ref, v_ref):
    r = jnp.dot(m_ref[...].astype(BF16), w_ref[...], preferred_element_type=F32)
    k_ref[...] = r[:, :MEM_WIDTH].astype(BF16)
    v_ref[...] = r[:, MEM_WIDTH:].astype(BF16)


def _memkv(mem2, w_mem_kv, n_mem):
    n_rows, d_model = mem2.shape
    row = lambda i: (i, 0)
    return pl.pallas_call(
        _memkv_kernel,
        grid=(n_rows // n_mem,),
        in_specs=[pl.BlockSpec((n_mem, d_model), row),
                  pl.BlockSpec((d_model, 2 * MEM_WIDTH), lambda i: (0, 0))],
        out_specs=[pl.BlockSpec((n_mem, MEM_WIDTH), row)] * 2,
        out_shape=[jax.ShapeDtypeStruct((n_rows, MEM_WIDTH), BF16)] * 2,
        compiler_params=_cparams(("parallel",)),
        name="memkv",
    )(mem2, w_mem_kv.astype(BF16))


_ROUTE_ROWS = 128


def _post_kernel(x_ref, oa_ref, ob_ref, qc_ref, mk_ref, mv_ref, wout_ref, ag_ref, mg_ref,
                 lg_ref, lb_ref, wr_ref, br_ref, x1_ref, ids_ref, gate_ref, *, alpha):
    tm = x_ref.shape[0]
    oa = oa_ref[...]
    oa = oa * lax.rsqrt(jnp.mean(oa * oa, axis=-1, keepdims=True) + RMS_EPS) * ag_ref[...]

    qc = qc_ref[...]
    mk = mk_ref[...]
    mv = mv_ref[...]
    cols = [slice(h * HEAD_DIM, (h + 1) * HEAD_DIM) for h in range(MEM_HEADS)]
    lgs = [_dot_nt(qc[:, c], mk[:, c]) * (HEAD_DIM ** -0.5) for c in cols]
    es = [jnp.exp(lg - jnp.max(lg, axis=-1, keepdims=True)) for lg in lgs]
    ps = [e / jnp.sum(e, axis=-1, keepdims=True) for e in es]
    oc = jnp.concatenate([jnp.dot(p.astype(BF16), mv[:, c], preferred_element_type=F32)
                          for p, c in zip(ps, cols)], axis=1)
    oc = oc * lax.rsqrt(jnp.mean(oc * oc, axis=-1, keepdims=True) + RMS_EPS) * mg_ref[...]

    mix = (jnp.dot(oa.astype(BF16), wout_ref[0:A_WIDTH, :], preferred_element_type=F32)
           + jnp.dot(ob_ref[...].astype(BF16), wout_ref[A_WIDTH:A_WIDTH + GDN_WIDTH, :],
                     preferred_element_type=F32)
           + jnp.dot(oc.astype(BF16), wout_ref[A_WIDTH + GDN_WIDTH:, :], preferred_element_type=F32))
    hres = alpha * x_ref[...] + mix
    mu = jnp.mean(hres, axis=-1, keepdims=True)
    var = jnp.mean(jnp.square(hres - mu), axis=-1, keepdims=True)
    x1 = (hres - mu) * lax.rsqrt(var + LN_EPS) * lg_ref[...] + lb_ref[...]
    x1_ref[...] = x1

    lt = _dot_nt(wr_ref[...], x1) + br_ref[...]
    sub = lax.broadcasted_iota(I32, (EXPERTS_PER_GROUP, tm), 0)
    gl = lt[0:N_GROUPS, :]
    gmax = jnp.max(gl, axis=0, keepdims=True)
    gprob = jnp.exp(gl - gmax) / jnp.sum(jnp.exp(gl - gmax), axis=0, keepdims=True)
    p_grp = jnp.max(gprob, axis=0, keepdims=True)
    grp = jnp.min(jnp.where(gprob == p_grp, sub, N_GROUPS), axis=0, keepdims=True)
    el = jnp.zeros((EXPERTS_PER_GROUP, tm), F32)
    for g in range(N_GROUPS):
        r0 = N_GROUPS + g * EXPERTS_PER_GROUP
        el = el + jnp.where(grp == g, lt[r0:r0 + EXPERTS_PER_GROUP, :], 0.0)
    ee = jnp.exp(el - jnp.max(el, axis=0, keepdims=True))
    pe = ee / jnp.sum(ee, axis=0, keepdims=True)
    p1 = jnp.max(pe, axis=0, keepdims=True)
    i1 = jnp.min(jnp.where(pe == p1, sub, EXPERTS_PER_GROUP), axis=0, keepdims=True)
    rest = jnp.where(sub == i1, -1.0, pe)
    p2 = jnp.max(rest, axis=0, keepdims=True)
    i2 = jnp.min(jnp.where(rest == p2, sub, EXPERTS_PER_GROUP), axis=0, keepdims=True)
    psum = p1 + p2
    g1 = p_grp * p1 / psum
    g2 = p_grp * p2 / psum
    e1 = grp * EXPERTS_PER_GROUP + i1
    e2 = grp * EXPERTS_PER_GROUP + i2
    ids_ref[...] = jnp.where(sub == 0, e1, jnp.where(sub == 1, e2, 0))
    gate_ref[...] = jnp.where(sub == 0, g1, jnp.where(sub == 1, g2, 0.0))


def _post(x2, oa, ob, qc, mk, mv, w_out, attn_g, mem_g, ln_g, ln_b, w_group, b_group,
          w_router, b_router, seq, n_mem, alpha, tm=256):
    n_tok, d_model = x2.shape
    n_t = seq // tm
    row = lambda i: (i, 0)
    const = lambda i: (0, 0)
    per_b = lambda i: (i // n_t, 0, 0)
    lane = lambda i: (0, i)
    bsz = n_tok // seq
    wr = jnp.pad(jnp.concatenate([w_group, w_router], axis=1).T,
                 ((0, _ROUTE_ROWS - N_GROUPS - N_EXPERTS), (0, 0))).astype(BF16)
    br = jnp.pad(jnp.concatenate([b_group, b_router]),
                 (0, _ROUTE_ROWS - N_GROUPS - N_EXPERTS)).reshape(_ROUTE_ROWS, 1).astype(F32)
    vec = lambda v: v.reshape(1, -1).astype(F32)
    return pl.pallas_call(
        functools.partial(_post_kernel, alpha=alpha),
        grid=(n_tok // tm,),
        in_specs=[pl.BlockSpec((tm, d_model), row),
                  pl.BlockSpec((tm, A_WIDTH), row),
                  pl.BlockSpec((tm, GDN_WIDTH), row),
                  pl.BlockSpec((tm, MEM_WIDTH), row),
                  pl.BlockSpec((None, n_mem, MEM_WIDTH), per_b),
                  pl.BlockSpec((None, n_mem, MEM_WIDTH), per_b),
                  pl.BlockSpec(w_out.shape, const),
                  pl.BlockSpec((1, A_WIDTH), const),
                  pl.BlockSpec((1, MEM_WIDTH), const),
                  pl.BlockSpec((1, d_model), const),
                  pl.BlockSpec((1, d_model), const),
                  pl.BlockSpec((_ROUTE_ROWS, d_model), const),
                  pl.BlockSpec((_ROUTE_ROWS, 1), const)],
        out_specs=[pl.BlockSpec((tm, d_model), row),
                   pl.BlockSpec((EXPERTS_PER_GROUP, tm), lane),
                   pl.BlockSpec((EXPERTS_PER_GROUP, tm), lane)],
        out_shape=[jax.ShapeDtypeStruct((n_tok, d_model), F32),
                   jax.ShapeDtypeStruct((EXPERTS_PER_GROUP, n_tok), I32),
                   jax.ShapeDtypeStruct((EXPERTS_PER_GROUP, n_tok), F32)],
        compiler_params=_cparams(("parallel",)),
        name="post",
    )(x2, oa, ob, qc, mk.reshape(bsz, n_mem, MEM_WIDTH), mv.reshape(bsz, n_mem, MEM_WIDTH),
      w_out.astype(BF16), vec(attn_g), vec(mem_g), vec(ln_g), vec(ln_b), wr, br)


def _rank_kernel(ids_ref, rank_ref, cnt_ref, carry_ref):
    i = pl.program_id(0)
    tm = ids_ref.shape[1]

    @pl.when(i == 0)
    def _():
        carry_ref[...] = jnp.zeros(carry_ref.shape, F32)

    ids = ids_ref[...]
    eio = lax.broadcasted_iota(I32, (N_EXPERTS, tm), 0)
    oh0 = jnp.where(eio == ids[0:1, :], 1.0, 0.0)
    oh1 = jnp.where(eio == ids[1:2, :], 1.0, 0.0)
    cnt = oh0 + oh1
    before = (lax.broadcasted_iota(I32, (tm, tm), 0) < lax.broadcasted_iota(I32, (tm, tm), 1))
    prefix = jnp.dot(cnt.astype(BF16), jnp.where(before, 1.0, 0.0).astype(BF16),
                     preferred_element_type=F32) + carry_ref[:, 0:1]
    r0 = jnp.sum(oh0 * prefix, axis=0, keepdims=True)
    r1 = jnp.sum(oh1 * prefix, axis=0, keepdims=True)
    sub = lax.broadcasted_iota(I32, (EXPERTS_PER_GROUP, tm), 0)
    rank_ref[...] = jnp.where(sub == 0, r0, jnp.where(sub == 1, r1, 0.0)).astype(I32)
    carry_ref[...] = carry_ref[...] + jnp.sum(cnt, axis=1, keepdims=True)
    cnt_ref[...] = carry_ref[...].astype(I32)


def _rank(ids, tm=512):
    n_tok = ids.shape[1]
    lane = lambda i: (0, i)
    return pl.pallas_call(
        _rank_kernel,
        grid=(n_tok // tm,),
        in_specs=[pl.BlockSpec((EXPERTS_PER_GROUP, tm), lane)],
        out_specs=[pl.BlockSpec((EXPERTS_PER_GROUP, tm), lane),
                   pl.BlockSpec((N_EXPERTS, LANES), lambda i: (0, 0))],
        out_shape=[jax.ShapeDtypeStruct((EXPERTS_PER_GROUP, n_tok), I32),
                   jax.ShapeDtypeStruct((N_EXPERTS, LANES), I32)],
        scratch_shapes=[pltpu.VMEM((N_EXPERTS, LANES), F32)],
        compiler_params=_cparams(("arbitrary",)),
        name="rank",
    )(ids)


def _row_copy(src_ref, src_row, dst_ref, dst_row, sem):
    return pltpu.make_async_copy(src_ref.at[pl.ds(src_row, 1)], dst_ref.at[pl.ds(dst_row, 1)], sem)


def _dest_kernel(ids_ref, rank_ref, ps_ref, dest_ref):
    ids = ids_ref[...]
    tm = ids.shape[1]
    eio = lax.broadcasted_iota(I32, (N_EXPERTS, tm), 0)
    ps = ps_ref[...]
    d0 = jnp.sum(jnp.where(eio == ids[0:1, :], ps, 0.0), axis=0, keepdims=True)
    d1 = jnp.sum(jnp.where(eio == ids[1:2, :], ps, 0.0), axis=0, keepdims=True)
    sub = lax.broadcasted_iota(I32, (EXPERTS_PER_GROUP, tm), 0)
    dest_ref[...] = rank_ref[...] + jnp.where(sub == 0, d0, jnp.where(sub == 1, d1, 0.0)).astype(I32)


def _dest(ids, rank, pad_start, tm=2048):
    n_tok = ids.shape[1]
    tm = min(tm, n_tok)
    lane = lambda i: (0, i)
    return pl.pallas_call(
        _dest_kernel,
        grid=(n_tok // tm,),
        in_specs=[pl.BlockSpec((EXPERTS_PER_GROUP, tm), lane),
                  pl.BlockSpec((EXPERTS_PER_GROUP, tm), lane),
                  pl.BlockSpec((N_EXPERTS, 1), lambda i: (0, 0))],
        out_specs=pl.BlockSpec((EXPERTS_PER_GROUP, tm), lane),
        out_shape=jax.ShapeDtypeStruct((EXPERTS_PER_GROUP, n_tok), I32),
        compiler_params=_cparams(("parallel",)),
        name="dest",
    )(ids, rank, pad_start.astype(F32).reshape(N_EXPERTS, 1))


def _dispatch_kernel(meta_ref, x_ref, zeros_ref, xbuf_ref, sem, *, td):
    del zeros_ref

    def issue(t, carry):
        for j in range(TOP_K):
            _row_copy(x_ref, t, xbuf_ref, meta_ref[0, j, t], sem).start()
        return carry

    lax.fori_loop(0, td, issue, 0)

    for j in range(TOP_K):
        pltpu.make_async_copy(x_ref, xbuf_ref.at[pl.ds(0, td)], sem).wait()


def _dispatch(meta, x1, cap, td):
    n_tok, d_model = x1.shape
    zeros = jnp.zeros((cap, d_model), F32)
    return pl.pallas_call(
        functools.partial(_dispatch_kernel, td=td),
        grid=(n_tok // td,),
        in_specs=[pl.BlockSpec((1, TOP_K, td), lambda i: (i, 0, 0), memory_space=pltpu.SMEM),
                  pl.BlockSpec((td, d_model), lambda i: (i, 0)),
                  pl.BlockSpec(memory_space=pl.ANY)],
        out_specs=pl.BlockSpec(memory_space=pl.ANY),
        scratch_shapes=[pltpu.SemaphoreType.DMA(())],
        out_shape=jax.ShapeDtypeStruct((cap, d_model), F32),
        input_output_aliases={2: 0},
        compiler_params=_cparams(("arbitrary",)),
        name="dispatch",
    )(meta, x1, zeros)


def _mlp_kernel(be_ref, nused_ref, x_ref, wg_ref, wu_ref, wd_ref, y_ref):
    i = pl.program_id(0)

    @pl.when(i < nused_ref[0])
    def _():
        xb = x_ref[...].astype(BF16)
        hg = jnp.dot(xb, wg_ref[...].astype(BF16), preferred_element_type=F32)
        hu = jnp.dot(xb, wu_ref[...].astype(BF16), preferred_element_type=F32)
        y_ref[...] = jnp.dot((_silu(hg) * hu).astype(BF16), wd_ref[...].astype(BF16),
                             preferred_element_type=F32)

    @pl.when(i >= nused_ref[0])
    def _():
        y_ref[...] = jnp.zeros(y_ref.shape, F32)


def _mlp(blk_expert, n_used, xbuf, w_gate, w_up, w_down):
    cap, d_model = xbuf.shape
    d_exp = w_gate.shape[2]
    blk = lambda i, be, nu: (i, 0)
    wsel = lambda i, be, nu: (be[i], 0, 0)
    return pl.pallas_call(
        _mlp_kernel,
        grid_spec=pltpu.PrefetchScalarGridSpec(
            num_scalar_prefetch=2,
            grid=(cap // MOE_BLOCK,),
            in_specs=[pl.BlockSpec((MOE_BLOCK, d_model), blk),
                      pl.BlockSpec((None, d_model, d_exp), wsel),
                      pl.BlockSpec((None, d_model, d_exp), wsel),
                      pl.BlockSpec((None, d_exp, d_model), wsel)],
            out_specs=pl.BlockSpec((MOE_BLOCK, d_model), blk)),
        out_shape=jax.ShapeDtypeStruct((cap, d_model), F32),
        compiler_params=_cparams(("arbitrary",)),
        name="mlp",
    )(blk_expert, n_used, xbuf, w_gate, w_up, w_down)


SC_CORES = 2
SC_SUBCORES = 16
SC_ROWS = 32


def _sc_gather_rows(table, idx):
    n = idx.shape[0]
    d = table.shape[1]
    n_workers = SC_CORES * SC_SUBCORES
    per_worker = n // n_workers
    assert n % (n_workers * SC_ROWS) == 0
    mesh = plsc.VectorSubcoreMesh(core_axis_name="c", subcore_axis_name="s",
                                  num_cores=SC_CORES, num_subcores=SC_SUBCORES)

    @functools.partial(
        pl.kernel, mesh=mesh, out_type=jax.ShapeDtypeStruct((n, d), table.dtype),
        scratch_types=[pltpu.VMEM((SC_ROWS,), I32), pltpu.VMEM((SC_ROWS, d), table.dtype),
                       pltpu.SemaphoreType.DMA],
        name="sc_gather")
    def gather(table_hbm, idx_hbm, out_hbm, idx_v, rows_v, sem):
        worker = lax.axis_index("s") * SC_CORES + lax.axis_index("c")
        base = worker * per_worker

        @pl.loop(0, per_worker // SC_ROWS)
        def _(i):
            off = base + i * SC_ROWS
            pltpu.sync_copy(idx_hbm.at[pl.ds(off, SC_ROWS)], idx_v)
            pltpu.async_copy(table_hbm.at[idx_v], rows_v, sem).wait()
            pltpu.sync_copy(rows_v, out_hbm.at[pl.ds(off, SC_ROWS)])

    return gather(table, idx)


def _combine_kernel(x1_ref, y0_ref, y1_ref, gt_ref, lg_ref, lb_ref, o_ref, *, alpha):
    gt = gt_ref[...]
    ffn = y0_ref[...] * gt[:, 0:1] + y1_ref[...] * gt[:, 1:2]
    hres = alpha * x1_ref[...] + ffn
    mu = jnp.mean(hres, axis=-1, keepdims=True)
    var = jnp.mean(jnp.square(hres - mu), axis=-1, keepdims=True)
    o_ref[...] = (hres - mu) * lax.rsqrt(var + LN_EPS) * lg_ref[...] + lb_ref[...]


def _combine(dest, x1, gates_t, ln_g, ln_b, ybuf, alpha, tc=512):
    n_tok, d_model = x1.shape
    n_tiles = n_tok // tc
    yrows = _sc_gather_rows(ybuf, dest.reshape(-1))
    row = lambda i: (i, 0)
    const = lambda i: (0, 0)
    vec = lambda v: v.reshape(1, -1).astype(F32)
    return pl.pallas_call(
        functools.partial(_combine_kernel, alpha=alpha),
        grid=(n_tiles,),
        in_specs=[pl.BlockSpec((tc, d_model), row),
                  pl.BlockSpec((tc, d_model), row),
                  pl.BlockSpec((tc, d_model), lambda i: (i + n_tiles, 0)),
                  pl.BlockSpec((tc, EXPERTS_PER_GROUP), row),
                  pl.BlockSpec((1, d_model), const),
                  pl.BlockSpec((1, d_model), const)],
        out_specs=pl.BlockSpec((tc, d_model), row),
        out_shape=jax.ShapeDtypeStruct((n_tok, d_model), F32),
        compiler_params=_cparams(("parallel",)),
        name="combine",
    )(x1, yrows, yrows, gates_t, vec(ln_g), vec(ln_b))


def _moe(x1, ids, gates, w_gate, w_up, w_down, ln_g, ln_b, alpha, tile_in=1024):
    n_tok, d_model = x1.shape
    rank, counts = _rank(ids)
    counts = counts[:, 0]
    padded = (counts + MOE_BLOCK - 1) // MOE_BLOCK * MOE_BLOCK
    pad_ends = jnp.cumsum(padded)
    pad_start = (pad_ends - padded).astype(I32)
    n_asg = n_tok * TOP_K
    cap = (n_asg + MOE_BLOCK - 1) // MOE_BLOCK * MOE_BLOCK + N_EXPERTS * MOE_BLOCK
    n_blk = cap // MOE_BLOCK
    blk_pos = jnp.arange(n_blk, dtype=I32) * MOE_BLOCK
    blk_expert = jnp.minimum(
        jnp.sum((pad_ends[None, :] <= blk_pos[:, None]).astype(I32), axis=1), N_EXPERTS - 1)
    n_used = (pad_ends[-1:] // MOE_BLOCK).astype(I32)
    dest = _dest(ids, rank, pad_start)[0:TOP_K]
    tile_in = min(tile_in, n_tok)

    def tiles(t):
        return dest.reshape(TOP_K, n_tok // t, t).transpose(1, 0, 2)

    xbuf = _dispatch(tiles(tile_in), x1, cap, tile_in)
    ybuf = _mlp(blk_expert, n_used, xbuf, w_gate, w_up, w_down)
    return _combine(dest, x1, gates.T, ln_g, ln_b, ybuf, alpha)


def kernel(x, mem, w_in, kv_norm_g, w_k_up, w_v_up, conv_w, A_log, dt_bias, gdn_norm_g, attn_norm_g, mem_norm_g, w_mem_kv, w_out, ln1_g, ln1_b, w_group, b_group, w_router, b_router, w_gate, w_up, w_down, ln2_g, ln2_b):
    bsz, seq, d_model = x.shape
    n_mem = mem.shape[1]
    depth = w_in.shape[0]
    alpha = (2 * depth) ** 0.25
    inv_freq = 1.0 / (ROPE_THETA ** (jnp.arange(0, HEAD_DIM, 2, dtype=F32) / HEAD_DIM))
    ang = jnp.arange(seq, dtype=F32)[:, None] * inv_freq[None, :]
    cos128 = jnp.tile(jnp.cos(ang), (1, LANES // (HEAD_DIM // 2)))
    sin128 = jnp.tile(jnp.sin(ang), (1, LANES // (HEAD_DIM // 2)))
    x2 = x.reshape(bsz * seq, d_model)
    mem2 = mem.reshape(bsz * n_mem, d_model)
    for l in range(depth):
        qa, iq, ik, ka, va, iw, qkvb, zb, ab, qc = _proj(
            x2, w_in[l], w_k_up[l], w_v_up[l], kv_norm_g[l], cos128, sin128, seq)
        oa = _dsa(qa, iq, iw, ik, ka, va, bsz, seq)
        ob = _gdn(qkvb, ab, zb, conv_w[l], A_log[l], dt_bias[l], gdn_norm_g[l], bsz, seq)
        mk, mv = _memkv(mem2, w_mem_kv[l], n_mem)
        x1, ids, gates = _post(x2, oa, ob, qc, mk, mv, w_out[l], attn_norm_g[l], mem_norm_g[l],
                               ln1_g[l], ln1_b[l], w_group[l], b_group[l], w_router[l],
                               b_router[l], seq, n_mem, alpha)
        x2 = _moe(x1, ids, gates, w_gate[l], w_up[l], w_down[l], ln2_g[l], ln2_b[l], alpha)
    return x2.reshape(bsz, seq, d_model)
```

```python
import functools

import numpy as np
import jax
import jax.numpy as jnp
from jax import lax
from jax.experimental import pallas as pl
from jax.experimental.pallas import tpu as pltpu
from jax.experimental.pallas import tpu_sc as plsc

F32 = jnp.float32
BF16 = jnp.bfloat16
I32 = jnp.int32
I16 = jnp.int16

HEAD_DIM = 64
ATTN_HEADS = 6
A_WIDTH = ATTN_HEADS * HEAD_DIM
KV_RANK = 128
IDX_HEADS = 4
IDX_DIM = 64
INDEX_TOPK = 256
GDN_HEADS = 6
GDN_WIDTH = GDN_HEADS * HEAD_DIM
CONV_WIDTH = 4
CHUNK = 64
MEM_HEADS = 4
MEM_WIDTH = MEM_HEADS * HEAD_DIM
SPLIT_SIZES = (A_WIDTH, KV_RANK, IDX_HEADS * IDX_DIM, IDX_DIM, IDX_HEADS,
               GDN_WIDTH, GDN_WIDTH, GDN_WIDTH, GDN_WIDTH, GDN_HEADS, GDN_HEADS,
               MEM_WIDTH)
ROPE_THETA = 10000.0
N_GROUPS = 8
EXPERTS_PER_GROUP = 8
N_EXPERTS = N_GROUPS * EXPERTS_PER_GROUP
TOP_K = 2
MOE_BLOCK = 256
LN_EPS = 1e-5
RMS_EPS = 1e-6
NEG_INF = -1e30
INT_MIN = -2 ** 31
I16_MIN = -2 ** 15

LANES = 128
SC_CORES = 2
SC_SUBCORES = 16
SC_ROWS = 32
VMEM_LIMIT = 56 * 1024 * 1024


def _cparams(sem):
    return pltpu.CompilerParams(dimension_semantics=sem, vmem_limit_bytes=VMEM_LIMIT)


def _dot(a, b):
    return jnp.dot(a.astype(BF16), b.astype(BF16), preferred_element_type=F32)


def _dot_nt(a, b):
    return lax.dot_general(a.astype(BF16), b.astype(BF16), (((1,), (1,)), ((), ())),
                           preferred_element_type=F32)


def _silu(t):
    return t * (1.0 / (1.0 + jnp.exp(-t)))


_P_QA = (0, 384)
_P_IQ = (384, 640)
_P_CKV = (640, 768)
_P_IK = (768, 896)
_P_IW = (896, 1024)
_P_QKVB = (1024, 2176)
_P_ZB = (2176, 2560)
_P_AB = (2560, 2688)
_P_QC = (2688, 2944)
_P_TOTAL = 2944
_V_ROWS = HEAD_DIM + 16


def _rot_cols(w, n_heads):
    k = w.shape[0]
    w4 = w.reshape(k, n_heads, 2, HEAD_DIM // 2)
    return jnp.concatenate([-w4[:, :, 1:2], w4[:, :, 0:1]], axis=2).reshape(k, n_heads * HEAD_DIM)


def _pad_cols(w, width):
    return jnp.pad(w, ((0, 0), (0, width - w.shape[1])))


def _proj_kernel(x_ref, w_ref, w2_ref, cos_ref, sin_ref, kvg_ref,
                 qa_ref, iq_ref, ik_ref, ka_ref, va_ref, iw_ref, qkvb_ref, zb_ref, ab_ref, qc_ref):
    xb = x_ref[...].astype(BF16)

    def mm(slab):
        return jnp.dot(xb, w_ref[:, slab[0]:slab[1]], preferred_element_type=F32)

    cos = cos_ref[...]
    sin = sin_ref[...]
    cos3 = jnp.concatenate([cos] * 3, axis=1)
    sin3 = jnp.concatenate([sin] * 3, axis=1)
    cos2 = jnp.concatenate([cos] * 2, axis=1)
    sin2 = jnp.concatenate([sin] * 2, axis=1)
    cos64 = cos[:, :HEAD_DIM]
    sin64 = sin[:, :HEAD_DIM]

    def rotate_half(t):
        slabs = []
        for j in range(t.shape[1] // LANES):
            ts = t[:, j * LANES:(j + 1) * LANES]
            first = lax.broadcasted_iota(I32, ts.shape, 1) % HEAD_DIM < HEAD_DIM // 2
            slabs.append(jnp.where(first, -pltpu.roll(ts, LANES - HEAD_DIM // 2, 1),
                                   pltpu.roll(ts, HEAD_DIM // 2, 1)))
        return jnp.concatenate(slabs, axis=1)

    qa = mm(_P_QA)
    qa_ref[...] = (qa * cos3 + rotate_half(qa) * sin3).astype(BF16)
    iq = mm(_P_IQ)
    iq_ref[...] = (iq * cos2 + rotate_half(iq) * sin2).astype(BF16)

    ckv = mm(_P_CKV)
    cn = ckv * lax.rsqrt(jnp.mean(ckv * ckv, axis=-1, keepdims=True) + RMS_EPS) * kvg_ref[...]
    r = jnp.dot(cn.astype(BF16), w2_ref[...], preferred_element_type=F32)
    ka_ref[...] = (r[:, 0:64] * cos64 + r[:, 64:128] * sin64).astype(BF16)
    va_ref[...] = jnp.concatenate(
        [r[:, 128:256].T[0:HEAD_DIM, :], jnp.ones((_V_ROWS - HEAD_DIM, r.shape[0]), F32)],
        axis=0).astype(BF16)

    ikk = mm(_P_IK)
    ik_ref[...] = (ikk[:, 0:64] * cos64 + ikk[:, 64:128] * sin64).astype(BF16)
    iw_ref[...] = mm(_P_IW) * (IDX_HEADS ** -0.5 * IDX_DIM ** -0.5)
    qkvb_ref[...] = mm(_P_QKVB)
    zb_ref[...] = mm(_P_ZB)
    ab_ref[...] = mm(_P_AB)
    qc_ref[...] = mm(_P_QC).astype(BF16)


def _proj(x2, w_in, w_k_up, w_v_up, kv_norm_g, cos128, sin128, seq, tm=256):
    n_tok, d_model = x2.shape
    offs = np.cumsum(SPLIT_SIZES)[:-1].tolist()
    (w_qa, w_ckv, w_iq, w_ik, w_iw, w_qb, w_kb, w_vb, w_zb, w_a, w_b, w_qc) = jnp.split(w_in, offs, axis=1)
    w1 = jnp.concatenate([
        w_qa, w_iq, w_ckv,
        w_ik, _rot_cols(w_ik, 1), _pad_cols(w_iw, LANES),
        w_qb, w_kb, w_vb, w_zb, _pad_cols(jnp.concatenate([w_a, w_b], axis=1), LANES), w_qc,
    ], axis=1).astype(BF16)
    assert w1.shape[1] == _P_TOTAL
    w2 = _pad_cols(jnp.concatenate([w_k_up, _rot_cols(w_k_up, 1), w_v_up], axis=1), 2 * LANES).astype(BF16)
    n_pos = seq // tm
    row = lambda i: (i, 0)
    const = lambda i: (0, 0)
    pos = lambda i: (i % n_pos, 0)
    outs = [(A_WIDTH, BF16), (IDX_HEADS * IDX_DIM, BF16), (IDX_DIM, BF16), (HEAD_DIM, BF16),
            None, (LANES, F32), (3 * GDN_WIDTH, F32), (GDN_WIDTH, F32), (LANES, F32),
            (MEM_WIDTH, BF16)]
    out_specs = [pl.BlockSpec((tm, o[0]), row) if o else pl.BlockSpec((_V_ROWS, tm), lambda i: (0, i))
                 for o in outs]
    out_shape = [jax.ShapeDtypeStruct((n_tok, o[0]), o[1]) if o
                 else jax.ShapeDtypeStruct((_V_ROWS, n_tok), BF16) for o in outs]
    return pl.pallas_call(
        _proj_kernel,
        grid=(n_tok // tm,),
        in_specs=[pl.BlockSpec((tm, d_model), row),
                  pl.BlockSpec(w1.shape, const),
                  pl.BlockSpec(w2.shape, const),
                  pl.BlockSpec((tm, LANES), pos),
                  pl.BlockSpec((tm, LANES), pos),
                  pl.BlockSpec((1, KV_RANK), const)],
        out_specs=out_specs,
        out_shape=out_shape,
        compiler_params=_cparams(("parallel",)),
        name="proj",
    )(x2, w1, w2, cos128, sin128, kv_norm_g.reshape(1, KV_RANK).astype(F32))


def _dsa_kernel(qa_ref, iq_ref, iw_ref, ik_ref, ka_ref, vat_ref, o_ref,
                key_ref, hi_ref, lo_ref, m_ref, acc_ref, *, tq, kc, top):
    qi = pl.program_id(1)
    row0 = qi * tq
    n_kc = (row0 + tq + kc - 1) // kc
    qpos = row0 + lax.broadcasted_iota(I32, (1, tq), 1)

    def key_fold(v):
        return jnp.sum(v.reshape(kc // 8, 8, tq), axis=0)

    def head_rows(x, n_heads, width):
        return jnp.concatenate([x[:, h * width:(h + 1) * width] for h in range(n_heads)], axis=0)

    iq_rows = head_rows(iq_ref[...], IDX_HEADS, IDX_DIM)
    iw_t = iw_ref[...].T

    def score_body(c, carry):
        k0 = pl.multiple_of(c * kc, kc)
        d = _dot_nt(ik_ref[pl.ds(k0, kc), :], iq_rows)
        s = jnp.zeros((kc, tq), F32)
        for h in range(IDX_HEADS):
            s = s + iw_t[h:h + 1, :] * jnp.maximum(d[:, h * tq:(h + 1) * tq], 0.0)
        kidx = k0 + lax.broadcasted_iota(I32, (kc, tq), 0)
        s = jnp.where(s == 0.0, 0.0, s)
        s = jnp.where(kidx <= qpos, s, NEG_INF)
        bits = pltpu.bitcast(s, I32)
        key = jnp.where(bits >= 0, bits, bits ^ 0x7FFFFFFF)
        key_ref[pl.ds(k0, kc), :] = key
        hi_ref[pl.ds(k0, kc), :] = lax.shift_right_arithmetic(key, 16).astype(I16)
        lo_ref[pl.ds(k0, kc), :] = ((key & 0xFFFF) + I16_MIN).astype(I16)
        return carry

    lax.fori_loop(0, n_kc, score_body, 0)

    k_eff = jnp.minimum(top, qpos + 1).astype(F32)

    @pl.when(n_kc % 2 == 1)
    def _():
        k_pad = pl.multiple_of(n_kc * kc, kc)
        hi_ref[pl.ds(k_pad, kc), :] = jnp.full((kc, tq), I16_MIN, I16)
        lo_ref[pl.ds(k_pad, kc), :] = jnp.full((kc, tq), I16_MIN, I16)

    n_pairs = (n_kc + 1) // 2
    one_b = jnp.ones((), BF16)
    zero_b = jnp.zeros((), BF16)

    def count16(ref, pred):
        def body(c, acc):
            k0 = pl.multiple_of(c * (2 * kc), 2 * kc)
            for j in range(2):
                hit = jnp.where(pred(ref[pl.ds(k0 + j * kc, kc), :]), one_b, zero_b)
                hit = hit.reshape(kc // 16, 16, tq)
                parts = [hit[r] for r in range(kc // 16)]
                while len(parts) > 1:
                    parts = [a + b for a, b in zip(parts[0::2], parts[1::2])]
                acc = acc + parts[0]
            return acc
        acc = lax.fori_loop(0, n_pairs, body, jnp.zeros((16, tq), BF16))
        return jnp.sum(acc.astype(F32), axis=0, keepdims=True)

    def search16(ref, start, n_bits, k_want):
        def bit_body(i, t):
            cand = t + lax.shift_left(jnp.int32(1), n_bits - 1 - i)
            c16 = cand.astype(I16)
            return jnp.where(count16(ref, lambda v: v >= c16) >= k_want, cand, t)
        return lax.fori_loop(0, n_bits, bit_body, start)

    hi0 = jnp.where(count16(hi_ref, lambda v: v >= 0) >= k_eff, 0, I16_MIN).astype(I32)
    thr_hi = search16(hi_ref, hi0, 15, k_eff)
    thr_hi16 = thr_hi.astype(I16)
    k_low = k_eff - count16(hi_ref, lambda v: v > thr_hi16)

    def band_body(c, carry):
        k0 = pl.multiple_of(c * kc, kc)
        lo_ref[pl.ds(k0, kc), :] = jnp.where(hi_ref[pl.ds(k0, kc), :] == thr_hi16,
                                             lo_ref[pl.ds(k0, kc), :], I16_MIN)
        return carry

    lax.fori_loop(0, n_kc, band_body, 0)
    thr_lo = search16(lo_ref, jnp.full((1, tq), I16_MIN, I32), 16, k_low)
    thr_lo16 = thr_lo.astype(I16)
    thr = thr_hi * 65536 + (thr_lo - I16_MIN)
    n_tie = k_low - count16(lo_ref, lambda v: v > thr_lo16)

    q_rows = head_rows((qa_ref[...].astype(F32) * (HEAD_DIM ** -0.5)).astype(BF16),
                       ATTN_HEADS, HEAD_DIM)
    m_ref[...] = jnp.full(m_ref.shape, NEG_INF, F32)
    acc_ref[...] = jnp.zeros(acc_ref.shape, F32)
    earlier = jnp.where(lax.broadcasted_iota(I32, (kc, kc), 1) < lax.broadcasted_iota(I32, (kc, kc), 0),
                        1.0, 0.0).astype(BF16)

    def attn_body(c, tie_seen):
        k0 = pl.multiple_of(c * kc, kc)
        kk = key_ref[pl.ds(k0, kc), :]
        kidx = k0 + lax.broadcasted_iota(I32, (kc, tq), 0)
        tie = kk == thr
        tie_f = jnp.where(tie, 1.0, 0.0)
        tie_rank = jnp.dot(earlier, tie_f.astype(BF16), preferred_element_type=F32) + tie_seen
        bias = jnp.where(kk > thr, 0.0,
                         jnp.where(tie, jnp.where(tie_rank < n_tie, 0.0, NEG_INF), NEG_INF))
        bias = jnp.where(kidx <= qpos, bias, NEG_INF)
        lg_all = _dot_nt(ka_ref[pl.ds(k0, kc), :], q_rows)
        lgs = [lg_all[:, h * tq:(h + 1) * tq] + bias for h in range(ATTN_HEADS)]
        m_prev = m_ref[...]
        m_new = jnp.maximum(m_prev, jnp.concatenate(
            [jnp.max(lg, axis=0, keepdims=True) for lg in lgs], axis=1))
        p = jnp.concatenate([jnp.exp(lgs[h] - m_new[0:1, h * tq:(h + 1) * tq])
                             for h in range(ATTN_HEADS)], axis=1)
        alpha = jnp.exp(m_prev - m_new)
        acc_ref[...] = alpha[0:1, :] * acc_ref[...] + jnp.dot(
            vat_ref[:, pl.ds(k0, kc)], p.astype(BF16), preferred_element_type=F32)
        m_ref[...] = m_new
        return tie_seen + jnp.sum(key_fold(tie_f), axis=0, keepdims=True)

    lax.fori_loop(0, n_kc, attn_body, jnp.zeros((1, tq), F32))
    o_t = acc_ref[0:HEAD_DIM, :] / acc_ref[HEAD_DIM:HEAD_DIM + 1, :]
    o_ref[...] = jnp.concatenate(
        [o_t[:, h * tq:(h + 1) * tq].T for h in range(ATTN_HEADS)], axis=1)


def _dsa(qa, iq, iw, ik, ka, va_t, bsz, seq, tq=256):
    tq = min(tq, seq)
    kc = tq
    top = min(INDEX_TOPK, seq // 4)
    n_q = seq // tq
    assert seq // 16 <= 256 and (seq // kc) % 2 == 0
    row = lambda b, i: (b * n_q + i, 0)
    per_b = lambda b, i: (b, 0, 0)
    kern = functools.partial(_dsa_kernel, tq=tq, kc=kc, top=top)
    return pl.pallas_call(
        kern,
        grid=(bsz, n_q),
        in_specs=[pl.BlockSpec((tq, A_WIDTH), row),
                  pl.BlockSpec((tq, IDX_HEADS * IDX_DIM), row),
                  pl.BlockSpec((tq, LANES), row),
                  pl.BlockSpec((None, seq, IDX_DIM), per_b),
                  pl.BlockSpec((None, seq, HEAD_DIM), per_b),
                  pl.BlockSpec((_V_ROWS, seq), lambda b, i: (0, b))],
        out_specs=pl.BlockSpec((tq, A_WIDTH), row),
        out_shape=jax.ShapeDtypeStruct((bsz * seq, A_WIDTH), F32),
        scratch_shapes=[pltpu.VMEM((seq, tq), I32),
                        pltpu.VMEM((seq, tq), I16),
                        pltpu.VMEM((seq, tq), I16),
                        pltpu.VMEM((8, ATTN_HEADS * tq), F32),
                        pltpu.VMEM((_V_ROWS, ATTN_HEADS * tq), F32)],
        compiler_params=_cparams(("parallel", "arbitrary")),
        name="dsa",
    )(qa, iq, iw, ik.reshape(bsz, seq, IDX_DIM), ka.reshape(bsz, seq, HEAD_DIM), va_t)


_HALO = 8


def _bmm(a, b):
    return lax.dot_general(a.astype(BF16), b.astype(BF16), (((2,), (1,)), ((0,), (0,))),
                           preferred_element_type=F32)


def _bmm_nt(a, b):
    return lax.dot_general(a.astype(BF16), b.astype(BF16), (((2,), (2,)), ((0,), (0,))),
                           preferred_element_type=F32)


def _gdn_kernel(qkv_ref, ab_ref, z_ref, cw_ref, alog_ref, dtb_ref, gn_ref, o_ref,
                ext_ref, state_ref, *, tr):
    ti = pl.program_id(1)
    width = 3 * GDN_WIDTH
    n_c = tr // CHUNK

    @pl.when(ti == 0)
    def _():
        ext_ref[0:_HALO, :] = jnp.zeros((_HALO, width), F32)
        state_ref[...] = jnp.zeros(state_ref.shape, F32)

    ext_ref[_HALO:_HALO + tr, :] = qkv_ref[...]
    conv = cw_ref[CONV_WIDTH - 1:CONV_WIDTH, :] * ext_ref[_HALO:_HALO + tr, :]
    for j in range(CONV_WIDTH - 1):
        off = _HALO - (CONV_WIDTH - 1) + j
        conv = conv + cw_ref[j:j + 1, :] * ext_ref[off:off + tr, :]
    ext_ref[0:_HALO, :] = ext_ref[tr:tr + _HALO, :]
    qkv = _silu(conv)

    ab = ab_ref[...]
    sp = jnp.maximum(ab + dtb_ref[...], 0.0) + jnp.log1p(jnp.exp(-jnp.abs(ab + dtb_ref[...])))
    g_all = -jnp.exp(alog_ref[...]) * sp
    beta_all = 1.0 / (1.0 + jnp.exp(-ab))

    lane = lax.broadcasted_iota(I32, (CHUNK, LANES), 1)
    row_i = lax.broadcasted_iota(I32, (CHUNK, LANES), 0)
    left = lane < HEAD_DIM
    incl = (row_i >= lane % HEAD_DIM)[None]
    strict = (row_i > lane % HEAD_DIM)[None]
    left_t = (lax.broadcasted_iota(I32, (tr, LANES), 1) < HEAD_DIM)
    tri = jnp.where(lax.broadcasted_iota(I32, (CHUNK, CHUNK), 0) >= lax.broadcasted_iota(I32, (CHUNK, CHUNK), 1),
                    1.0, 0.0).astype(BF16)

    def blk(x):
        return jnp.concatenate([jnp.where(left[None], x, 0.0), jnp.where(left[None], 0.0, x)], axis=1)

    def head_scale(x, eps, scale):
        sq = x * x
        keep = left_t if x.shape[0] == tr else left
        s_l = jnp.sum(jnp.where(keep, sq, 0.0), axis=-1, keepdims=True)
        s_r = jnp.sum(jnp.where(keep, 0.0, sq), axis=-1, keepdims=True)
        return x * jnp.where(keep, lax.rsqrt(s_l * scale + eps), lax.rsqrt(s_r * scale + eps))

    gc_cols, gc_rows = [], []
    for c in range(n_c):
        g_c = g_all[c * CHUNK:(c + 1) * CHUNK, :]
        g_hi = g_c.astype(BF16)
        g_r1 = g_c - g_hi.astype(F32)
        g_mid = g_r1.astype(BF16)
        g_lo = (g_r1 - g_mid.astype(F32)).astype(BF16)
        gc = (jnp.dot(tri, g_hi, preferred_element_type=F32)
              + jnp.dot(tri, g_mid, preferred_element_type=F32)
              + jnp.dot(tri, g_lo, preferred_element_type=F32))
        gc_cols.append(gc)
        gc_rows.append(gc.T)

    n_p = GDN_HEADS // 2
    q_n = [head_scale(qkv[:, p * LANES:(p + 1) * LANES], RMS_EPS, 1.0) * (HEAD_DIM ** -0.5)
           for p in range(n_p)]
    k_n = [head_scale(qkv[:, GDN_WIDTH + p * LANES:GDN_WIDTH + (p + 1) * LANES], RMS_EPS, 1.0)
           for p in range(n_p)]

    def stack(fn):
        return jnp.stack([fn(c, p) for c in range(n_c) for p in range(n_p)], axis=0)

    def rows(c):
        return slice(c * CHUNK, (c + 1) * CHUNK)

    def pair_cols(x, c, p, base):
        return jnp.where(left, x[rows(c), base + 2 * p:base + 2 * p + 1],
                         x[rows(c), base + 2 * p + 1:base + 2 * p + 2])

    qs = stack(lambda c, p: q_n[p][rows(c), :])
    ks = stack(lambda c, p: k_n[p][rows(c), :])
    vs = stack(lambda c, p: qkv[rows(c), 2 * GDN_WIDTH + p * LANES:2 * GDN_WIDTH + (p + 1) * LANES])
    beta = stack(lambda c, p: pair_cols(beta_all, c, p, GDN_HEADS))
    gcc = stack(lambda c, p: pair_cols(gc_cols[c], 0, p, 0))
    gcr = stack(lambda c, p: jnp.concatenate(
        [gc_rows[c][2 * p:2 * p + 1, :], gc_rows[c][2 * p + 1:2 * p + 2, :]], axis=1))
    decay = jnp.where(incl, jnp.exp(jnp.where(incl, gcc - gcr, 0.0)), 0.0)
    kb = ks * beta
    k_blk = blk(ks)
    low = jnp.where(strict, _bmm_nt(kb, k_blk) * decay, 0.0)
    e_gc = jnp.exp(gcc)
    u = vs * beta
    w = kb * e_gc
    u = u - _bmm(low, blk(u))
    w = w - _bmm(low, blk(w))
    pw = low
    for _ in range(5):
        pw = _bmm(pw, blk(pw))
        u = u + _bmm(pw, blk(u))
        w = w + _bmm(pw, blk(w))
    a_intra = jnp.where(incl, _bmm_nt(qs, k_blk) * decay, 0.0)
    q_dec = qs * e_gc
    gc_last = gcc[:, CHUNK - 1:CHUNK, :]
    kd_blk = blk(ks * jnp.exp(gc_last - gcc))
    kd_blk_t = jnp.stack([kd_blk[i].T for i in range(n_c * n_p)], axis=0)
    g_last = jnp.exp(gc_last)

    gn = gn_ref[...]
    st = state_ref[...]
    for c in range(n_c):
        sl = slice(c * n_p, (c + 1) * n_p)
        v_new = blk(u[sl] - _bmm(w[sl], st))
        o = _bmm(q_dec[sl], st) + _bmm(a_intra[sl], v_new)
        st = st * g_last[sl] + _bmm(kd_blk_t[sl], v_new)
        for p in range(n_p):
            y = head_scale(o[p], RMS_EPS, 1.0 / HEAD_DIM) * gn
            o_ref[rows(c), p * LANES:(p + 1) * LANES] = y * _silu(z_ref[rows(c), p * LANES:(p + 1) * LANES])
    state_ref[...] = st


def _gdn(qkvb, ab, zb, conv_w, a_log, dt_bias, gdn_norm_g, bsz, seq, tr=256):
    tr = min(tr, seq)
    n_t = seq // tr
    row = lambda b, i: (b * n_t + i, 0)
    const = lambda b, i: (0, 0)
    alog = _pad_cols(a_log.reshape(1, GDN_HEADS).astype(F32), LANES)
    dtb = _pad_cols(dt_bias.reshape(1, GDN_HEADS).astype(F32), LANES)
    return pl.pallas_call(
        functools.partial(_gdn_kernel, tr=tr),
        grid=(bsz, n_t),
        in_specs=[pl.BlockSpec((tr, 3 * GDN_WIDTH), row),
                  pl.BlockSpec((tr, LANES), row),
                  pl.BlockSpec((tr, GDN_WIDTH), row),
                  pl.BlockSpec((CONV_WIDTH, 3 * GDN_WIDTH), const),
                  pl.BlockSpec((1, LANES), const),
                  pl.BlockSpec((1, LANES), const),
                  pl.BlockSpec((1, LANES), const)],
        out_specs=pl.BlockSpec((tr, GDN_WIDTH), row),
        out_shape=jax.ShapeDtypeStruct((bsz * seq, GDN_WIDTH), F32),
        scratch_shapes=[pltpu.VMEM((tr + _HALO, 3 * GDN_WIDTH), F32),
                        pltpu.VMEM((GDN_HEADS // 2, LANES, LANES), F32)],
        compiler_params=_cparams(("parallel", "arbitrary")),
        name="gdn",
    )(qkvb, ab, zb, conv_w.astype(F32), alog, dtb,
      jnp.tile(gdn_norm_g.reshape(1, HEAD_DIM).astype(F32), (1, LANES // HEAD_DIM)))


def _memkv_kernel(m_ref, w_ref, k_ref, v_ref):
    r = jnp.dot(m_ref[...].astype(BF16), w_ref[...], preferred_element_type=F32)
    k_ref[...] = r[:, :MEM_WIDTH].astype(BF16)
    v_ref[...] = r[:, MEM_WIDTH:].astype(BF16)


def _memkv(mem2, w_mem_kv, n_mem):
    n_rows, d_model = mem2.shape
    row = lambda i: (i, 0)
    return pl.pallas_call(
        _memkv_kernel,
        grid=(n_rows // n_mem,),
        in_specs=[pl.BlockSpec((n_mem, d_model), row),
                  pl.BlockSpec((d_model, 2 * MEM_WIDTH), lambda i: (0, 0))],
        out_specs=[pl.BlockSpec((n_mem, MEM_WIDTH), row)] * 2,
        out_shape=[jax.ShapeDtypeStruct((n_rows, MEM_WIDTH), BF16)] * 2,
        compiler_params=_cparams(("parallel",)),
        name="memkv",
    )(mem2, w_mem_kv.astype(BF16))


_ROUTE_ROWS = 128


def _post_kernel(x_ref, oa_ref, ob_ref, qc_ref, mk_ref, mv_ref, wout_ref, ag_ref, mg_ref,
                 lg_ref, lb_ref, wr_ref, br_ref, x1_ref, ids_ref, gate_ref, *, alpha):
    tm = x_ref.shape[0]
    oa = oa_ref[...]
    oa = oa * lax.rsqrt(jnp.mean(oa * oa, axis=-1, keepdims=True) + RMS_EPS) * ag_ref[...]

    qc = qc_ref[...]
    mk = mk_ref[...]
    mv = mv_ref[...]
    cols = [slice(h * HEAD_DIM, (h + 1) * HEAD_DIM) for h in range(MEM_HEADS)]
    lgs = [_dot_nt(qc[:, c], mk[:, c]) * (HEAD_DIM ** -0.5) for c in cols]
    es = [jnp.exp(lg - jnp.max(lg, axis=-1, keepdims=True)) for lg in lgs]
    ps = [e / jnp.sum(e, axis=-1, keepdims=True) for e in es]
    oc = jnp.concatenate([jnp.dot(p.astype(BF16), mv[:, c], preferred_element_type=F32)
                          for p, c in zip(ps, cols)], axis=1)
    oc = oc * lax.rsqrt(jnp.mean(oc * oc, axis=-1, keepdims=True) + RMS_EPS) * mg_ref[...]

    mix = (jnp.dot(oa.astype(BF16), wout_ref[0:A_WIDTH, :], preferred_element_type=F32)
           + jnp.dot(ob_ref[...].astype(BF16), wout_ref[A_WIDTH:A_WIDTH + GDN_WIDTH, :],
                     preferred_element_type=F32)
           + jnp.dot(oc.astype(BF16), wout_ref[A_WIDTH + GDN_WIDTH:, :], preferred_element_type=F32))
    hres = alpha * x_ref[...] + mix
    mu = jnp.mean(hres, axis=-1, keepdims=True)
    var = jnp.mean(jnp.square(hres - mu), axis=-1, keepdims=True)
    x1 = (hres - mu) * lax.rsqrt(var + LN_EPS) * lg_ref[...] + lb_ref[...]
    x1_ref[...] = x1

    lt = _dot_nt(wr_ref[...], x1) + br_ref[...]
    sub = lax.broadcasted_iota(I32, (EXPERTS_PER_GROUP, tm), 0)
    gl = lt[0:N_GROUPS, :]
    gmax = jnp.max(gl, axis=0, keepdims=True)
    gprob = jnp.exp(gl - gmax) / jnp.sum(jnp.exp(gl - gmax), axis=0, keepdims=True)
    p_grp = jnp.max(gprob, axis=0, keepdims=True)
    grp = jnp.min(jnp.where(gprob == p_grp, sub, N_GROUPS), axis=0, keepdims=True)
    el = jnp.zeros((EXPERTS_PER_GROUP, tm), F32)
    for g in range(N_GROUPS):
        r0 = N_GROUPS + g * EXPERTS_PER_GROUP
        el = el + jnp.where(grp == g, lt[r0:r0 + EXPERTS_PER_GROUP, :], 0.0)
    ee = jnp.exp(el - jnp.max(el, axis=0, keepdims=True))
    pe = ee / jnp.sum(ee, axis=0, keepdims=True)
    p1 = jnp.max(pe, axis=0, keepdims=True)
    i1 = jnp.min(jnp.where(pe == p1, sub, EXPERTS_PER_GROUP), axis=0, keepdims=True)
    rest = jnp.where(sub == i1, -1.0, pe)
    p2 = jnp.max(rest, axis=0, keepdims=True)
    i2 = jnp.min(jnp.where(rest == p2, sub, EXPERTS_PER_GROUP), axis=0, keepdims=True)
    psum = p1 + p2
    g1 = p_grp * p1 / psum
    g2 = p_grp * p2 / psum
    e1 = grp * EXPERTS_PER_GROUP + i1
    e2 = grp * EXPERTS_PER_GROUP + i2
    ids_ref[...] = jnp.where(sub == 0, e1, jnp.where(sub == 1, e2, 0))
    gate_ref[...] = jnp.where(sub == 0, g1, jnp.where(sub == 1, g2, 0.0))


def _post(x2, oa, ob, qc, mk, mv, w_out, attn_g, mem_g, ln_g, ln_b, w_group, b_group,
          w_router, b_router, seq, n_mem, alpha, tm=256):
    n_tok, d_model = x2.shape
    n_t = seq // tm
    row = lambda i: (i, 0)
    const = lambda i: (0, 0)
    per_b = lambda i: (i // n_t, 0, 0)
    lane = lambda i: (0, i)
    bsz = n_tok // seq
    wr = jnp.pad(jnp.concatenate([w_group, w_router], axis=1).T,
                 ((0, _ROUTE_ROWS - N_GROUPS - N_EXPERTS), (0, 0))).astype(BF16)
    br = jnp.pad(jnp.concatenate([b_group, b_router]),
                 (0, _ROUTE_ROWS - N_GROUPS - N_EXPERTS)).reshape(_ROUTE_ROWS, 1).astype(F32)
    vec = lambda v: v.reshape(1, -1).astype(F32)
    return pl.pallas_call(
        functools.partial(_post_kernel, alpha=alpha),
        grid=(n_tok // tm,),
        in_specs=[pl.BlockSpec((tm, d_model), row),
                  pl.BlockSpec((tm, A_WIDTH), row),
                  pl.BlockSpec((tm, GDN_WIDTH), row),
                  pl.BlockSpec((tm, MEM_WIDTH), row),
                  pl.BlockSpec((None, n_mem, MEM_WIDTH), per_b),
                  pl.BlockSpec((None, n_mem, MEM_WIDTH), per_b),
                  pl.BlockSpec(w_out.shape, const),
                  pl.BlockSpec((1, A_WIDTH), const),
                  pl.BlockSpec((1, MEM_WIDTH), const),
                  pl.BlockSpec((1, d_model), const),
                  pl.BlockSpec((1, d_model), const),
                  pl.BlockSpec((_ROUTE_ROWS, d_model), const),
                  pl.BlockSpec((_ROUTE_ROWS, 1), const)],
        out_specs=[pl.BlockSpec((tm, d_model), row),
                   pl.BlockSpec((EXPERTS_PER_GROUP, tm), lane),
                   pl.BlockSpec((EXPERTS_PER_GROUP, tm), lane)],
        out_shape=[jax.ShapeDtypeStruct((n_tok, d_model), F32),
                   jax.ShapeDtypeStruct((EXPERTS_PER_GROUP, n_tok), I32),
                   jax.ShapeDtypeStruct((EXPERTS_PER_GROUP, n_tok), F32)],
        compiler_params=_cparams(("parallel",)),
        name="post",
    )(x2, oa, ob, qc, mk.reshape(bsz, n_mem, MEM_WIDTH), mv.reshape(bsz, n_mem, MEM_WIDTH),
      w_out.astype(BF16), vec(attn_g), vec(mem_g), vec(ln_g), vec(ln_b), wr, br)


def _rank_kernel(ids_ref, rank_ref, cnt_ref, carry_ref):
    i = pl.program_id(0)
    tm = ids_ref.shape[1]

    @pl.when(i == 0)
    def _():
        carry_ref[...] = jnp.zeros(carry_ref.shape, F32)

    ids = ids_ref[...]
    eio = lax.broadcasted_iota(I32, (N_EXPERTS, tm), 0)
    oh0 = jnp.where(eio == ids[0:1, :], 1.0, 0.0)
    oh1 = jnp.where(eio == ids[1:2, :], 1.0, 0.0)
    cnt = oh0 + oh1
    before = (lax.broadcasted_iota(I32, (tm, tm), 0) < lax.broadcasted_iota(I32, (tm, tm), 1))
    prefix = jnp.dot(cnt.astype(BF16), jnp.where(before, 1.0, 0.0).astype(BF16),
                     preferred_element_type=F32) + carry_ref[:, 0:1]
    r0 = jnp.sum(oh0 * prefix, axis=0, keepdims=True)
    r1 = jnp.sum(oh1 * prefix, axis=0, keepdims=True)
    sub = lax.broadcasted_iota(I32, (EXPERTS_PER_GROUP, tm), 0)
    rank_ref[...] = jnp.where(sub == 0, r0, jnp.where(sub == 1, r1, 0.0)).astype(I32)
    carry_ref[...] = carry_ref[...] + jnp.sum(cnt, axis=1, keepdims=True)
    cnt_ref[...] = carry_ref[...].astype(I32)


def _rank(ids, tm=512):
    n_tok = ids.shape[1]
    lane = lambda i: (0, i)
    return pl.pallas_call(
        _rank_kernel,
        grid=(n_tok // tm,),
        in_specs=[pl.BlockSpec((EXPERTS_PER_GROUP, tm), lane)],
        out_specs=[pl.BlockSpec((EXPERTS_PER_GROUP, tm), lane),
                   pl.BlockSpec((N_EXPERTS, LANES), lambda i: (0, 0))],
        out_shape=[jax.ShapeDtypeStruct((EXPERTS_PER_GROUP, n_tok), I32),
                   jax.ShapeDtypeStruct((N_EXPERTS, LANES), I32)],
        scratch_shapes=[pltpu.VMEM((N_EXPERTS, LANES), F32)],
        compiler_params=_cparams(("arbitrary",)),
        name="rank",
    )(ids)


def _dest_kernel(ids_ref, rank_ref, ps_ref, dest_ref):
    ids = ids_ref[...]
    tm = ids.shape[1]
    eio = lax.broadcasted_iota(I32, (N_EXPERTS, tm), 0)
    ps = ps_ref[...]
    d0 = jnp.sum(jnp.where(eio == ids[0:1, :], ps, 0.0), axis=0, keepdims=True)
    d1 = jnp.sum(jnp.where(eio == ids[1:2, :], ps, 0.0), axis=0, keepdims=True)
    sub = lax.broadcasted_iota(I32, (EXPERTS_PER_GROUP, tm), 0)
    dest_ref[...] = rank_ref[...] + jnp.where(sub == 0, d0, jnp.where(sub == 1, d1, 0.0)).astype(I32)


def _dest(ids, rank, pad_start, tm=2048):
    n_tok = ids.shape[1]
    tm = min(tm, n_tok)
    lane = lambda i: (0, i)
    return pl.pallas_call(
        _dest_kernel,
        grid=(n_tok // tm,),
        in_specs=[pl.BlockSpec((EXPERTS_PER_GROUP, tm), lane),
                  pl.BlockSpec((EXPERTS_PER_GROUP, tm), lane),
                  pl.BlockSpec((N_EXPERTS, 1), lambda i: (0, 0))],
        out_specs=pl.BlockSpec((EXPERTS_PER_GROUP, tm), lane),
        out_shape=jax.ShapeDtypeStruct((EXPERTS_PER_GROUP, n_tok), I32),
        compiler_params=_cparams(("parallel",)),
        name="dest",
    )(ids, rank, pad_start.astype(F32).reshape(N_EXPERTS, 1))


def _sc_mesh():
    return plsc.VectorSubcoreMesh(core_axis_name="c", subcore_axis_name="s",
                                  num_cores=SC_CORES, num_subcores=SC_SUBCORES)


def _sc_scatter_rows(x1, dest, pad_rows, cap):
    n_tok, d = x1.shape
    n_pad = pad_rows.shape[0]
    n_workers = SC_CORES * SC_SUBCORES
    tok_per_worker = n_tok // n_workers
    pad_per_worker = n_pad // n_workers
    assert n_tok % (n_workers * SC_ROWS) == 0 and n_pad % (n_workers * SC_ROWS) == 0
    zero_rows = jnp.zeros((SC_ROWS, d), x1.dtype)

    @functools.partial(
        pl.kernel, mesh=_sc_mesh(), out_type=jax.ShapeDtypeStruct((cap, d), x1.dtype),
        scratch_types=[pltpu.VMEM((SC_ROWS,), I32), pltpu.VMEM((SC_ROWS, d), x1.dtype),
                       pltpu.SemaphoreType.DMA],
        name="sc_scatter")
    def scatter(x_hbm, dest_hbm, pad_hbm, zero_hbm, out_hbm, idx_v, rows_v, sem):
        worker = lax.axis_index("s") * SC_CORES + lax.axis_index("c")

        @pl.loop(0, tok_per_worker // SC_ROWS)
        def _(i):
            off = worker * tok_per_worker + i * SC_ROWS
            pltpu.sync_copy(x_hbm.at[pl.ds(off, SC_ROWS)], rows_v)
            for j in range(TOP_K):
                pltpu.sync_copy(dest_hbm.at[pl.ds(j * n_tok + off, SC_ROWS)], idx_v)
                pltpu.async_copy(rows_v, out_hbm.at[idx_v], sem).wait()

        pltpu.sync_copy(zero_hbm, rows_v)

        @pl.loop(0, pad_per_worker // SC_ROWS)
        def _(i):
            off = worker * pad_per_worker + i * SC_ROWS
            pltpu.sync_copy(pad_hbm.at[pl.ds(off, SC_ROWS)], idx_v)
            pltpu.async_copy(rows_v, out_hbm.at[idx_v], sem).wait()

    return scatter(x1, dest, pad_rows, zero_rows)


def _mlp_kernel(be_ref, nused_ref, x_ref, wg_ref, wu_ref, wd_ref, y_ref):
    i = pl.program_id(0)

    @pl.when(i < nused_ref[0])
    def _():
        xb = x_ref[...].astype(BF16)
        hg = jnp.dot(xb, wg_ref[...].astype(BF16), preferred_element_type=F32)
        hu = jnp.dot(xb, wu_ref[...].astype(BF16), preferred_element_type=F32)
        y_ref[...] = jnp.dot((_silu(hg) * hu).astype(BF16), wd_ref[...].astype(BF16),
                             preferred_element_type=F32)

    @pl.when(i >= nused_ref[0])
    def _():
        y_ref[...] = jnp.zeros(y_ref.shape, F32)


def _mlp(blk_expert, n_used, xbuf, w_gate, w_up, w_down):
    cap, d_model = xbuf.shape
    d_exp = w_gate.shape[2]
    blk = lambda i, be, nu: (i, 0)
    used_blk = lambda i, be, nu: (jnp.minimum(i, nu[0] - 1), 0)
    wsel = lambda i, be, nu: (be[i], 0, 0)
    return pl.pallas_call(
        _mlp_kernel,
        grid_spec=pltpu.PrefetchScalarGridSpec(
            num_scalar_prefetch=2,
            grid=(cap // MOE_BLOCK,),
            in_specs=[pl.BlockSpec((MOE_BLOCK, d_model), used_blk),
                      pl.BlockSpec((None, d_model, d_exp), wsel),
                      pl.BlockSpec((None, d_model, d_exp), wsel),
                      pl.BlockSpec((None, d_exp, d_model), wsel)],
            out_specs=pl.BlockSpec((MOE_BLOCK, d_model), blk)),
        out_shape=jax.ShapeDtypeStruct((cap, d_model), F32),
        compiler_params=_cparams(("arbitrary",)),
        name="mlp",
    )(blk_expert, n_used, xbuf, w_gate, w_up, w_down)


def _sc_gather_rows(table, idx):
    n = idx.shape[0]
    d = table.shape[1]
    n_workers = SC_CORES * SC_SUBCORES
    per_worker = n // n_workers
    assert n % (n_workers * SC_ROWS) == 0

    @functools.partial(
        pl.kernel, mesh=_sc_mesh(), out_type=jax.ShapeDtypeStruct((n, d), table.dtype),
        scratch_types=[pltpu.VMEM((SC_ROWS,), I32), pltpu.VMEM((SC_ROWS, d), table.dtype),
                       pltpu.SemaphoreType.DMA],
        name="sc_gather")
    def gather(table_hbm, idx_hbm, out_hbm, idx_v, rows_v, sem):
        worker = lax.axis_index("s") * SC_CORES + lax.axis_index("c")
        base = worker * per_worker

        @pl.loop(0, per_worker // SC_ROWS)
        def _(i):
            off = base + i * SC_ROWS
            pltpu.sync_copy(idx_hbm.at[pl.ds(off, SC_ROWS)], idx_v)
            pltpu.async_copy(table_hbm.at[idx_v], rows_v, sem).wait()
            pltpu.sync_copy(rows_v, out_hbm.at[pl.ds(off, SC_ROWS)])

    return gather(table, idx)


def _combine_kernel(x1_ref, y0_ref, y1_ref, gt_ref, lg_ref, lb_ref, o_ref, *, alpha):
    gt = gt_ref[...]
    ffn = y0_ref[...] * gt[:, 0:1] + y1_ref[...] * gt[:, 1:2]
    hres = alpha * x1_ref[...] + ffn
    mu = jnp.mean(hres, axis=-1, keepdims=True)
    var = jnp.mean(jnp.square(hres - mu), axis=-1, keepdims=True)
    o_ref[...] = (hres - mu) * lax.rsqrt(var + LN_EPS) * lg_ref[...] + lb_ref[...]


def _combine(dest, x1, gates_t, ln_g, ln_b, ybuf, alpha, tc=512):
    n_tok, d_model = x1.shape
    n_tiles = n_tok // tc
    yrows = _sc_gather_rows(ybuf, dest.reshape(-1))
    row = lambda i: (i, 0)
    const = lambda i: (0, 0)
    vec = lambda v: v.reshape(1, -1).astype(F32)
    return pl.pallas_call(
        functools.partial(_combine_kernel, alpha=alpha),
        grid=(n_tiles,),
        in_specs=[pl.BlockSpec((tc, d_model), row),
                  pl.BlockSpec((tc, d_model), row),
                  pl.BlockSpec((tc, d_model), lambda i: (i + n_tiles, 0)),
                  pl.BlockSpec((tc, EXPERTS_PER_GROUP), row),
                  pl.BlockSpec((1, d_model), const),
                  pl.BlockSpec((1, d_model), const)],
        out_specs=pl.BlockSpec((tc, d_model), row),
        out_shape=jax.ShapeDtypeStruct((n_tok, d_model), F32),
        compiler_params=_cparams(("parallel",)),
        name="combine",
    )(x1, yrows, yrows, gates_t, vec(ln_g), vec(ln_b))


def _moe(x1, ids, gates, w_gate, w_up, w_down, ln_g, ln_b, alpha):
    n_tok, d_model = x1.shape
    rank, counts = _rank(ids)
    counts = counts[:, 0]
    padded = (counts + MOE_BLOCK - 1) // MOE_BLOCK * MOE_BLOCK
    pad_ends = jnp.cumsum(padded)
    pad_start = (pad_ends - padded).astype(I32)
    n_asg = n_tok * TOP_K
    cap = (n_asg + MOE_BLOCK - 1) // MOE_BLOCK * MOE_BLOCK + N_EXPERTS * MOE_BLOCK
    n_blk = cap // MOE_BLOCK
    blk_pos = jnp.arange(n_blk, dtype=I32) * MOE_BLOCK
    blk_expert = jnp.minimum(
        jnp.sum((pad_ends[None, :] <= blk_pos[:, None]).astype(I32), axis=1), N_EXPERTS - 1)
    n_used = (pad_ends[-1:] // MOE_BLOCK).astype(I32)
    dest = _dest(ids, rank, pad_start)[0:TOP_K]
    slot = jnp.arange(MOE_BLOCK, dtype=I32)[None, :]
    spare = cap - 1 - (jnp.arange(N_EXPERTS * MOE_BLOCK, dtype=I32).reshape(N_EXPERTS, MOE_BLOCK)
                       % N_EXPERTS)
    pad_rows = jnp.where(slot < (padded - counts)[:, None], (pad_start + counts)[:, None] + slot,
                         spare).reshape(-1)
    xbuf = _sc_scatter_rows(x1, dest.reshape(-1), pad_rows, cap)
    ybuf = _mlp(blk_expert, n_used, xbuf, w_gate, w_up, w_down)
    return _combine(dest, x1, gates.T, ln_g, ln_b, ybuf, alpha)


def kernel(x, mem, w_in, kv_norm_g, w_k_up, w_v_up, conv_w, A_log, dt_bias, gdn_norm_g, attn_norm_g, mem_norm_g, w_mem_kv, w_out, ln1_g, ln1_b, w_group, b_group, w_router, b_router, w_gate, w_up, w_down, ln2_g, ln2_b):
    bsz, seq, d_model = x.shape
    n_mem = mem.shape[1]
    depth = w_in.shape[0]
    alpha = (2 * depth) ** 0.25
    inv_freq = 1.0 / (ROPE_THETA ** (jnp.arange(0, HEAD_DIM, 2, dtype=F32) / HEAD_DIM))
    ang = jnp.arange(seq, dtype=F32)[:, None] * inv_freq[None, :]
    cos128 = jnp.tile(jnp.cos(ang), (1, LANES // (HEAD_DIM // 2)))
    sin128 = jnp.tile(jnp.sin(ang), (1, LANES // (HEAD_DIM // 2)))
    x2 = x.reshape(bsz * seq, d_model)
    mem2 = mem.reshape(bsz * n_mem, d_model)
    for l in range(depth):
        qa, iq, ik, ka, va, iw, qkvb, zb, ab, qc = _proj(
            x2, w_in[l], w_k_up[l], w_v_up[l], kv_norm_g[l], cos128, sin128, seq)
        oa = _dsa(qa, iq, iw, ik, ka, va, bsz, seq)
        ob = _gdn(qkvb, ab, zb, conv_w[l], A_log[l], dt_bias[l], gdn_norm_g[l], bsz, seq)
        mk, mv = _memkv(mem2, w_mem_kv[l], n_mem)
        x1, ids, gates = _post(x2, oa, ob, qc, mk, mv, w_out[l], attn_norm_g[l], mem_norm_g[l],
                               ln1_g[l], ln1_b[l], w_group[l], b_group[l], w_router[l],
                               b_router[l], seq, n_mem, alpha)
        x2 = _moe(x1, ids, gates, w_gate[l], w_up[l], w_down[l], ln2_g[l], ln2_b[l], alpha)
    return x2.reshape(bsz, seq, d_model)
```

```python
import functools

import numpy as np
import jax
import jax.numpy as jnp
from jax import lax
from jax.experimental import pallas as pl
from jax.experimental.pallas import tpu as pltpu
from jax.experimental.pallas import tpu_sc as plsc

F32 = jnp.float32
BF16 = jnp.bfloat16
I32 = jnp.int32
I16 = jnp.int16

HEAD_DIM = 64
ATTN_HEADS = 6
A_WIDTH = ATTN_HEADS * HEAD_DIM
KV_RANK = 128
IDX_HEADS = 4
IDX_DIM = 64
INDEX_TOPK = 256
GDN_HEADS = 6
GDN_WIDTH = GDN_HEADS * HEAD_DIM
CONV_WIDTH = 4
CHUNK = 64
MEM_HEADS = 4
MEM_WIDTH = MEM_HEADS * HEAD_DIM
SPLIT_SIZES = (A_WIDTH, KV_RANK, IDX_HEADS * IDX_DIM, IDX_DIM, IDX_HEADS,
               GDN_WIDTH, GDN_WIDTH, GDN_WIDTH, GDN_WIDTH, GDN_HEADS, GDN_HEADS,
               MEM_WIDTH)
ROPE_THETA = 10000.0
N_GROUPS = 8
EXPERTS_PER_GROUP = 8
N_EXPERTS = N_GROUPS * EXPERTS_PER_GROUP
TOP_K = 2
MOE_BLOCK = 256
LN_EPS = 1e-5
RMS_EPS = 1e-6
NEG_INF = -1e30
INT_MIN = -2 ** 31
I16_MIN = -2 ** 15

LANES = 128
SC_CORES = 2
SC_SUBCORES = 16
SC_ROWS = 32
VMEM_LIMIT = 56 * 1024 * 1024


def _cparams(sem):
    return pltpu.CompilerParams(dimension_semantics=sem, vmem_limit_bytes=VMEM_LIMIT)


def _dot(a, b):
    return jnp.dot(a.astype(BF16), b.astype(BF16), preferred_element_type=F32)


def _dot_nt(a, b):
    return lax.dot_general(a.astype(BF16), b.astype(BF16), (((1,), (1,)), ((), ())),
                           preferred_element_type=F32)


def _silu(t):
    return t * (1.0 / (1.0 + jnp.exp(-t)))


_P_QA = (0, 384)
_P_IQ = (384, 640)
_P_CKV = (640, 768)
_P_IK = (768, 896)
_P_IW = (896, 1024)
_P_QKVB = (1024, 2176)
_P_ZB = (2176, 2560)
_P_AB = (2560, 2688)
_P_QC = (2688, 2944)
_P_TOTAL = 2944
_V_ROWS = HEAD_DIM + 16


def _rot_cols(w, n_heads):
    k = w.shape[0]
    w4 = w.reshape(k, n_heads, 2, HEAD_DIM // 2)
    return jnp.concatenate([-w4[:, :, 1:2], w4[:, :, 0:1]], axis=2).reshape(k, n_heads * HEAD_DIM)


def _pad_cols(w, width):
    return jnp.pad(w, ((0, 0), (0, width - w.shape[1])))


def _proj_kernel(x_ref, w_ref, w2_ref, cos_ref, sin_ref, kvg_ref,
                 qa_ref, iq_ref, ik_ref, ka_ref, va_ref, iw_ref, qkvb_ref, zb_ref, ab_ref, qc_ref):
    xb = x_ref[...].astype(BF16)

    def mm(slab):
        return jnp.dot(xb, w_ref[:, slab[0]:slab[1]], preferred_element_type=F32)

    cos = cos_ref[...]
    sin = sin_ref[...]
    cos3 = jnp.concatenate([cos] * 3, axis=1)
    sin3 = jnp.concatenate([sin] * 3, axis=1)
    cos2 = jnp.concatenate([cos] * 2, axis=1)
    sin2 = jnp.concatenate([sin] * 2, axis=1)
    cos64 = cos[:, :HEAD_DIM]
    sin64 = sin[:, :HEAD_DIM]

    def rotate_half(t):
        slabs = []
        for j in range(t.shape[1] // LANES):
            ts = t[:, j * LANES:(j + 1) * LANES]
            first = lax.broadcasted_iota(I32, ts.shape, 1) % HEAD_DIM < HEAD_DIM // 2
            slabs.append(jnp.where(first, -pltpu.roll(ts, LANES - HEAD_DIM // 2, 1),
                                   pltpu.roll(ts, HEAD_DIM // 2, 1)))
        return jnp.concatenate(slabs, axis=1)

    qa = mm(_P_QA)
    qa_ref[...] = (qa * cos3 + rotate_half(qa) * sin3).astype(BF16)
    iq = mm(_P_IQ)
    iq_ref[...] = (iq * cos2 + rotate_half(iq) * sin2).astype(BF16)

    ckv = mm(_P_CKV)
    cn = ckv * lax.rsqrt(jnp.mean(ckv * ckv, axis=-1, keepdims=True) + RMS_EPS) * kvg_ref[...]
    r = jnp.dot(cn.astype(BF16), w2_ref[...], preferred_element_type=F32)
    ka_ref[...] = (r[:, 0:64] * cos64 + r[:, 64:128] * sin64).astype(BF16)
    va_ref[...] = jnp.concatenate(
        [r[:, 128:256].T[0:HEAD_DIM, :], jnp.ones((_V_ROWS - HEAD_DIM, r.shape[0]), F32)],
        axis=0).astype(BF16)

    ikk = mm(_P_IK)
    ik_ref[...] = (ikk[:, 0:64] * cos64 + ikk[:, 64:128] * sin64).astype(BF16)
    iw_ref[...] = mm(_P_IW) * (IDX_HEADS ** -0.5 * IDX_DIM ** -0.5)
    qkvb_ref[...] = mm(_P_QKVB)
    zb_ref[...] = mm(_P_ZB)
    ab_ref[...] = mm(_P_AB)
    qc_ref[...] = mm(_P_QC).astype(BF16)


def _proj(x2, w_in, w_k_up, w_v_up, kv_norm_g, cos128, sin128, seq, tm=256):
    n_tok, d_model = x2.shape
    offs = np.cumsum(SPLIT_SIZES)[:-1].tolist()
    (w_qa, w_ckv, w_iq, w_ik, w_iw, w_qb, w_kb, w_vb, w_zb, w_a, w_b, w_qc) = jnp.split(w_in, offs, axis=1)
    w1 = jnp.concatenate([
        w_qa, w_iq, w_ckv,
        w_ik, _rot_cols(w_ik, 1), _pad_cols(w_iw, LANES),
        w_qb, w_kb, w_vb, w_zb, _pad_cols(jnp.concatenate([w_a, w_b], axis=1), LANES), w_qc,
    ], axis=1).astype(BF16)
    assert w1.shape[1] == _P_TOTAL
    w2 = _pad_cols(jnp.concatenate([w_k_up, _rot_cols(w_k_up, 1), w_v_up], axis=1), 2 * LANES).astype(BF16)
    n_pos = seq // tm
    row = lambda i: (i, 0)
    const = lambda i: (0, 0)
    pos = lambda i: (i % n_pos, 0)
    outs = [(A_WIDTH, BF16), (IDX_HEADS * IDX_DIM, BF16), (IDX_DIM, BF16), (HEAD_DIM, BF16),
            None, (LANES, F32), (3 * GDN_WIDTH, F32), (GDN_WIDTH, F32), (LANES, F32),
            (MEM_WIDTH, BF16)]
    out_specs = [pl.BlockSpec((tm, o[0]), row) if o else pl.BlockSpec((_V_ROWS, tm), lambda i: (0, i))
                 for o in outs]
    out_shape = [jax.ShapeDtypeStruct((n_tok, o[0]), o[1]) if o
                 else jax.ShapeDtypeStruct((_V_ROWS, n_tok), BF16) for o in outs]
    return pl.pallas_call(
        _proj_kernel,
        grid=(n_tok // tm,),
        in_specs=[pl.BlockSpec((tm, d_model), row),
                  pl.BlockSpec(w1.shape, const),
                  pl.BlockSpec(w2.shape, const),
                  pl.BlockSpec((tm, LANES), pos),
                  pl.BlockSpec((tm, LANES), pos),
                  pl.BlockSpec((1, KV_RANK), const)],
        out_specs=out_specs,
        out_shape=out_shape,
        compiler_params=_cparams(("parallel",)),
        name="proj",
    )(x2, w1, w2, cos128, sin128, kv_norm_g.reshape(1, KV_RANK).astype(F32))


def _dsa_kernel(qa_ref, iq_ref, iw_ref, ik_ref, ka_ref, vat_ref, o_ref,
                key_ref, hi_ref, lo_ref, m_ref, acc_ref, *, tq, kc, top):
    qi = pl.program_id(1)
    row0 = qi * tq
    n_kc = (row0 + tq + kc - 1) // kc
    qpos = row0 + lax.broadcasted_iota(I32, (1, tq), 1)

    def key_fold(v):
        return jnp.sum(v.reshape(kc // 8, 8, tq), axis=0)

    def head_rows(x, n_heads, width):
        return jnp.concatenate([x[:, h * width:(h + 1) * width] for h in range(n_heads)], axis=0)

    iq_rows = head_rows(iq_ref[...], IDX_HEADS, IDX_DIM)
    iw_t = iw_ref[...].T

    def score_body(c, carry):
        k0 = pl.multiple_of(c * kc, kc)
        d = _dot_nt(ik_ref[pl.ds(k0, kc), :], iq_rows)
        s = jnp.zeros((kc, tq), F32)
        for h in range(IDX_HEADS):
            s = s + iw_t[h:h + 1, :] * jnp.maximum(d[:, h * tq:(h + 1) * tq], 0.0)
        kidx = k0 + lax.broadcasted_iota(I32, (kc, tq), 0)
        s = jnp.where(s == 0.0, 0.0, s)
        s = jnp.where(kidx <= qpos, s, NEG_INF)
        bits = pltpu.bitcast(s, I32)
        key = jnp.where(bits >= 0, bits, bits ^ 0x7FFFFFFF)
        key_ref[pl.ds(k0, kc), :] = key
        hi_ref[pl.ds(k0, kc), :] = lax.shift_right_arithmetic(key, 16).astype(I16)
        lo_ref[pl.ds(k0, kc), :] = ((key & 0xFFFF) + I16_MIN).astype(I16)
        return carry

    lax.fori_loop(0, n_kc, score_body, 0)

    k_eff = jnp.minimum(top, qpos + 1).astype(F32)

    @pl.when(n_kc % 2 == 1)
    def _():
        k_pad = pl.multiple_of(n_kc * kc, kc)
        hi_ref[pl.ds(k_pad, kc), :] = jnp.full((kc, tq), I16_MIN, I16)
        lo_ref[pl.ds(k_pad, kc), :] = jnp.full((kc, tq), I16_MIN, I16)

    n_pairs = (n_kc + 1) // 2
    one_b = jnp.ones((), BF16)
    zero_b = jnp.zeros((), BF16)

    def count16(ref, pred):
        def body(c, acc):
            k0 = pl.multiple_of(c * (2 * kc), 2 * kc)
            for j in range(2):
                hit = jnp.where(pred(ref[pl.ds(k0 + j * kc, kc), :]), one_b, zero_b)
                hit = hit.reshape(kc // 16, 16, tq)
                parts = [hit[r] for r in range(kc // 16)]
                while len(parts) > 1:
                    parts = [a + b for a, b in zip(parts[0::2], parts[1::2])]
                acc = acc + parts[0]
            return acc
        acc = lax.fori_loop(0, n_pairs, body, jnp.zeros((16, tq), BF16))
        return jnp.sum(acc.astype(F32), axis=0, keepdims=True)

    def search16(ref, start, n_bits, k_want):
        def bit_body(i, t):
            cand = t + lax.shift_left(jnp.int32(1), n_bits - 1 - i)
            c16 = cand.astype(I16)
            return jnp.where(count16(ref, lambda v: v >= c16) >= k_want, cand, t)
        return lax.fori_loop(0, n_bits, bit_body, start)

    hi0 = jnp.where(count16(hi_ref, lambda v: v >= 0) >= k_eff, 0, I16_MIN).astype(I32)
    thr_hi = search16(hi_ref, hi0, 15, k_eff)
    thr_hi16 = thr_hi.astype(I16)
    k_low = k_eff - count16(hi_ref, lambda v: v > thr_hi16)

    def band_body(c, carry):
        k0 = pl.multiple_of(c * kc, kc)
        lo_ref[pl.ds(k0, kc), :] = jnp.where(hi_ref[pl.ds(k0, kc), :] == thr_hi16,
                                             lo_ref[pl.ds(k0, kc), :], I16_MIN)
        return carry

    lax.fori_loop(0, n_kc, band_body, 0)
    thr_lo = search16(lo_ref, jnp.full((1, tq), I16_MIN, I32), 16, k_low)
    thr_lo16 = thr_lo.astype(I16)
    thr = thr_hi * 65536 + (thr_lo - I16_MIN)
    n_tie = k_low - count16(lo_ref, lambda v: v > thr_lo16)

    q_rows = head_rows((qa_ref[...].astype(F32) * (HEAD_DIM ** -0.5)).astype(BF16),
                       ATTN_HEADS, HEAD_DIM)
    m_ref[...] = jnp.full(m_ref.shape, NEG_INF, F32)
    acc_ref[...] = jnp.zeros(acc_ref.shape, F32)
    earlier = jnp.where(lax.broadcasted_iota(I32, (kc, kc), 1) < lax.broadcasted_iota(I32, (kc, kc), 0),
                        1.0, 0.0).astype(BF16)

    def attn_body(c, tie_seen):
        k0 = pl.multiple_of(c * kc, kc)
        kk = key_ref[pl.ds(k0, kc), :]
        kidx = k0 + lax.broadcasted_iota(I32, (kc, tq), 0)
        tie = kk == thr
        tie_f = jnp.where(tie, 1.0, 0.0)
        tie_rank = jnp.dot(earlier, tie_f.astype(BF16), preferred_element_type=F32) + tie_seen
        bias = jnp.where(kk > thr, 0.0,
                         jnp.where(tie, jnp.where(tie_rank < n_tie, 0.0, NEG_INF), NEG_INF))
        bias = jnp.where(kidx <= qpos, bias, NEG_INF)
        lg_all = _dot_nt(ka_ref[pl.ds(k0, kc), :], q_rows)
        lgs = [lg_all[:, h * tq:(h + 1) * tq] + bias for h in range(ATTN_HEADS)]
        m_prev = m_ref[...]
        m_new = jnp.maximum(m_prev, jnp.concatenate(
            [jnp.max(lg, axis=0, keepdims=True) for lg in lgs], axis=1))
        p = jnp.concatenate([jnp.exp(lgs[h] - m_new[0:1, h * tq:(h + 1) * tq])
                             for h in range(ATTN_HEADS)], axis=1)
        alpha = jnp.exp(m_prev - m_new)
        acc_ref[...] = alpha[0:1, :] * acc_ref[...] + jnp.dot(
            vat_ref[:, pl.ds(k0, kc)], p.astype(BF16), preferred_element_type=F32)
        m_ref[...] = m_new
        return tie_seen + jnp.sum(key_fold(tie_f), axis=0, keepdims=True)

    lax.fori_loop(0, n_kc, attn_body, jnp.zeros((1, tq), F32))
    o_t = acc_ref[0:HEAD_DIM, :] / acc_ref[HEAD_DIM:HEAD_DIM + 1, :]
    o_ref[...] = jnp.concatenate(
        [o_t[:, h * tq:(h + 1) * tq].T for h in range(ATTN_HEADS)], axis=1)


def _dsa(qa, iq, iw, ik, ka, va_t, bsz, seq, tq=256):
    tq = min(tq, seq)
    kc = tq
    top = min(INDEX_TOPK, seq // 4)
    n_q = seq // tq
    assert seq // 16 <= 256 and (seq // kc) % 2 == 0
    row = lambda b, i: (b * n_q + i, 0)
    per_b = lambda b, i: (b, 0, 0)
    kern = functools.partial(_dsa_kernel, tq=tq, kc=kc, top=top)
    return pl.pallas_call(
        kern,
        grid=(bsz, n_q),
        in_specs=[pl.BlockSpec((tq, A_WIDTH), row),
                  pl.BlockSpec((tq, IDX_HEADS * IDX_DIM), row),
                  pl.BlockSpec((tq, LANES), row),
                  pl.BlockSpec((None, seq, IDX_DIM), per_b),
                  pl.BlockSpec((None, seq, HEAD_DIM), per_b),
                  pl.BlockSpec((_V_ROWS, seq), lambda b, i: (0, b))],
        out_specs=pl.BlockSpec((tq, A_WIDTH), row),
        out_shape=jax.ShapeDtypeStruct((bsz * seq, A_WIDTH), F32),
        scratch_shapes=[pltpu.VMEM((seq, tq), I32),
                        pltpu.VMEM((seq, tq), I16),
                        pltpu.VMEM((seq, tq), I16),
                        pltpu.VMEM((8, ATTN_HEADS * tq), F32),
                        pltpu.VMEM((_V_ROWS, ATTN_HEADS * tq), F32)],
        compiler_params=_cparams(("parallel", "arbitrary")),
        name="dsa",
    )(qa, iq, iw, ik.reshape(bsz, seq, IDX_DIM), ka.reshape(bsz, seq, HEAD_DIM), va_t)


_HALO = 8


def _bmm(a, b):
    return lax.dot_general(a.astype(BF16), b.astype(BF16), (((2,), (1,)), ((0,), (0,))),
                           preferred_element_type=F32)


def _bmm_nt(a, b):
    return lax.dot_general(a.astype(BF16), b.astype(BF16), (((2,), (2,)), ((0,), (0,))),
                           preferred_element_type=F32)


def _gdn_kernel(qkv_ref, ab_ref, z_ref, cw_ref, alog_ref, dtb_ref, gn_ref, o_ref,
                ext_ref, state_ref, *, tr):
    ti = pl.program_id(1)
    width = 3 * GDN_WIDTH
    n_c = tr // CHUNK

    @pl.when(ti == 0)
    def _():
        ext_ref[0:_HALO, :] = jnp.zeros((_HALO, width), F32)
        state_ref[...] = jnp.zeros(state_ref.shape, F32)

    ext_ref[_HALO:_HALO + tr, :] = qkv_ref[...]
    conv = cw_ref[CONV_WIDTH - 1:CONV_WIDTH, :] * ext_ref[_HALO:_HALO + tr, :]
    for j in range(CONV_WIDTH - 1):
        off = _HALO - (CONV_WIDTH - 1) + j
        conv = conv + cw_ref[j:j + 1, :] * ext_ref[off:off + tr, :]
    ext_ref[0:_HALO, :] = ext_ref[tr:tr + _HALO, :]
    qkv = _silu(conv)

    ab = ab_ref[...]
    sp = jnp.maximum(ab + dtb_ref[...], 0.0) + jnp.log1p(jnp.exp(-jnp.abs(ab + dtb_ref[...])))
    g_all = -jnp.exp(alog_ref[...]) * sp
    beta_all = 1.0 / (1.0 + jnp.exp(-ab))

    lane = lax.broadcasted_iota(I32, (CHUNK, LANES), 1)
    row_i = lax.broadcasted_iota(I32, (CHUNK, LANES), 0)
    left = lane < HEAD_DIM
    incl = (row_i >= lane % HEAD_DIM)[None]
    strict = (row_i > lane % HEAD_DIM)[None]
    left_t = (lax.broadcasted_iota(I32, (tr, LANES), 1) < HEAD_DIM)
    tri = jnp.where(lax.broadcasted_iota(I32, (CHUNK, CHUNK), 0) >= lax.broadcasted_iota(I32, (CHUNK, CHUNK), 1),
                    1.0, 0.0).astype(BF16)

    def blk(x):
        return jnp.concatenate([jnp.where(left[None], x, 0.0), jnp.where(left[None], 0.0, x)], axis=1)

    def head_scale(x, eps, scale):
        sq = x * x
        keep = left_t if x.shape[0] == tr else left
        s_l = jnp.sum(jnp.where(keep, sq, 0.0), axis=-1, keepdims=True)
        s_r = jnp.sum(jnp.where(keep, 0.0, sq), axis=-1, keepdims=True)
        return x * jnp.where(keep, lax.rsqrt(s_l * scale + eps), lax.rsqrt(s_r * scale + eps))

    gc_cols, gc_rows = [], []
    for c in range(n_c):
        g_c = g_all[c * CHUNK:(c + 1) * CHUNK, :]
        g_hi = g_c.astype(BF16)
        g_r1 = g_c - g_hi.astype(F32)
        g_mid = g_r1.astype(BF16)
        g_lo = (g_r1 - g_mid.astype(F32)).astype(BF16)
        gc = (jnp.dot(tri, g_hi, preferred_element_type=F32)
              + jnp.dot(tri, g_mid, preferred_element_type=F32)
              + jnp.dot(tri, g_lo, preferred_element_type=F32))
        gc_cols.append(gc)
        gc_rows.append(gc.T)

    n_p = GDN_HEADS // 2
    q_n = [head_scale(qkv[:, p * LANES:(p + 1) * LANES], RMS_EPS, 1.0) * (HEAD_DIM ** -0.5)
           for p in range(n_p)]
    k_n = [head_scale(qkv[:, GDN_WIDTH + p * LANES:GDN_WIDTH + (p + 1) * LANES], RMS_EPS, 1.0)
           for p in range(n_p)]

    def stack(fn):
        return jnp.stack([fn(c, p) for c in range(n_c) for p in range(n_p)], axis=0)

    def rows(c):
        return slice(c * CHUNK, (c + 1) * CHUNK)

    def pair_cols(x, c, p, base):
        return jnp.where(left, x[rows(c), base + 2 * p:base + 2 * p + 1],
                         x[rows(c), base + 2 * p + 1:base + 2 * p + 2])

    qs = stack(lambda c, p: q_n[p][rows(c), :])
    ks = stack(lambda c, p: k_n[p][rows(c), :])
    vs = stack(lambda c, p: qkv[rows(c), 2 * GDN_WIDTH + p * LANES:2 * GDN_WIDTH + (p + 1) * LANES])
    beta = stack(lambda c, p: pair_cols(beta_all, c, p, GDN_HEADS))
    gcc = stack(lambda c, p: pair_cols(gc_cols[c], 0, p, 0))
    gcr = stack(lambda c, p: jnp.concatenate(
        [gc_rows[c][2 * p:2 * p + 1, :], gc_rows[c][2 * p + 1:2 * p + 2, :]], axis=1))
    decay = jnp.where(incl, jnp.exp(jnp.where(incl, gcc - gcr, 0.0)), 0.0)
    kb = ks * beta
    k_blk = blk(ks)
    low = jnp.where(strict, _bmm_nt(kb, k_blk) * decay, 0.0)
    e_gc = jnp.exp(gcc)
    u = vs * beta
    w = kb * e_gc
    u = u - _bmm(low, blk(u))
    w = w - _bmm(low, blk(w))
    pw = low
    for _ in range(5):
        pw = _bmm(pw, blk(pw))
        u = u + _bmm(pw, blk(u))
        w = w + _bmm(pw, blk(w))
    a_intra = jnp.where(incl, _bmm_nt(qs, k_blk) * decay, 0.0)
    q_dec = qs * e_gc
    gc_last = gcc[:, CHUNK - 1:CHUNK, :]
    kd_blk = blk(ks * jnp.exp(gc_last - gcc))
    kd_blk_t = jnp.stack([kd_blk[i].T for i in range(n_c * n_p)], axis=0)
    g_last = jnp.exp(gc_last)

    gn = gn_ref[...]
    st = state_ref[...]
    for c in range(n_c):
        sl = slice(c * n_p, (c + 1) * n_p)
        v_new = blk(u[sl] - _bmm(w[sl], st))
        o = _bmm(q_dec[sl], st) + _bmm(a_intra[sl], v_new)
        st = st * g_last[sl] + _bmm(kd_blk_t[sl], v_new)
        for p in range(n_p):
            y = head_scale(o[p], RMS_EPS, 1.0 / HEAD_DIM) * gn
            o_ref[rows(c), p * LANES:(p + 1) * LANES] = y * _silu(z_ref[rows(c), p * LANES:(p + 1) * LANES])
    state_ref[...] = st


def _gdn(qkvb, ab, zb, conv_w, a_log, dt_bias, gdn_norm_g, bsz, seq, tr=256):
    tr = min(tr, seq)
    n_t = seq // tr
    row = lambda b, i: (b * n_t + i, 0)
    const = lambda b, i: (0, 0)
    alog = _pad_cols(a_log.reshape(1, GDN_HEADS).astype(F32), LANES)
    dtb = _pad_cols(dt_bias.reshape(1, GDN_HEADS).astype(F32), LANES)
    return pl.pallas_call(
        functools.partial(_gdn_kernel, tr=tr),
        grid=(bsz, n_t),
        in_specs=[pl.BlockSpec((tr, 3 * GDN_WIDTH), row),
                  pl.BlockSpec((tr, LANES), row),
                  pl.BlockSpec((tr, GDN_WIDTH), row),
                  pl.BlockSpec((CONV_WIDTH, 3 * GDN_WIDTH), const),
                  pl.BlockSpec((1, LANES), const),
                  pl.BlockSpec((1, LANES), const),
                  pl.BlockSpec((1, LANES), const)],
        out_specs=pl.BlockSpec((tr, GDN_WIDTH), row),
        out_shape=jax.ShapeDtypeStruct((bsz * seq, GDN_WIDTH), F32),
        scratch_shapes=[pltpu.VMEM((tr + _HALO, 3 * GDN_WIDTH), F32),
                        pltpu.VMEM((GDN_HEADS // 2, LANES, LANES), F32)],
        compiler_params=_cparams(("parallel", "arbitrary")),
        name="gdn",
    )(qkvb, ab, zb, conv_w.astype(F32), alog, dtb,
      jnp.tile(gdn_norm_g.reshape(1, HEAD_DIM).astype(F32), (1, LANES // HEAD_DIM)))


def _memkv_kernel(m_ref, w_ref, k_ref, v_ref):
    r = jnp.dot(m_ref[...].astype(BF16), w_ref[...], preferred_element_type=F32)
    k_ref[...] = r[:, :MEM_WIDTH].astype(BF16)
    v_ref[...] = r[:, MEM_WIDTH:].astype(BF16)


def _memkv(mem2, w_mem_kv, n_mem):
    n_rows, d_model = mem2.shape
    row = lambda i: (i, 0)
    return pl.pallas_call(
        _memkv_kernel,
        grid=(n_rows // n_mem,),
        in_specs=[pl.BlockSpec((n_mem, d_model), row),
                  pl.BlockSpec((d_model, 2 * MEM_WIDTH), lambda i: (0, 0))],
        out_specs=[pl.BlockSpec((n_mem, MEM_WIDTH), row)] * 2,
        out_shape=[jax.ShapeDtypeStruct((n_rows, MEM_WIDTH), BF16)] * 2,
        compiler_params=_cparams(("parallel",)),
        name="memkv",
    )(mem2, w_mem_kv.astype(BF16))


_ROUTE_ROWS = 128


def _post_kernel(x_ref, oa_ref, ob_ref, qc_ref, mk_ref, mv_ref, wout_ref, ag_ref, mg_ref,
                 lg_ref, lb_ref, wr_ref, br_ref, x1_ref, ids_ref, gate_ref, *, alpha):
    tm = x_ref.shape[0]
    oa = oa_ref[...]
    oa = oa * lax.rsqrt(jnp.mean(oa * oa, axis=-1, keepdims=True) + RMS_EPS) * ag_ref[...]

    qc = qc_ref[...]
    mk = mk_ref[...]
    mv = mv_ref[...]
    cols = [slice(h * HEAD_DIM, (h + 1) * HEAD_DIM) for h in range(MEM_HEADS)]
    lgs = [_dot_nt(qc[:, c], mk[:, c]) * (HEAD_DIM ** -0.5) for c in cols]
    es = [jnp.exp(lg - jnp.max(lg, axis=-1, keepdims=True)) for lg in lgs]
    ps = [e / jnp.sum(e, axis=-1, keepdims=True) for e in es]
    oc = jnp.concatenate([jnp.dot(p.astype(BF16), mv[:, c], preferred_element_type=F32)
                          for p, c in zip(ps, cols)], axis=1)
    oc = oc * lax.rsqrt(jnp.mean(oc * oc, axis=-1, keepdims=True) + RMS_EPS) * mg_ref[...]

    mix = (jnp.dot(oa.astype(BF16), wout_ref[0:A_WIDTH, :], preferred_element_type=F32)
           + jnp.dot(ob_ref[...].astype(BF16), wout_ref[A_WIDTH:A_WIDTH + GDN_WIDTH, :],
                     preferred_element_type=F32)
           + jnp.dot(oc.astype(BF16), wout_ref[A_WIDTH + GDN_WIDTH:, :], preferred_element_type=F32))
    hres = alpha * x_ref[...] + mix
    mu = jnp.mean(hres, axis=-1, keepdims=True)
    var = jnp.mean(jnp.square(hres - mu), axis=-1, keepdims=True)
    x1 = (hres - mu) * lax.rsqrt(var + LN_EPS) * lg_ref[...] + lb_ref[...]
    x1_ref[...] = x1

    lt = _dot_nt(wr_ref[...], x1) + br_ref[...]
    sub = lax.broadcasted_iota(I32, (EXPERTS_PER_GROUP, tm), 0)
    gl = lt[0:N_GROUPS, :]
    gmax = jnp.max(gl, axis=0, keepdims=True)
    gprob = jnp.exp(gl - gmax) / jnp.sum(jnp.exp(gl - gmax), axis=0, keepdims=True)
    p_grp = jnp.max(gprob, axis=0, keepdims=True)
    grp = jnp.min(jnp.where(gprob == p_grp, sub, N_GROUPS), axis=0, keepdims=True)
    el = jnp.zeros((EXPERTS_PER_GROUP, tm), F32)
    for g in range(N_GROUPS):
        r0 = N_GROUPS + g * EXPERTS_PER_GROUP
        el = el + jnp.where(grp == g, lt[r0:r0 + EXPERTS_PER_GROUP, :], 0.0)
    ee = jnp.exp(el - jnp.max(el, axis=0, keepdims=True))
    pe = ee / jnp.sum(ee, axis=0, keepdims=True)
    p1 = jnp.max(pe, axis=0, keepdims=True)
    i1 = jnp.min(jnp.where(pe == p1, sub, EXPERTS_PER_GROUP), axis=0, keepdims=True)
    rest = jnp.where(sub == i1, -1.0, pe)
    p2 = jnp.max(rest, axis=0, keepdims=True)
    i2 = jnp.min(jnp.where(rest == p2, sub, EXPERTS_PER_GROUP), axis=0, keepdims=True)
    psum = p1 + p2
    g1 = p_grp * p1 / psum
    g2 = p_grp * p2 / psum
    e1 = grp * EXPERTS_PER_GROUP + i1
    e2 = grp * EXPERTS_PER_GROUP + i2
    ids_ref[...] = jnp.where(sub == 0, e1, jnp.where(sub == 1, e2, 0))
    gate_ref[...] = jnp.where(sub == 0, g1, jnp.where(sub == 1, g2, 0.0))


def _post(x2, oa, ob, qc, mk, mv, w_out, attn_g, mem_g, ln_g, ln_b, w_group, b_group,
          w_router, b_router, seq, n_mem, alpha, tm=256):
    n_tok, d_model = x2.shape
    n_t = seq // tm
    row = lambda i: (i, 0)
    const = lambda i: (0, 0)
    per_b = lambda i: (i // n_t, 0, 0)
    lane = lambda i: (0, i)
    bsz = n_tok // seq
    wr = jnp.pad(jnp.concatenate([w_group, w_router], axis=1).T,
                 ((0, _ROUTE_ROWS - N_GROUPS - N_EXPERTS), (0, 0))).astype(BF16)
    br = jnp.pad(jnp.concatenate([b_group, b_router]),
                 (0, _ROUTE_ROWS - N_GROUPS - N_EXPERTS)).reshape(_ROUTE_ROWS, 1).astype(F32)
    vec = lambda v: v.reshape(1, -1).astype(F32)
    return pl.pallas_call(
        functools.partial(_post_kernel, alpha=alpha),
        grid=(n_tok // tm,),
        in_specs=[pl.BlockSpec((tm, d_model), row),
                  pl.BlockSpec((tm, A_WIDTH), row),
                  pl.BlockSpec((tm, GDN_WIDTH), row),
                  pl.BlockSpec((tm, MEM_WIDTH), row),
                  pl.BlockSpec((None, n_mem, MEM_WIDTH), per_b),
                  pl.BlockSpec((None, n_mem, MEM_WIDTH), per_b),
                  pl.BlockSpec(w_out.shape, const),
                  pl.BlockSpec((1, A_WIDTH), const),
                  pl.BlockSpec((1, MEM_WIDTH), const),
                  pl.BlockSpec((1, d_model), const),
                  pl.BlockSpec((1, d_model), const),
                  pl.BlockSpec((_ROUTE_ROWS, d_model), const),
                  pl.BlockSpec((_ROUTE_ROWS, 1), const)],
        out_specs=[pl.BlockSpec((tm, d_model), row),
                   pl.BlockSpec((EXPERTS_PER_GROUP, tm), lane),
                   pl.BlockSpec((EXPERTS_PER_GROUP, tm), lane)],
        out_shape=[jax.ShapeDtypeStruct((n_tok, d_model), F32),
                   jax.ShapeDtypeStruct((EXPERTS_PER_GROUP, n_tok), I32),
                   jax.ShapeDtypeStruct((EXPERTS_PER_GROUP, n_tok), F32)],
        compiler_params=_cparams(("parallel",)),
        name="post",
    )(x2, oa, ob, qc, mk.reshape(bsz, n_mem, MEM_WIDTH), mv.reshape(bsz, n_mem, MEM_WIDTH),
      w_out.astype(BF16), vec(attn_g), vec(mem_g), vec(ln_g), vec(ln_b), wr, br)


def _rank_kernel(ids_ref, rank_ref, cnt_ref, carry_ref):
    i = pl.program_id(0)
    tm = ids_ref.shape[1]

    @pl.when(i == 0)
    def _():
        carry_ref[...] = jnp.zeros(carry_ref.shape, F32)

    ids = ids_ref[...]
    eio = lax.broadcasted_iota(I32, (N_EXPERTS, tm), 0)
    oh0 = jnp.where(eio == ids[0:1, :], 1.0, 0.0)
    oh1 = jnp.where(eio == ids[1:2, :], 1.0, 0.0)
    cnt = oh0 + oh1
    before = (lax.broadcasted_iota(I32, (tm, tm), 0) < lax.broadcasted_iota(I32, (tm, tm), 1))
    prefix = jnp.dot(cnt.astype(BF16), jnp.where(before, 1.0, 0.0).astype(BF16),
                     preferred_element_type=F32) + carry_ref[:, 0:1]
    r0 = jnp.sum(oh0 * prefix, axis=0, keepdims=True)
    r1 = jnp.sum(oh1 * prefix, axis=0, keepdims=True)
    sub = lax.broadcasted_iota(I32, (EXPERTS_PER_GROUP, tm), 0)
    rank_ref[...] = jnp.where(sub == 0, r0, jnp.where(sub == 1, r1, 0.0)).astype(I32)
    carry_ref[...] = carry_ref[...] + jnp.sum(cnt, axis=1, keepdims=True)
    cnt_ref[...] = carry_ref[...].astype(I32)


def _rank(ids, tm=512):
    n_tok = ids.shape[1]
    lane = lambda i: (0, i)
    return pl.pallas_call(
        _rank_kernel,
        grid=(n_tok // tm,),
        in_specs=[pl.BlockSpec((EXPERTS_PER_GROUP, tm), lane)],
        out_specs=[pl.BlockSpec((EXPERTS_PER_GROUP, tm), lane),
                   pl.BlockSpec((N_EXPERTS, LANES), lambda i: (0, 0))],
        out_shape=[jax.ShapeDtypeStruct((EXPERTS_PER_GROUP, n_tok), I32),
                   jax.ShapeDtypeStruct((N_EXPERTS, LANES), I32)],
        scratch_shapes=[pltpu.VMEM((N_EXPERTS, LANES), F32)],
        compiler_params=_cparams(("arbitrary",)),
        name="rank",
    )(ids)


def _dest_kernel(ids_ref, rank_ref, ps_ref, dest_ref):
    ids = ids_ref[...]
    tm = ids.shape[1]
    eio = lax.broadcasted_iota(I32, (N_EXPERTS, tm), 0)
    ps = ps_ref[...]
    d0 = jnp.sum(jnp.where(eio == ids[0:1, :], ps, 0.0), axis=0, keepdims=True)
    d1 = jnp.sum(jnp.where(eio == ids[1:2, :], ps, 0.0), axis=0, keepdims=True)
    sub = lax.broadcasted_iota(I32, (EXPERTS_PER_GROUP, tm), 0)
    dest_ref[...] = rank_ref[...] + jnp.where(sub == 0, d0, jnp.where(sub == 1, d1, 0.0)).astype(I32)


def _dest(ids, rank, pad_start, tm=2048):
    n_tok = ids.shape[1]
    tm = min(tm, n_tok)
    lane = lambda i: (0, i)
    return pl.pallas_call(
        _dest_kernel,
        grid=(n_tok // tm,),
        in_specs=[pl.BlockSpec((EXPERTS_PER_GROUP, tm), lane),
                  pl.BlockSpec((EXPERTS_PER_GROUP, tm), lane),
                  pl.BlockSpec((N_EXPERTS, 1), lambda i: (0, 0))],
        out_specs=pl.BlockSpec((EXPERTS_PER_GROUP, tm), lane),
        out_shape=jax.ShapeDtypeStruct((EXPERTS_PER_GROUP, n_tok), I32),
        compiler_params=_cparams(("parallel",)),
        name="dest",
    )(ids, rank, pad_start.astype(F32).reshape(N_EXPERTS, 1))


def _sc_mesh():
    return plsc.VectorSubcoreMesh(core_axis_name="c", subcore_axis_name="s",
                                  num_cores=SC_CORES, num_subcores=SC_SUBCORES)


def _sc_scatter_rows(x1, dest, pad_rows, cap):
    n_tok, d = x1.shape
    n_pad = pad_rows.shape[0]
    n_workers = SC_CORES * SC_SUBCORES
    tok_per_worker = n_tok // n_workers
    pad_per_worker = n_pad // n_workers
    n_win = tok_per_worker // SC_ROWS
    assert n_tok % (n_workers * SC_ROWS) == 0 and n_pad % (n_workers * SC_ROWS) == 0 and TOP_K == 2
    zero_rows = jnp.zeros((SC_ROWS, d), x1.dtype)

    @functools.partial(
        pl.kernel, mesh=_sc_mesh(), out_type=jax.ShapeDtypeStruct((cap, d), x1.dtype),
        scratch_types=[pltpu.VMEM((tok_per_worker,), I32), pltpu.VMEM((tok_per_worker,), I32),
                       pltpu.VMEM((pad_per_worker,), I32), pltpu.VMEM((2, SC_ROWS, d), x1.dtype),
                       pltpu.SemaphoreType.DMA((2,)), pltpu.SemaphoreType.DMA((2,))],
        name="sc_scatter")
    def scatter(x_hbm, dest_hbm, pad_hbm, zero_hbm, out_hbm, idx0_v, idx1_v, pad_v, rows_v, lsem, ssem):
        worker = lax.axis_index("s") * SC_CORES + lax.axis_index("c")
        base = worker * tok_per_worker
        pltpu.sync_copy(dest_hbm.at[pl.ds(base, tok_per_worker)], idx0_v)
        pltpu.sync_copy(dest_hbm.at[pl.ds(n_tok + base, tok_per_worker)], idx1_v)
        pltpu.sync_copy(pad_hbm.at[pl.ds(worker * pad_per_worker, pad_per_worker)], pad_v)

        def load(i, slot):
            return pltpu.make_async_copy(x_hbm.at[pl.ds(base + i * SC_ROWS, SC_ROWS)],
                                         rows_v.at[slot], lsem.at[slot])

        def store(idx_v, i, slot):
            return pltpu.make_async_copy(rows_v.at[slot],
                                         out_hbm.at[idx_v.at[pl.ds(i * SC_ROWS, SC_ROWS)]], ssem.at[slot])

        load(0, 0).start()

        @pl.loop(0, n_win)
        def _(i):
            slot = i % 2
            load(i, slot).wait()
            store(idx0_v, i, slot).start()
            store(idx1_v, i, slot).start()

            @pl.when(i >= 1)
            def _():
                store(idx0_v, i - 1, 1 - slot).wait()
                store(idx1_v, i - 1, 1 - slot).wait()

            @pl.when(i + 1 < n_win)
            def _():
                load(i + 1, 1 - slot).start()

        last = (n_win - 1) % 2
        store(idx0_v, n_win - 1, last).wait()
        store(idx1_v, n_win - 1, last).wait()

        pltpu.sync_copy(zero_hbm, rows_v.at[0])

        @pl.loop(0, pad_per_worker // SC_ROWS)
        def _(i):
            pltpu.async_copy(rows_v.at[0], out_hbm.at[pad_v.at[pl.ds(i * SC_ROWS, SC_ROWS)]],
                             ssem.at[0]).wait()

    return scatter(x1, dest, pad_rows, zero_rows)


def _mlp_kernel(be_ref, nused_ref, x_ref, wg_ref, wu_ref, wd_ref, y_ref):
    i = pl.program_id(0)

    @pl.when(i < nused_ref[0])
    def _():
        xb = x_ref[...].astype(BF16)
        hg = jnp.dot(xb, wg_ref[...].astype(BF16), preferred_element_type=F32)
        hu = jnp.dot(xb, wu_ref[...].astype(BF16), preferred_element_type=F32)
        y_ref[...] = jnp.dot((_silu(hg) * hu).astype(BF16), wd_ref[...].astype(BF16),
                             preferred_element_type=F32)

    @pl.when(i >= nused_ref[0])
    def _():
        y_ref[...] = jnp.zeros(y_ref.shape, F32)


def _mlp(blk_expert, n_used, xbuf, w_gate, w_up, w_down):
    cap, d_model = xbuf.shape
    d_exp = w_gate.shape[2]
    blk = lambda i, be, nu: (i, 0)
    used_blk = lambda i, be, nu: (jnp.minimum(i, nu[0] - 1), 0)
    wsel = lambda i, be, nu: (be[i], 0, 0)
    return pl.pallas_call(
        _mlp_kernel,
        grid_spec=pltpu.PrefetchScalarGridSpec(
            num_scalar_prefetch=2,
            grid=(cap // MOE_BLOCK,),
            in_specs=[pl.BlockSpec((MOE_BLOCK, d_model), used_blk),
                      pl.BlockSpec((None, d_model, d_exp), wsel),
                      pl.BlockSpec((None, d_model, d_exp), wsel),
                      pl.BlockSpec((None, d_exp, d_model), wsel)],
            out_specs=pl.BlockSpec((MOE_BLOCK, d_model), blk)),
        out_shape=jax.ShapeDtypeStruct((cap, d_model), F32),
        compiler_params=_cparams(("arbitrary",)),
        name="mlp",
    )(blk_expert, n_used, xbuf, w_gate, w_up, w_down)


def _sc_gather_rows(table, idx):
    n = idx.shape[0]
    d = table.shape[1]
    n_workers = SC_CORES * SC_SUBCORES
    per_worker = n // n_workers
    n_win = per_worker // SC_ROWS
    assert n % (n_workers * SC_ROWS) == 0

    @functools.partial(
        pl.kernel, mesh=_sc_mesh(), out_type=jax.ShapeDtypeStruct((n, d), table.dtype),
        scratch_types=[pltpu.VMEM((per_worker,), I32), pltpu.VMEM((2, SC_ROWS, d), table.dtype),
                       pltpu.SemaphoreType.DMA((2,)), pltpu.SemaphoreType.DMA((2,))],
        name="sc_gather")
    def gather(table_hbm, idx_hbm, out_hbm, idx_v, rows_v, gsem, wsem):
        worker = lax.axis_index("s") * SC_CORES + lax.axis_index("c")
        base = worker * per_worker
        pltpu.sync_copy(idx_hbm.at[pl.ds(base, per_worker)], idx_v)

        def fetch(i, slot):
            return pltpu.make_async_copy(table_hbm.at[idx_v.at[pl.ds(i * SC_ROWS, SC_ROWS)]],
                                         rows_v.at[slot], gsem.at[slot])

        def write(i, slot):
            return pltpu.make_async_copy(rows_v.at[slot],
                                         out_hbm.at[pl.ds(base + i * SC_ROWS, SC_ROWS)], wsem.at[slot])

        fetch(0, 0).start()

        @pl.loop(0, n_win)
        def _(i):
            slot = i % 2
            fetch(i, slot).wait()
            write(i, slot).start()

            @pl.when(i >= 1)
            def _():
                write(i - 1, 1 - slot).wait()

            @pl.when(i + 1 < n_win)
            def _():
                fetch(i + 1, 1 - slot).start()

        write(n_win - 1, (n_win - 1) % 2).wait()

    return gather(table, idx)


def _combine_kernel(x1_ref, y0_ref, y1_ref, gt_ref, lg_ref, lb_ref, o_ref, *, alpha):
    gt = gt_ref[...]
    ffn = y0_ref[...] * gt[:, 0:1] + y1_ref[...] * gt[:, 1:2]
    hres = alpha * x1_ref[...] + ffn
    mu = jnp.mean(hres, axis=-1, keepdims=True)
    var = jnp.mean(jnp.square(hres - mu), axis=-1, keepdims=True)
    o_ref[...] = (hres - mu) * lax.rsqrt(var + LN_EPS) * lg_ref[...] + lb_ref[...]


def _combine(dest, x1, gates_t, ln_g, ln_b, ybuf, alpha, tc=512):
    n_tok, d_model = x1.shape
    n_tiles = n_tok // tc
    yrows = _sc_gather_rows(ybuf, dest.reshape(-1))
    row = lambda i: (i, 0)
    const = lambda i: (0, 0)
    vec = lambda v: v.reshape(1, -1).astype(F32)
    return pl.pallas_call(
        functools.partial(_combine_kernel, alpha=alpha),
        grid=(n_tiles,),
        in_specs=[pl.BlockSpec((tc, d_model), row),
                  pl.BlockSpec((tc, d_model), row),
                  pl.BlockSpec((tc, d_model), lambda i: (i + n_tiles, 0)),
                  pl.BlockSpec((tc, EXPERTS_PER_GROUP), row),
                  pl.BlockSpec((1, d_model), const),
                  pl.BlockSpec((1, d_model), const)],
        out_specs=pl.BlockSpec((tc, d_model), row),
        out_shape=jax.ShapeDtypeStruct((n_tok, d_model), F32),
        compiler_params=_cparams(("parallel",)),
        name="combine",
    )(x1, yrows, yrows, gates_t, vec(ln_g), vec(ln_b))


def _moe(x1, ids, gates, w_gate, w_up, w_down, ln_g, ln_b, alpha):
    n_tok, d_model = x1.shape
    rank, counts = _rank(ids)
    counts = counts[:, 0]
    padded = (counts + MOE_BLOCK - 1) // MOE_BLOCK * MOE_BLOCK
    pad_ends = jnp.cumsum(padded)
    pad_start = (pad_ends - padded).astype(I32)
    n_asg = n_tok * TOP_K
    cap = (n_asg + MOE_BLOCK - 1) // MOE_BLOCK * MOE_BLOCK + N_EXPERTS * MOE_BLOCK
    n_blk = cap // MOE_BLOCK
    blk_pos = jnp.arange(n_blk, dtype=I32) * MOE_BLOCK
    blk_expert = jnp.minimum(
        jnp.sum((pad_ends[None, :] <= blk_pos[:, None]).astype(I32), axis=1), N_EXPERTS - 1)
    n_used = (pad_ends[-1:] // MOE_BLOCK).astype(I32)
    dest = _dest(ids, rank, pad_start)[0:TOP_K]
    slot = jnp.arange(MOE_BLOCK, dtype=I32)[None, :]
    spare = cap - 1 - (jnp.arange(N_EXPERTS * MOE_BLOCK, dtype=I32).reshape(N_EXPERTS, MOE_BLOCK)
                       % N_EXPERTS)
    pad_rows = jnp.where(slot < (padded - counts)[:, None], (pad_start + counts)[:, None] + slot,
                         spare).reshape(-1)
    xbuf = _sc_scatter_rows(x1, dest.reshape(-1), pad_rows, cap)
    ybuf = _mlp(blk_expert, n_used, xbuf, w_gate, w_up, w_down)
    return _combine(dest, x1, gates.T, ln_g, ln_b, ybuf, alpha)


def kernel(x, mem, w_in, kv_norm_g, w_k_up, w_v_up, conv_w, A_log, dt_bias, gdn_norm_g, attn_norm_g, mem_norm_g, w_mem_kv, w_out, ln1_g, ln1_b, w_group, b_group, w_router, b_router, w_gate, w_up, w_down, ln2_g, ln2_b):
    bsz, seq, d_model = x.shape
    n_mem = mem.shape[1]
    depth = w_in.shape[0]
    alpha = (2 * depth) ** 0.25
    inv_freq = 1.0 / (ROPE_THETA ** (jnp.arange(0, HEAD_DIM, 2, dtype=F32) / HEAD_DIM))
    ang = jnp.arange(seq, dtype=F32)[:, None] * inv_freq[None, :]
    cos128 = jnp.tile(jnp.cos(ang), (1, LANES // (HEAD_DIM // 2)))
    sin128 = jnp.tile(jnp.sin(ang), (1, LANES // (HEAD_DIM // 2)))
    x2 = x.reshape(bsz * seq, d_model)
    mem2 = mem.reshape(bsz * n_mem, d_model)
    for l in range(depth):
        qa, iq, ik, ka, va, iw, qkvb, zb, ab, qc = _proj(
            x2, w_in[l], w_k_up[l], w_v_up[l], kv_norm_g[l], cos128, sin128, seq)
        oa = _dsa(qa, iq, iw, ik, ka, va, bsz, seq)
        ob = _gdn(qkvb, ab, zb, conv_w[l], A_log[l], dt_bias[l], gdn_norm_g[l], bsz, seq)
        mk, mv = _memkv(mem2, w_mem_kv[l], n_mem)
        x1, ids, gates = _post(x2, oa, ob, qc, mk, mv, w_out[l], attn_norm_g[l], mem_norm_g[l],
                               ln1_g[l], ln1_b[l], w_group[l], b_group[l], w_router[l],
                               b_router[l], seq, n_mem, alpha)
        x2 = _moe(x1, ids, gates, w_gate[l], w_up[l], w_down[l], ln2_g[l], ln2_b[l], alpha)
    return x2.reshape(bsz, seq, d_model)
```

```python
import functools

import numpy as np
import jax
import jax.numpy as jnp
from jax import lax
from jax.experimental import pallas as pl
from jax.experimental.pallas import tpu as pltpu
from jax.experimental.pallas import tpu_sc as plsc

F32 = jnp.float32
BF16 = jnp.bfloat16
I32 = jnp.int32
I16 = jnp.int16

HEAD_DIM = 64
ATTN_HEADS = 6
A_WIDTH = ATTN_HEADS * HEAD_DIM
KV_RANK = 128
IDX_HEADS = 4
IDX_DIM = 64
INDEX_TOPK = 256
GDN_HEADS = 6
GDN_WIDTH = GDN_HEADS * HEAD_DIM
CONV_WIDTH = 4
CHUNK = 64
MEM_HEADS = 4
MEM_WIDTH = MEM_HEADS * HEAD_DIM
SPLIT_SIZES = (A_WIDTH, KV_RANK, IDX_HEADS * IDX_DIM, IDX_DIM, IDX_HEADS,
               GDN_WIDTH, GDN_WIDTH, GDN_WIDTH, GDN_WIDTH, GDN_HEADS, GDN_HEADS,
               MEM_WIDTH)
ROPE_THETA = 10000.0
N_GROUPS = 8
EXPERTS_PER_GROUP = 8
N_EXPERTS = N_GROUPS * EXPERTS_PER_GROUP
TOP_K = 2
MOE_BLOCK = 256
LN_EPS = 1e-5
RMS_EPS = 1e-6
NEG_INF = -1e30
INT_MIN = -2 ** 31
I16_MIN = -2 ** 15

LANES = 128
SC_CORES = 2
SC_SUBCORES = 16
SC_ROWS = 32
VMEM_LIMIT = 56 * 1024 * 1024


def _cparams(sem):
    return pltpu.CompilerParams(dimension_semantics=sem, vmem_limit_bytes=VMEM_LIMIT)


def _dot(a, b):
    return jnp.dot(a.astype(BF16), b.astype(BF16), preferred_element_type=F32)


def _dot_nt(a, b):
    return lax.dot_general(a.astype(BF16), b.astype(BF16), (((1,), (1,)), ((), ())),
                           preferred_element_type=F32)


def _silu(t):
    return t * (1.0 / (1.0 + jnp.exp(-t)))


_P_QA = (0, 384)
_P_IQ = (384, 640)
_P_CKV = (640, 768)
_P_IK = (768, 896)
_P_IW = (896, 1024)
_P_QKVB = (1024, 2176)
_P_ZB = (2176, 2560)
_P_AB = (2560, 2688)
_P_QC = (2688, 2944)
_P_TOTAL = 2944
_V_ROWS = HEAD_DIM + 16


def _rot_cols(w, n_heads):
    k = w.shape[0]
    w4 = w.reshape(k, n_heads, 2, HEAD_DIM // 2)
    return jnp.concatenate([-w4[:, :, 1:2], w4[:, :, 0:1]], axis=2).reshape(k, n_heads * HEAD_DIM)


def _pad_cols(w, width):
    return jnp.pad(w, ((0, 0), (0, width - w.shape[1])))


def _proj_kernel(x_ref, w_ref, w2_ref, cos_ref, sin_ref, kvg_ref,
                 qa_ref, iq_ref, ik_ref, ka_ref, va_ref, iw_ref, qkvb_ref, zb_ref, ab_ref, qc_ref):
    xb = x_ref[...].astype(BF16)

    def mm(slab):
        return jnp.dot(xb, w_ref[:, slab[0]:slab[1]], preferred_element_type=F32)

    cos = cos_ref[...]
    sin = sin_ref[...]
    cos3 = jnp.concatenate([cos] * 3, axis=1)
    sin3 = jnp.concatenate([sin] * 3, axis=1)
    cos2 = jnp.concatenate([cos] * 2, axis=1)
    sin2 = jnp.concatenate([sin] * 2, axis=1)
    cos64 = cos[:, :HEAD_DIM]
    sin64 = sin[:, :HEAD_DIM]

    def rotate_half(t):
        slabs = []
        for j in range(t.shape[1] // LANES):
            ts = t[:, j * LANES:(j + 1) * LANES]
            first = lax.broadcasted_iota(I32, ts.shape, 1) % HEAD_DIM < HEAD_DIM // 2
            slabs.append(jnp.where(first, -pltpu.roll(ts, LANES - HEAD_DIM // 2, 1),
                                   pltpu.roll(ts, HEAD_DIM // 2, 1)))
        return jnp.concatenate(slabs, axis=1)

    qa = mm(_P_QA)
    qa_ref[...] = (qa * cos3 + rotate_half(qa) * sin3).astype(BF16)
    iq = mm(_P_IQ)
    iq_ref[...] = (iq * cos2 + rotate_half(iq) * sin2).astype(BF16)

    ckv = mm(_P_CKV)
    cn = ckv * lax.rsqrt(jnp.mean(ckv * ckv, axis=-1, keepdims=True) + RMS_EPS) * kvg_ref[...]
    r = jnp.dot(cn.astype(BF16), w2_ref[...], preferred_element_type=F32)
    ka_ref[...] = (r[:, 0:64] * cos64 + r[:, 64:128] * sin64).astype(BF16)
    va_ref[...] = jnp.concatenate(
        [r[:, 128:256].T[0:HEAD_DIM, :], jnp.ones((_V_ROWS - HEAD_DIM, r.shape[0]), F32)],
        axis=0).astype(BF16)

    ikk = mm(_P_IK)
    ik_ref[...] = (ikk[:, 0:64] * cos64 + ikk[:, 64:128] * sin64).astype(BF16)
    iw_ref[...] = mm(_P_IW) * (IDX_HEADS ** -0.5 * IDX_DIM ** -0.5)
    qkvb_ref[...] = mm(_P_QKVB)
    zb_ref[...] = mm(_P_ZB)
    ab_ref[...] = mm(_P_AB)
    qc_ref[...] = mm(_P_QC).astype(BF16)


def _proj(x2, w_in, w_k_up, w_v_up, kv_norm_g, cos128, sin128, seq, tm=256):
    n_tok, d_model = x2.shape
    offs = np.cumsum(SPLIT_SIZES)[:-1].tolist()
    (w_qa, w_ckv, w_iq, w_ik, w_iw, w_qb, w_kb, w_vb, w_zb, w_a, w_b, w_qc) = jnp.split(w_in, offs, axis=1)
    w1 = jnp.concatenate([
        w_qa, w_iq, w_ckv,
        w_ik, _rot_cols(w_ik, 1), _pad_cols(w_iw, LANES),
        w_qb, w_kb, w_vb, w_zb, _pad_cols(jnp.concatenate([w_a, w_b], axis=1), LANES), w_qc,
    ], axis=1).astype(BF16)
    assert w1.shape[1] == _P_TOTAL
    w2 = _pad_cols(jnp.concatenate([w_k_up, _rot_cols(w_k_up, 1), w_v_up], axis=1), 2 * LANES).astype(BF16)
    n_pos = seq // tm
    row = lambda i: (i, 0)
    const = lambda i: (0, 0)
    pos = lambda i: (i % n_pos, 0)
    outs = [(A_WIDTH, BF16), (IDX_HEADS * IDX_DIM, BF16), (IDX_DIM, BF16), (HEAD_DIM, BF16),
            None, (LANES, F32), (3 * GDN_WIDTH, F32), (GDN_WIDTH, F32), (LANES, F32),
            (MEM_WIDTH, BF16)]
    out_specs = [pl.BlockSpec((tm, o[0]), row) if o else pl.BlockSpec((_V_ROWS, tm), lambda i: (0, i))
                 for o in outs]
    out_shape = [jax.ShapeDtypeStruct((n_tok, o[0]), o[1]) if o
                 else jax.ShapeDtypeStruct((_V_ROWS, n_tok), BF16) for o in outs]
    return pl.pallas_call(
        _proj_kernel,
        grid=(n_tok // tm,),
        in_specs=[pl.BlockSpec((tm, d_model), row),
                  pl.BlockSpec(w1.shape, const),
                  pl.BlockSpec(w2.shape, const),
                  pl.BlockSpec((tm, LANES), pos),
                  pl.BlockSpec((tm, LANES), pos),
                  pl.BlockSpec((1, KV_RANK), const)],
        out_specs=out_specs,
        out_shape=out_shape,
        compiler_params=_cparams(("parallel",)),
        name="proj",
    )(x2, w1, w2, cos128, sin128, kv_norm_g.reshape(1, KV_RANK).astype(F32))


def _dsa_kernel(qa_ref, iq_ref, iw_ref, ik_ref, ka_ref, vat_ref, o_ref,
                key_ref, hi_ref, lo_ref, m_ref, acc_ref, *, tq, kc, top):
    qi = pl.program_id(1)
    row0 = qi * tq
    n_kc = (row0 + tq + kc - 1) // kc
    qpos = row0 + lax.broadcasted_iota(I32, (1, tq), 1)

    def key_fold(v):
        return jnp.sum(v.reshape(kc // 8, 8, tq), axis=0)

    def head_rows(x, n_heads, width):
        return jnp.concatenate([x[:, h * width:(h + 1) * width] for h in range(n_heads)], axis=0)

    iq_rows = head_rows(iq_ref[...], IDX_HEADS, IDX_DIM)
    iw_t = iw_ref[...].T

    def score_body(c, carry):
        k0 = pl.multiple_of(c * kc, kc)
        d = _dot_nt(ik_ref[pl.ds(k0, kc), :], iq_rows)
        s = jnp.zeros((kc, tq), F32)
        for h in range(IDX_HEADS):
            s = s + iw_t[h:h + 1, :] * jnp.maximum(d[:, h * tq:(h + 1) * tq], 0.0)
        kidx = k0 + lax.broadcasted_iota(I32, (kc, tq), 0)
        s = jnp.where(s == 0.0, 0.0, s)
        s = jnp.where(kidx <= qpos, s, NEG_INF)
        bits = pltpu.bitcast(s, I32)
        key = jnp.where(bits >= 0, bits, bits ^ 0x7FFFFFFF)
        key_ref[pl.ds(k0, kc), :] = key
        hi_ref[pl.ds(k0, kc), :] = lax.shift_right_arithmetic(key, 16).astype(I16)
        lo_ref[pl.ds(k0, kc), :] = ((key & 0xFFFF) + I16_MIN).astype(I16)
        return carry

    lax.fori_loop(0, n_kc, score_body, 0)

    k_eff = jnp.minimum(top, qpos + 1).astype(F32)

    @pl.when(n_kc % 2 == 1)
    def _():
        k_pad = pl.multiple_of(n_kc * kc, kc)
        key_ref[pl.ds(k_pad, kc), :] = jnp.full((kc, tq), INT_MIN, I32)
        hi_ref[pl.ds(k_pad, kc), :] = jnp.full((kc, tq), I16_MIN, I16)
        lo_ref[pl.ds(k_pad, kc), :] = jnp.full((kc, tq), I16_MIN, I16)

    n_pairs = (n_kc + 1) // 2
    one_b = jnp.ones((), BF16)
    zero_b = jnp.zeros((), BF16)

    def count16(ref, pred):
        def body(c, acc):
            k0 = pl.multiple_of(c * (2 * kc), 2 * kc)
            for j in range(2):
                hit = jnp.where(pred(ref[pl.ds(k0 + j * kc, kc), :]), one_b, zero_b)
                hit = hit.reshape(kc // 16, 16, tq)
                parts = [hit[r] for r in range(kc // 16)]
                while len(parts) > 1:
                    parts = [a + b for a, b in zip(parts[0::2], parts[1::2])]
                acc = acc + parts[0]
            return acc
        acc = lax.fori_loop(0, n_pairs, body, jnp.zeros((16, tq), BF16))
        return jnp.sum(acc.astype(F32), axis=0, keepdims=True)

    def search16(ref, start, n_bits, k_want):
        def bit_body(i, t):
            cand = t + lax.shift_left(jnp.int32(1), n_bits - 1 - i)
            c16 = cand.astype(I16)
            return jnp.where(count16(ref, lambda v: v >= c16) >= k_want, cand, t)
        return lax.fori_loop(0, n_bits, bit_body, start)

    hi0 = jnp.where(count16(hi_ref, lambda v: v >= 0) >= k_eff, 0, I16_MIN).astype(I32)
    thr_hi = search16(hi_ref, hi0, 15, k_eff)
    thr_hi16 = thr_hi.astype(I16)
    k_low = k_eff - count16(hi_ref, lambda v: v > thr_hi16)

    def band_body(c, carry):
        k0 = pl.multiple_of(c * kc, kc)
        lo_ref[pl.ds(k0, kc), :] = jnp.where(hi_ref[pl.ds(k0, kc), :] == thr_hi16,
                                             lo_ref[pl.ds(k0, kc), :], I16_MIN)
        return carry

    lax.fori_loop(0, n_kc, band_body, 0)
    thr_lo = search16(lo_ref, jnp.full((1, tq), I16_MIN, I32), 16, k_low)
    thr_lo16 = thr_lo.astype(I16)
    thr = thr_hi * 65536 + (thr_lo - I16_MIN)
    n_tie = k_low - count16(lo_ref, lambda v: v > thr_lo16)

    q_rows = head_rows((qa_ref[...].astype(F32) * (HEAD_DIM ** -0.5)).astype(BF16),
                       ATTN_HEADS, HEAD_DIM)
    m_ref[...] = jnp.full(m_ref.shape, NEG_INF, F32)
    acc_ref[...] = jnp.zeros(acc_ref.shape, F32)
    earlier = jnp.where(lax.broadcasted_iota(I32, (kc, kc), 1) < lax.broadcasted_iota(I32, (kc, kc), 0),
                        1.0, 0.0).astype(BF16)

    def chunk_bias(k0, tie_seen):
        kk = key_ref[pl.ds(k0, kc), :]
        kidx = k0 + lax.broadcasted_iota(I32, (kc, tq), 0)
        tie = kk == thr
        tie_f = jnp.where(tie, 1.0, 0.0)
        tie_rank = jnp.dot(earlier, tie_f.astype(BF16), preferred_element_type=F32) + tie_seen
        bias = jnp.where(kk > thr, 0.0,
                         jnp.where(tie, jnp.where(tie_rank < n_tie, 0.0, NEG_INF), NEG_INF))
        bias = jnp.where(kidx <= qpos, bias, NEG_INF)
        return bias.astype(BF16), tie_seen + jnp.sum(key_fold(tie_f), axis=0, keepdims=True)

    def attn_body(c, tie_seen):
        k0 = pl.multiple_of(c * (2 * kc), 2 * kc)
        bias_0, tie_seen = chunk_bias(k0, tie_seen)
        bias_1, tie_seen = chunk_bias(k0 + kc, tie_seen)
        bias_b = jnp.concatenate([bias_0, bias_1], axis=0)
        lg_all = _dot_nt(ka_ref[pl.ds(k0, 2 * kc), :], q_rows).astype(BF16)
        lgs = [lg_all[:, h * tq:(h + 1) * tq] + bias_b for h in range(ATTN_HEADS)]
        m_prev = m_ref[...]
        m_new = jnp.maximum(m_prev, jnp.concatenate(
            [jnp.max(lg, axis=0, keepdims=True) for lg in lgs], axis=1).astype(F32))
        m_b = m_new[0:1, :].astype(BF16)
        p = jnp.concatenate([jnp.exp(lgs[h] - m_b[:, h * tq:(h + 1) * tq])
                             for h in range(ATTN_HEADS)], axis=1)
        alpha = jnp.exp(m_prev - m_new)
        acc_ref[...] = alpha[0:1, :] * acc_ref[...] + jnp.dot(
            vat_ref[:, pl.ds(k0, 2 * kc)], p, preferred_element_type=F32)
        m_ref[...] = m_new
        return tie_seen

    lax.fori_loop(0, n_pairs, attn_body, jnp.zeros((1, tq), F32))
    o_t = acc_ref[0:HEAD_DIM, :] / acc_ref[HEAD_DIM:HEAD_DIM + 1, :]
    o_ref[...] = jnp.concatenate(
        [o_t[:, h * tq:(h + 1) * tq].T for h in range(ATTN_HEADS)], axis=1)


def _dsa(qa, iq, iw, ik, ka, va_t, bsz, seq, tq=256):
    tq = min(tq, seq)
    kc = tq
    top = min(INDEX_TOPK, seq // 4)
    n_q = seq // tq
    assert seq // 16 <= 256 and (seq // kc) % 2 == 0
    row = lambda b, i: (b * n_q + i, 0)
    per_b = lambda b, i: (b, 0, 0)
    kern = functools.partial(_dsa_kernel, tq=tq, kc=kc, top=top)
    return pl.pallas_call(
        kern,
        grid=(bsz, n_q),
        in_specs=[pl.BlockSpec((tq, A_WIDTH), row),
                  pl.BlockSpec((tq, IDX_HEADS * IDX_DIM), row),
                  pl.BlockSpec((tq, LANES), row),
                  pl.BlockSpec((None, seq, IDX_DIM), per_b),
                  pl.BlockSpec((None, seq, HEAD_DIM), per_b),
                  pl.BlockSpec((_V_ROWS, seq), lambda b, i: (0, b))],
        out_specs=pl.BlockSpec((tq, A_WIDTH), row),
        out_shape=jax.ShapeDtypeStruct((bsz * seq, A_WIDTH), F32),
        scratch_shapes=[pltpu.VMEM((seq, tq), I32),
                        pltpu.VMEM((seq, tq), I16),
                        pltpu.VMEM((seq, tq), I16),
                        pltpu.VMEM((8, ATTN_HEADS * tq), F32),
                        pltpu.VMEM((_V_ROWS, ATTN_HEADS * tq), F32)],
        compiler_params=_cparams(("parallel", "arbitrary")),
        name="dsa",
    )(qa, iq, iw, ik.reshape(bsz, seq, IDX_DIM), ka.reshape(bsz, seq, HEAD_DIM), va_t)


_HALO = 8


def _bmm(a, b):
    return lax.dot_general(a.astype(BF16), b.astype(BF16), (((2,), (1,)), ((0,), (0,))),
                           preferred_element_type=F32)


def _bmm_nt(a, b):
    return lax.dot_general(a.astype(BF16), b.astype(BF16), (((2,), (2,)), ((0,), (0,))),
                           preferred_element_type=F32)


def _gdn_kernel(qkv_ref, ab_ref, z_ref, cw_ref, alog_ref, dtb_ref, gn_ref, o_ref,
                ext_ref, state_ref, *, tr):
    ti = pl.program_id(1)
    width = 3 * GDN_WIDTH
    n_c = tr // CHUNK

    @pl.when(ti == 0)
    def _():
        ext_ref[0:_HALO, :] = jnp.zeros((_HALO, width), F32)
        state_ref[...] = jnp.zeros(state_ref.shape, F32)

    ext_ref[_HALO:_HALO + tr, :] = qkv_ref[...]
    conv = cw_ref[CONV_WIDTH - 1:CONV_WIDTH, :] * ext_ref[_HALO:_HALO + tr, :]
    for j in range(CONV_WIDTH - 1):
        off = _HALO - (CONV_WIDTH - 1) + j
        conv = conv + cw_ref[j:j + 1, :] * ext_ref[off:off + tr, :]
    ext_ref[0:_HALO, :] = ext_ref[tr:tr + _HALO, :]
    qkv = _silu(conv)

    ab = ab_ref[...]
    sp = jnp.maximum(ab + dtb_ref[...], 0.0) + jnp.log1p(jnp.exp(-jnp.abs(ab + dtb_ref[...])))
    g_all = -jnp.exp(alog_ref[...]) * sp
    beta_all = 1.0 / (1.0 + jnp.exp(-ab))

    lane = lax.broadcasted_iota(I32, (CHUNK, LANES), 1)
    row_i = lax.broadcasted_iota(I32, (CHUNK, LANES), 0)
    left = lane < HEAD_DIM
    incl = (row_i >= lane % HEAD_DIM)[None]
    strict = (row_i > lane % HEAD_DIM)[None]
    left_t = (lax.broadcasted_iota(I32, (tr, LANES), 1) < HEAD_DIM)
    tri = jnp.where(lax.broadcasted_iota(I32, (CHUNK, CHUNK), 0) >= lax.broadcasted_iota(I32, (CHUNK, CHUNK), 1),
                    1.0, 0.0).astype(BF16)

    def blk(x):
        return jnp.concatenate([jnp.where(left[None], x, 0.0), jnp.where(left[None], 0.0, x)], axis=1)

    def head_scale(x, eps, scale):
        sq = x * x
        keep = left_t if x.shape[0] == tr else left
        s_l = jnp.sum(jnp.where(keep, sq, 0.0), axis=-1, keepdims=True)
        s_r = jnp.sum(jnp.where(keep, 0.0, sq), axis=-1, keepdims=True)
        return x * jnp.where(keep, lax.rsqrt(s_l * scale + eps), lax.rsqrt(s_r * scale + eps))

    gc_cols, gc_rows = [], []
    for c in range(n_c):
        g_c = g_all[c * CHUNK:(c + 1) * CHUNK, :]
        g_hi = g_c.astype(BF16)
        g_r1 = g_c - g_hi.astype(F32)
        g_mid = g_r1.astype(BF16)
        g_lo = (g_r1 - g_mid.astype(F32)).astype(BF16)
        gc = (jnp.dot(tri, g_hi, preferred_element_type=F32)
              + jnp.dot(tri, g_mid, preferred_element_type=F32)
              + jnp.dot(tri, g_lo, preferred_element_type=F32))
        gc_cols.append(gc)
        gc_rows.append(gc.T)

    n_p = GDN_HEADS // 2
    q_n = [head_scale(qkv[:, p * LANES:(p + 1) * LANES], RMS_EPS, 1.0) * (HEAD_DIM ** -0.5)
           for p in range(n_p)]
    k_n = [head_scale(qkv[:, GDN_WIDTH + p * LANES:GDN_WIDTH + (p + 1) * LANES], RMS_EPS, 1.0)
           for p in range(n_p)]

    def stack(fn):
        return jnp.stack([fn(c, p) for c in range(n_c) for p in range(n_p)], axis=0)

    def rows(c):
        return slice(c * CHUNK, (c + 1) * CHUNK)

    def pair_cols(x, c, p, base):
        return jnp.where(left, x[rows(c), base + 2 * p:base + 2 * p + 1],
                         x[rows(c), base + 2 * p + 1:base + 2 * p + 2])

    qs = stack(lambda c, p: q_n[p][rows(c), :])
    ks = stack(lambda c, p: k_n[p][rows(c), :])
    vs = stack(lambda c, p: qkv[rows(c), 2 * GDN_WIDTH + p * LANES:2 * GDN_WIDTH + (p + 1) * LANES])
    beta = stack(lambda c, p: pair_cols(beta_all, c, p, GDN_HEADS))
    gcc = stack(lambda c, p: pair_cols(gc_cols[c], 0, p, 0))
    gcr = stack(lambda c, p: jnp.concatenate(
        [gc_rows[c][2 * p:2 * p + 1, :], gc_rows[c][2 * p + 1:2 * p + 2, :]], axis=1))
    decay = jnp.where(incl, jnp.exp(jnp.where(incl, gcc - gcr, 0.0)), 0.0)
    kb = ks * beta
    k_blk = blk(ks)
    low = jnp.where(strict, _bmm_nt(kb, k_blk) * decay, 0.0)
    e_gc = jnp.exp(gcc)
    u = vs * beta
    w = kb * e_gc
    u = u - _bmm(low, blk(u))
    w = w - _bmm(low, blk(w))
    pw = low
    for _ in range(5):
        pw = _bmm(pw, blk(pw))
        u = u + _bmm(pw, blk(u))
        w = w + _bmm(pw, blk(w))
    a_intra = jnp.where(incl, _bmm_nt(qs, k_blk) * decay, 0.0)
    q_dec = qs * e_gc
    gc_last = gcc[:, CHUNK - 1:CHUNK, :]
    kd_blk = blk(ks * jnp.exp(gc_last - gcc))
    kd_blk_t = jnp.stack([kd_blk[i].T for i in range(n_c * n_p)], axis=0)
    g_last = jnp.exp(gc_last)

    gn = gn_ref[...]
    st = state_ref[...]
    for c in range(n_c):
        sl = slice(c * n_p, (c + 1) * n_p)
        v_new = blk(u[sl] - _bmm(w[sl], st))
        o = _bmm(q_dec[sl], st) + _bmm(a_intra[sl], v_new)
        st = st * g_last[sl] + _bmm(kd_blk_t[sl], v_new)
        for p in range(n_p):
            y = head_scale(o[p], RMS_EPS, 1.0 / HEAD_DIM) * gn
            o_ref[rows(c), p * LANES:(p + 1) * LANES] = y * _silu(z_ref[rows(c), p * LANES:(p + 1) * LANES])
    state_ref[...] = st


def _gdn(qkvb, ab, zb, conv_w, a_log, dt_bias, gdn_norm_g, bsz, seq, tr=256):
    tr = min(tr, seq)
    n_t = seq // tr
    row = lambda b, i: (b * n_t + i, 0)
    const = lambda b, i: (0, 0)
    alog = _pad_cols(a_log.reshape(1, GDN_HEADS).astype(F32), LANES)
    dtb = _pad_cols(dt_bias.reshape(1, GDN_HEADS).astype(F32), LANES)
    return pl.pallas_call(
        functools.partial(_gdn_kernel, tr=tr),
        grid=(bsz, n_t),
        in_specs=[pl.BlockSpec((tr, 3 * GDN_WIDTH), row),
                  pl.BlockSpec((tr, LANES), row),
                  pl.BlockSpec((tr, GDN_WIDTH), row),
                  pl.BlockSpec((CONV_WIDTH, 3 * GDN_WIDTH), const),
                  pl.BlockSpec((1, LANES), const),
                  pl.BlockSpec((1, LANES), const),
                  pl.BlockSpec((1, LANES), const)],
        out_specs=pl.BlockSpec((tr, GDN_WIDTH), row),
        out_shape=jax.ShapeDtypeStruct((bsz * seq, GDN_WIDTH), F32),
        scratch_shapes=[pltpu.VMEM((tr + _HALO, 3 * GDN_WIDTH), F32),
                        pltpu.VMEM((GDN_HEADS // 2, LANES, LANES), F32)],
        compiler_params=_cparams(("parallel", "arbitrary")),
        name="gdn",
    )(qkvb, ab, zb, conv_w.astype(F32), alog, dtb,
      jnp.tile(gdn_norm_g.reshape(1, HEAD_DIM).astype(F32), (1, LANES // HEAD_DIM)))


def _memkv_kernel(m_ref, w_ref, k_ref, v_ref):
    r = jnp.dot(m_ref[...].astype(BF16), w_ref[...], preferred_element_type=F32)
    k_ref[...] = r[:, :MEM_WIDTH].astype(BF16)
    v_ref[...] = r[:, MEM_WIDTH:].astype(BF16)


def _memkv(mem2, w_mem_kv, n_mem):
    n_rows, d_model = mem2.shape
    row = lambda i: (i, 0)
    return pl.pallas_call(
        _memkv_kernel,
        grid=(n_rows // n_mem,),
        in_specs=[pl.BlockSpec((n_mem, d_model), row),
                  pl.BlockSpec((d_model, 2 * MEM_WIDTH), lambda i: (0, 0))],
        out_specs=[pl.BlockSpec((n_mem, MEM_WIDTH), row)] * 2,
        out_shape=[jax.ShapeDtypeStruct((n_rows, MEM_WIDTH), BF16)] * 2,
        compiler_params=_cparams(("parallel",)),
        name="memkv",
    )(mem2, w_mem_kv.astype(BF16))


_ROUTE_ROWS = 128


def _post_kernel(x_ref, oa_ref, ob_ref, qc_ref, mk_ref, mv_ref, wout_ref, ag_ref, mg_ref,
                 lg_ref, lb_ref, wr_ref, br_ref, x1_ref, ids_ref, gate_ref, *, alpha):
    tm = x_ref.shape[0]
    oa = oa_ref[...]
    oa = oa * lax.rsqrt(jnp.mean(oa * oa, axis=-1, keepdims=True) + RMS_EPS) * ag_ref[...]

    qc = qc_ref[...]
    mk = mk_ref[...]
    mv = mv_ref[...]
    cols = [slice(h * HEAD_DIM, (h + 1) * HEAD_DIM) for h in range(MEM_HEADS)]
    lgs = [_dot_nt(qc[:, c], mk[:, c]) * (HEAD_DIM ** -0.5) for c in cols]
    es = [jnp.exp(lg - jnp.max(lg, axis=-1, keepdims=True)) for lg in lgs]
    ps = [e / jnp.sum(e, axis=-1, keepdims=True) for e in es]
    oc = jnp.concatenate([jnp.dot(p.astype(BF16), mv[:, c], preferred_element_type=F32)
                          for p, c in zip(ps, cols)], axis=1)
    oc = oc * lax.rsqrt(jnp.mean(oc * oc, axis=-1, keepdims=True) + RMS_EPS) * mg_ref[...]

    mix = (jnp.dot(oa.astype(BF16), wout_ref[0:A_WIDTH, :], preferred_element_type=F32)
           + jnp.dot(ob_ref[...].astype(BF16), wout_ref[A_WIDTH:A_WIDTH + GDN_WIDTH, :],
                     preferred_element_type=F32)
           + jnp.dot(oc.astype(BF16), wout_ref[A_WIDTH + GDN_WIDTH:, :], preferred_element_type=F32))
    hres = alpha * x_ref[...] + mix
    mu = jnp.mean(hres, axis=-1, keepdims=True)
    var = jnp.mean(jnp.square(hres - mu), axis=-1, keepdims=True)
    x1 = (hres - mu) * lax.rsqrt(var + LN_EPS) * lg_ref[...] + lb_ref[...]
    x1_ref[...] = x1

    lt = _dot_nt(wr_ref[...], x1) + br_ref[...]
    sub = lax.broadcasted_iota(I32, (EXPERTS_PER_GROUP, tm), 0)
    gl = lt[0:N_GROUPS, :]
    gmax = jnp.max(gl, axis=0, keepdims=True)
    gprob = jnp.exp(gl - gmax) / jnp.sum(jnp.exp(gl - gmax), axis=0, keepdims=True)
    p_grp = jnp.max(gprob, axis=0, keepdims=True)
    grp = jnp.min(jnp.where(gprob == p_grp, sub, N_GROUPS), axis=0, keepdims=True)
    el = jnp.zeros((EXPERTS_PER_GROUP, tm), F32)
    for g in range(N_GROUPS):
        r0 = N_GROUPS + g * EXPERTS_PER_GROUP
        el = el + jnp.where(grp == g, lt[r0:r0 + EXPERTS_PER_GROUP, :], 0.0)
    ee = jnp.exp(el - jnp.max(el, axis=0, keepdims=True))
    pe = ee / jnp.sum(ee, axis=0, keepdims=True)
    p1 = jnp.max(pe, axis=0, keepdims=True)
    i1 = jnp.min(jnp.where(pe == p1, sub, EXPERTS_PER_GROUP), axis=0, keepdims=True)
    rest = jnp.where(sub == i1, -1.0, pe)
    p2 = jnp.max(rest, axis=0, keepdims=True)
    i2 = jnp.min(jnp.where(rest == p2, sub, EXPERTS_PER_GROUP), axis=0, keepdims=True)
    psum = p1 + p2
    g1 = p_grp * p1 / psum
    g2 = p_grp * p2 / psum
    e1 = grp * EXPERTS_PER_GROUP + i1
    e2 = grp * EXPERTS_PER_GROUP + i2
    ids_ref[...] = jnp.where(sub == 0, e1, jnp.where(sub == 1, e2, 0))
    gate_ref[...] = jnp.where(sub == 0, g1, jnp.where(sub == 1, g2, 0.0))


def _post(x2, oa, ob, qc, mk, mv, w_out, attn_g, mem_g, ln_g, ln_b, w_group, b_group,
          w_router, b_router, seq, n_mem, alpha, tm=256):
    n_tok, d_model = x2.shape
    n_t = seq // tm
    row = lambda i: (i, 0)
    const = lambda i: (0, 0)
    per_b = lambda i: (i // n_t, 0, 0)
    lane = lambda i: (0, i)
    bsz = n_tok // seq
    wr = jnp.pad(jnp.concatenate([w_group, w_router], axis=1).T,
                 ((0, _ROUTE_ROWS - N_GROUPS - N_EXPERTS), (0, 0))).astype(BF16)
    br = jnp.pad(jnp.concatenate([b_group, b_router]),
                 (0, _ROUTE_ROWS - N_GROUPS - N_EXPERTS)).reshape(_ROUTE_ROWS, 1).astype(F32)
    vec = lambda v: v.reshape(1, -1).astype(F32)
    return pl.pallas_call(
        functools.partial(_post_kernel, alpha=alpha),
        grid=(n_tok // tm,),
        in_specs=[pl.BlockSpec((tm, d_model), row),
                  pl.BlockSpec((tm, A_WIDTH), row),
                  pl.BlockSpec((tm, GDN_WIDTH), row),
                  pl.BlockSpec((tm, MEM_WIDTH), row),
                  pl.BlockSpec((None, n_mem, MEM_WIDTH), per_b),
                  pl.BlockSpec((None, n_mem, MEM_WIDTH), per_b),
                  pl.BlockSpec(w_out.shape, const),
                  pl.BlockSpec((1, A_WIDTH), const),
                  pl.BlockSpec((1, MEM_WIDTH), const),
                  pl.BlockSpec((1, d_model), const),
                  pl.BlockSpec((1, d_model), const),
                  pl.BlockSpec((_ROUTE_ROWS, d_model), const),
                  pl.BlockSpec((_ROUTE_ROWS, 1), const)],
        out_specs=[pl.BlockSpec((tm, d_model), row),
                   pl.BlockSpec((EXPERTS_PER_GROUP, tm), lane),
                   pl.BlockSpec((EXPERTS_PER_GROUP, tm), lane)],
        out_shape=[jax.ShapeDtypeStruct((n_tok, d_model), F32),
                   jax.ShapeDtypeStruct((EXPERTS_PER_GROUP, n_tok), I32),
                   jax.ShapeDtypeStruct((EXPERTS_PER_GROUP, n_tok), F32)],
        compiler_params=_cparams(("parallel",)),
        name="post",
    )(x2, oa, ob, qc, mk.reshape(bsz, n_mem, MEM_WIDTH), mv.reshape(bsz, n_mem, MEM_WIDTH),
      w_out.astype(BF16), vec(attn_g), vec(mem_g), vec(ln_g), vec(ln_b), wr, br)


def _rank_kernel(ids_ref, rank_ref, cnt_ref, carry_ref):
    i = pl.program_id(0)
    tm = ids_ref.shape[1]

    @pl.when(i == 0)
    def _():
        carry_ref[...] = jnp.zeros(carry_ref.shape, F32)

    ids = ids_ref[...]
    eio = lax.broadcasted_iota(I32, (N_EXPERTS, tm), 0)
    oh0 = jnp.where(eio == ids[0:1, :], 1.0, 0.0)
    oh1 = jnp.where(eio == ids[1:2, :], 1.0, 0.0)
    cnt = oh0 + oh1
    before = (lax.broadcasted_iota(I32, (tm, tm), 0) < lax.broadcasted_iota(I32, (tm, tm), 1))
    prefix = jnp.dot(cnt.astype(BF16), jnp.where(before, 1.0, 0.0).astype(BF16),
                     preferred_element_type=F32) + carry_ref[:, 0:1]
    r0 = jnp.sum(oh0 * prefix, axis=0, keepdims=True)
    r1 = jnp.sum(oh1 * prefix, axis=0, keepdims=True)
    sub = lax.broadcasted_iota(I32, (EXPERTS_PER_GROUP, tm), 0)
    rank_ref[...] = jnp.where(sub == 0, r0, jnp.where(sub == 1, r1, 0.0)).astype(I32)
    carry_ref[...] = carry_ref[...] + jnp.sum(cnt, axis=1, keepdims=True)
    cnt_ref[...] = carry_ref[...].astype(I32)


def _rank(ids, tm=512):
    n_tok = ids.shape[1]
    lane = lambda i: (0, i)
    return pl.pallas_call(
        _rank_kernel,
        grid=(n_tok // tm,),
        in_specs=[pl.BlockSpec((EXPERTS_PER_GROUP, tm), lane)],
        out_specs=[pl.BlockSpec((EXPERTS_PER_GROUP, tm), lane),
                   pl.BlockSpec((N_EXPERTS, LANES), lambda i: (0, 0))],
        out_shape=[jax.ShapeDtypeStruct((EXPERTS_PER_GROUP, n_tok), I32),
                   jax.ShapeDtypeStruct((N_EXPERTS, LANES), I32)],
        scratch_shapes=[pltpu.VMEM((N_EXPERTS, LANES), F32)],
        compiler_params=_cparams(("arbitrary",)),
        name="rank",
    )(ids)


def _dest_kernel(ids_ref, rank_ref, ps_ref, dest_ref):
    ids = ids_ref[...]
    tm = ids.shape[1]
    eio = lax.broadcasted_iota(I32, (N_EXPERTS, tm), 0)
    ps = ps_ref[...]
    d0 = jnp.sum(jnp.where(eio == ids[0:1, :], ps, 0.0), axis=0, keepdims=True)
    d1 = jnp.sum(jnp.where(eio == ids[1:2, :], ps, 0.0), axis=0, keepdims=True)
    sub = lax.broadcasted_iota(I32, (EXPERTS_PER_GROUP, tm), 0)
    dest_ref[...] = rank_ref[...] + jnp.where(sub == 0, d0, jnp.where(sub == 1, d1, 0.0)).astype(I32)


def _dest(ids, rank, pad_start, tm=2048):
    n_tok = ids.shape[1]
    tm = min(tm, n_tok)
    lane = lambda i: (0, i)
    return pl.pallas_call(
        _dest_kernel,
        grid=(n_tok // tm,),
        in_specs=[pl.BlockSpec((EXPERTS_PER_GROUP, tm), lane),
                  pl.BlockSpec((EXPERTS_PER_GROUP, tm), lane),
                  pl.BlockSpec((N_EXPERTS, 1), lambda i: (0, 0))],
        out_specs=pl.BlockSpec((EXPERTS_PER_GROUP, tm), lane),
        out_shape=jax.ShapeDtypeStruct((EXPERTS_PER_GROUP, n_tok), I32),
        compiler_params=_cparams(("parallel",)),
        name="dest",
    )(ids, rank, pad_start.astype(F32).reshape(N_EXPERTS, 1))


def _sc_mesh():
    return plsc.VectorSubcoreMesh(core_axis_name="c", subcore_axis_name="s",
                                  num_cores=SC_CORES, num_subcores=SC_SUBCORES)


def _sc_scatter_rows(x1, dest, pad_rows, cap):
    n_tok, d = x1.shape
    n_pad = pad_rows.shape[0]
    n_workers = SC_CORES * SC_SUBCORES
    tok_per_worker = n_tok // n_workers
    pad_per_worker = n_pad // n_workers
    n_win = tok_per_worker // SC_ROWS
    assert n_tok % (n_workers * SC_ROWS) == 0 and n_pad % (n_workers * SC_ROWS) == 0 and TOP_K == 2
    zero_rows = jnp.zeros((SC_ROWS, d), x1.dtype)

    @functools.partial(
        pl.kernel, mesh=_sc_mesh(), out_type=jax.ShapeDtypeStruct((cap, d), x1.dtype),
        scratch_types=[pltpu.VMEM((tok_per_worker,), I32), pltpu.VMEM((tok_per_worker,), I32),
                       pltpu.VMEM((pad_per_worker,), I32), pltpu.VMEM((2, SC_ROWS, d), x1.dtype),
                       pltpu.SemaphoreType.DMA((2,)), pltpu.SemaphoreType.DMA((2,))],
        name="sc_scatter")
    def scatter(x_hbm, dest_hbm, pad_hbm, zero_hbm, out_hbm, idx0_v, idx1_v, pad_v, rows_v, lsem, ssem):
        worker = lax.axis_index("s") * SC_CORES + lax.axis_index("c")
        base = worker * tok_per_worker
        pltpu.sync_copy(dest_hbm.at[pl.ds(base, tok_per_worker)], idx0_v)
        pltpu.sync_copy(dest_hbm.at[pl.ds(n_tok + base, tok_per_worker)], idx1_v)
        pltpu.sync_copy(pad_hbm.at[pl.ds(worker * pad_per_worker, pad_per_worker)], pad_v)

        def load(i, slot):
            return pltpu.make_async_copy(x_hbm.at[pl.ds(base + i * SC_ROWS, SC_ROWS)],
                                         rows_v.at[slot], lsem.at[slot])

        def store(idx_v, i, slot):
            return pltpu.make_async_copy(rows_v.at[slot],
                                         out_hbm.at[idx_v.at[pl.ds(i * SC_ROWS, SC_ROWS)]], ssem.at[slot])

        load(0, 0).start()

        @pl.loop(0, n_win)
        def _(i):
            slot = i % 2
            load(i, slot).wait()
            store(idx0_v, i, slot).start()
            store(idx1_v, i, slot).start()

            @pl.when(i >= 1)
            def _():
                store(idx0_v, i - 1, 1 - slot).wait()
                store(idx1_v, i - 1, 1 - slot).wait()

            @pl.when(i + 1 < n_win)
            def _():
                load(i + 1, 1 - slot).start()

        last = (n_win - 1) % 2
        store(idx0_v, n_win - 1, last).wait()
        store(idx1_v, n_win - 1, last).wait()

        pltpu.sync_copy(zero_hbm, rows_v.at[0])

        @pl.loop(0, pad_per_worker // SC_ROWS)
        def _(i):
            pltpu.async_copy(rows_v.at[0], out_hbm.at[pad_v.at[pl.ds(i * SC_ROWS, SC_ROWS)]],
                             ssem.at[0]).wait()

    return scatter(x1, dest, pad_rows, zero_rows)


def _mlp_kernel(be_ref, nused_ref, x_ref, wg_ref, wu_ref, wd_ref, y_ref):
    i = pl.program_id(0)

    @pl.when(i < nused_ref[0])
    def _():
        xb = x_ref[...].astype(BF16)
        hg = jnp.dot(xb, wg_ref[...].astype(BF16), preferred_element_type=F32)
        hu = jnp.dot(xb, wu_ref[...].astype(BF16), preferred_element_type=F32)
        y_ref[...] = jnp.dot((_silu(hg) * hu).astype(BF16), wd_ref[...].astype(BF16),
                             preferred_element_type=F32)

    @pl.when(i >= nused_ref[0])
    def _():
        y_ref[...] = jnp.zeros(y_ref.shape, F32)


def _mlp(blk_expert, n_used, xbuf, w_gate, w_up, w_down):
    cap, d_model = xbuf.shape
    d_exp = w_gate.shape[2]
    blk = lambda i, be, nu: (i, 0)
    used_blk = lambda i, be, nu: (jnp.minimum(i, nu[0] - 1), 0)
    wsel = lambda i, be, nu: (be[i], 0, 0)
    return pl.pallas_call(
        _mlp_kernel,
        grid_spec=pltpu.PrefetchScalarGridSpec(
            num_scalar_prefetch=2,
            grid=(cap // MOE_BLOCK,),
            in_specs=[pl.BlockSpec((MOE_BLOCK, d_model), used_blk),
                      pl.BlockSpec((None, d_model, d_exp), wsel),
                      pl.BlockSpec((None, d_model, d_exp), wsel),
                      pl.BlockSpec((None, d_exp, d_model), wsel)],
            out_specs=pl.BlockSpec((MOE_BLOCK, d_model), blk)),
        out_shape=jax.ShapeDtypeStruct((cap, d_model), F32),
        compiler_params=_cparams(("arbitrary",)),
        name="mlp",
    )(blk_expert, n_used, xbuf, w_gate, w_up, w_down)


def _sc_gather_rows(table, idx):
    n = idx.shape[0]
    d = table.shape[1]
    n_workers = SC_CORES * SC_SUBCORES
    per_worker = n // n_workers
    n_win = per_worker // SC_ROWS
    assert n % (n_workers * SC_ROWS) == 0

    @functools.partial(
        pl.kernel, mesh=_sc_mesh(), out_type=jax.ShapeDtypeStruct((n, d), table.dtype),
        scratch_types=[pltpu.VMEM((per_worker,), I32), pltpu.VMEM((2, SC_ROWS, d), table.dtype),
                       pltpu.SemaphoreType.DMA((2,)), pltpu.SemaphoreType.DMA((2,))],
        name="sc_gather")
    def gather(table_hbm, idx_hbm, out_hbm, idx_v, rows_v, gsem, wsem):
        worker = lax.axis_index("s") * SC_CORES + lax.axis_index("c")
        base = worker * per_worker
        pltpu.sync_copy(idx_hbm.at[pl.ds(base, per_worker)], idx_v)

        def fetch(i, slot):
            return pltpu.make_async_copy(table_hbm.at[idx_v.at[pl.ds(i * SC_ROWS, SC_ROWS)]],
                                         rows_v.at[slot], gsem.at[slot])

        def write(i, slot):
            return pltpu.make_async_copy(rows_v.at[slot],
                                         out_hbm.at[pl.ds(base + i * SC_ROWS, SC_ROWS)], wsem.at[slot])

        fetch(0, 0).start()

        @pl.loop(0, n_win)
        def _(i):
            slot = i % 2
            fetch(i, slot).wait()
            write(i, slot).start()

            @pl.when(i >= 1)
            def _():
                write(i - 1, 1 - slot).wait()

            @pl.when(i + 1 < n_win)
            def _():
                fetch(i + 1, 1 - slot).start()

        write(n_win - 1, (n_win - 1) % 2).wait()

    return gather(table, idx)


def _combine_kernel(x1_ref, y0_ref, y1_ref, gt_ref, lg_ref, lb_ref, o_ref, *, alpha):
    gt = gt_ref[...]
    ffn = y0_ref[...] * gt[:, 0:1] + y1_ref[...] * gt[:, 1:2]
    hres = alpha * x1_ref[...] + ffn
    mu = jnp.mean(hres, axis=-1, keepdims=True)
    var = jnp.mean(jnp.square(hres - mu), axis=-1, keepdims=True)
    o_ref[...] = (hres - mu) * lax.rsqrt(var + LN_EPS) * lg_ref[...] + lb_ref[...]


def _combine(dest, x1, gates_t, ln_g, ln_b, ybuf, alpha, tc=512):
    n_tok, d_model = x1.shape
    n_tiles = n_tok // tc
    yrows = _sc_gather_rows(ybuf, dest.reshape(-1))
    row = lambda i: (i, 0)
    const = lambda i: (0, 0)
    vec = lambda v: v.reshape(1, -1).astype(F32)
    return pl.pallas_call(
        functools.partial(_combine_kernel, alpha=alpha),
        grid=(n_tiles,),
        in_specs=[pl.BlockSpec((tc, d_model), row),
                  pl.BlockSpec((tc, d_model), row),
                  pl.BlockSpec((tc, d_model), lambda i: (i + n_tiles, 0)),
                  pl.BlockSpec((tc, EXPERTS_PER_GROUP), row),
                  pl.BlockSpec((1, d_model), const),
                  pl.BlockSpec((1, d_model), const)],
        out_specs=pl.BlockSpec((tc, d_model), row),
        out_shape=jax.ShapeDtypeStruct((n_tok, d_model), F32),
        compiler_params=_cparams(("parallel",)),
        name="combine",
    )(x1, yrows, yrows, gates_t, vec(ln_g), vec(ln_b))


def _moe(x1, ids, gates, w_gate, w_up, w_down, ln_g, ln_b, alpha):
    n_tok, d_model = x1.shape
    rank, counts = _rank(ids)
    counts = counts[:, 0]
    padded = (counts + MOE_BLOCK - 1) // MOE_BLOCK * MOE_BLOCK
    pad_ends = jnp.cumsum(padded)
    pad_start = (pad_ends - padded).astype(I32)
    n_asg = n_tok * TOP_K
    cap = (n_asg + MOE_BLOCK - 1) // MOE_BLOCK * MOE_BLOCK + N_EXPERTS * MOE_BLOCK
    n_blk = cap // MOE_BLOCK
    blk_pos = jnp.arange(n_blk, dtype=I32) * MOE_BLOCK
    blk_expert = jnp.minimum(
        jnp.sum((pad_ends[None, :] <= blk_pos[:, None]).astype(I32), axis=1), N_EXPERTS - 1)
    n_used = (pad_ends[-1:] // MOE_BLOCK).astype(I32)
    dest = _dest(ids, rank, pad_start)[0:TOP_K]
    slot = jnp.arange(MOE_BLOCK, dtype=I32)[None, :]
    spare = cap - 1 - (jnp.arange(N_EXPERTS * MOE_BLOCK, dtype=I32).reshape(N_EXPERTS, MOE_BLOCK)
                       % N_EXPERTS)
    pad_rows = jnp.where(slot < (padded - counts)[:, None], (pad_start + counts)[:, None] + slot,
                         spare).reshape(-1)
    xbuf = _sc_scatter_rows(x1, dest.reshape(-1), pad_rows, cap)
    ybuf = _mlp(blk_expert, n_used, xbuf, w_gate, w_up, w_down)
    return _combine(dest, x1, gates.T, ln_g, ln_b, ybuf, alpha)


def kernel(x, mem, w_in, kv_norm_g, w_k_up, w_v_up, conv_w, A_log, dt_bias, gdn_norm_g, attn_norm_g, mem_norm_g, w_mem_kv, w_out, ln1_g, ln1_b, w_group, b_group, w_router, b_router, w_gate, w_up, w_down, ln2_g, ln2_b):
    bsz, seq, d_model = x.shape
    n_mem = mem.shape[1]
    depth = w_in.shape[0]
    alpha = (2 * depth) ** 0.25
    inv_freq = 1.0 / (ROPE_THETA ** (jnp.arange(0, HEAD_DIM, 2, dtype=F32) / HEAD_DIM))
    ang = jnp.arange(seq, dtype=F32)[:, None] * inv_freq[None, :]
    cos128 = jnp.tile(jnp.cos(ang), (1, LANES // (HEAD_DIM // 2)))
    sin128 = jnp.tile(jnp.sin(ang), (1, LANES // (HEAD_DIM // 2)))
    x2 = x.reshape(bsz * seq, d_model)
    mem2 = mem.reshape(bsz * n_mem, d_model)
    for l in range(depth):
        qa, iq, ik, ka, va, iw, qkvb, zb, ab, qc = _proj(
            x2, w_in[l], w_k_up[l], w_v_up[l], kv_norm_g[l], cos128, sin128, seq)
        oa = _dsa(qa, iq, iw, ik, ka, va, bsz, seq)
        ob = _gdn(qkvb, ab, zb, conv_w[l], A_log[l], dt_bias[l], gdn_norm_g[l], bsz, seq)
        mk, mv = _memkv(mem2, w_mem_kv[l], n_mem)
        x1, ids, gates = _post(x2, oa, ob, qc, mk, mv, w_out[l], attn_norm_g[l], mem_norm_g[l],
                               ln1_g[l], ln1_b[l], w_group[l], b_group[l], w_router[l],
                               b_router[l], seq, n_mem, alpha)
        x2 = _moe(x1, ids, gates, w_gate[l], w_up[l], w_down[l], ln2_g[l], ln2_b[l], alpha)
    return x2.reshape(bsz, seq, d_model)
```

```python
import functools

import numpy as np
import jax
import jax.numpy as jnp
from jax import lax
from jax.experimental import pallas as pl
from jax.experimental.pallas import tpu as pltpu
from jax.experimental.pallas import tpu_sc as plsc

F32 = jnp.float32
BF16 = jnp.bfloat16
I32 = jnp.int32
I16 = jnp.int16

HEAD_DIM = 64
ATTN_HEADS = 6
A_WIDTH = ATTN_HEADS * HEAD_DIM
KV_RANK = 128
IDX_HEADS = 4
IDX_DIM = 64
INDEX_TOPK = 256
GDN_HEADS = 6
GDN_WIDTH = GDN_HEADS * HEAD_DIM
CONV_WIDTH = 4
CHUNK = 64
MEM_HEADS = 4
MEM_WIDTH = MEM_HEADS * HEAD_DIM
SPLIT_SIZES = (A_WIDTH, KV_RANK, IDX_HEADS * IDX_DIM, IDX_DIM, IDX_HEADS,
               GDN_WIDTH, GDN_WIDTH, GDN_WIDTH, GDN_WIDTH, GDN_HEADS, GDN_HEADS,
               MEM_WIDTH)
ROPE_THETA = 10000.0
N_GROUPS = 8
EXPERTS_PER_GROUP = 8
N_EXPERTS = N_GROUPS * EXPERTS_PER_GROUP
TOP_K = 2
MOE_BLOCK = 256
LN_EPS = 1e-5
RMS_EPS = 1e-6
NEG_INF = -1e30
INT_MIN = -2 ** 31
I16_MIN = -2 ** 15

LANES = 128
SC_CORES = 2
SC_SUBCORES = 16
SC_ROWS = 32
VMEM_LIMIT = 56 * 1024 * 1024


def _cparams(sem):
    return pltpu.CompilerParams(dimension_semantics=sem, vmem_limit_bytes=VMEM_LIMIT)


def _dot(a, b):
    return jnp.dot(a.astype(BF16), b.astype(BF16), preferred_element_type=F32)


def _dot_nt(a, b):
    return lax.dot_general(a.astype(BF16), b.astype(BF16), (((1,), (1,)), ((), ())),
                           preferred_element_type=F32)


def _silu(t):
    return t * (1.0 / (1.0 + jnp.exp(-t)))


_P_QA = (0, 384)
_P_IQ = (384, 640)
_P_CKV = (640, 768)
_P_IK = (768, 896)
_P_IW = (896, 1024)
_P_QKVB = (1024, 2176)
_P_ZB = (2176, 2560)
_P_AB = (2560, 2688)
_P_QC = (2688, 2944)
_P_TOTAL = 2944
_V_ROWS = HEAD_DIM + 16


def _rot_cols(w, n_heads):
    k = w.shape[0]
    w4 = w.reshape(k, n_heads, 2, HEAD_DIM // 2)
    return jnp.concatenate([-w4[:, :, 1:2], w4[:, :, 0:1]], axis=2).reshape(k, n_heads * HEAD_DIM)


def _pad_cols(w, width):
    return jnp.pad(w, ((0, 0), (0, width - w.shape[1])))


def _proj_kernel(x_ref, w_ref, w2_ref, cos_ref, sin_ref, kvg_ref,
                 qa_ref, iq_ref, ik_ref, ka_ref, va_ref, iw_ref, qkvb_ref, zb_ref, ab_ref, qc_ref):
    xb = x_ref[...].astype(BF16)

    def mm(slab):
        return jnp.dot(xb, w_ref[:, slab[0]:slab[1]], preferred_element_type=F32)

    cos = cos_ref[...]
    sin = sin_ref[...]
    cos3 = jnp.concatenate([cos] * 3, axis=1)
    sin3 = jnp.concatenate([sin] * 3, axis=1)
    cos2 = jnp.concatenate([cos] * 2, axis=1)
    sin2 = jnp.concatenate([sin] * 2, axis=1)
    cos64 = cos[:, :HEAD_DIM]
    sin64 = sin[:, :HEAD_DIM]

    def rotate_half(t):
        slabs = []
        for j in range(t.shape[1] // LANES):
            ts = t[:, j * LANES:(j + 1) * LANES]
            first = lax.broadcasted_iota(I32, ts.shape, 1) % HEAD_DIM < HEAD_DIM // 2
            slabs.append(jnp.where(first, -pltpu.roll(ts, LANES - HEAD_DIM // 2, 1),
                                   pltpu.roll(ts, HEAD_DIM // 2, 1)))
        return jnp.concatenate(slabs, axis=1)

    qa = mm(_P_QA)
    qa_ref[...] = (qa * cos3 + rotate_half(qa) * sin3).astype(BF16)
    iq = mm(_P_IQ)
    iq_ref[...] = (iq * cos2 + rotate_half(iq) * sin2).astype(BF16)

    ckv = mm(_P_CKV)
    cn = ckv * lax.rsqrt(jnp.mean(ckv * ckv, axis=-1, keepdims=True) + RMS_EPS) * kvg_ref[...]
    r = jnp.dot(cn.astype(BF16), w2_ref[...], preferred_element_type=F32)
    ka_ref[...] = (r[:, 0:64] * cos64 + r[:, 64:128] * sin64).astype(BF16)
    va_ref[...] = jnp.concatenate(
        [r[:, 128:256].T[0:HEAD_DIM, :], jnp.ones((_V_ROWS - HEAD_DIM, r.shape[0]), F32)],
        axis=0).astype(BF16)

    ikk = mm(_P_IK)
    ik_ref[...] = (ikk[:, 0:64] * cos64 + ikk[:, 64:128] * sin64).astype(BF16)
    iw_ref[...] = mm(_P_IW) * (IDX_HEADS ** -0.5 * IDX_DIM ** -0.5)
    qkvb_ref[...] = mm(_P_QKVB)
    zb_ref[...] = mm(_P_ZB)
    ab_ref[...] = mm(_P_AB)
    qc_ref[...] = mm(_P_QC).astype(BF16)


def _proj(x2, w_in, w_k_up, w_v_up, kv_norm_g, cos128, sin128, seq, tm=256):
    n_tok, d_model = x2.shape
    offs = np.cumsum(SPLIT_SIZES)[:-1].tolist()
    (w_qa, w_ckv, w_iq, w_ik, w_iw, w_qb, w_kb, w_vb, w_zb, w_a, w_b, w_qc) = jnp.split(w_in, offs, axis=1)
    w1 = jnp.concatenate([
        w_qa, w_iq, w_ckv,
        w_ik, _rot_cols(w_ik, 1), _pad_cols(w_iw, LANES),
        w_qb, w_kb, w_vb, w_zb, _pad_cols(jnp.concatenate([w_a, w_b], axis=1), LANES), w_qc,
    ], axis=1).astype(BF16)
    assert w1.shape[1] == _P_TOTAL
    w2 = _pad_cols(jnp.concatenate([w_k_up, _rot_cols(w_k_up, 1), w_v_up], axis=1), 2 * LANES).astype(BF16)
    n_pos = seq // tm
    row = lambda i: (i, 0)
    const = lambda i: (0, 0)
    pos = lambda i: (i % n_pos, 0)
    outs = [(A_WIDTH, BF16), (IDX_HEADS * IDX_DIM, BF16), (IDX_DIM, BF16), (HEAD_DIM, BF16),
            None, (LANES, F32), (3 * GDN_WIDTH, F32), (GDN_WIDTH, F32), (LANES, F32),
            (MEM_WIDTH, BF16)]
    out_specs = [pl.BlockSpec((tm, o[0]), row) if o else pl.BlockSpec((_V_ROWS, tm), lambda i: (0, i))
                 for o in outs]
    out_shape = [jax.ShapeDtypeStruct((n_tok, o[0]), o[1]) if o
                 else jax.ShapeDtypeStruct((_V_ROWS, n_tok), BF16) for o in outs]
    return pl.pallas_call(
        _proj_kernel,
        grid=(n_tok // tm,),
        in_specs=[pl.BlockSpec((tm, d_model), row),
                  pl.BlockSpec(w1.shape, const),
                  pl.BlockSpec(w2.shape, const),
                  pl.BlockSpec((tm, LANES), pos),
                  pl.BlockSpec((tm, LANES), pos),
                  pl.BlockSpec((1, KV_RANK), const)],
        out_specs=out_specs,
        out_shape=out_shape,
        compiler_params=_cparams(("parallel",)),
        name="proj",
    )(x2, w1, w2, cos128, sin128, kv_norm_g.reshape(1, KV_RANK).astype(F32))


def _dsa_kernel(qa_ref, iq_ref, iw_ref, ik_ref, ka_ref, vat_ref, o_ref,
                key_ref, hi_ref, lo_ref, m_ref, acc_ref, *, tq, kc, top):
    qi = pl.program_id(1)
    row0 = qi * tq
    n_kc = (row0 + tq + kc - 1) // kc
    qpos = row0 + lax.broadcasted_iota(I32, (1, tq), 1)

    def key_fold(v):
        return jnp.sum(v.reshape(kc // 8, 8, tq), axis=0)

    def head_rows(x, n_heads, width):
        return jnp.concatenate([x[:, h * width:(h + 1) * width] for h in range(n_heads)], axis=0)

    iq_rows = head_rows(iq_ref[...], IDX_HEADS, IDX_DIM)
    iw_t = iw_ref[...].T

    def score_body(c, carry):
        k0 = pl.multiple_of(c * kc, kc)
        d = _dot_nt(ik_ref[pl.ds(k0, kc), :], iq_rows)
        s = jnp.zeros((kc, tq), F32)
        for h in range(IDX_HEADS):
            s = s + iw_t[h:h + 1, :] * jnp.maximum(d[:, h * tq:(h + 1) * tq], 0.0)
        kidx = k0 + lax.broadcasted_iota(I32, (kc, tq), 0)
        s = jnp.where(s == 0.0, 0.0, s)
        s = jnp.where(kidx <= qpos, s, NEG_INF)
        bits = pltpu.bitcast(s, I32)
        key = jnp.where(bits >= 0, bits, bits ^ 0x7FFFFFFF)
        key_ref[pl.ds(k0, kc), :] = key
        hi_ref[pl.ds(k0, kc), :] = lax.shift_right_arithmetic(key, 16).astype(I16)
        lo_ref[pl.ds(k0, kc), :] = ((key & 0xFFFF) + I16_MIN).astype(I16)
        return carry

    lax.fori_loop(0, n_kc, score_body, 0)

    k_eff = jnp.minimum(top, qpos + 1).astype(F32)

    @pl.when(n_kc % 2 == 1)
    def _():
        k_pad = pl.multiple_of(n_kc * kc, kc)
        key_ref[pl.ds(k_pad, kc), :] = jnp.full((kc, tq), INT_MIN, I32)
        hi_ref[pl.ds(k_pad, kc), :] = jnp.full((kc, tq), I16_MIN, I16)
        lo_ref[pl.ds(k_pad, kc), :] = jnp.full((kc, tq), I16_MIN, I16)

    n_pairs = (n_kc + 1) // 2
    one_b = jnp.ones((), BF16)
    zero_b = jnp.zeros((), BF16)

    def count16(ref, pred):
        def body(c, acc):
            k0 = pl.multiple_of(c * (2 * kc), 2 * kc)
            for j in range(2):
                hit = jnp.where(pred(ref[pl.ds(k0 + j * kc, kc), :]), one_b, zero_b)
                hit = hit.reshape(kc // 16, 16, tq)
                parts = [hit[r] for r in range(kc // 16)]
                while len(parts) > 1:
                    parts = [a + b for a, b in zip(parts[0::2], parts[1::2])]
                acc = acc + parts[0]
            return acc
        acc = lax.fori_loop(0, n_pairs, body, jnp.zeros((16, tq), BF16))
        return jnp.sum(acc.astype(F32), axis=0, keepdims=True)

    def search16(ref, start, n_bits, k_want):
        def bit_body(i, t):
            cand = t + lax.shift_left(jnp.int32(1), n_bits - 1 - i)
            c16 = cand.astype(I16)
            return jnp.where(count16(ref, lambda v: v >= c16) >= k_want, cand, t)
        return lax.fori_loop(0, n_bits, bit_body, start)

    hi0 = jnp.where(count16(hi_ref, lambda v: v >= 0) >= k_eff, 0, I16_MIN).astype(I32)
    thr_hi = search16(hi_ref, hi0, 15, k_eff)
    thr_hi16 = thr_hi.astype(I16)
    k_low = k_eff - count16(hi_ref, lambda v: v > thr_hi16)

    def band_body(c, carry):
        k0 = pl.multiple_of(c * kc, kc)
        lo_ref[pl.ds(k0, kc), :] = jnp.where(hi_ref[pl.ds(k0, kc), :] == thr_hi16,
                                             lo_ref[pl.ds(k0, kc), :], I16_MIN)
        return carry

    lax.fori_loop(0, n_kc, band_body, 0)
    thr_lo = search16(lo_ref, jnp.full((1, tq), I16_MIN, I32), 16, k_low)
    thr_lo16 = thr_lo.astype(I16)
    thr = thr_hi * 65536 + (thr_lo - I16_MIN)
    n_tie = k_low - count16(lo_ref, lambda v: v > thr_lo16)

    q_rows = head_rows((qa_ref[...].astype(F32) * (HEAD_DIM ** -0.5)).astype(BF16),
                       ATTN_HEADS, HEAD_DIM)
    m_ref[...] = jnp.full(m_ref.shape, NEG_INF, F32)
    acc_ref[...] = jnp.zeros(acc_ref.shape, F32)
    earlier = jnp.where(lax.broadcasted_iota(I32, (kc, kc), 1) < lax.broadcasted_iota(I32, (kc, kc), 0),
                        1.0, 0.0).astype(BF16)

    def chunk_bias(k0, tie_seen):
        kk = key_ref[pl.ds(k0, kc), :]
        kidx = k0 + lax.broadcasted_iota(I32, (kc, tq), 0)
        tie = kk == thr
        tie_f = jnp.where(tie, 1.0, 0.0)
        tie_rank = jnp.dot(earlier, tie_f.astype(BF16), preferred_element_type=F32) + tie_seen
        bias = jnp.where(kk > thr, 0.0,
                         jnp.where(tie, jnp.where(tie_rank < n_tie, 0.0, NEG_INF), NEG_INF))
        bias = jnp.where(kidx <= qpos, bias, NEG_INF)
        return bias.astype(BF16), tie_seen + jnp.sum(key_fold(tie_f), axis=0, keepdims=True)

    def attn_body(c, tie_seen):
        k0 = pl.multiple_of(c * (2 * kc), 2 * kc)
        bias_0, tie_seen = chunk_bias(k0, tie_seen)
        bias_1, tie_seen = chunk_bias(k0 + kc, tie_seen)
        bias_b = jnp.concatenate([bias_0, bias_1], axis=0)
        lg_all = _dot_nt(ka_ref[pl.ds(k0, 2 * kc), :], q_rows).astype(BF16)
        lgs = [lg_all[:, h * tq:(h + 1) * tq] + bias_b for h in range(ATTN_HEADS)]
        m_prev = m_ref[...]
        m_new = jnp.maximum(m_prev, jnp.concatenate(
            [jnp.max(lg, axis=0, keepdims=True) for lg in lgs], axis=1).astype(F32))
        m_b = m_new[0:1, :].astype(BF16)
        p = jnp.concatenate([jnp.exp(lgs[h] - m_b[:, h * tq:(h + 1) * tq])
                             for h in range(ATTN_HEADS)], axis=1)
        alpha = jnp.exp(m_prev - m_new)
        acc_ref[...] = alpha[0:1, :] * acc_ref[...] + jnp.dot(
            vat_ref[:, pl.ds(k0, 2 * kc)], p, preferred_element_type=F32)
        m_ref[...] = m_new
        return tie_seen

    lax.fori_loop(0, n_pairs, attn_body, jnp.zeros((1, tq), F32))
    o_t = acc_ref[0:HEAD_DIM, :] / acc_ref[HEAD_DIM:HEAD_DIM + 1, :]
    o_ref[...] = jnp.concatenate(
        [o_t[:, h * tq:(h + 1) * tq].T for h in range(ATTN_HEADS)], axis=1)


def _dsa(qa, iq, iw, ik, ka, va_t, bsz, seq, tq=256):
    tq = min(tq, seq)
    kc = tq
    top = min(INDEX_TOPK, seq // 4)
    n_q = seq // tq
    assert seq // 16 <= 256 and (seq // kc) % 2 == 0
    row = lambda b, i: (b * n_q + i, 0)
    per_b = lambda b, i: (b, 0, 0)
    kern = functools.partial(_dsa_kernel, tq=tq, kc=kc, top=top)
    return pl.pallas_call(
        kern,
        grid=(bsz, n_q),
        in_specs=[pl.BlockSpec((tq, A_WIDTH), row),
                  pl.BlockSpec((tq, IDX_HEADS * IDX_DIM), row),
                  pl.BlockSpec((tq, LANES), row),
                  pl.BlockSpec((None, seq, IDX_DIM), per_b),
                  pl.BlockSpec((None, seq, HEAD_DIM), per_b),
                  pl.BlockSpec((_V_ROWS, seq), lambda b, i: (0, b))],
        out_specs=pl.BlockSpec((tq, A_WIDTH), row),
        out_shape=jax.ShapeDtypeStruct((bsz * seq, A_WIDTH), F32),
        scratch_shapes=[pltpu.VMEM((seq, tq), I32),
                        pltpu.VMEM((seq, tq), I16),
                        pltpu.VMEM((seq, tq), I16),
                        pltpu.VMEM((8, ATTN_HEADS * tq), F32),
                        pltpu.VMEM((_V_ROWS, ATTN_HEADS * tq), F32)],
        compiler_params=_cparams(("parallel", "arbitrary")),
        name="dsa",
    )(qa, iq, iw, ik.reshape(bsz, seq, IDX_DIM), ka.reshape(bsz, seq, HEAD_DIM), va_t)


_HALO = 8


def _bmm(a, b):
    return lax.dot_general(a.astype(BF16), b.astype(BF16), (((2,), (1,)), ((0,), (0,))),
                           preferred_element_type=F32)


def _bmm_nt(a, b):
    return lax.dot_general(a.astype(BF16), b.astype(BF16), (((2,), (2,)), ((0,), (0,))),
                           preferred_element_type=F32)


def _gdn_kernel(qkv_ref, ab_ref, z_ref, cw_ref, alog_ref, dtb_ref, gn_ref, o_ref,
                ext_ref, state_ref, *, tr, nb):
    ti = pl.program_id(1)
    width = 3 * GDN_WIDTH
    n_c = tr // CHUNK
    n_p = GDN_HEADS // 2

    @pl.when(ti == 0)
    def _():
        ext_ref[:, 0:_HALO, :] = jnp.zeros((nb, _HALO, width), F32)
        state_ref[...] = jnp.zeros(state_ref.shape, F32)

    lane = lax.broadcasted_iota(I32, (CHUNK, LANES), 1)
    row_i = lax.broadcasted_iota(I32, (CHUNK, LANES), 0)
    left = lane < HEAD_DIM
    incl = (row_i >= lane % HEAD_DIM)[None]
    strict = (row_i > lane % HEAD_DIM)[None]
    left_t = (lax.broadcasted_iota(I32, (tr, LANES), 1) < HEAD_DIM)
    tri = jnp.where(lax.broadcasted_iota(I32, (CHUNK, CHUNK), 0) >= lax.broadcasted_iota(I32, (CHUNK, CHUNK), 1),
                    1.0, 0.0).astype(BF16)

    def blk(x):
        return jnp.concatenate([jnp.where(left[None], x, 0.0), jnp.where(left[None], 0.0, x)], axis=1)

    def head_scale(x, eps, scale):
        sq = x * x
        keep = left_t if x.shape[0] == tr else left
        s_l = jnp.sum(jnp.where(keep, sq, 0.0), axis=-1, keepdims=True)
        s_r = jnp.sum(jnp.where(keep, 0.0, sq), axis=-1, keepdims=True)
        return x * jnp.where(keep, lax.rsqrt(s_l * scale + eps), lax.rsqrt(s_r * scale + eps))

    def rows(c):
        return slice(c * CHUNK, (c + 1) * CHUNK)

    qkvs, betas, q_n, k_n, gc_cols, gc_rows = [], [], [], [], [], []
    for s in range(nb):
        ext_ref[s, _HALO:_HALO + tr, :] = qkv_ref[s]
        conv = cw_ref[CONV_WIDTH - 1:CONV_WIDTH, :] * ext_ref[s, _HALO:_HALO + tr, :]
        for j in range(CONV_WIDTH - 1):
            off = _HALO - (CONV_WIDTH - 1) + j
            conv = conv + cw_ref[j:j + 1, :] * ext_ref[s, off:off + tr, :]
        ext_ref[s, 0:_HALO, :] = ext_ref[s, tr:tr + _HALO, :]
        qkv = _silu(conv)
        qkvs.append(qkv)

        ab = ab_ref[s]
        sp_in = ab + dtb_ref[...]
        sp = jnp.maximum(sp_in, 0.0) + jnp.log1p(jnp.exp(-jnp.abs(sp_in)))
        g_all = -jnp.exp(alog_ref[...]) * sp
        betas.append(1.0 / (1.0 + jnp.exp(-ab)))

        cols, rws = [], []
        for c in range(n_c):
            g_c = g_all[rows(c), :]
            g_hi = g_c.astype(BF16)
            g_r1 = g_c - g_hi.astype(F32)
            g_mid = g_r1.astype(BF16)
            g_lo = (g_r1 - g_mid.astype(F32)).astype(BF16)
            gc = (jnp.dot(tri, g_hi, preferred_element_type=F32)
                  + jnp.dot(tri, g_mid, preferred_element_type=F32)
                  + jnp.dot(tri, g_lo, preferred_element_type=F32))
            cols.append(gc)
            rws.append(gc.T)
        gc_cols.append(cols)
        gc_rows.append(rws)
        q_n.append([head_scale(qkv[:, p * LANES:(p + 1) * LANES], RMS_EPS, 1.0) * (HEAD_DIM ** -0.5)
                    for p in range(n_p)])
        k_n.append([head_scale(qkv[:, GDN_WIDTH + p * LANES:GDN_WIDTH + (p + 1) * LANES], RMS_EPS, 1.0)
                    for p in range(n_p)])

    def stack(fn):
        return jnp.stack([fn(c, s, p) for c in range(n_c) for s in range(nb) for p in range(n_p)], axis=0)

    def pair_cols(x, c, p, base):
        return jnp.where(left, x[rows(c), base + 2 * p:base + 2 * p + 1],
                         x[rows(c), base + 2 * p + 1:base + 2 * p + 2])

    qs = stack(lambda c, s, p: q_n[s][p][rows(c), :])
    ks = stack(lambda c, s, p: k_n[s][p][rows(c), :])
    vs = stack(lambda c, s, p: qkvs[s][rows(c), 2 * GDN_WIDTH + p * LANES:2 * GDN_WIDTH + (p + 1) * LANES])
    beta = stack(lambda c, s, p: pair_cols(betas[s], c, p, GDN_HEADS))
    gcc = stack(lambda c, s, p: pair_cols(gc_cols[s][c], 0, p, 0))
    gcr = stack(lambda c, s, p: jnp.concatenate(
        [gc_rows[s][c][2 * p:2 * p + 1, :], gc_rows[s][c][2 * p + 1:2 * p + 2, :]], axis=1))
    decay = jnp.where(incl, jnp.exp(jnp.where(incl, gcc - gcr, 0.0)), 0.0)
    kb = ks * beta
    k_blk = blk(ks)
    low = jnp.where(strict, _bmm_nt(kb, k_blk) * decay, 0.0)
    e_gc = jnp.exp(gcc)
    u = vs * beta
    w = kb * e_gc
    u = u - _bmm(low, blk(u))
    w = w - _bmm(low, blk(w))
    pw = low
    for _ in range(5):
        pw = _bmm(pw, blk(pw))
        u = u + _bmm(pw, blk(u))
        w = w + _bmm(pw, blk(w))
    a_intra = jnp.where(incl, _bmm_nt(qs, k_blk) * decay, 0.0)
    q_dec = qs * e_gc
    gc_last = gcc[:, CHUNK - 1:CHUNK, :]
    kd_blk = blk(ks * jnp.exp(gc_last - gcc))
    kd_blk_t = jnp.stack([kd_blk[i].T for i in range(n_c * nb * n_p)], axis=0)
    g_last = jnp.exp(gc_last)

    gn = gn_ref[...]
    st = state_ref[...]
    per_c = nb * n_p
    for c in range(n_c):
        sl = slice(c * per_c, (c + 1) * per_c)
        v_new = blk(u[sl] - _bmm(w[sl], st))
        o = _bmm(q_dec[sl], st) + _bmm(a_intra[sl], v_new)
        st = st * g_last[sl] + _bmm(kd_blk_t[sl], v_new)
        for s in range(nb):
            for p in range(n_p):
                y = head_scale(o[s * n_p + p], RMS_EPS, 1.0 / HEAD_DIM) * gn
                o_ref[s, rows(c), p * LANES:(p + 1) * LANES] = y * _silu(
                    z_ref[s, rows(c), p * LANES:(p + 1) * LANES])
    state_ref[...] = st


def _gdn(qkvb, ab, zb, conv_w, a_log, dt_bias, gdn_norm_g, bsz, seq, tr=256):
    tr = min(tr, seq)
    n_t = seq // tr
    nb = 2 if bsz % 2 == 0 else 1
    blk3 = lambda b, i: (b, i, 0)
    const = lambda b, i: (0, 0)
    alog = _pad_cols(a_log.reshape(1, GDN_HEADS).astype(F32), LANES)
    dtb = _pad_cols(dt_bias.reshape(1, GDN_HEADS).astype(F32), LANES)
    out = pl.pallas_call(
        functools.partial(_gdn_kernel, tr=tr, nb=nb),
        grid=(bsz // nb, n_t),
        in_specs=[pl.BlockSpec((nb, tr, 3 * GDN_WIDTH), blk3),
                  pl.BlockSpec((nb, tr, LANES), blk3),
                  pl.BlockSpec((nb, tr, GDN_WIDTH), blk3),
                  pl.BlockSpec((CONV_WIDTH, 3 * GDN_WIDTH), const),
                  pl.BlockSpec((1, LANES), const),
                  pl.BlockSpec((1, LANES), const),
                  pl.BlockSpec((1, LANES), const)],
        out_specs=pl.BlockSpec((nb, tr, GDN_WIDTH), blk3),
        out_shape=jax.ShapeDtypeStruct((bsz, seq, GDN_WIDTH), F32),
        scratch_shapes=[pltpu.VMEM((nb, tr + _HALO, 3 * GDN_WIDTH), F32),
                        pltpu.VMEM((nb * (GDN_HEADS // 2), LANES, LANES), F32)],
        compiler_params=_cparams(("parallel", "arbitrary")),
        name="gdn",
    )(qkvb.reshape(bsz, seq, 3 * GDN_WIDTH), ab.reshape(bsz, seq, LANES), zb.reshape(bsz, seq, GDN_WIDTH),
      conv_w.astype(F32), alog, dtb,
      jnp.tile(gdn_norm_g.reshape(1, HEAD_DIM).astype(F32), (1, LANES // HEAD_DIM)))
    return out.reshape(bsz * seq, GDN_WIDTH)


def _memkv_kernel(m_ref, w_ref, k_ref, v_ref):
    r = jnp.dot(m_ref[...].astype(BF16), w_ref[...], preferred_element_type=F32)
    k_ref[...] = r[:, :MEM_WIDTH].astype(BF16)
    v_ref[...] = r[:, MEM_WIDTH:].astype(BF16)


def _memkv(mem2, w_mem_kv, n_mem):
    n_rows, d_model = mem2.shape
    row = lambda i: (i, 0)
    return pl.pallas_call(
        _memkv_kernel,
        grid=(n_rows // n_mem,),
        in_specs=[pl.BlockSpec((n_mem, d_model), row),
                  pl.BlockSpec((d_model, 2 * MEM_WIDTH), lambda i: (0, 0))],
        out_specs=[pl.BlockSpec((n_mem, MEM_WIDTH), row)] * 2,
        out_shape=[jax.ShapeDtypeStruct((n_rows, MEM_WIDTH), BF16)] * 2,
        compiler_params=_cparams(("parallel",)),
        name="memkv",
    )(mem2, w_mem_kv.astype(BF16))


_ROUTE_ROWS = 128


def _post_kernel(x_ref, oa_ref, ob_ref, qc_ref, mk_ref, mv_ref, wout_ref, ag_ref, mg_ref,
                 lg_ref, lb_ref, wr_ref, br_ref, x1_ref, ids_ref, gate_ref, *, alpha):
    tm = x_ref.shape[0]
    oa = oa_ref[...]
    oa = oa * lax.rsqrt(jnp.mean(oa * oa, axis=-1, keepdims=True) + RMS_EPS) * ag_ref[...]

    qc = qc_ref[...]
    mk = mk_ref[...]
    mv = mv_ref[...]
    cols = [slice(h * HEAD_DIM, (h + 1) * HEAD_DIM) for h in range(MEM_HEADS)]
    lgs = [_dot_nt(qc[:, c], mk[:, c]) * (HEAD_DIM ** -0.5) for c in cols]
    es = [jnp.exp(lg - jnp.max(lg, axis=-1, keepdims=True)) for lg in lgs]
    ps = [e / jnp.sum(e, axis=-1, keepdims=True) for e in es]
    oc = jnp.concatenate([jnp.dot(p.astype(BF16), mv[:, c], preferred_element_type=F32)
                          for p, c in zip(ps, cols)], axis=1)
    oc = oc * lax.rsqrt(jnp.mean(oc * oc, axis=-1, keepdims=True) + RMS_EPS) * mg_ref[...]

    mix = (jnp.dot(oa.astype(BF16), wout_ref[0:A_WIDTH, :], preferred_element_type=F32)
           + jnp.dot(ob_ref[...].astype(BF16), wout_ref[A_WIDTH:A_WIDTH + GDN_WIDTH, :],
                     preferred_element_type=F32)
           + jnp.dot(oc.astype(BF16), wout_ref[A_WIDTH + GDN_WIDTH:, :], preferred_element_type=F32))
    hres = alpha * x_ref[...] + mix
    mu = jnp.mean(hres, axis=-1, keepdims=True)
    var = jnp.mean(jnp.square(hres - mu), axis=-1, keepdims=True)
    x1 = (hres - mu) * lax.rsqrt(var + LN_EPS) * lg_ref[...] + lb_ref[...]
    x1_ref[...] = x1

    lt = _dot_nt(wr_ref[...], x1) + br_ref[...]
    sub = lax.broadcasted_iota(I32, (EXPERTS_PER_GROUP, tm), 0)
    gl = lt[0:N_GROUPS, :]
    gmax = jnp.max(gl, axis=0, keepdims=True)
    gprob = jnp.exp(gl - gmax) / jnp.sum(jnp.exp(gl - gmax), axis=0, keepdims=True)
    p_grp = jnp.max(gprob, axis=0, keepdims=True)
    grp = jnp.min(jnp.where(gprob == p_grp, sub, N_GROUPS), axis=0, keepdims=True)
    el = jnp.zeros((EXPERTS_PER_GROUP, tm), F32)
    for g in range(N_GROUPS):
        r0 = N_GROUPS + g * EXPERTS_PER_GROUP
        el = el + jnp.where(grp == g, lt[r0:r0 + EXPERTS_PER_GROUP, :], 0.0)
    ee = jnp.exp(el - jnp.max(el, axis=0, keepdims=True))
    pe = ee / jnp.sum(ee, axis=0, keepdims=True)
    p1 = jnp.max(pe, axis=0, keepdims=True)
    i1 = jnp.min(jnp.where(pe == p1, sub, EXPERTS_PER_GROUP), axis=0, keepdims=True)
    rest = jnp.where(sub == i1, -1.0, pe)
    p2 = jnp.max(rest, axis=0, keepdims=True)
    i2 = jnp.min(jnp.where(rest == p2, sub, EXPERTS_PER_GROUP), axis=0, keepdims=True)
    psum = p1 + p2
    g1 = p_grp * p1 / psum
    g2 = p_grp * p2 / psum
    e1 = grp * EXPERTS_PER_GROUP + i1
    e2 = grp * EXPERTS_PER_GROUP + i2
    ids_ref[...] = jnp.where(sub == 0, e1, jnp.where(sub == 1, e2, 0))
    gate_ref[...] = jnp.where(sub == 0, g1, jnp.where(sub == 1, g2, 0.0))


def _post(x2, oa, ob, qc, mk, mv, w_out, attn_g, mem_g, ln_g, ln_b, w_group, b_group,
          w_router, b_router, seq, n_mem, alpha, tm=256):
    n_tok, d_model = x2.shape
    n_t = seq // tm
    row = lambda i: (i, 0)
    const = lambda i: (0, 0)
    per_b = lambda i: (i // n_t, 0, 0)
    lane = lambda i: (0, i)
    bsz = n_tok // seq
    wr = jnp.pad(jnp.concatenate([w_group, w_router], axis=1).T,
                 ((0, _ROUTE_ROWS - N_GROUPS - N_EXPERTS), (0, 0))).astype(BF16)
    br = jnp.pad(jnp.concatenate([b_group, b_router]),
                 (0, _ROUTE_ROWS - N_GROUPS - N_EXPERTS)).reshape(_ROUTE_ROWS, 1).astype(F32)
    vec = lambda v: v.reshape(1, -1).astype(F32)
    return pl.pallas_call(
        functools.partial(_post_kernel, alpha=alpha),
        grid=(n_tok // tm,),
        in_specs=[pl.BlockSpec((tm, d_model), row),
                  pl.BlockSpec((tm, A_WIDTH), row),
                  pl.BlockSpec((tm, GDN_WIDTH), row),
                  pl.BlockSpec((tm, MEM_WIDTH), row),
                  pl.BlockSpec((None, n_mem, MEM_WIDTH), per_b),
                  pl.BlockSpec((None, n_mem, MEM_WIDTH), per_b),
                  pl.BlockSpec(w_out.shape, const),
                  pl.BlockSpec((1, A_WIDTH), const),
                  pl.BlockSpec((1, MEM_WIDTH), const),
                  pl.BlockSpec((1, d_model), const),
                  pl.BlockSpec((1, d_model), const),
                  pl.BlockSpec((_ROUTE_ROWS, d_model), const),
                  pl.BlockSpec((_ROUTE_ROWS, 1), const)],
        out_specs=[pl.BlockSpec((tm, d_model), row),
                   pl.BlockSpec((EXPERTS_PER_GROUP, tm), lane),
                   pl.BlockSpec((EXPERTS_PER_GROUP, tm), lane)],
        out_shape=[jax.ShapeDtypeStruct((n_tok, d_model), F32),
                   jax.ShapeDtypeStruct((EXPERTS_PER_GROUP, n_tok), I32),
                   jax.ShapeDtypeStruct((EXPERTS_PER_GROUP, n_tok), F32)],
        compiler_params=_cparams(("parallel",)),
        name="post",
    )(x2, oa, ob, qc, mk.reshape(bsz, n_mem, MEM_WIDTH), mv.reshape(bsz, n_mem, MEM_WIDTH),
      w_out.astype(BF16), vec(attn_g), vec(mem_g), vec(ln_g), vec(ln_b), wr, br)


def _rank_kernel(ids_ref, rank_ref, cnt_ref, carry_ref):
    i = pl.program_id(0)
    tm = ids_ref.shape[1]

    @pl.when(i == 0)
    def _():
        carry_ref[...] = jnp.zeros(carry_ref.shape, F32)

    ids = ids_ref[...]
    eio = lax.broadcasted_iota(I32, (N_EXPERTS, tm), 0)
    oh0 = jnp.where(eio == ids[0:1, :], 1.0, 0.0)
    oh1 = jnp.where(eio == ids[1:2, :], 1.0, 0.0)
    cnt = oh0 + oh1
    before = (lax.broadcasted_iota(I32, (tm, tm), 0) < lax.broadcasted_iota(I32, (tm, tm), 1))
    prefix = jnp.dot(cnt.astype(BF16), jnp.where(before, 1.0, 0.0).astype(BF16),
                     preferred_element_type=F32) + carry_ref[:, 0:1]
    r0 = jnp.sum(oh0 * prefix, axis=0, keepdims=True)
    r1 = jnp.sum(oh1 * prefix, axis=0, keepdims=True)
    sub = lax.broadcasted_iota(I32, (EXPERTS_PER_GROUP, tm), 0)
    rank_ref[...] = jnp.where(sub == 0, r0, jnp.where(sub == 1, r1, 0.0)).astype(I32)
    carry_ref[...] = carry_ref[...] + jnp.sum(cnt, axis=1, keepdims=True)
    cnt_ref[...] = carry_ref[...].astype(I32)


def _rank(ids, tm=512):
    n_tok = ids.shape[1]
    lane = lambda i: (0, i)
    return pl.pallas_call(
        _rank_kernel,
        grid=(n_tok // tm,),
        in_specs=[pl.BlockSpec((EXPERTS_PER_GROUP, tm), lane)],
        out_specs=[pl.BlockSpec((EXPERTS_PER_GROUP, tm), lane),
                   pl.BlockSpec((N_EXPERTS, LANES), lambda i: (0, 0))],
        out_shape=[jax.ShapeDtypeStruct((EXPERTS_PER_GROUP, n_tok), I32),
                   jax.ShapeDtypeStruct((N_EXPERTS, LANES), I32)],
        scratch_shapes=[pltpu.VMEM((N_EXPERTS, LANES), F32)],
        compiler_params=_cparams(("arbitrary",)),
        name="rank",
    )(ids)


def _dest_kernel(ids_ref, rank_ref, ps_ref, dest_ref):
    ids = ids_ref[...]
    tm = ids.shape[1]
    eio = lax.broadcasted_iota(I32, (N_EXPERTS, tm), 0)
    ps = ps_ref[...]
    d0 = jnp.sum(jnp.where(eio == ids[0:1, :], ps, 0.0), axis=0, keepdims=True)
    d1 = jnp.sum(jnp.where(eio == ids[1:2, :], ps, 0.0), axis=0, keepdims=True)
    sub = lax.broadcasted_iota(I32, (EXPERTS_PER_GROUP, tm), 0)
    dest_ref[...] = rank_ref[...] + jnp.where(sub == 0, d0, jnp.where(sub == 1, d1, 0.0)).astype(I32)


def _dest(ids, rank, pad_start, tm=2048):
    n_tok = ids.shape[1]
    tm = min(tm, n_tok)
    lane = lambda i: (0, i)
    return pl.pallas_call(
        _dest_kernel,
        grid=(n_tok // tm,),
        in_specs=[pl.BlockSpec((EXPERTS_PER_GROUP, tm), lane),
                  pl.BlockSpec((EXPERTS_PER_GROUP, tm), lane),
                  pl.BlockSpec((N_EXPERTS, 1), lambda i: (0, 0))],
        out_specs=pl.BlockSpec((EXPERTS_PER_GROUP, tm), lane),
        out_shape=jax.ShapeDtypeStruct((EXPERTS_PER_GROUP, n_tok), I32),
        compiler_params=_cparams(("parallel",)),
        name="dest",
    )(ids, rank, pad_start.astype(F32).reshape(N_EXPERTS, 1))


def _sc_mesh():
    return plsc.VectorSubcoreMesh(core_axis_name="c", subcore_axis_name="s",
                                  num_cores=SC_CORES, num_subcores=SC_SUBCORES)


def _sc_scatter_rows(x1, dest, pad_rows, cap):
    n_tok, d = x1.shape
    n_pad = pad_rows.shape[0]
    n_workers = SC_CORES * SC_SUBCORES
    tok_per_worker = n_tok // n_workers
    pad_per_worker = n_pad // n_workers
    n_win = tok_per_worker // SC_ROWS
    assert n_tok % (n_workers * SC_ROWS) == 0 and n_pad % (n_workers * SC_ROWS) == 0 and TOP_K == 2
    zero_rows = jnp.zeros((SC_ROWS, d), x1.dtype)

    @functools.partial(
        pl.kernel, mesh=_sc_mesh(), out_type=jax.ShapeDtypeStruct((cap, d), x1.dtype),
        scratch_types=[pltpu.VMEM((tok_per_worker,), I32), pltpu.VMEM((tok_per_worker,), I32),
                       pltpu.VMEM((pad_per_worker,), I32), pltpu.VMEM((2, SC_ROWS, d), x1.dtype),
                       pltpu.SemaphoreType.DMA((2,)), pltpu.SemaphoreType.DMA((2,))],
        name="sc_scatter")
    def scatter(x_hbm, dest_hbm, pad_hbm, zero_hbm, out_hbm, idx0_v, idx1_v, pad_v, rows_v, lsem, ssem):
        worker = lax.axis_index("s") * SC_CORES + lax.axis_index("c")
        base = worker * tok_per_worker
        pltpu.sync_copy(dest_hbm.at[pl.ds(base, tok_per_worker)], idx0_v)
        pltpu.sync_copy(dest_hbm.at[pl.ds(n_tok + base, tok_per_worker)], idx1_v)
        pltpu.sync_copy(pad_hbm.at[pl.ds(worker * pad_per_worker, pad_per_worker)], pad_v)

        def load(i, slot):
            return pltpu.make_async_copy(x_hbm.at[pl.ds(base + i * SC_ROWS, SC_ROWS)],
                                         rows_v.at[slot], lsem.at[slot])

        def store(idx_v, i, slot):
            return pltpu.make_async_copy(rows_v.at[slot],
                                         out_hbm.at[idx_v.at[pl.ds(i * SC_ROWS, SC_ROWS)]], ssem.at[slot])

        load(0, 0).start()

        @pl.loop(0, n_win)
        def _(i):
            slot = i % 2
            load(i, slot).wait()
            store(idx0_v, i, slot).start()
            store(idx1_v, i, slot).start()

            @pl.when(i >= 1)
            def _():
                store(idx0_v, i - 1, 1 - slot).wait()
                store(idx1_v, i - 1, 1 - slot).wait()

            @pl.when(i + 1 < n_win)
            def _():
                load(i + 1, 1 - slot).start()

        last = (n_win - 1) % 2
        store(idx0_v, n_win - 1, last).wait()
        store(idx1_v, n_win - 1, last).wait()

        pltpu.sync_copy(zero_hbm, rows_v.at[0])

        @pl.loop(0, pad_per_worker // SC_ROWS)
        def _(i):
            pltpu.async_copy(rows_v.at[0], out_hbm.at[pad_v.at[pl.ds(i * SC_ROWS, SC_ROWS)]],
                             ssem.at[0]).wait()

    return scatter(x1, dest, pad_rows, zero_rows)


def _mlp_kernel(be_ref, nused_ref, x_ref, wg_ref, wu_ref, wd_ref, y_ref):
    i = pl.program_id(0)

    @pl.when(i < nused_ref[0])
    def _():
        xb = x_ref[...].astype(BF16)
        hg = jnp.dot(xb, wg_ref[...].astype(BF16), preferred_element_type=F32)
        hu = jnp.dot(xb, wu_ref[...].astype(BF16), preferred_element_type=F32)
        y_ref[...] = jnp.dot((_silu(hg) * hu).astype(BF16), wd_ref[...].astype(BF16),
                             preferred_element_type=F32)

    @pl.when(i >= nused_ref[0])
    def _():
        y_ref[...] = jnp.zeros(y_ref.shape, F32)


def _mlp(blk_expert, n_used, xbuf, w_gate, w_up, w_down):
    cap, d_model = xbuf.shape
    d_exp = w_gate.shape[2]
    blk = lambda i, be, nu: (i, 0)
    used_blk = lambda i, be, nu: (jnp.minimum(i, nu[0] - 1), 0)
    wsel = lambda i, be, nu: (be[i], 0, 0)
    return pl.pallas_call(
        _mlp_kernel,
        grid_spec=pltpu.PrefetchScalarGridSpec(
            num_scalar_prefetch=2,
            grid=(cap // MOE_BLOCK,),
            in_specs=[pl.BlockSpec((MOE_BLOCK, d_model), used_blk),
                      pl.BlockSpec((None, d_model, d_exp), wsel),
                      pl.BlockSpec((None, d_model, d_exp), wsel),
                      pl.BlockSpec((None, d_exp, d_model), wsel)],
            out_specs=pl.BlockSpec((MOE_BLOCK, d_model), blk)),
        out_shape=jax.ShapeDtypeStruct((cap, d_model), F32),
        compiler_params=_cparams(("arbitrary",)),
        name="mlp",
    )(blk_expert, n_used, xbuf, w_gate, w_up, w_down)


def _sc_gather_rows(table, idx):
    n = idx.shape[0]
    d = table.shape[1]
    n_workers = SC_CORES * SC_SUBCORES
    per_worker = n // n_workers
    n_win = per_worker // SC_ROWS
    assert n % (n_workers * SC_ROWS) == 0

    @functools.partial(
        pl.kernel, mesh=_sc_mesh(), out_type=jax.ShapeDtypeStruct((n, d), table.dtype),
        scratch_types=[pltpu.VMEM((per_worker,), I32), pltpu.VMEM((2, SC_ROWS, d), table.dtype),
                       pltpu.SemaphoreType.DMA((2,)), pltpu.SemaphoreType.DMA((2,))],
        name="sc_gather")
    def gather(table_hbm, idx_hbm, out_hbm, idx_v, rows_v, gsem, wsem):
        worker = lax.axis_index("s") * SC_CORES + lax.axis_index("c")
        base = worker * per_worker
        pltpu.sync_copy(idx_hbm.at[pl.ds(base, per_worker)], idx_v)

        def fetch(i, slot):
            return pltpu.make_async_copy(table_hbm.at[idx_v.at[pl.ds(i * SC_ROWS, SC_ROWS)]],
                                         rows_v.at[slot], gsem.at[slot])

        def write(i, slot):
            return pltpu.make_async_copy(rows_v.at[slot],
                                         out_hbm.at[pl.ds(base + i * SC_ROWS, SC_ROWS)], wsem.at[slot])

        fetch(0, 0).start()

        @pl.loop(0, n_win)
        def _(i):
            slot = i % 2
            fetch(i, slot).wait()
            write(i, slot).start()

            @pl.when(i >= 1)
            def _():
                write(i - 1, 1 - slot).wait()

            @pl.when(i + 1 < n_win)
            def _():
                fetch(i + 1, 1 - slot).start()

        write(n_win - 1, (n_win - 1) % 2).wait()

    return gather(table, idx)


def _combine_kernel(x1_ref, y0_ref, y1_ref, gt_ref, lg_ref, lb_ref, o_ref, *, alpha):
    gt = gt_ref[...]
    ffn = y0_ref[...] * gt[:, 0:1] + y1_ref[...] * gt[:, 1:2]
    hres = alpha * x1_ref[...] + ffn
    mu = jnp.mean(hres, axis=-1, keepdims=True)
    var = jnp.mean(jnp.square(hres - mu), axis=-1, keepdims=True)
    o_ref[...] = (hres - mu) * lax.rsqrt(var + LN_EPS) * lg_ref[...] + lb_ref[...]


def _combine(dest, x1, gates_t, ln_g, ln_b, ybuf, alpha, tc=512):
    n_tok, d_model = x1.shape
    n_tiles = n_tok // tc
    yrows = _sc_gather_rows(ybuf, dest.reshape(-1))
    row = lambda i: (i, 0)
    const = lambda i: (0, 0)
    vec = lambda v: v.reshape(1, -1).astype(F32)
    return pl.pallas_call(
        functools.partial(_combine_kernel, alpha=alpha),
        grid=(n_tiles,),
        in_specs=[pl.BlockSpec((tc, d_model), row),
                  pl.BlockSpec((tc, d_model), row),
                  pl.BlockSpec((tc, d_model), lambda i: (i + n_tiles, 0)),
                  pl.BlockSpec((tc, EXPERTS_PER_GROUP), row),
                  pl.BlockSpec((1, d_model), const),
                  pl.BlockSpec((1, d_model), const)],
        out_specs=pl.BlockSpec((tc, d_model), row),
        out_shape=jax.ShapeDtypeStruct((n_tok, d_model), F32),
        compiler_params=_cparams(("parallel",)),
        name="combine",
    )(x1, yrows, yrows, gates_t, vec(ln_g), vec(ln_b))


def _moe(x1, ids, gates, w_gate, w_up, w_down, ln_g, ln_b, alpha):
    n_tok, d_model = x1.shape
    rank, counts = _rank(ids)
    counts = counts[:, 0]
    padded = (counts + MOE_BLOCK - 1) // MOE_BLOCK * MOE_BLOCK
    pad_ends = jnp.cumsum(padded)
    pad_start = (pad_ends - padded).astype(I32)
    n_asg = n_tok * TOP_K
    cap = (n_asg + MOE_BLOCK - 1) // MOE_BLOCK * MOE_BLOCK + N_EXPERTS * MOE_BLOCK
    n_blk = cap // MOE_BLOCK
    blk_pos = jnp.arange(n_blk, dtype=I32) * MOE_BLOCK
    blk_expert = jnp.minimum(
        jnp.sum((pad_ends[None, :] <= blk_pos[:, None]).astype(I32), axis=1), N_EXPERTS - 1)
    n_used = (pad_ends[-1:] // MOE_BLOCK).astype(I32)
    dest = _dest(ids, rank, pad_start)[0:TOP_K]
    slot = jnp.arange(MOE_BLOCK, dtype=I32)[None, :]
    spare = cap - 1 - (jnp.arange(N_EXPERTS * MOE_BLOCK, dtype=I32).reshape(N_EXPERTS, MOE_BLOCK)
                       % N_EXPERTS)
    pad_rows = jnp.where(slot < (padded - counts)[:, None], (pad_start + counts)[:, None] + slot,
                         spare).reshape(-1)
    xbuf = _sc_scatter_rows(x1, dest.reshape(-1), pad_rows, cap)
    ybuf = _mlp(blk_expert, n_used, xbuf, w_gate, w_up, w_down)
    return _combine(dest, x1, gates.T, ln_g, ln_b, ybuf, alpha)


def kernel(x, mem, w_in, kv_norm_g, w_k_up, w_v_up, conv_w, A_log, dt_bias, gdn_norm_g, attn_norm_g, mem_norm_g, w_mem_kv, w_out, ln1_g, ln1_b, w_group, b_group, w_router, b_router, w_gate, w_up, w_down, ln2_g, ln2_b):
    bsz, seq, d_model = x.shape
    n_mem = mem.shape[1]
    depth = w_in.shape[0]
    alpha = (2 * depth) ** 0.25
    inv_freq = 1.0 / (ROPE_THETA ** (jnp.arange(0, HEAD_DIM, 2, dtype=F32) / HEAD_DIM))
    ang = jnp.arange(seq, dtype=F32)[:, None] * inv_freq[None, :]
    cos128 = jnp.tile(jnp.cos(ang), (1, LANES // (HEAD_DIM // 2)))
    sin128 = jnp.tile(jnp.sin(ang), (1, LANES // (HEAD_DIM // 2)))
    x2 = x.reshape(bsz * seq, d_model)
    mem2 = mem.reshape(bsz * n_mem, d_model)
    for l in range(depth):
        qa, iq, ik, ka, va, iw, qkvb, zb, ab, qc = _proj(
            x2, w_in[l], w_k_up[l], w_v_up[l], kv_norm_g[l], cos128, sin128, seq)
        oa = _dsa(qa, iq, iw, ik, ka, va, bsz, seq)
        ob = _gdn(qkvb, ab, zb, conv_w[l], A_log[l], dt_bias[l], gdn_norm_g[l], bsz, seq)
        mk, mv = _memkv(mem2, w_mem_kv[l], n_mem)
        x1, ids, gates = _post(x2, oa, ob, qc, mk, mv, w_out[l], attn_norm_g[l], mem_norm_g[l],
                               ln1_g[l], ln1_b[l], w_group[l], b_group[l], w_router[l],
                               b_router[l], seq, n_mem, alpha)
        x2 = _moe(x1, ids, gates, w_gate[l], w_up[l], w_down[l], ln2_g[l], ln2_b[l], alpha)
    return x2.reshape(bsz, seq, d_model)
```

```python
import functools

import numpy as np
import jax
import jax.numpy as jnp
from jax import lax
from jax.experimental import pallas as pl
from jax.experimental.pallas import tpu as pltpu
from jax.experimental.pallas import tpu_sc as plsc

F32 = jnp.float32
BF16 = jnp.bfloat16
I32 = jnp.int32
I16 = jnp.int16

HEAD_DIM = 64
ATTN_HEADS = 6
A_WIDTH = ATTN_HEADS * HEAD_DIM
KV_RANK = 128
IDX_HEADS = 4
IDX_DIM = 64
INDEX_TOPK = 256
GDN_HEADS = 6
GDN_WIDTH = GDN_HEADS * HEAD_DIM
CONV_WIDTH = 4
CHUNK = 64
MEM_HEADS = 4
MEM_WIDTH = MEM_HEADS * HEAD_DIM
SPLIT_SIZES = (A_WIDTH, KV_RANK, IDX_HEADS * IDX_DIM, IDX_DIM, IDX_HEADS,
               GDN_WIDTH, GDN_WIDTH, GDN_WIDTH, GDN_WIDTH, GDN_HEADS, GDN_HEADS,
               MEM_WIDTH)
ROPE_THETA = 10000.0
N_GROUPS = 8
EXPERTS_PER_GROUP = 8
N_EXPERTS = N_GROUPS * EXPERTS_PER_GROUP
TOP_K = 2
MOE_BLOCK = 256
LN_EPS = 1e-5
RMS_EPS = 1e-6
NEG_INF = -1e30
INT_MIN = -2 ** 31
I16_MIN = -2 ** 15

LANES = 128
SC_CORES = 2
SC_SUBCORES = 16
SC_ROWS = 32
VMEM_LIMIT = 56 * 1024 * 1024


def _cparams(sem):
    return pltpu.CompilerParams(dimension_semantics=sem, vmem_limit_bytes=VMEM_LIMIT)


def _dot(a, b):
    return jnp.dot(a.astype(BF16), b.astype(BF16), preferred_element_type=F32)


def _dot_nt(a, b):
    return lax.dot_general(a.astype(BF16), b.astype(BF16), (((1,), (1,)), ((), ())),
                           preferred_element_type=F32)


def _silu(t):
    return t * (1.0 / (1.0 + jnp.exp(-t)))


_P_QA = (0, 384)
_P_IQ = (384, 640)
_P_CKV = (640, 768)
_P_IK = (768, 896)
_P_IW = (896, 1024)
_P_QKVB = (1024, 2176)
_P_ZB = (2176, 2560)
_P_AB = (2560, 2688)
_P_QC = (2688, 2944)
_P_TOTAL = 2944
_V_ROWS = HEAD_DIM + 16


def _rot_cols(w, n_heads):
    k = w.shape[0]
    w4 = w.reshape(k, n_heads, 2, HEAD_DIM // 2)
    return jnp.concatenate([-w4[:, :, 1:2], w4[:, :, 0:1]], axis=2).reshape(k, n_heads * HEAD_DIM)


def _pad_cols(w, width):
    return jnp.pad(w, ((0, 0), (0, width - w.shape[1])))


def _proj_kernel(x_ref, w_ref, w2_ref, cos_ref, sin_ref, kvg_ref,
                 qa_ref, iq_ref, ik_ref, ka_ref, va_ref, iw_ref, qkvb_ref, zb_ref, ab_ref, qc_ref):
    xb = x_ref[...].astype(BF16)

    def mm(slab):
        return jnp.dot(xb, w_ref[:, slab[0]:slab[1]], preferred_element_type=F32)

    cos = cos_ref[...]
    sin = sin_ref[...]
    cos3 = jnp.concatenate([cos] * 3, axis=1)
    sin3 = jnp.concatenate([sin] * 3, axis=1)
    cos2 = jnp.concatenate([cos] * 2, axis=1)
    sin2 = jnp.concatenate([sin] * 2, axis=1)
    cos64 = cos[:, :HEAD_DIM]
    sin64 = sin[:, :HEAD_DIM]

    def rotate_half(t):
        slabs = []
        for j in range(t.shape[1] // LANES):
            ts = t[:, j * LANES:(j + 1) * LANES]
            first = lax.broadcasted_iota(I32, ts.shape, 1) % HEAD_DIM < HEAD_DIM // 2
            slabs.append(jnp.where(first, -pltpu.roll(ts, LANES - HEAD_DIM // 2, 1),
                                   pltpu.roll(ts, HEAD_DIM // 2, 1)))
        return jnp.concatenate(slabs, axis=1)

    qa = mm(_P_QA)
    qa_ref[...] = (qa * cos3 + rotate_half(qa) * sin3).astype(BF16)
    iq = mm(_P_IQ)
    iq_ref[...] = (iq * cos2 + rotate_half(iq) * sin2).astype(BF16)

    ckv = mm(_P_CKV)
    cn = ckv * lax.rsqrt(jnp.mean(ckv * ckv, axis=-1, keepdims=True) + RMS_EPS) * kvg_ref[...]
    r = jnp.dot(cn.astype(BF16), w2_ref[...], preferred_element_type=F32)
    ka_ref[...] = (r[:, 0:64] * cos64 + r[:, 64:128] * sin64).astype(BF16)
    va_ref[...] = jnp.concatenate(
        [r[:, 128:256].T[0:HEAD_DIM, :], jnp.ones((_V_ROWS - HEAD_DIM, r.shape[0]), F32)],
        axis=0).astype(BF16)

    ikk = mm(_P_IK)
    ik_ref[...] = (ikk[:, 0:64] * cos64 + ikk[:, 64:128] * sin64).astype(BF16)
    iw_ref[...] = mm(_P_IW) * (IDX_HEADS ** -0.5 * IDX_DIM ** -0.5)
    qkvb_ref[...] = mm(_P_QKVB)
    zb_ref[...] = mm(_P_ZB)
    ab_ref[...] = mm(_P_AB)
    qc_ref[...] = mm(_P_QC).astype(BF16)


def _proj(x2, w_in, w_k_up, w_v_up, kv_norm_g, cos128, sin128, seq, tm=256):
    n_tok, d_model = x2.shape
    offs = np.cumsum(SPLIT_SIZES)[:-1].tolist()
    (w_qa, w_ckv, w_iq, w_ik, w_iw, w_qb, w_kb, w_vb, w_zb, w_a, w_b, w_qc) = jnp.split(w_in, offs, axis=1)
    w1 = jnp.concatenate([
        w_qa, w_iq, w_ckv,
        w_ik, _rot_cols(w_ik, 1), _pad_cols(w_iw, LANES),
        w_qb, w_kb, w_vb, w_zb, _pad_cols(jnp.concatenate([w_a, w_b], axis=1), LANES), w_qc,
    ], axis=1).astype(BF16)
    assert w1.shape[1] == _P_TOTAL
    w2 = _pad_cols(jnp.concatenate([w_k_up, _rot_cols(w_k_up, 1), w_v_up], axis=1), 2 * LANES).astype(BF16)
    n_pos = seq // tm
    row = lambda i: (i, 0)
    const = lambda i: (0, 0)
    pos = lambda i: (i % n_pos, 0)
    outs = [(A_WIDTH, BF16), (IDX_HEADS * IDX_DIM, BF16), (IDX_DIM, BF16), (HEAD_DIM, BF16),
            None, (LANES, F32), (3 * GDN_WIDTH, F32), (GDN_WIDTH, F32), (LANES, F32),
            (MEM_WIDTH, BF16)]
    out_specs = [pl.BlockSpec((tm, o[0]), row) if o else pl.BlockSpec((_V_ROWS, tm), lambda i: (0, i))
                 for o in outs]
    out_shape = [jax.ShapeDtypeStruct((n_tok, o[0]), o[1]) if o
                 else jax.ShapeDtypeStruct((_V_ROWS, n_tok), BF16) for o in outs]
    return pl.pallas_call(
        _proj_kernel,
        grid=(n_tok // tm,),
        in_specs=[pl.BlockSpec((tm, d_model), row),
                  pl.BlockSpec(w1.shape, const),
                  pl.BlockSpec(w2.shape, const),
                  pl.BlockSpec((tm, LANES), pos),
                  pl.BlockSpec((tm, LANES), pos),
                  pl.BlockSpec((1, KV_RANK), const)],
        out_specs=out_specs,
        out_shape=out_shape,
        compiler_params=_cparams(("parallel",)),
        name="proj",
    )(x2, w1, w2, cos128, sin128, kv_norm_g.reshape(1, KV_RANK).astype(F32))


def _dsa_kernel(qa_ref, iq_ref, iw_ref, ik_ref, ka_ref, vat_ref, o_ref,
                key_ref, hi_ref, lo_ref, m_ref, acc_ref, *, tq, kc, top):
    qi = pl.program_id(1)
    row0 = qi * tq
    n_kc = (row0 + tq + kc - 1) // kc
    qpos = row0 + lax.broadcasted_iota(I32, (1, tq), 1)

    def key_fold(v):
        return jnp.sum(v.reshape(kc // 8, 8, tq), axis=0)

    def head_rows(x, n_heads, width):
        return jnp.concatenate([x[:, h * width:(h + 1) * width] for h in range(n_heads)], axis=0)

    iq_rows = head_rows(iq_ref[...], IDX_HEADS, IDX_DIM)
    iw_t = iw_ref[...].T

    def score_body(c, carry):
        k0 = pl.multiple_of(c * kc, kc)
        d = _dot_nt(ik_ref[pl.ds(k0, kc), :], iq_rows)
        s = jnp.zeros((kc, tq), F32)
        for h in range(IDX_HEADS):
            s = s + iw_t[h:h + 1, :] * jnp.maximum(d[:, h * tq:(h + 1) * tq], 0.0)
        kidx = k0 + lax.broadcasted_iota(I32, (kc, tq), 0)
        s = jnp.where(s == 0.0, 0.0, s)
        s = jnp.where(kidx <= qpos, s, NEG_INF)
        bits = pltpu.bitcast(s, I32)
        key = jnp.where(bits >= 0, bits, bits ^ 0x7FFFFFFF)
        key_ref[pl.ds(k0, kc), :] = key
        hi_ref[pl.ds(k0, kc), :] = lax.shift_right_arithmetic(key, 16).astype(I16)
        lo_ref[pl.ds(k0, kc), :] = ((key & 0xFFFF) + I16_MIN).astype(I16)
        return carry

    lax.fori_loop(0, n_kc, score_body, 0)

    k_eff = jnp.minimum(top, qpos + 1).astype(F32)

    @pl.when(n_kc % 2 == 1)
    def _():
        k_pad = pl.multiple_of(n_kc * kc, kc)
        key_ref[pl.ds(k_pad, kc), :] = jnp.full((kc, tq), INT_MIN, I32)
        hi_ref[pl.ds(k_pad, kc), :] = jnp.full((kc, tq), I16_MIN, I16)
        lo_ref[pl.ds(k_pad, kc), :] = jnp.full((kc, tq), I16_MIN, I16)

    n_pairs = (n_kc + 1) // 2
    one_b = jnp.ones((), BF16)
    zero_b = jnp.zeros((), BF16)

    def count16(ref, pred):
        def body(c, acc):
            k0 = pl.multiple_of(c * (2 * kc), 2 * kc)
            for j in range(2):
                hit = jnp.where(pred(ref[pl.ds(k0 + j * kc, kc), :]), one_b, zero_b)
                hit = hit.reshape(kc // 16, 16, tq)
                parts = [hit[r] for r in range(kc // 16)]
                while len(parts) > 1:
                    parts = [a + b for a, b in zip(parts[0::2], parts[1::2])]
                acc = acc + parts[0]
            return acc
        acc = lax.fori_loop(0, n_pairs, body, jnp.zeros((16, tq), BF16))
        return jnp.sum(acc.astype(F32), axis=0, keepdims=True)

    def search16(ref, start, n_bits, k_want):
        def bit_body(i, t):
            cand = t + lax.shift_left(jnp.int32(1), n_bits - 1 - i)
            c16 = cand.astype(I16)
            return jnp.where(count16(ref, lambda v: v >= c16) >= k_want, cand, t)
        return lax.fori_loop(0, n_bits, bit_body, start)

    hi0 = jnp.where(count16(hi_ref, lambda v: v >= 0) >= k_eff, 0, I16_MIN).astype(I32)
    thr_hi = search16(hi_ref, hi0, 15, k_eff)
    thr_hi16 = thr_hi.astype(I16)
    k_low = k_eff - count16(hi_ref, lambda v: v > thr_hi16)

    def band_body(c, carry):
        k0 = pl.multiple_of(c * kc, kc)
        lo_ref[pl.ds(k0, kc), :] = jnp.where(hi_ref[pl.ds(k0, kc), :] == thr_hi16,
                                             lo_ref[pl.ds(k0, kc), :], I16_MIN)
        return carry

    lax.fori_loop(0, n_kc, band_body, 0)
    thr_lo = search16(lo_ref, jnp.full((1, tq), I16_MIN, I32), 16, k_low)
    thr_lo16 = thr_lo.astype(I16)
    thr = thr_hi * 65536 + (thr_lo - I16_MIN)
    n_tie = k_low - count16(lo_ref, lambda v: v > thr_lo16)

    q_rows = head_rows((qa_ref[...].astype(F32) * (HEAD_DIM ** -0.5)).astype(BF16),
                       ATTN_HEADS, HEAD_DIM)
    m_ref[...] = jnp.full(m_ref.shape, NEG_INF, F32)
    acc_ref[...] = jnp.zeros(acc_ref.shape, F32)
    earlier = jnp.where(lax.broadcasted_iota(I32, (kc, kc), 1) < lax.broadcasted_iota(I32, (kc, kc), 0),
                        1.0, 0.0).astype(BF16)

    def chunk_bias(k0, tie_seen):
        kk = key_ref[pl.ds(k0, kc), :]
        kidx = k0 + lax.broadcasted_iota(I32, (kc, tq), 0)
        tie = kk == thr
        tie_f = jnp.where(tie, 1.0, 0.0)
        tie_rank = jnp.dot(earlier, tie_f.astype(BF16), preferred_element_type=F32) + tie_seen
        bias = jnp.where(kk > thr, 0.0,
                         jnp.where(tie, jnp.where(tie_rank < n_tie, 0.0, NEG_INF), NEG_INF))
        bias = jnp.where(kidx <= qpos, bias, NEG_INF)
        return bias.astype(BF16), tie_seen + jnp.sum(key_fold(tie_f), axis=0, keepdims=True)

    def attn_body(c, tie_seen):
        k0 = pl.multiple_of(c * (2 * kc), 2 * kc)
        bias_0, tie_seen = chunk_bias(k0, tie_seen)
        bias_1, tie_seen = chunk_bias(k0 + kc, tie_seen)
        bias_b = jnp.concatenate([bias_0, bias_1], axis=0)
        lg_all = _dot_nt(ka_ref[pl.ds(k0, 2 * kc), :], q_rows).astype(BF16)
        lgs = [lg_all[:, h * tq:(h + 1) * tq] + bias_b for h in range(ATTN_HEADS)]
        m_prev = m_ref[...]
        m_new = jnp.maximum(m_prev, jnp.concatenate(
            [jnp.max(lg, axis=0, keepdims=True) for lg in lgs], axis=1).astype(F32))
        m_b = m_new[0:1, :].astype(BF16)
        p = jnp.concatenate([jnp.exp(lgs[h] - m_b[:, h * tq:(h + 1) * tq])
                             for h in range(ATTN_HEADS)], axis=1)
        alpha = jnp.exp(m_prev - m_new)
        acc_ref[...] = alpha[0:1, :] * acc_ref[...] + jnp.dot(
            vat_ref[:, pl.ds(k0, 2 * kc)], p, preferred_element_type=F32)
        m_ref[...] = m_new
        return tie_seen

    lax.fori_loop(0, n_pairs, attn_body, jnp.zeros((1, tq), F32))
    o_t = acc_ref[0:HEAD_DIM, :] / acc_ref[HEAD_DIM:HEAD_DIM + 1, :]
    o_ref[...] = jnp.concatenate(
        [o_t[:, h * tq:(h + 1) * tq].T for h in range(ATTN_HEADS)], axis=1)


def _dsa(qa, iq, iw, ik, ka, va_t, bsz, seq, tq=256):
    tq = min(tq, seq)
    kc = tq
    top = min(INDEX_TOPK, seq // 4)
    n_q = seq // tq
    assert seq // 16 <= 256 and (seq // kc) % 2 == 0
    row = lambda b, i: (b * n_q + i, 0)
    per_b = lambda b, i: (b, 0, 0)
    kern = functools.partial(_dsa_kernel, tq=tq, kc=kc, top=top)
    return pl.pallas_call(
        kern,
        grid=(bsz, n_q),
        in_specs=[pl.BlockSpec((tq, A_WIDTH), row),
                  pl.BlockSpec((tq, IDX_HEADS * IDX_DIM), row),
                  pl.BlockSpec((tq, LANES), row),
                  pl.BlockSpec((None, seq, IDX_DIM), per_b),
                  pl.BlockSpec((None, seq, HEAD_DIM), per_b),
                  pl.BlockSpec((_V_ROWS, seq), lambda b, i: (0, b))],
        out_specs=pl.BlockSpec((tq, A_WIDTH), row),
        out_shape=jax.ShapeDtypeStruct((bsz * seq, A_WIDTH), F32),
        scratch_shapes=[pltpu.VMEM((seq, tq), I32),
                        pltpu.VMEM((seq, tq), I16),
                        pltpu.VMEM((seq, tq), I16),
                        pltpu.VMEM((8, ATTN_HEADS * tq), F32),
                        pltpu.VMEM((_V_ROWS, ATTN_HEADS * tq), F32)],
        compiler_params=_cparams(("parallel", "arbitrary")),
        name="dsa",
    )(qa, iq, iw, ik.reshape(bsz, seq, IDX_DIM), ka.reshape(bsz, seq, HEAD_DIM), va_t)


_HALO = 8


def _bmm(a, b):
    return lax.dot_general(a.astype(BF16), b.astype(BF16), (((2,), (1,)), ((0,), (0,))),
                           preferred_element_type=F32)


def _bmm_nt(a, b):
    return lax.dot_general(a.astype(BF16), b.astype(BF16), (((2,), (2,)), ((0,), (0,))),
                           preferred_element_type=F32)


def _gdn_kernel(qkv_ref, ab_ref, z_ref, cw_ref, alog_ref, dtb_ref, gn_ref, o_ref,
                ext_ref, state_ref, *, tr, nb):
    ti = pl.program_id(1)
    width = 3 * GDN_WIDTH
    n_c = tr // CHUNK
    n_p = GDN_HEADS // 2

    @pl.when(ti == 0)
    def _():
        ext_ref[:, 0:_HALO, :] = jnp.zeros((nb, _HALO, width), F32)
        state_ref[...] = jnp.zeros(state_ref.shape, F32)

    lane = lax.broadcasted_iota(I32, (CHUNK, LANES), 1)
    row_i = lax.broadcasted_iota(I32, (CHUNK, LANES), 0)
    left = lane < HEAD_DIM
    incl = (row_i >= lane % HEAD_DIM)[None]
    strict = (row_i > lane % HEAD_DIM)[None]
    left_t = (lax.broadcasted_iota(I32, (tr, LANES), 1) < HEAD_DIM)
    tri = jnp.where(lax.broadcasted_iota(I32, (CHUNK, CHUNK), 0) >= lax.broadcasted_iota(I32, (CHUNK, CHUNK), 1),
                    1.0, 0.0).astype(BF16)

    def blk(x):
        return jnp.concatenate([jnp.where(left[None], x, 0.0), jnp.where(left[None], 0.0, x)], axis=1)

    def head_scale(x, eps, scale):
        sq = x * x
        keep = left_t if x.shape[0] == tr else left
        s_l = jnp.sum(jnp.where(keep, sq, 0.0), axis=-1, keepdims=True)
        s_r = jnp.sum(jnp.where(keep, 0.0, sq), axis=-1, keepdims=True)
        return x * jnp.where(keep, lax.rsqrt(s_l * scale + eps), lax.rsqrt(s_r * scale + eps))

    def rows(c):
        return slice(c * CHUNK, (c + 1) * CHUNK)

    qkvs, betas, q_n, k_n, gc_cols, gc_rows = [], [], [], [], [], []
    for s in range(nb):
        ext_ref[s, _HALO:_HALO + tr, :] = qkv_ref[s]
        conv = cw_ref[CONV_WIDTH - 1:CONV_WIDTH, :] * ext_ref[s, _HALO:_HALO + tr, :]
        for j in range(CONV_WIDTH - 1):
            off = _HALO - (CONV_WIDTH - 1) + j
            conv = conv + cw_ref[j:j + 1, :] * ext_ref[s, off:off + tr, :]
        ext_ref[s, 0:_HALO, :] = ext_ref[s, tr:tr + _HALO, :]
        qkv = _silu(conv)
        qkvs.append(qkv)

        ab = ab_ref[s]
        sp_in = ab + dtb_ref[...]
        sp = jnp.maximum(sp_in, 0.0) + jnp.log1p(jnp.exp(-jnp.abs(sp_in)))
        g_all = -jnp.exp(alog_ref[...]) * sp
        betas.append(1.0 / (1.0 + jnp.exp(-ab)))

        cols, rws = [], []
        for c in range(n_c):
            g_c = g_all[rows(c), :]
            g_hi = g_c.astype(BF16)
            g_r1 = g_c - g_hi.astype(F32)
            g_mid = g_r1.astype(BF16)
            g_lo = (g_r1 - g_mid.astype(F32)).astype(BF16)
            gc = (jnp.dot(tri, g_hi, preferred_element_type=F32)
                  + jnp.dot(tri, g_mid, preferred_element_type=F32)
                  + jnp.dot(tri, g_lo, preferred_element_type=F32))
            cols.append(gc)
            rws.append(gc.T)
        gc_cols.append(cols)
        gc_rows.append(rws)
        q_n.append([head_scale(qkv[:, p * LANES:(p + 1) * LANES], RMS_EPS, 1.0) * (HEAD_DIM ** -0.5)
                    for p in range(n_p)])
        k_n.append([head_scale(qkv[:, GDN_WIDTH + p * LANES:GDN_WIDTH + (p + 1) * LANES], RMS_EPS, 1.0)
                    for p in range(n_p)])

    def stack(fn):
        return jnp.stack([fn(c, s, p) for c in range(n_c) for s in range(nb) for p in range(n_p)], axis=0)

    def pair_cols(x, c, p, base):
        return jnp.where(left, x[rows(c), base + 2 * p:base + 2 * p + 1],
                         x[rows(c), base + 2 * p + 1:base + 2 * p + 2])

    qs = stack(lambda c, s, p: q_n[s][p][rows(c), :])
    ks = stack(lambda c, s, p: k_n[s][p][rows(c), :])
    vs = stack(lambda c, s, p: qkvs[s][rows(c), 2 * GDN_WIDTH + p * LANES:2 * GDN_WIDTH + (p + 1) * LANES])
    beta = stack(lambda c, s, p: pair_cols(betas[s], c, p, GDN_HEADS))
    gcc = stack(lambda c, s, p: pair_cols(gc_cols[s][c], 0, p, 0))
    gcr = stack(lambda c, s, p: jnp.concatenate(
        [gc_rows[s][c][2 * p:2 * p + 1, :], gc_rows[s][c][2 * p + 1:2 * p + 2, :]], axis=1))
    decay = jnp.where(incl, jnp.exp(jnp.where(incl, gcc - gcr, 0.0)), 0.0)
    kb = ks * beta
    k_blk = blk(ks)
    low = jnp.where(strict, _bmm_nt(kb, k_blk) * decay, 0.0)
    e_gc = jnp.exp(gcc)
    u = vs * beta
    w = kb * e_gc
    u = u - _bmm(low, blk(u))
    w = w - _bmm(low, blk(w))
    pw = low
    for _ in range(5):
        pw = _bmm(pw, blk(pw))
        u = u + _bmm(pw, blk(u))
        w = w + _bmm(pw, blk(w))
    a_intra = jnp.where(incl, _bmm_nt(qs, k_blk) * decay, 0.0)
    q_dec = qs * e_gc
    gc_last = gcc[:, CHUNK - 1:CHUNK, :]
    kd_blk = blk(ks * jnp.exp(gc_last - gcc))
    kd_blk_t = jnp.stack([kd_blk[i].T for i in range(n_c * nb * n_p)], axis=0)
    g_last = jnp.exp(gc_last)

    gn = gn_ref[...]
    st = state_ref[...]
    per_c = nb * n_p
    for c in range(n_c):
        sl = slice(c * per_c, (c + 1) * per_c)
        v_new = blk(u[sl] - _bmm(w[sl], st))
        o = _bmm(q_dec[sl], st) + _bmm(a_intra[sl], v_new)
        st = st * g_last[sl] + _bmm(kd_blk_t[sl], v_new)
        for s in range(nb):
            for p in range(n_p):
                y = head_scale(o[s * n_p + p], RMS_EPS, 1.0 / HEAD_DIM) * gn
                o_ref[s, rows(c), p * LANES:(p + 1) * LANES] = y * _silu(
                    z_ref[s, rows(c), p * LANES:(p + 1) * LANES])
    state_ref[...] = st


def _gdn(qkvb, ab, zb, conv_w, a_log, dt_bias, gdn_norm_g, bsz, seq, tr=256):
    tr = min(tr, seq)
    n_t = seq // tr
    nb = 2 if bsz % 2 == 0 else 1
    blk3 = lambda b, i: (b, i, 0)
    const = lambda b, i: (0, 0)
    alog = _pad_cols(a_log.reshape(1, GDN_HEADS).astype(F32), LANES)
    dtb = _pad_cols(dt_bias.reshape(1, GDN_HEADS).astype(F32), LANES)
    out = pl.pallas_call(
        functools.partial(_gdn_kernel, tr=tr, nb=nb),
        grid=(bsz // nb, n_t),
        in_specs=[pl.BlockSpec((nb, tr, 3 * GDN_WIDTH), blk3),
                  pl.BlockSpec((nb, tr, LANES), blk3),
                  pl.BlockSpec((nb, tr, GDN_WIDTH), blk3),
                  pl.BlockSpec((CONV_WIDTH, 3 * GDN_WIDTH), const),
                  pl.BlockSpec((1, LANES), const),
                  pl.BlockSpec((1, LANES), const),
                  pl.BlockSpec((1, LANES), const)],
        out_specs=pl.BlockSpec((nb, tr, GDN_WIDTH), blk3),
        out_shape=jax.ShapeDtypeStruct((bsz, seq, GDN_WIDTH), F32),
        scratch_shapes=[pltpu.VMEM((nb, tr + _HALO, 3 * GDN_WIDTH), F32),
                        pltpu.VMEM((nb * (GDN_HEADS // 2), LANES, LANES), F32)],
        compiler_params=_cparams(("parallel", "arbitrary")),
        name="gdn",
    )(qkvb.reshape(bsz, seq, 3 * GDN_WIDTH), ab.reshape(bsz, seq, LANES), zb.reshape(bsz, seq, GDN_WIDTH),
      conv_w.astype(F32), alog, dtb,
      jnp.tile(gdn_norm_g.reshape(1, HEAD_DIM).astype(F32), (1, LANES // HEAD_DIM)))
    return out.reshape(bsz * seq, GDN_WIDTH)


def _memkv_kernel(m_ref, w_ref, k_ref, v_ref):
    r = jnp.dot(m_ref[...].astype(BF16), w_ref[...], preferred_element_type=F32)
    k_ref[...] = r[:, :MEM_WIDTH].astype(BF16)
    v_ref[...] = r[:, MEM_WIDTH:].astype(BF16)


def _memkv(mem2, w_mem_kv, n_mem):
    n_rows, d_model = mem2.shape
    row = lambda i: (i, 0)
    return pl.pallas_call(
        _memkv_kernel,
        grid=(n_rows // n_mem,),
        in_specs=[pl.BlockSpec((n_mem, d_model), row),
                  pl.BlockSpec((d_model, 2 * MEM_WIDTH), lambda i: (0, 0))],
        out_specs=[pl.BlockSpec((n_mem, MEM_WIDTH), row)] * 2,
        out_shape=[jax.ShapeDtypeStruct((n_rows, MEM_WIDTH), BF16)] * 2,
        compiler_params=_cparams(("parallel",)),
        name="memkv",
    )(mem2, w_mem_kv.astype(BF16))


_ROUTE_ROWS = 128


def _post_kernel(x_ref, oa_ref, ob_ref, qc_ref, mk_ref, mv_ref, wout_ref, ag_ref, mg_ref,
                 lg_ref, lb_ref, wr_ref, br_ref, x1_ref, ids_ref, gate_ref, *, alpha):
    tm = x_ref.shape[0]
    oa = oa_ref[...]
    oa = oa * lax.rsqrt(jnp.mean(oa * oa, axis=-1, keepdims=True) + RMS_EPS) * ag_ref[...]

    qc = qc_ref[...]
    mk = mk_ref[...]
    mv = mv_ref[...]
    cols = [slice(h * HEAD_DIM, (h + 1) * HEAD_DIM) for h in range(MEM_HEADS)]
    lgs = [_dot_nt(qc[:, c], mk[:, c]) * (HEAD_DIM ** -0.5) for c in cols]
    es = [jnp.exp(lg - jnp.max(lg, axis=-1, keepdims=True)) for lg in lgs]
    ps = [e / jnp.sum(e, axis=-1, keepdims=True) for e in es]
    oc = jnp.concatenate([jnp.dot(p.astype(BF16), mv[:, c], preferred_element_type=F32)
                          for p, c in zip(ps, cols)], axis=1)
    oc = oc * lax.rsqrt(jnp.mean(oc * oc, axis=-1, keepdims=True) + RMS_EPS) * mg_ref[...]

    mix = (jnp.dot(oa.astype(BF16), wout_ref[0:A_WIDTH, :], preferred_element_type=F32)
           + jnp.dot(ob_ref[...].astype(BF16), wout_ref[A_WIDTH:A_WIDTH + GDN_WIDTH, :],
                     preferred_element_type=F32)
           + jnp.dot(oc.astype(BF16), wout_ref[A_WIDTH + GDN_WIDTH:, :], preferred_element_type=F32))
    hres = alpha * x_ref[...] + mix
    mu = jnp.mean(hres, axis=-1, keepdims=True)
    var = jnp.mean(jnp.square(hres - mu), axis=-1, keepdims=True)
    x1 = (hres - mu) * lax.rsqrt(var + LN_EPS) * lg_ref[...] + lb_ref[...]
    x1_ref[...] = x1

    lt = _dot_nt(wr_ref[...], x1) + br_ref[...]
    sub = lax.broadcasted_iota(I32, (EXPERTS_PER_GROUP, tm), 0)
    gl = lt[0:N_GROUPS, :]
    gmax = jnp.max(gl, axis=0, keepdims=True)
    gprob = jnp.exp(gl - gmax) / jnp.sum(jnp.exp(gl - gmax), axis=0, keepdims=True)
    p_grp = jnp.max(gprob, axis=0, keepdims=True)
    grp = jnp.min(jnp.where(gprob == p_grp, sub, N_GROUPS), axis=0, keepdims=True)
    el = jnp.zeros((EXPERTS_PER_GROUP, tm), F32)
    for g in range(N_GROUPS):
        r0 = N_GROUPS + g * EXPERTS_PER_GROUP
        el = el + jnp.where(grp == g, lt[r0:r0 + EXPERTS_PER_GROUP, :], 0.0)
    ee = jnp.exp(el - jnp.max(el, axis=0, keepdims=True))
    pe = ee / jnp.sum(ee, axis=0, keepdims=True)
    p1 = jnp.max(pe, axis=0, keepdims=True)
    i1 = jnp.min(jnp.where(pe == p1, sub, EXPERTS_PER_GROUP), axis=0, keepdims=True)
    rest = jnp.where(sub == i1, -1.0, pe)
    p2 = jnp.max(rest, axis=0, keepdims=True)
    i2 = jnp.min(jnp.where(rest == p2, sub, EXPERTS_PER_GROUP), axis=0, keepdims=True)
    psum = p1 + p2
    g1 = p_grp * p1 / psum
    g2 = p_grp * p2 / psum
    e1 = grp * EXPERTS_PER_GROUP + i1
    e2 = grp * EXPERTS_PER_GROUP + i2
    ids_ref[...] = jnp.where(sub == 0, e1, jnp.where(sub == 1, e2, 0))
    gate_ref[...] = jnp.where(sub == 0, g1, jnp.where(sub == 1, g2, 0.0))


def _post(x2, oa, ob, qc, mk, mv, w_out, attn_g, mem_g, ln_g, ln_b, w_group, b_group,
          w_router, b_router, seq, n_mem, alpha, tm=512):
    n_tok, d_model = x2.shape
    n_t = seq // tm
    row = lambda i: (i, 0)
    const = lambda i: (0, 0)
    per_b = lambda i: (i // n_t, 0, 0)
    lane = lambda i: (0, i)
    bsz = n_tok // seq
    wr = jnp.pad(jnp.concatenate([w_group, w_router], axis=1).T,
                 ((0, _ROUTE_ROWS - N_GROUPS - N_EXPERTS), (0, 0))).astype(BF16)
    br = jnp.pad(jnp.concatenate([b_group, b_router]),
                 (0, _ROUTE_ROWS - N_GROUPS - N_EXPERTS)).reshape(_ROUTE_ROWS, 1).astype(F32)
    vec = lambda v: v.reshape(1, -1).astype(F32)
    return pl.pallas_call(
        functools.partial(_post_kernel, alpha=alpha),
        grid=(n_tok // tm,),
        in_specs=[pl.BlockSpec((tm, d_model), row),
                  pl.BlockSpec((tm, A_WIDTH), row),
                  pl.BlockSpec((tm, GDN_WIDTH), row),
                  pl.BlockSpec((tm, MEM_WIDTH), row),
                  pl.BlockSpec((None, n_mem, MEM_WIDTH), per_b),
                  pl.BlockSpec((None, n_mem, MEM_WIDTH), per_b),
                  pl.BlockSpec(w_out.shape, const),
                  pl.BlockSpec((1, A_WIDTH), const),
                  pl.BlockSpec((1, MEM_WIDTH), const),
                  pl.BlockSpec((1, d_model), const),
                  pl.BlockSpec((1, d_model), const),
                  pl.BlockSpec((_ROUTE_ROWS, d_model), const),
                  pl.BlockSpec((_ROUTE_ROWS, 1), const)],
        out_specs=[pl.BlockSpec((tm, d_model), row),
                   pl.BlockSpec((EXPERTS_PER_GROUP, tm), lane),
                   pl.BlockSpec((EXPERTS_PER_GROUP, tm), lane)],
        out_shape=[jax.ShapeDtypeStruct((n_tok, d_model), F32),
                   jax.ShapeDtypeStruct((EXPERTS_PER_GROUP, n_tok), I32),
                   jax.ShapeDtypeStruct((EXPERTS_PER_GROUP, n_tok), F32)],
        compiler_params=_cparams(("parallel",)),
        name="post",
    )(x2, oa, ob, qc, mk.reshape(bsz, n_mem, MEM_WIDTH), mv.reshape(bsz, n_mem, MEM_WIDTH),
      w_out.astype(BF16), vec(attn_g), vec(mem_g), vec(ln_g), vec(ln_b), wr, br)


def _rank_kernel(ids_ref, rank_ref, cnt_ref, carry_ref):
    i = pl.program_id(0)
    tm = ids_ref.shape[1]

    @pl.when(i == 0)
    def _():
        carry_ref[...] = jnp.zeros(carry_ref.shape, F32)

    ids = ids_ref[...]
    eio = lax.broadcasted_iota(I32, (N_EXPERTS, tm), 0)
    oh0 = jnp.where(eio == ids[0:1, :], 1.0, 0.0)
    oh1 = jnp.where(eio == ids[1:2, :], 1.0, 0.0)
    cnt = oh0 + oh1
    before = (lax.broadcasted_iota(I32, (tm, tm), 0) < lax.broadcasted_iota(I32, (tm, tm), 1))
    prefix = jnp.dot(cnt.astype(BF16), jnp.where(before, 1.0, 0.0).astype(BF16),
                     preferred_element_type=F32) + carry_ref[:, 0:1]
    r0 = jnp.sum(oh0 * prefix, axis=0, keepdims=True)
    r1 = jnp.sum(oh1 * prefix, axis=0, keepdims=True)
    sub = lax.broadcasted_iota(I32, (EXPERTS_PER_GROUP, tm), 0)
    rank_ref[...] = jnp.where(sub == 0, r0, jnp.where(sub == 1, r1, 0.0)).astype(I32)
    carry_ref[...] = carry_ref[...] + jnp.sum(cnt, axis=1, keepdims=True)
    cnt_ref[...] = carry_ref[...].astype(I32)


def _rank(ids, tm=512):
    n_tok = ids.shape[1]
    lane = lambda i: (0, i)
    return pl.pallas_call(
        _rank_kernel,
        grid=(n_tok // tm,),
        in_specs=[pl.BlockSpec((EXPERTS_PER_GROUP, tm), lane)],
        out_specs=[pl.BlockSpec((EXPERTS_PER_GROUP, tm), lane),
                   pl.BlockSpec((N_EXPERTS, LANES), lambda i: (0, 0))],
        out_shape=[jax.ShapeDtypeStruct((EXPERTS_PER_GROUP, n_tok), I32),
                   jax.ShapeDtypeStruct((N_EXPERTS, LANES), I32)],
        scratch_shapes=[pltpu.VMEM((N_EXPERTS, LANES), F32)],
        compiler_params=_cparams(("arbitrary",)),
        name="rank",
    )(ids)


def _dest_kernel(ids_ref, rank_ref, ps_ref, dest_ref):
    ids = ids_ref[...]
    tm = ids.shape[1]
    eio = lax.broadcasted_iota(I32, (N_EXPERTS, tm), 0)
    ps = ps_ref[...]
    d0 = jnp.sum(jnp.where(eio == ids[0:1, :], ps, 0.0), axis=0, keepdims=True)
    d1 = jnp.sum(jnp.where(eio == ids[1:2, :], ps, 0.0), axis=0, keepdims=True)
    sub = lax.broadcasted_iota(I32, (EXPERTS_PER_GROUP, tm), 0)
    dest_ref[...] = rank_ref[...] + jnp.where(sub == 0, d0, jnp.where(sub == 1, d1, 0.0)).astype(I32)


def _dest(ids, rank, pad_start, tm=2048):
    n_tok = ids.shape[1]
    tm = min(tm, n_tok)
    lane = lambda i: (0, i)
    return pl.pallas_call(
        _dest_kernel,
        grid=(n_tok // tm,),
        in_specs=[pl.BlockSpec((EXPERTS_PER_GROUP, tm), lane),
                  pl.BlockSpec((EXPERTS_PER_GROUP, tm), lane),
                  pl.BlockSpec((N_EXPERTS, 1), lambda i: (0, 0))],
        out_specs=pl.BlockSpec((EXPERTS_PER_GROUP, tm), lane),
        out_shape=jax.ShapeDtypeStruct((EXPERTS_PER_GROUP, n_tok), I32),
        compiler_params=_cparams(("parallel",)),
        name="dest",
    )(ids, rank, pad_start.astype(F32).reshape(N_EXPERTS, 1))


def _sc_mesh():
    return plsc.VectorSubcoreMesh(core_axis_name="c", subcore_axis_name="s",
                                  num_cores=SC_CORES, num_subcores=SC_SUBCORES)


def _sc_scatter_rows(x1, dest, pad_rows, cap):
    n_tok, d = x1.shape
    n_pad = pad_rows.shape[0]
    n_workers = SC_CORES * SC_SUBCORES
    tok_per_worker = n_tok // n_workers
    pad_per_worker = n_pad // n_workers
    n_win = tok_per_worker // SC_ROWS
    assert n_tok % (n_workers * SC_ROWS) == 0 and n_pad % (n_workers * SC_ROWS) == 0 and TOP_K == 2
    zero_rows = jnp.zeros((SC_ROWS, d), x1.dtype)

    @functools.partial(
        pl.kernel, mesh=_sc_mesh(), out_type=jax.ShapeDtypeStruct((cap, d), x1.dtype),
        scratch_types=[pltpu.VMEM((tok_per_worker,), I32), pltpu.VMEM((tok_per_worker,), I32),
                       pltpu.VMEM((pad_per_worker,), I32), pltpu.VMEM((2, SC_ROWS, d), x1.dtype),
                       pltpu.SemaphoreType.DMA((2,)), pltpu.SemaphoreType.DMA((2,))],
        name="sc_scatter")
    def scatter(x_hbm, dest_hbm, pad_hbm, zero_hbm, out_hbm, idx0_v, idx1_v, pad_v, rows_v, lsem, ssem):
        worker = lax.axis_index("s") * SC_CORES + lax.axis_index("c")
        base = worker * tok_per_worker
        pltpu.sync_copy(dest_hbm.at[pl.ds(base, tok_per_worker)], idx0_v)
        pltpu.sync_copy(dest_hbm.at[pl.ds(n_tok + base, tok_per_worker)], idx1_v)
        pltpu.sync_copy(pad_hbm.at[pl.ds(worker * pad_per_worker, pad_per_worker)], pad_v)

        def load(i, slot):
            return pltpu.make_async_copy(x_hbm.at[pl.ds(base + i * SC_ROWS, SC_ROWS)],
                                         rows_v.at[slot], lsem.at[slot])

        def store(idx_v, i, slot):
            return pltpu.make_async_copy(rows_v.at[slot],
                                         out_hbm.at[idx_v.at[pl.ds(i * SC_ROWS, SC_ROWS)]], ssem.at[slot])

        load(0, 0).start()

        @pl.loop(0, n_win)
        def _(i):
            slot = i % 2
            load(i, slot).wait()
            store(idx0_v, i, slot).start()
            store(idx1_v, i, slot).start()

            @pl.when(i >= 1)
            def _():
                store(idx0_v, i - 1, 1 - slot).wait()
                store(idx1_v, i - 1, 1 - slot).wait()

            @pl.when(i + 1 < n_win)
            def _():
                load(i + 1, 1 - slot).start()

        last = (n_win - 1) % 2
        store(idx0_v, n_win - 1, last).wait()
        store(idx1_v, n_win - 1, last).wait()

        pltpu.sync_copy(zero_hbm, rows_v.at[0])

        @pl.loop(0, pad_per_worker // SC_ROWS)
        def _(i):
            pltpu.async_copy(rows_v.at[0], out_hbm.at[pad_v.at[pl.ds(i * SC_ROWS, SC_ROWS)]],
                             ssem.at[0]).wait()

    return scatter(x1, dest, pad_rows, zero_rows)


def _mlp_kernel(be_ref, nused_ref, x_ref, wg_ref, wu_ref, wd_ref, y_ref):
    i = pl.program_id(0)

    @pl.when(i < nused_ref[0])
    def _():
        xb = x_ref[...].astype(BF16)
        hg = jnp.dot(xb, wg_ref[...].astype(BF16), preferred_element_type=F32)
        hu = jnp.dot(xb, wu_ref[...].astype(BF16), preferred_element_type=F32)
        y_ref[...] = jnp.dot((_silu(hg) * hu).astype(BF16), wd_ref[...].astype(BF16),
                             preferred_element_type=F32)

    @pl.when(i >= nused_ref[0])
    def _():
        y_ref[...] = jnp.zeros(y_ref.shape, F32)


def _mlp(blk_expert, n_used, xbuf, w_gate, w_up, w_down):
    cap, d_model = xbuf.shape
    d_exp = w_gate.shape[2]
    blk = lambda i, be, nu: (i, 0)
    used_blk = lambda i, be, nu: (jnp.minimum(i, nu[0] - 1), 0)
    wsel = lambda i, be, nu: (be[i], 0, 0)
    return pl.pallas_call(
        _mlp_kernel,
        grid_spec=pltpu.PrefetchScalarGridSpec(
            num_scalar_prefetch=2,
            grid=(cap // MOE_BLOCK,),
            in_specs=[pl.BlockSpec((MOE_BLOCK, d_model), used_blk),
                      pl.BlockSpec((None, d_model, d_exp), wsel),
                      pl.BlockSpec((None, d_model, d_exp), wsel),
                      pl.BlockSpec((None, d_exp, d_model), wsel)],
            out_specs=pl.BlockSpec((MOE_BLOCK, d_model), blk)),
        out_shape=jax.ShapeDtypeStruct((cap, d_model), F32),
        compiler_params=_cparams(("arbitrary",)),
        name="mlp",
    )(blk_expert, n_used, xbuf, w_gate, w_up, w_down)


def _sc_gather_rows(table, idx):
    n = idx.shape[0]
    d = table.shape[1]
    n_workers = SC_CORES * SC_SUBCORES
    per_worker = n // n_workers
    n_win = per_worker // SC_ROWS
    assert n % (n_workers * SC_ROWS) == 0

    @functools.partial(
        pl.kernel, mesh=_sc_mesh(), out_type=jax.ShapeDtypeStruct((n, d), table.dtype),
        scratch_types=[pltpu.VMEM((per_worker,), I32), pltpu.VMEM((2, SC_ROWS, d), table.dtype),
                       pltpu.SemaphoreType.DMA((2,)), pltpu.SemaphoreType.DMA((2,))],
        name="sc_gather")
    def gather(table_hbm, idx_hbm, out_hbm, idx_v, rows_v, gsem, wsem):
        worker = lax.axis_index("s") * SC_CORES + lax.axis_index("c")
        base = worker * per_worker
        pltpu.sync_copy(idx_hbm.at[pl.ds(base, per_worker)], idx_v)

        def fetch(i, slot):
            return pltpu.make_async_copy(table_hbm.at[idx_v.at[pl.ds(i * SC_ROWS, SC_ROWS)]],
                                         rows_v.at[slot], gsem.at[slot])

        def write(i, slot):
            return pltpu.make_async_copy(rows_v.at[slot],
                                         out_hbm.at[pl.ds(base + i * SC_ROWS, SC_ROWS)], wsem.at[slot])

        fetch(0, 0).start()

        @pl.loop(0, n_win)
        def _(i):
            slot = i % 2
            fetch(i, slot).wait()
            write(i, slot).start()

            @pl.when(i >= 1)
            def _():
                write(i - 1, 1 - slot).wait()

            @pl.when(i + 1 < n_win)
            def _():
                fetch(i + 1, 1 - slot).start()

        write(n_win - 1, (n_win - 1) % 2).wait()

    return gather(table, idx)


def _combine_kernel(x1_ref, y0_ref, y1_ref, gt_ref, lg_ref, lb_ref, o_ref, *, alpha):
    gt = gt_ref[...]
    ffn = y0_ref[...] * gt[:, 0:1] + y1_ref[...] * gt[:, 1:2]
    hres = alpha * x1_ref[...] + ffn
    mu = jnp.mean(hres, axis=-1, keepdims=True)
    var = jnp.mean(jnp.square(hres - mu), axis=-1, keepdims=True)
    o_ref[...] = (hres - mu) * lax.rsqrt(var + LN_EPS) * lg_ref[...] + lb_ref[...]


def _combine(dest, x1, gates_t, ln_g, ln_b, ybuf, alpha, tc=512):
    n_tok, d_model = x1.shape
    n_tiles = n_tok // tc
    yrows = _sc_gather_rows(ybuf, dest.reshape(-1))
    row = lambda i: (i, 0)
    const = lambda i: (0, 0)
    vec = lambda v: v.reshape(1, -1).astype(F32)
    return pl.pallas_call(
        functools.partial(_combine_kernel, alpha=alpha),
        grid=(n_tiles,),
        in_specs=[pl.BlockSpec((tc, d_model), row),
                  pl.BlockSpec((tc, d_model), row),
                  pl.BlockSpec((tc, d_model), lambda i: (i + n_tiles, 0)),
                  pl.BlockSpec((tc, EXPERTS_PER_GROUP), row),
                  pl.BlockSpec((1, d_model), const),
                  pl.BlockSpec((1, d_model), const)],
        out_specs=pl.BlockSpec((tc, d_model), row),
        out_shape=jax.ShapeDtypeStruct((n_tok, d_model), F32),
        compiler_params=_cparams(("parallel",)),
        name="combine",
    )(x1, yrows, yrows, gates_t, vec(ln_g), vec(ln_b))


def _moe(x1, ids, gates, w_gate, w_up, w_down, ln_g, ln_b, alpha):
    n_tok, d_model = x1.shape
    rank, counts = _rank(ids)
    counts = counts[:, 0]
    padded = (counts + MOE_BLOCK - 1) // MOE_BLOCK * MOE_BLOCK
    pad_ends = jnp.cumsum(padded)
    pad_start = (pad_ends - padded).astype(I32)
    n_asg = n_tok * TOP_K
    cap = (n_asg + MOE_BLOCK - 1) // MOE_BLOCK * MOE_BLOCK + N_EXPERTS * MOE_BLOCK
    n_blk = cap // MOE_BLOCK
    blk_pos = jnp.arange(n_blk, dtype=I32) * MOE_BLOCK
    blk_expert = jnp.minimum(
        jnp.sum((pad_ends[None, :] <= blk_pos[:, None]).astype(I32), axis=1), N_EXPERTS - 1)
    n_used = (pad_ends[-1:] // MOE_BLOCK).astype(I32)
    dest = _dest(ids, rank, pad_start)[0:TOP_K]
    slot = jnp.arange(MOE_BLOCK, dtype=I32)[None, :]
    spare = cap - 1 - (jnp.arange(N_EXPERTS * MOE_BLOCK, dtype=I32).reshape(N_EXPERTS, MOE_BLOCK)
                       % N_EXPERTS)
    pad_rows = jnp.where(slot < (padded - counts)[:, None], (pad_start + counts)[:, None] + slot,
                         spare).reshape(-1)
    xbuf = _sc_scatter_rows(x1, dest.reshape(-1), pad_rows, cap)
    ybuf = _mlp(blk_expert, n_used, xbuf, w_gate, w_up, w_down)
    return _combine(dest, x1, gates.T, ln_g, ln_b, ybuf, alpha)


def kernel(x, mem, w_in, kv_norm_g, w_k_up, w_v_up, conv_w, A_log, dt_bias, gdn_norm_g, attn_norm_g, mem_norm_g, w_mem_kv, w_out, ln1_g, ln1_b, w_group, b_group, w_router, b_router, w_gate, w_up, w_down, ln2_g, ln2_b):
    bsz, seq, d_model = x.shape
    n_mem = mem.shape[1]
    depth = w_in.shape[0]
    alpha = (2 * depth) ** 0.25
    inv_freq = 1.0 / (ROPE_THETA ** (jnp.arange(0, HEAD_DIM, 2, dtype=F32) / HEAD_DIM))
    ang = jnp.arange(seq, dtype=F32)[:, None] * inv_freq[None, :]
    cos128 = jnp.tile(jnp.cos(ang), (1, LANES // (HEAD_DIM // 2)))
    sin128 = jnp.tile(jnp.sin(ang), (1, LANES // (HEAD_DIM // 2)))
    x2 = x.reshape(bsz * seq, d_model)
    mem2 = mem.reshape(bsz * n_mem, d_model)
    for l in range(depth):
        qa, iq, ik, ka, va, iw, qkvb, zb, ab, qc = _proj(
            x2, w_in[l], w_k_up[l], w_v_up[l], kv_norm_g[l], cos128, sin128, seq)
        oa = _dsa(qa, iq, iw, ik, ka, va, bsz, seq)
        ob = _gdn(qkvb, ab, zb, conv_w[l], A_log[l], dt_bias[l], gdn_norm_g[l], bsz, seq)
        mk, mv = _memkv(mem2, w_mem_kv[l], n_mem)
        x1, ids, gates = _post(x2, oa, ob, qc, mk, mv, w_out[l], attn_norm_g[l], mem_norm_g[l],
                               ln1_g[l], ln1_b[l], w_group[l], b_group[l], w_router[l],
                               b_router[l], seq, n_mem, alpha)
        x2 = _moe(x1, ids, gates, w_gate[l], w_up[l], w_down[l], ln2_g[l], ln2_b[l], alpha)
    return x2.reshape(bsz, seq, d_model)
```

```python
import functools

import numpy as np
import jax
import jax.numpy as jnp
from jax import lax
from jax.experimental import pallas as pl
from jax.experimental.pallas import tpu as pltpu
from jax.experimental.pallas import tpu_sc as plsc

F32 = jnp.float32
BF16 = jnp.bfloat16
I32 = jnp.int32
I16 = jnp.int16

HEAD_DIM = 64
ATTN_HEADS = 6
A_WIDTH = ATTN_HEADS * HEAD_DIM
KV_RANK = 128
IDX_HEADS = 4
IDX_DIM = 64
INDEX_TOPK = 256
GDN_HEADS = 6
GDN_WIDTH = GDN_HEADS * HEAD_DIM
CONV_WIDTH = 4
CHUNK = 64
MEM_HEADS = 4
MEM_WIDTH = MEM_HEADS * HEAD_DIM
SPLIT_SIZES = (A_WIDTH, KV_RANK, IDX_HEADS * IDX_DIM, IDX_DIM, IDX_HEADS,
               GDN_WIDTH, GDN_WIDTH, GDN_WIDTH, GDN_WIDTH, GDN_HEADS, GDN_HEADS,
               MEM_WIDTH)
ROPE_THETA = 10000.0
N_GROUPS = 8
EXPERTS_PER_GROUP = 8
N_EXPERTS = N_GROUPS * EXPERTS_PER_GROUP
TOP_K = 2
MOE_BLOCK = 256
LN_EPS = 1e-5
RMS_EPS = 1e-6
NEG_INF = -1e30
INT_MIN = -2 ** 31
I16_MIN = -2 ** 15

LANES = 128
SC_CORES = 2
SC_SUBCORES = 16
SC_ROWS = 32
VMEM_LIMIT = 56 * 1024 * 1024


def _cparams(sem):
    return pltpu.CompilerParams(dimension_semantics=sem, vmem_limit_bytes=VMEM_LIMIT)


def _dot(a, b):
    return jnp.dot(a.astype(BF16), b.astype(BF16), preferred_element_type=F32)


def _dot_nt(a, b):
    return lax.dot_general(a.astype(BF16), b.astype(BF16), (((1,), (1,)), ((), ())),
                           preferred_element_type=F32)


def _silu(t):
    return t * (1.0 / (1.0 + jnp.exp(-t)))


_P_QA = (0, 384)
_P_IQ = (384, 640)
_P_CKV = (640, 768)
_P_IK = (768, 896)
_P_IW = (896, 1024)
_P_QKVB = (1024, 2176)
_P_ZB = (2176, 2560)
_P_AB = (2560, 2688)
_P_QC = (2688, 2944)
_P_TOTAL = 2944
_V_ROWS = HEAD_DIM + 16


def _rot_cols(w, n_heads):
    k = w.shape[0]
    w4 = w.reshape(k, n_heads, 2, HEAD_DIM // 2)
    return jnp.concatenate([-w4[:, :, 1:2], w4[:, :, 0:1]], axis=2).reshape(k, n_heads * HEAD_DIM)


def _pad_cols(w, width):
    return jnp.pad(w, ((0, 0), (0, width - w.shape[1])))


def _proj_kernel(x_ref, w_ref, w2_ref, cos_ref, sin_ref, kvg_ref,
                 qa_ref, iq_ref, ik_ref, ka_ref, va_ref, iw_ref, qkvb_ref, zb_ref, ab_ref, qc_ref):
    xb = x_ref[...].astype(BF16)

    def mm(slab):
        return jnp.dot(xb, w_ref[:, slab[0]:slab[1]], preferred_element_type=F32)

    cos = cos_ref[...]
    sin = sin_ref[...]
    cos3 = jnp.concatenate([cos] * 3, axis=1)
    sin3 = jnp.concatenate([sin] * 3, axis=1)
    cos2 = jnp.concatenate([cos] * 2, axis=1)
    sin2 = jnp.concatenate([sin] * 2, axis=1)
    cos64 = cos[:, :HEAD_DIM]
    sin64 = sin[:, :HEAD_DIM]

    def rotate_half(t):
        slabs = []
        for j in range(t.shape[1] // LANES):
            ts = t[:, j * LANES:(j + 1) * LANES]
            first = lax.broadcasted_iota(I32, ts.shape, 1) % HEAD_DIM < HEAD_DIM // 2
            slabs.append(jnp.where(first, -pltpu.roll(ts, LANES - HEAD_DIM // 2, 1),
                                   pltpu.roll(ts, HEAD_DIM // 2, 1)))
        return jnp.concatenate(slabs, axis=1)

    qa = mm(_P_QA)
    qa_ref[...] = (qa * cos3 + rotate_half(qa) * sin3).astype(BF16)
    iq = mm(_P_IQ)
    iq_ref[...] = (iq * cos2 + rotate_half(iq) * sin2).astype(BF16)

    ckv = mm(_P_CKV)
    cn = ckv * lax.rsqrt(jnp.mean(ckv * ckv, axis=-1, keepdims=True) + RMS_EPS) * kvg_ref[...]
    r = jnp.dot(cn.astype(BF16), w2_ref[...], preferred_element_type=F32)
    ka_ref[...] = (r[:, 0:64] * cos64 + r[:, 64:128] * sin64).astype(BF16)
    va_ref[...] = jnp.concatenate(
        [r[:, 128:256].T[0:HEAD_DIM, :], jnp.ones((_V_ROWS - HEAD_DIM, r.shape[0]), F32)],
        axis=0).astype(BF16)

    ikk = mm(_P_IK)
    ik_ref[...] = (ikk[:, 0:64] * cos64 + ikk[:, 64:128] * sin64).astype(BF16)
    iw_ref[...] = mm(_P_IW) * (IDX_HEADS ** -0.5 * IDX_DIM ** -0.5)
    qkvb_ref[...] = mm(_P_QKVB)
    zb_ref[...] = mm(_P_ZB)
    ab_ref[...] = mm(_P_AB)
    qc_ref[...] = mm(_P_QC).astype(BF16)


def _proj(x2, w_in, w_k_up, w_v_up, kv_norm_g, cos128, sin128, seq, tm=256):
    n_tok, d_model = x2.shape
    offs = np.cumsum(SPLIT_SIZES)[:-1].tolist()
    (w_qa, w_ckv, w_iq, w_ik, w_iw, w_qb, w_kb, w_vb, w_zb, w_a, w_b, w_qc) = jnp.split(w_in, offs, axis=1)
    w1 = jnp.concatenate([
        w_qa, w_iq, w_ckv,
        w_ik, _rot_cols(w_ik, 1), _pad_cols(w_iw, LANES),
        w_qb, w_kb, w_vb, w_zb, _pad_cols(jnp.concatenate([w_a, w_b], axis=1), LANES), w_qc,
    ], axis=1).astype(BF16)
    assert w1.shape[1] == _P_TOTAL
    w2 = _pad_cols(jnp.concatenate([w_k_up, _rot_cols(w_k_up, 1), w_v_up], axis=1), 2 * LANES).astype(BF16)
    n_pos = seq // tm
    row = lambda i: (i, 0)
    const = lambda i: (0, 0)
    pos = lambda i: (i % n_pos, 0)
    outs = [(A_WIDTH, BF16), (IDX_HEADS * IDX_DIM, BF16), (IDX_DIM, BF16), (HEAD_DIM, BF16),
            None, (LANES, F32), (3 * GDN_WIDTH, F32), (GDN_WIDTH, F32), (LANES, F32),
            (MEM_WIDTH, BF16)]
    out_specs = [pl.BlockSpec((tm, o[0]), row) if o else pl.BlockSpec((_V_ROWS, tm), lambda i: (0, i))
                 for o in outs]
    out_shape = [jax.ShapeDtypeStruct((n_tok, o[0]), o[1]) if o
                 else jax.ShapeDtypeStruct((_V_ROWS, n_tok), BF16) for o in outs]
    return pl.pallas_call(
        _proj_kernel,
        grid=(n_tok // tm,),
        in_specs=[pl.BlockSpec((tm, d_model), row),
                  pl.BlockSpec(w1.shape, const),
                  pl.BlockSpec(w2.shape, const),
                  pl.BlockSpec((tm, LANES), pos),
                  pl.BlockSpec((tm, LANES), pos),
                  pl.BlockSpec((1, KV_RANK), const)],
        out_specs=out_specs,
        out_shape=out_shape,
        compiler_params=_cparams(("parallel",)),
        name="proj",
    )(x2, w1, w2, cos128, sin128, kv_norm_g.reshape(1, KV_RANK).astype(F32))


def _dsa_kernel(qa_ref, iq_ref, iw_ref, ik_ref, ka_ref, vat_ref, o_ref,
                key_ref, hi_ref, lo_ref, m_ref, acc_ref, *, tq, kc, top):
    qi = pl.program_id(1)
    row0 = qi * tq
    n_kc = (row0 + tq + kc - 1) // kc
    qpos = row0 + lax.broadcasted_iota(I32, (1, tq), 1)

    def key_fold(v):
        return jnp.sum(v.reshape(kc // 8, 8, tq), axis=0)

    def head_rows(x, n_heads, width):
        return jnp.concatenate([x[:, h * width:(h + 1) * width] for h in range(n_heads)], axis=0)

    iq_rows = head_rows(iq_ref[...], IDX_HEADS, IDX_DIM)
    iw_t = iw_ref[...].T

    def score_body(c, carry):
        k0 = pl.multiple_of(c * kc, kc)
        d = _dot_nt(ik_ref[pl.ds(k0, kc), :], iq_rows)
        s = jnp.zeros((kc, tq), F32)
        for h in range(IDX_HEADS):
            s = s + iw_t[h:h + 1, :] * jnp.maximum(d[:, h * tq:(h + 1) * tq], 0.0)
        kidx = k0 + lax.broadcasted_iota(I32, (kc, tq), 0)
        s = jnp.where(s == 0.0, 0.0, s)
        s = jnp.where(kidx <= qpos, s, NEG_INF)
        bits = pltpu.bitcast(s, I32)
        key = jnp.where(bits >= 0, bits, bits ^ 0x7FFFFFFF)
        key_ref[pl.ds(k0, kc), :] = key
        hi_ref[pl.ds(k0, kc), :] = lax.shift_right_arithmetic(key, 16).astype(I16)
        lo_ref[pl.ds(k0, kc), :] = ((key & 0xFFFF) + I16_MIN).astype(I16)
        return carry

    lax.fori_loop(0, n_kc, score_body, 0)

    k_eff = jnp.minimum(top, qpos + 1).astype(F32)

    @pl.when(n_kc % 2 == 1)
    def _():
        k_pad = pl.multiple_of(n_kc * kc, kc)
        key_ref[pl.ds(k_pad, kc), :] = jnp.full((kc, tq), INT_MIN, I32)
        hi_ref[pl.ds(k_pad, kc), :] = jnp.full((kc, tq), I16_MIN, I16)
        lo_ref[pl.ds(k_pad, kc), :] = jnp.full((kc, tq), I16_MIN, I16)

    n_pairs = (n_kc + 1) // 2
    one_b = jnp.ones((), BF16)
    zero_b = jnp.zeros((), BF16)

    def count16(ref, pred):
        def body(c, acc):
            k0 = pl.multiple_of(c * (2 * kc), 2 * kc)
            for j in range(2):
                hit = jnp.where(pred(ref[pl.ds(k0 + j * kc, kc), :]), one_b, zero_b)
                hit = hit.reshape(kc // 16, 16, tq)
                parts = [hit[r] for r in range(kc // 16)]
                while len(parts) > 1:
                    parts = [a + b for a, b in zip(parts[0::2], parts[1::2])]
                acc = acc + parts[0]
            return acc
        acc = lax.fori_loop(0, n_pairs, body, jnp.zeros((16, tq), BF16))
        return jnp.sum(acc.astype(F32), axis=0, keepdims=True)

    def search16(ref, start, n_bits, k_want):
        def bit_body(i, t):
            cand = t + lax.shift_left(jnp.int32(1), n_bits - 1 - i)
            c16 = cand.astype(I16)
            return jnp.where(count16(ref, lambda v: v >= c16) >= k_want, cand, t)
        return lax.fori_loop(0, n_bits, bit_body, start)

    hi0 = jnp.where(count16(hi_ref, lambda v: v >= 0) >= k_eff, 0, I16_MIN).astype(I32)
    thr_hi = search16(hi_ref, hi0, 15, k_eff)
    thr_hi16 = thr_hi.astype(I16)
    k_low = k_eff - count16(hi_ref, lambda v: v > thr_hi16)

    def band_body(c, carry):
        k0 = pl.multiple_of(c * kc, kc)
        lo_ref[pl.ds(k0, kc), :] = jnp.where(hi_ref[pl.ds(k0, kc), :] == thr_hi16,
                                             lo_ref[pl.ds(k0, kc), :], I16_MIN)
        return carry

    lax.fori_loop(0, n_kc, band_body, 0)
    thr_lo = search16(lo_ref, jnp.full((1, tq), I16_MIN, I32), 16, k_low)
    thr_lo16 = thr_lo.astype(I16)
    thr = thr_hi * 65536 + (thr_lo - I16_MIN)
    n_tie = k_low - count16(lo_ref, lambda v: v > thr_lo16)

    q_rows = head_rows((qa_ref[...].astype(F32) * (HEAD_DIM ** -0.5)).astype(BF16),
                       ATTN_HEADS, HEAD_DIM)
    m_ref[...] = jnp.full(m_ref.shape, NEG_INF, F32)
    acc_ref[...] = jnp.zeros(acc_ref.shape, F32)
    earlier = jnp.where(lax.broadcasted_iota(I32, (kc, kc), 1) < lax.broadcasted_iota(I32, (kc, kc), 0),
                        1.0, 0.0).astype(BF16)

    def chunk_bias(k0, tie_seen):
        kk = key_ref[pl.ds(k0, kc), :]
        kidx = k0 + lax.broadcasted_iota(I32, (kc, tq), 0)
        tie = kk == thr
        tie_f = jnp.where(tie, 1.0, 0.0)
        tie_rank = jnp.dot(earlier, tie_f.astype(BF16), preferred_element_type=F32) + tie_seen
        bias = jnp.where(kk > thr, 0.0,
                         jnp.where(tie, jnp.where(tie_rank < n_tie, 0.0, NEG_INF), NEG_INF))
        bias = jnp.where(kidx <= qpos, bias, NEG_INF)
        return bias.astype(BF16), tie_seen + jnp.sum(key_fold(tie_f), axis=0, keepdims=True)

    def attn_body(c, tie_seen):
        k0 = pl.multiple_of(c * (2 * kc), 2 * kc)
        bias_0, tie_seen = chunk_bias(k0, tie_seen)
        bias_1, tie_seen = chunk_bias(k0 + kc, tie_seen)
        bias_b = jnp.concatenate([bias_0, bias_1], axis=0)
        lg_all = _dot_nt(ka_ref[pl.ds(k0, 2 * kc), :], q_rows).astype(BF16)
        lgs = [lg_all[:, h * tq:(h + 1) * tq] + bias_b for h in range(ATTN_HEADS)]
        m_prev = m_ref[...]
        m_new = jnp.maximum(m_prev, jnp.concatenate(
            [jnp.max(lg, axis=0, keepdims=True) for lg in lgs], axis=1).astype(F32))
        m_b = m_new[0:1, :].astype(BF16)
        p = jnp.concatenate([jnp.exp(lgs[h] - m_b[:, h * tq:(h + 1) * tq])
                             for h in range(ATTN_HEADS)], axis=1)
        alpha = jnp.exp(m_prev - m_new)
        acc_ref[...] = alpha[0:1, :] * acc_ref[...] + jnp.dot(
            vat_ref[:, pl.ds(k0, 2 * kc)], p, preferred_element_type=F32)
        m_ref[...] = m_new
        return tie_seen

    lax.fori_loop(0, n_pairs, attn_body, jnp.zeros((1, tq), F32))
    o_t = acc_ref[0:HEAD_DIM, :] / acc_ref[HEAD_DIM:HEAD_DIM + 1, :]
    o_ref[...] = jnp.concatenate(
        [o_t[:, h * tq:(h + 1) * tq].T for h in range(ATTN_HEADS)], axis=1)


def _dsa(qa, iq, iw, ik, ka, va_t, bsz, seq, tq=256):
    tq = min(tq, seq)
    kc = tq
    top = min(INDEX_TOPK, seq // 4)
    n_q = seq // tq
    assert seq // 16 <= 256 and (seq // kc) % 2 == 0
    row = lambda b, i: (b * n_q + i, 0)
    per_b = lambda b, i: (b, 0, 0)
    kern = functools.partial(_dsa_kernel, tq=tq, kc=kc, top=top)
    return pl.pallas_call(
        kern,
        grid=(bsz, n_q),
        in_specs=[pl.BlockSpec((tq, A_WIDTH), row),
                  pl.BlockSpec((tq, IDX_HEADS * IDX_DIM), row),
                  pl.BlockSpec((tq, LANES), row),
                  pl.BlockSpec((None, seq, IDX_DIM), per_b),
                  pl.BlockSpec((None, seq, HEAD_DIM), per_b),
                  pl.BlockSpec((_V_ROWS, seq), lambda b, i: (0, b))],
        out_specs=pl.BlockSpec((tq, A_WIDTH), row),
        out_shape=jax.ShapeDtypeStruct((bsz * seq, A_WIDTH), F32),
        scratch_shapes=[pltpu.VMEM((seq, tq), I32),
                        pltpu.VMEM((seq, tq), I16),
                        pltpu.VMEM((seq, tq), I16),
                        pltpu.VMEM((8, ATTN_HEADS * tq), F32),
                        pltpu.VMEM((_V_ROWS, ATTN_HEADS * tq), F32)],
        compiler_params=_cparams(("parallel", "arbitrary")),
        name="dsa",
    )(qa, iq, iw, ik.reshape(bsz, seq, IDX_DIM), ka.reshape(bsz, seq, HEAD_DIM), va_t)


_HALO = 8


def _bmm(a, b):
    return lax.dot_general(a.astype(BF16), b.astype(BF16), (((2,), (1,)), ((0,), (0,))),
                           preferred_element_type=F32)


def _bmm_nt(a, b):
    return lax.dot_general(a.astype(BF16), b.astype(BF16), (((2,), (2,)), ((0,), (0,))),
                           preferred_element_type=F32)


def _gdn_kernel(qkv_ref, ab_ref, z_ref, cw_ref, alog_ref, dtb_ref, gn_ref, o_ref,
                ext_ref, state_ref, *, tr, nb):
    ti = pl.program_id(1)
    width = 3 * GDN_WIDTH
    n_c = tr // CHUNK
    n_p = GDN_HEADS // 2

    @pl.when(ti == 0)
    def _():
        ext_ref[:, 0:_HALO, :] = jnp.zeros((nb, _HALO, width), F32)
        state_ref[...] = jnp.zeros(state_ref.shape, F32)

    lane = lax.broadcasted_iota(I32, (CHUNK, LANES), 1)
    row_i = lax.broadcasted_iota(I32, (CHUNK, LANES), 0)
    left = lane < HEAD_DIM
    incl = (row_i >= lane % HEAD_DIM)[None]
    strict = (row_i > lane % HEAD_DIM)[None]
    left_t = (lax.broadcasted_iota(I32, (tr, LANES), 1) < HEAD_DIM)
    tri = jnp.where(lax.broadcasted_iota(I32, (CHUNK, CHUNK), 0) >= lax.broadcasted_iota(I32, (CHUNK, CHUNK), 1),
                    1.0, 0.0).astype(BF16)

    def blk(x):
        return jnp.concatenate([jnp.where(left[None], x, 0.0), jnp.where(left[None], 0.0, x)], axis=1)

    def head_scale(x, eps, scale):
        sq = x * x
        keep = left_t if x.shape[0] == tr else left
        s_l = jnp.sum(jnp.where(keep, sq, 0.0), axis=-1, keepdims=True)
        s_r = jnp.sum(jnp.where(keep, 0.0, sq), axis=-1, keepdims=True)
        return x * jnp.where(keep, lax.rsqrt(s_l * scale + eps), lax.rsqrt(s_r * scale + eps))

    def rows(c):
        return slice(c * CHUNK, (c + 1) * CHUNK)

    qkvs, betas, q_n, k_n, gc_cols, gc_rows = [], [], [], [], [], []
    for s in range(nb):
        ext_ref[s, _HALO:_HALO + tr, :] = qkv_ref[s]
        conv = cw_ref[CONV_WIDTH - 1:CONV_WIDTH, :] * ext_ref[s, _HALO:_HALO + tr, :]
        for j in range(CONV_WIDTH - 1):
            off = _HALO - (CONV_WIDTH - 1) + j
            conv = conv + cw_ref[j:j + 1, :] * ext_ref[s, off:off + tr, :]
        ext_ref[s, 0:_HALO, :] = ext_ref[s, tr:tr + _HALO, :]
        qkv = _silu(conv)
        qkvs.append(qkv)

        ab = ab_ref[s]
        sp_in = ab + dtb_ref[...]
        sp = jnp.maximum(sp_in, 0.0) + jnp.log1p(jnp.exp(-jnp.abs(sp_in)))
        g_all = -jnp.exp(alog_ref[...]) * sp
        betas.append(1.0 / (1.0 + jnp.exp(-ab)))

        cols, rws = [], []
        for c in range(n_c):
            g_c = g_all[rows(c), :]
            g_hi = g_c.astype(BF16)
            g_r1 = g_c - g_hi.astype(F32)
            g_mid = g_r1.astype(BF16)
            g_lo = (g_r1 - g_mid.astype(F32)).astype(BF16)
            gc = (jnp.dot(tri, g_hi, preferred_element_type=F32)
                  + jnp.dot(tri, g_mid, preferred_element_type=F32)
                  + jnp.dot(tri, g_lo, preferred_element_type=F32))
            cols.append(gc)
            rws.append(gc.T)
        gc_cols.append(cols)
        gc_rows.append(rws)
        q_n.append([head_scale(qkv[:, p * LANES:(p + 1) * LANES], RMS_EPS, 1.0) * (HEAD_DIM ** -0.5)
                    for p in range(n_p)])
        k_n.append([head_scale(qkv[:, GDN_WIDTH + p * LANES:GDN_WIDTH + (p + 1) * LANES], RMS_EPS, 1.0)
                    for p in range(n_p)])

    def stack(fn):
        return jnp.stack([fn(c, s, p) for c in range(n_c) for s in range(nb) for p in range(n_p)], axis=0)

    def pair_cols(x, c, p, base):
        return jnp.where(left, x[rows(c), base + 2 * p:base + 2 * p + 1],
                         x[rows(c), base + 2 * p + 1:base + 2 * p + 2])

    qs = stack(lambda c, s, p: q_n[s][p][rows(c), :])
    ks = stack(lambda c, s, p: k_n[s][p][rows(c), :])
    vs = stack(lambda c, s, p: qkvs[s][rows(c), 2 * GDN_WIDTH + p * LANES:2 * GDN_WIDTH + (p + 1) * LANES])
    beta = stack(lambda c, s, p: pair_cols(betas[s], c, p, GDN_HEADS))
    gcc = stack(lambda c, s, p: pair_cols(gc_cols[s][c], 0, p, 0))
    gcr = stack(lambda c, s, p: jnp.concatenate(
        [gc_rows[s][c][2 * p:2 * p + 1, :], gc_rows[s][c][2 * p + 1:2 * p + 2, :]], axis=1))
    decay = jnp.where(incl, jnp.exp(jnp.where(incl, gcc - gcr, 0.0)), 0.0)
    kb = ks * beta
    k_blk = blk(ks)
    low = jnp.where(strict, _bmm_nt(kb, k_blk) * decay, 0.0)
    e_gc = jnp.exp(gcc)
    u = vs * beta
    w = kb * e_gc
    u = u - _bmm(low, blk(u))
    w = w - _bmm(low, blk(w))
    pw = low
    for _ in range(5):
        pw = _bmm(pw, blk(pw))
        u = u + _bmm(pw, blk(u))
        w = w + _bmm(pw, blk(w))
    a_intra = jnp.where(incl, _bmm_nt(qs, k_blk) * decay, 0.0)
    q_dec = qs * e_gc
    gc_last = gcc[:, CHUNK - 1:CHUNK, :]
    kd_blk = blk(ks * jnp.exp(gc_last - gcc))
    kd_blk_t = jnp.stack([kd_blk[i].T for i in range(n_c * nb * n_p)], axis=0)
    g_last = jnp.exp(gc_last)

    gn = gn_ref[...]
    st = state_ref[...]
    per_c = nb * n_p
    for c in range(n_c):
        sl = slice(c * per_c, (c + 1) * per_c)
        v_new = blk(u[sl] - _bmm(w[sl], st))
        o = _bmm(q_dec[sl], st) + _bmm(a_intra[sl], v_new)
        st = st * g_last[sl] + _bmm(kd_blk_t[sl], v_new)
        for s in range(nb):
            for p in range(n_p):
                y = head_scale(o[s * n_p + p], RMS_EPS, 1.0 / HEAD_DIM) * gn
                o_ref[s, rows(c), p * LANES:(p + 1) * LANES] = y * _silu(
                    z_ref[s, rows(c), p * LANES:(p + 1) * LANES])
    state_ref[...] = st


def _gdn(qkvb, ab, zb, conv_w, a_log, dt_bias, gdn_norm_g, bsz, seq, tr=256):
    tr = min(tr, seq)
    n_t = seq // tr
    nb = 2 if bsz % 2 == 0 else 1
    blk3 = lambda b, i: (b, i, 0)
    const = lambda b, i: (0, 0)
    alog = _pad_cols(a_log.reshape(1, GDN_HEADS).astype(F32), LANES)
    dtb = _pad_cols(dt_bias.reshape(1, GDN_HEADS).astype(F32), LANES)
    out = pl.pallas_call(
        functools.partial(_gdn_kernel, tr=tr, nb=nb),
        grid=(bsz // nb, n_t),
        in_specs=[pl.BlockSpec((nb, tr, 3 * GDN_WIDTH), blk3),
                  pl.BlockSpec((nb, tr, LANES), blk3),
                  pl.BlockSpec((nb, tr, GDN_WIDTH), blk3),
                  pl.BlockSpec((CONV_WIDTH, 3 * GDN_WIDTH), const),
                  pl.BlockSpec((1, LANES), const),
                  pl.BlockSpec((1, LANES), const),
                  pl.BlockSpec((1, LANES), const)],
        out_specs=pl.BlockSpec((nb, tr, GDN_WIDTH), blk3),
        out_shape=jax.ShapeDtypeStruct((bsz, seq, GDN_WIDTH), F32),
        scratch_shapes=[pltpu.VMEM((nb, tr + _HALO, 3 * GDN_WIDTH), F32),
                        pltpu.VMEM((nb * (GDN_HEADS // 2), LANES, LANES), F32)],
        compiler_params=_cparams(("parallel", "arbitrary")),
        name="gdn",
    )(qkvb.reshape(bsz, seq, 3 * GDN_WIDTH), ab.reshape(bsz, seq, LANES), zb.reshape(bsz, seq, GDN_WIDTH),
      conv_w.astype(F32), alog, dtb,
      jnp.tile(gdn_norm_g.reshape(1, HEAD_DIM).astype(F32), (1, LANES // HEAD_DIM)))
    return out.reshape(bsz * seq, GDN_WIDTH)


def _memkv_kernel(m_ref, w_ref, k_ref, v_ref):
    r = jnp.dot(m_ref[...].astype(BF16), w_ref[...], preferred_element_type=F32)
    k_ref[...] = r[:, :MEM_WIDTH].astype(BF16)
    v_ref[...] = r[:, MEM_WIDTH:].astype(BF16)


def _memkv(mem2, w_mem_kv, n_mem):
    n_rows, d_model = mem2.shape
    row = lambda i: (i, 0)
    return pl.pallas_call(
        _memkv_kernel,
        grid=(n_rows // n_mem,),
        in_specs=[pl.BlockSpec((n_mem, d_model), row),
                  pl.BlockSpec((d_model, 2 * MEM_WIDTH), lambda i: (0, 0))],
        out_specs=[pl.BlockSpec((n_mem, MEM_WIDTH), row)] * 2,
        out_shape=[jax.ShapeDtypeStruct((n_rows, MEM_WIDTH), BF16)] * 2,
        compiler_params=_cparams(("parallel",)),
        name="memkv",
    )(mem2, w_mem_kv.astype(BF16))


_ROUTE_ROWS = 128


def _post_kernel(x_ref, oa_ref, ob_ref, qc_ref, mk_ref, mv_ref, wout_ref, ag_ref, mg_ref,
                 lg_ref, lb_ref, wr_ref, br_ref, x1_ref, ids_ref, gate_ref, *, alpha):
    tm = x_ref.shape[0]
    oa = oa_ref[...]
    oa = oa * lax.rsqrt(jnp.mean(oa * oa, axis=-1, keepdims=True) + RMS_EPS) * ag_ref[...]

    qc = qc_ref[...]
    mk = mk_ref[...]
    mv = mv_ref[...]
    cols = [slice(h * HEAD_DIM, (h + 1) * HEAD_DIM) for h in range(MEM_HEADS)]
    lgs = [_dot_nt(qc[:, c], mk[:, c]) * (HEAD_DIM ** -0.5) for c in cols]
    es = [jnp.exp(lg - jnp.max(lg, axis=-1, keepdims=True)) for lg in lgs]
    ps = [e / jnp.sum(e, axis=-1, keepdims=True) for e in es]
    oc = jnp.concatenate([jnp.dot(p.astype(BF16), mv[:, c], preferred_element_type=F32)
                          for p, c in zip(ps, cols)], axis=1)
    oc = oc * lax.rsqrt(jnp.mean(oc * oc, axis=-1, keepdims=True) + RMS_EPS) * mg_ref[...]

    mix = (jnp.dot(oa.astype(BF16), wout_ref[0:A_WIDTH, :], preferred_element_type=F32)
           + jnp.dot(ob_ref[...].astype(BF16), wout_ref[A_WIDTH:A_WIDTH + GDN_WIDTH, :],
                     preferred_element_type=F32)
           + jnp.dot(oc.astype(BF16), wout_ref[A_WIDTH + GDN_WIDTH:, :], preferred_element_type=F32))
    hres = alpha * x_ref[...] + mix
    mu = jnp.mean(hres, axis=-1, keepdims=True)
    var = jnp.mean(jnp.square(hres - mu), axis=-1, keepdims=True)
    x1 = (hres - mu) * lax.rsqrt(var + LN_EPS) * lg_ref[...] + lb_ref[...]
    x1_ref[...] = x1

    lt = _dot_nt(wr_ref[...], x1) + br_ref[...]
    sub = lax.broadcasted_iota(I32, (EXPERTS_PER_GROUP, tm), 0)
    gl = lt[0:N_GROUPS, :]
    gmax = jnp.max(gl, axis=0, keepdims=True)
    gprob = jnp.exp(gl - gmax) / jnp.sum(jnp.exp(gl - gmax), axis=0, keepdims=True)
    p_grp = jnp.max(gprob, axis=0, keepdims=True)
    grp = jnp.min(jnp.where(gprob == p_grp, sub, N_GROUPS), axis=0, keepdims=True)
    el = jnp.zeros((EXPERTS_PER_GROUP, tm), F32)
    for g in range(N_GROUPS):
        r0 = N_GROUPS + g * EXPERTS_PER_GROUP
        el = el + jnp.where(grp == g, lt[r0:r0 + EXPERTS_PER_GROUP, :], 0.0)
    ee = jnp.exp(el - jnp.max(el, axis=0, keepdims=True))
    pe = ee / jnp.sum(ee, axis=0, keepdims=True)
    p1 = jnp.max(pe, axis=0, keepdims=True)
    i1 = jnp.min(jnp.where(pe == p1, sub, EXPERTS_PER_GROUP), axis=0, keepdims=True)
    rest = jnp.where(sub == i1, -1.0, pe)
    p2 = jnp.max(rest, axis=0, keepdims=True)
    i2 = jnp.min(jnp.where(rest == p2, sub, EXPERTS_PER_GROUP), axis=0, keepdims=True)
    psum = p1 + p2
    g1 = p_grp * p1 / psum
    g2 = p_grp * p2 / psum
    e1 = grp * EXPERTS_PER_GROUP + i1
    e2 = grp * EXPERTS_PER_GROUP + i2
    ids_ref[...] = jnp.where(sub == 0, e1, jnp.where(sub == 1, e2, 0))
    gate_ref[...] = jnp.where(sub == 0, g1, jnp.where(sub == 1, g2, 0.0))


def _post(x2, oa, ob, qc, mk, mv, w_out, attn_g, mem_g, ln_g, ln_b, w_group, b_group,
          w_router, b_router, seq, n_mem, alpha, tm=512):
    n_tok, d_model = x2.shape
    n_t = seq // tm
    row = lambda i: (i, 0)
    const = lambda i: (0, 0)
    per_b = lambda i: (i // n_t, 0, 0)
    lane = lambda i: (0, i)
    bsz = n_tok // seq
    wr = jnp.pad(jnp.concatenate([w_group, w_router], axis=1).T,
                 ((0, _ROUTE_ROWS - N_GROUPS - N_EXPERTS), (0, 0))).astype(BF16)
    br = jnp.pad(jnp.concatenate([b_group, b_router]),
                 (0, _ROUTE_ROWS - N_GROUPS - N_EXPERTS)).reshape(_ROUTE_ROWS, 1).astype(F32)
    vec = lambda v: v.reshape(1, -1).astype(F32)
    return pl.pallas_call(
        functools.partial(_post_kernel, alpha=alpha),
        grid=(n_tok // tm,),
        in_specs=[pl.BlockSpec((tm, d_model), row),
                  pl.BlockSpec((tm, A_WIDTH), row),
                  pl.BlockSpec((tm, GDN_WIDTH), row),
                  pl.BlockSpec((tm, MEM_WIDTH), row),
                  pl.BlockSpec((None, n_mem, MEM_WIDTH), per_b),
                  pl.BlockSpec((None, n_mem, MEM_WIDTH), per_b),
                  pl.BlockSpec(w_out.shape, const),
                  pl.BlockSpec((1, A_WIDTH), const),
                  pl.BlockSpec((1, MEM_WIDTH), const),
                  pl.BlockSpec((1, d_model), const),
                  pl.BlockSpec((1, d_model), const),
                  pl.BlockSpec((_ROUTE_ROWS, d_model), const),
                  pl.BlockSpec((_ROUTE_ROWS, 1), const)],
        out_specs=[pl.BlockSpec((tm, d_model), row),
                   pl.BlockSpec((EXPERTS_PER_GROUP, tm), lane),
                   pl.BlockSpec((EXPERTS_PER_GROUP, tm), lane)],
        out_shape=[jax.ShapeDtypeStruct((n_tok, d_model), F32),
                   jax.ShapeDtypeStruct((EXPERTS_PER_GROUP, n_tok), I32),
                   jax.ShapeDtypeStruct((EXPERTS_PER_GROUP, n_tok), F32)],
        compiler_params=_cparams(("parallel",)),
        name="post",
    )(x2, oa, ob, qc, mk.reshape(bsz, n_mem, MEM_WIDTH), mv.reshape(bsz, n_mem, MEM_WIDTH),
      w_out.astype(BF16), vec(attn_g), vec(mem_g), vec(ln_g), vec(ln_b), wr, br)


def _rank_kernel(ids_ref, rank_ref, cnt_ref, carry_ref):
    i = pl.program_id(0)
    tm = ids_ref.shape[1]

    @pl.when(i == 0)
    def _():
        carry_ref[...] = jnp.zeros(carry_ref.shape, F32)

    ids = ids_ref[...]
    eio = lax.broadcasted_iota(I32, (N_EXPERTS, tm), 0)
    oh0 = jnp.where(eio == ids[0:1, :], 1.0, 0.0)
    oh1 = jnp.where(eio == ids[1:2, :], 1.0, 0.0)
    cnt = oh0 + oh1
    before = (lax.broadcasted_iota(I32, (tm, tm), 0) < lax.broadcasted_iota(I32, (tm, tm), 1))
    prefix = jnp.dot(cnt.astype(BF16), jnp.where(before, 1.0, 0.0).astype(BF16),
                     preferred_element_type=F32) + carry_ref[:, 0:1]
    r0 = jnp.sum(oh0 * prefix, axis=0, keepdims=True)
    r1 = jnp.sum(oh1 * prefix, axis=0, keepdims=True)
    sub = lax.broadcasted_iota(I32, (EXPERTS_PER_GROUP, tm), 0)
    rank_ref[...] = jnp.where(sub == 0, r0, jnp.where(sub == 1, r1, 0.0)).astype(I32)
    carry_ref[...] = carry_ref[...] + jnp.sum(cnt, axis=1, keepdims=True)
    cnt_ref[...] = carry_ref[...].astype(I32)


def _rank(ids, tm=512):
    n_tok = ids.shape[1]
    lane = lambda i: (0, i)
    return pl.pallas_call(
        _rank_kernel,
        grid=(n_tok // tm,),
        in_specs=[pl.BlockSpec((EXPERTS_PER_GROUP, tm), lane)],
        out_specs=[pl.BlockSpec((EXPERTS_PER_GROUP, tm), lane),
                   pl.BlockSpec((N_EXPERTS, LANES), lambda i: (0, 0))],
        out_shape=[jax.ShapeDtypeStruct((EXPERTS_PER_GROUP, n_tok), I32),
                   jax.ShapeDtypeStruct((N_EXPERTS, LANES), I32)],
        scratch_shapes=[pltpu.VMEM((N_EXPERTS, LANES), F32)],
        compiler_params=_cparams(("arbitrary",)),
        name="rank",
    )(ids)


def _dest_kernel(ids_ref, rank_ref, ps_ref, dest_ref):
    ids = ids_ref[...]
    tm = ids.shape[1]
    eio = lax.broadcasted_iota(I32, (N_EXPERTS, tm), 0)
    ps = ps_ref[...]
    d0 = jnp.sum(jnp.where(eio == ids[0:1, :], ps, 0.0), axis=0, keepdims=True)
    d1 = jnp.sum(jnp.where(eio == ids[1:2, :], ps, 0.0), axis=0, keepdims=True)
    sub = lax.broadcasted_iota(I32, (EXPERTS_PER_GROUP, tm), 0)
    dest_ref[...] = rank_ref[...] + jnp.where(sub == 0, d0, jnp.where(sub == 1, d1, 0.0)).astype(I32)


def _dest(ids, rank, pad_start, tm=2048):
    n_tok = ids.shape[1]
    tm = min(tm, n_tok)
    lane = lambda i: (0, i)
    return pl.pallas_call(
        _dest_kernel,
        grid=(n_tok // tm,),
        in_specs=[pl.BlockSpec((EXPERTS_PER_GROUP, tm), lane),
                  pl.BlockSpec((EXPERTS_PER_GROUP, tm), lane),
                  pl.BlockSpec((N_EXPERTS, 1), lambda i: (0, 0))],
        out_specs=pl.BlockSpec((EXPERTS_PER_GROUP, tm), lane),
        out_shape=jax.ShapeDtypeStruct((EXPERTS_PER_GROUP, n_tok), I32),
        compiler_params=_cparams(("parallel",)),
        name="dest",
    )(ids, rank, pad_start.astype(F32).reshape(N_EXPERTS, 1))


def _sc_mesh():
    return plsc.VectorSubcoreMesh(core_axis_name="c", subcore_axis_name="s",
                                  num_cores=SC_CORES, num_subcores=SC_SUBCORES)


def _sc_scatter_rows(x1, dest, pad_rows, cap):
    n_tok, d = x1.shape
    n_pad = pad_rows.shape[0]
    n_workers = SC_CORES * SC_SUBCORES
    tok_per_worker = n_tok // n_workers
    pad_per_worker = n_pad // n_workers
    n_win = tok_per_worker // SC_ROWS
    assert n_tok % (n_workers * SC_ROWS) == 0 and n_pad % (n_workers * SC_ROWS) == 0 and TOP_K == 2
    zero_rows = jnp.zeros((SC_ROWS, d), x1.dtype)

    @functools.partial(
        pl.kernel, mesh=_sc_mesh(), out_type=jax.ShapeDtypeStruct((cap, d), x1.dtype),
        scratch_types=[pltpu.VMEM((tok_per_worker,), I32), pltpu.VMEM((tok_per_worker,), I32),
                       pltpu.VMEM((pad_per_worker,), I32), pltpu.VMEM((2, SC_ROWS, d), x1.dtype),
                       pltpu.SemaphoreType.DMA((2,)), pltpu.SemaphoreType.DMA((2,))],
        name="sc_scatter")
    def scatter(x_hbm, dest_hbm, pad_hbm, zero_hbm, out_hbm, idx0_v, idx1_v, pad_v, rows_v, lsem, ssem):
        worker = lax.axis_index("s") * SC_CORES + lax.axis_index("c")
        base = worker * tok_per_worker
        pltpu.sync_copy(dest_hbm.at[pl.ds(base, tok_per_worker)], idx0_v)
        pltpu.sync_copy(dest_hbm.at[pl.ds(n_tok + base, tok_per_worker)], idx1_v)
        pltpu.sync_copy(pad_hbm.at[pl.ds(worker * pad_per_worker, pad_per_worker)], pad_v)

        def load(i, slot):
            return pltpu.make_async_copy(x_hbm.at[pl.ds(base + i * SC_ROWS, SC_ROWS)],
                                         rows_v.at[slot], lsem.at[slot])

        def store(idx_v, i, slot):
            return pltpu.make_async_copy(rows_v.at[slot],
                                         out_hbm.at[idx_v.at[pl.ds(i * SC_ROWS, SC_ROWS)]], ssem.at[slot])

        load(0, 0).start()

        @pl.loop(0, n_win)
        def _(i):
            slot = i % 2
            load(i, slot).wait()
            store(idx0_v, i, slot).start()
            store(idx1_v, i, slot).start()

            @pl.when(i >= 1)
            def _():
                store(idx0_v, i - 1, 1 - slot).wait()
                store(idx1_v, i - 1, 1 - slot).wait()

            @pl.when(i + 1 < n_win)
            def _():
                load(i + 1, 1 - slot).start()

        last = (n_win - 1) % 2
        store(idx0_v, n_win - 1, last).wait()
        store(idx1_v, n_win - 1, last).wait()

        pltpu.sync_copy(zero_hbm, rows_v.at[0])

        @pl.loop(0, pad_per_worker // SC_ROWS)
        def _(i):
            pltpu.async_copy(rows_v.at[0], out_hbm.at[pad_v.at[pl.ds(i * SC_ROWS, SC_ROWS)]],
                             ssem.at[0]).wait()

    return scatter(x1, dest, pad_rows, zero_rows)


def _mlp_kernel(be_ref, nused_ref, x_ref, wg_ref, wu_ref, wd_ref, y_ref):
    i = pl.program_id(0)

    @pl.when(i < nused_ref[0])
    def _():
        xb = x_ref[...].astype(BF16)
        hg = jnp.dot(xb, wg_ref[...].astype(BF16), preferred_element_type=F32)
        hu = jnp.dot(xb, wu_ref[...].astype(BF16), preferred_element_type=F32)
        y_ref[...] = jnp.dot((_silu(hg) * hu).astype(BF16), wd_ref[...].astype(BF16),
                             preferred_element_type=F32)

    @pl.when(i >= nused_ref[0])
    def _():
        y_ref[...] = jnp.zeros(y_ref.shape, F32)


def _mlp(blk_expert, n_used, xbuf, w_gate, w_up, w_down):
    cap, d_model = xbuf.shape
    d_exp = w_gate.shape[2]
    blk = lambda i, be, nu: (i, 0)
    used_blk = lambda i, be, nu: (jnp.minimum(i, nu[0] - 1), 0)
    wsel = lambda i, be, nu: (be[i], 0, 0)
    return pl.pallas_call(
        _mlp_kernel,
        grid_spec=pltpu.PrefetchScalarGridSpec(
            num_scalar_prefetch=2,
            grid=(cap // MOE_BLOCK,),
            in_specs=[pl.BlockSpec((MOE_BLOCK, d_model), used_blk),
                      pl.BlockSpec((None, d_model, d_exp), wsel),
                      pl.BlockSpec((None, d_model, d_exp), wsel),
                      pl.BlockSpec((None, d_exp, d_model), wsel)],
            out_specs=pl.BlockSpec((MOE_BLOCK, d_model), blk)),
        out_shape=jax.ShapeDtypeStruct((cap, d_model), F32),
        compiler_params=_cparams(("arbitrary",)),
        name="mlp",
    )(blk_expert, n_used, xbuf, w_gate, w_up, w_down)


def _sc_gather_rows(table, idx):
    n = idx.shape[0]
    d = table.shape[1]
    n_workers = SC_CORES * SC_SUBCORES
    per_worker = n // n_workers
    n_win = per_worker // SC_ROWS
    assert n % (n_workers * SC_ROWS) == 0

    @functools.partial(
        pl.kernel, mesh=_sc_mesh(), out_type=jax.ShapeDtypeStruct((n, d), table.dtype),
        scratch_types=[pltpu.VMEM((per_worker,), I32), pltpu.VMEM((2, SC_ROWS, d), table.dtype),
                       pltpu.SemaphoreType.DMA((2,)), pltpu.SemaphoreType.DMA((2,))],
        name="sc_gather")
    def gather(table_hbm, idx_hbm, out_hbm, idx_v, rows_v, gsem, wsem):
        worker = lax.axis_index("s") * SC_CORES + lax.axis_index("c")
        base = worker * per_worker
        pltpu.sync_copy(idx_hbm.at[pl.ds(base, per_worker)], idx_v)

        def fetch(i, slot):
            return pltpu.make_async_copy(table_hbm.at[idx_v.at[pl.ds(i * SC_ROWS, SC_ROWS)]],
                                         rows_v.at[slot], gsem.at[slot])

        def write(i, slot):
            return pltpu.make_async_copy(rows_v.at[slot],
                                         out_hbm.at[pl.ds(base + i * SC_ROWS, SC_ROWS)], wsem.at[slot])

        fetch(0, 0).start()

        @pl.loop(0, n_win)
        def _(i):
            slot = i % 2
            fetch(i, slot).wait()
            write(i, slot).start()

            @pl.when(i >= 1)
            def _():
                write(i - 1, 1 - slot).wait()

            @pl.when(i + 1 < n_win)
            def _():
                fetch(i + 1, 1 - slot).start()

        write(n_win - 1, (n_win - 1) % 2).wait()

    return gather(table, idx)


COMBINE_PARTS = 4


def _combine_kernel(x1_ref, y0_ref, y1_ref, gt_ref, lg_ref, lb_ref, *rest, alpha):
    o_ref = rest[-1]
    gt = gt_ref[...]
    ffn = y0_ref[...] * gt[:, 0:1] + y1_ref[...] * gt[:, 1:2]
    hres = alpha * x1_ref[...] + ffn
    mu = jnp.mean(hres, axis=-1, keepdims=True)
    var = jnp.mean(jnp.square(hres - mu), axis=-1, keepdims=True)
    o_ref[...] = (hres - mu) * lax.rsqrt(var + LN_EPS) * lg_ref[...] + lb_ref[...]


def _combine(dest, x1, gates_t, ln_g, ln_b, ybuf, alpha, tc=512):
    n_tok, d_model = x1.shape
    n_parts = COMBINE_PARTS if n_tok % (COMBINE_PARTS * tc) == 0 else 1
    part = n_tok // n_parts
    tiles = part // tc
    const = lambda i: (0, 0)
    vec = lambda v: v.reshape(1, -1).astype(F32)
    out = None
    for k in range(n_parts):
        idx = dest[:, k * part:(k + 1) * part].reshape(-1)
        yrows = _sc_gather_rows(ybuf, idx)
        tok = lambda i, k=k: (k * tiles + i, 0)
        in_specs = [pl.BlockSpec((tc, d_model), tok),
                    pl.BlockSpec((tc, d_model), lambda i: (i, 0)),
                    pl.BlockSpec((tc, d_model), lambda i: (i + tiles, 0)),
                    pl.BlockSpec((tc, EXPERTS_PER_GROUP), tok),
                    pl.BlockSpec((1, d_model), const),
                    pl.BlockSpec((1, d_model), const)]
        args = [x1, yrows, yrows, gates_t, vec(ln_g), vec(ln_b)]
        if out is not None:
            in_specs.append(pl.BlockSpec(memory_space=pl.ANY))
            args.append(out)
        out = pl.pallas_call(
            functools.partial(_combine_kernel, alpha=alpha),
            grid=(tiles,),
            in_specs=in_specs,
            out_specs=pl.BlockSpec((tc, d_model), tok),
            out_shape=jax.ShapeDtypeStruct((n_tok, d_model), F32),
            input_output_aliases={} if k == 0 else {len(args) - 1: 0},
            compiler_params=_cparams(("parallel",)),
            name="combine",
        )(*args)
    return out


def _moe(x1, ids, gates, w_gate, w_up, w_down, ln_g, ln_b, alpha):
    n_tok, d_model = x1.shape
    rank, counts = _rank(ids)
    counts = counts[:, 0]
    padded = (counts + MOE_BLOCK - 1) // MOE_BLOCK * MOE_BLOCK
    pad_ends = jnp.cumsum(padded)
    pad_start = (pad_ends - padded).astype(I32)
    n_asg = n_tok * TOP_K
    cap = (n_asg + MOE_BLOCK - 1) // MOE_BLOCK * MOE_BLOCK + N_EXPERTS * MOE_BLOCK
    n_blk = cap // MOE_BLOCK
    blk_pos = jnp.arange(n_blk, dtype=I32) * MOE_BLOCK
    blk_expert = jnp.minimum(
        jnp.sum((pad_ends[None, :] <= blk_pos[:, None]).astype(I32), axis=1), N_EXPERTS - 1)
    n_used = (pad_ends[-1:] // MOE_BLOCK).astype(I32)
    dest = _dest(ids, rank, pad_start)[0:TOP_K]
    slot = jnp.arange(MOE_BLOCK, dtype=I32)[None, :]
    spare = cap - 1 - (jnp.arange(N_EXPERTS * MOE_BLOCK, dtype=I32).reshape(N_EXPERTS, MOE_BLOCK)
                       % N_EXPERTS)
    pad_rows = jnp.where(slot < (padded - counts)[:, None], (pad_start + counts)[:, None] + slot,
                         spare).reshape(-1)
    xbuf = _sc_scatter_rows(x1, dest.reshape(-1), pad_rows, cap)
    ybuf = _mlp(blk_expert, n_used, xbuf, w_gate, w_up, w_down)
    return _combine(dest, x1, gates.T, ln_g, ln_b, ybuf, alpha)


def kernel(x, mem, w_in, kv_norm_g, w_k_up, w_v_up, conv_w, A_log, dt_bias, gdn_norm_g, attn_norm_g, mem_norm_g, w_mem_kv, w_out, ln1_g, ln1_b, w_group, b_group, w_router, b_router, w_gate, w_up, w_down, ln2_g, ln2_b):
    bsz, seq, d_model = x.shape
    n_mem = mem.shape[1]
    depth = w_in.shape[0]
    alpha = (2 * depth) ** 0.25
    inv_freq = 1.0 / (ROPE_THETA ** (jnp.arange(0, HEAD_DIM, 2, dtype=F32) / HEAD_DIM))
    ang = jnp.arange(seq, dtype=F32)[:, None] * inv_freq[None, :]
    cos128 = jnp.tile(jnp.cos(ang), (1, LANES // (HEAD_DIM // 2)))
    sin128 = jnp.tile(jnp.sin(ang), (1, LANES // (HEAD_DIM // 2)))
    x2 = x.reshape(bsz * seq, d_model)
    mem2 = mem.reshape(bsz * n_mem, d_model)
    for l in range(depth):
        qa, iq, ik, ka, va, iw, qkvb, zb, ab, qc = _proj(
            x2, w_in[l], w_k_up[l], w_v_up[l], kv_norm_g[l], cos128, sin128, seq)
        oa = _dsa(qa, iq, iw, ik, ka, va, bsz, seq)
        ob = _gdn(qkvb, ab, zb, conv_w[l], A_log[l], dt_bias[l], gdn_norm_g[l], bsz, seq)
        mk, mv = _memkv(mem2, w_mem_kv[l], n_mem)
        x1, ids, gates = _post(x2, oa, ob, qc, mk, mv, w_out[l], attn_norm_g[l], mem_norm_g[l],
                               ln1_g[l], ln1_b[l], w_group[l], b_group[l], w_router[l],
                               b_router[l], seq, n_mem, alpha)
        x2 = _moe(x1, ids, gates, w_gate[l], w_up[l], w_down[l], ln2_g[l], ln2_b[l], alpha)
    return x2.reshape(bsz, seq, d_model)
```

```python
import functools

import numpy as np
import jax
import jax.numpy as jnp
from jax import lax
from jax.experimental import pallas as pl
from jax.experimental.pallas import tpu as pltpu
from jax.experimental.pallas import tpu_sc as plsc

F32 = jnp.float32
BF16 = jnp.bfloat16
I32 = jnp.int32
I16 = jnp.int16

HEAD_DIM = 64
ATTN_HEADS = 6
A_WIDTH = ATTN_HEADS * HEAD_DIM
KV_RANK = 128
IDX_HEADS = 4
IDX_DIM = 64
INDEX_TOPK = 256
GDN_HEADS = 6
GDN_WIDTH = GDN_HEADS * HEAD_DIM
CONV_WIDTH = 4
CHUNK = 64
MEM_HEADS = 4
MEM_WIDTH = MEM_HEADS * HEAD_DIM
SPLIT_SIZES = (A_WIDTH, KV_RANK, IDX_HEADS * IDX_DIM, IDX_DIM, IDX_HEADS,
               GDN_WIDTH, GDN_WIDTH, GDN_WIDTH, GDN_WIDTH, GDN_HEADS, GDN_HEADS,
               MEM_WIDTH)
ROPE_THETA = 10000.0
N_GROUPS = 8
EXPERTS_PER_GROUP = 8
N_EXPERTS = N_GROUPS * EXPERTS_PER_GROUP
TOP_K = 2
MOE_BLOCK = 256
LN_EPS = 1e-5
RMS_EPS = 1e-6
NEG_INF = -1e30
INT_MIN = -2 ** 31
I16_MIN = -2 ** 15

LANES = 128
SUBLANES = 8
PACKED_ROWS = 16
HALF_BITS = 16
HALF_SPAN = 1 << HALF_BITS
INT_MAX = 2 ** 31 - 1
SC_CORES = 2
SC_SUBCORES = 16
SC_ROWS = 32
VMEM_LIMIT = 56 * 1024 * 1024


def _cparams(sem):
    return pltpu.CompilerParams(dimension_semantics=sem, vmem_limit_bytes=VMEM_LIMIT)


def _dot(a, b):
    return jnp.dot(a.astype(BF16), b.astype(BF16), preferred_element_type=F32)


def _dot_nt(a, b):
    return lax.dot_general(a.astype(BF16), b.astype(BF16), (((1,), (1,)), ((), ())),
                           preferred_element_type=F32)


def _silu(t):
    return t * (1.0 / (1.0 + jnp.exp(-t)))


_P_QA = (0, 384)
_P_IQ = (384, 640)
_P_CKV = (640, 768)
_P_IK = (768, 896)
_P_IW = (896, 1024)
_P_QKVB = (1024, 2176)
_P_ZB = (2176, 2560)
_P_AB = (2560, 2688)
_P_QC = (2688, 2944)
_P_TOTAL = 2944
_V_ROWS = HEAD_DIM + PACKED_ROWS


def _rot_cols(w, n_heads):
    k = w.shape[0]
    w4 = w.reshape(k, n_heads, 2, HEAD_DIM // 2)
    return jnp.concatenate([-w4[:, :, 1:2], w4[:, :, 0:1]], axis=2).reshape(k, n_heads * HEAD_DIM)


def _pad_cols(w, width):
    return jnp.pad(w, ((0, 0), (0, width - w.shape[1])))


def _proj_kernel(x_ref, w_ref, w2_ref, cos_ref, sin_ref, kvg_ref,
                 qa_ref, iq_ref, ik_ref, ka_ref, va_ref, iw_ref, qkvb_ref, zb_ref, ab_ref, qc_ref):
    xb = x_ref[...].astype(BF16)

    def mm(slab):
        return jnp.dot(xb, w_ref[:, slab[0]:slab[1]], preferred_element_type=F32)

    cos = cos_ref[...]
    sin = sin_ref[...]
    cos3 = jnp.concatenate([cos] * 3, axis=1)
    sin3 = jnp.concatenate([sin] * 3, axis=1)
    cos2 = jnp.concatenate([cos] * 2, axis=1)
    sin2 = jnp.concatenate([sin] * 2, axis=1)
    cos64 = cos[:, :HEAD_DIM]
    sin64 = sin[:, :HEAD_DIM]

    def rotate_half(t):
        slabs = []
        for j in range(t.shape[1] // LANES):
            ts = t[:, j * LANES:(j + 1) * LANES]
            first = lax.broadcasted_iota(I32, ts.shape, 1) % HEAD_DIM < HEAD_DIM // 2
            slabs.append(jnp.where(first, -pltpu.roll(ts, LANES - HEAD_DIM // 2, 1),
                                   pltpu.roll(ts, HEAD_DIM // 2, 1)))
        return jnp.concatenate(slabs, axis=1)

    qa = mm(_P_QA)
    qa_ref[...] = (qa * cos3 + rotate_half(qa) * sin3).astype(BF16)
    iq = mm(_P_IQ)
    iq_ref[...] = (iq * cos2 + rotate_half(iq) * sin2).astype(BF16)

    ckv = mm(_P_CKV)
    cn = ckv * lax.rsqrt(jnp.mean(ckv * ckv, axis=-1, keepdims=True) + RMS_EPS) * kvg_ref[...]
    r = jnp.dot(cn.astype(BF16), w2_ref[...], preferred_element_type=F32)
    ka_ref[...] = (r[:, 0:64] * cos64 + r[:, 64:128] * sin64).astype(BF16)
    va_ref[...] = jnp.concatenate(
        [r[:, 128:256].T[0:HEAD_DIM, :], jnp.ones((_V_ROWS - HEAD_DIM, r.shape[0]), F32)],
        axis=0).astype(BF16)

    ikk = mm(_P_IK)
    ik_ref[...] = (ikk[:, 0:64] * cos64 + ikk[:, 64:128] * sin64).astype(BF16)
    iw_ref[...] = mm(_P_IW) * (IDX_HEADS ** -0.5 * IDX_DIM ** -0.5)
    qkvb_ref[...] = mm(_P_QKVB)
    zb_ref[...] = mm(_P_ZB)
    ab_ref[...] = mm(_P_AB)
    qc_ref[...] = mm(_P_QC).astype(BF16)


def _proj(x2, w_in, w_k_up, w_v_up, kv_norm_g, cos128, sin128, seq, tm=256):
    n_tok, d_model = x2.shape
    offs = np.cumsum(SPLIT_SIZES)[:-1].tolist()
    (w_qa, w_ckv, w_iq, w_ik, w_iw, w_qb, w_kb, w_vb, w_zb, w_a, w_b, w_qc) = jnp.split(w_in, offs, axis=1)
    w1 = jnp.concatenate([
        w_qa, w_iq, w_ckv,
        w_ik, _rot_cols(w_ik, 1), _pad_cols(w_iw, LANES),
        w_qb, w_kb, w_vb, w_zb, _pad_cols(jnp.concatenate([w_a, w_b], axis=1), LANES), w_qc,
    ], axis=1).astype(BF16)
    assert w1.shape[1] == _P_TOTAL
    w2 = _pad_cols(jnp.concatenate([w_k_up, _rot_cols(w_k_up, 1), w_v_up], axis=1), 2 * LANES).astype(BF16)
    n_pos = seq // tm
    row = lambda i: (i, 0)
    const = lambda i: (0, 0)
    pos = lambda i: (i % n_pos, 0)
    outs = [(A_WIDTH, BF16), (IDX_HEADS * IDX_DIM, BF16), (IDX_DIM, BF16), (HEAD_DIM, BF16),
            None, (LANES, F32), (3 * GDN_WIDTH, F32), (GDN_WIDTH, F32), (LANES, F32),
            (MEM_WIDTH, BF16)]
    out_specs = [pl.BlockSpec((tm, o[0]), row) if o else pl.BlockSpec((_V_ROWS, tm), lambda i: (0, i))
                 for o in outs]
    out_shape = [jax.ShapeDtypeStruct((n_tok, o[0]), o[1]) if o
                 else jax.ShapeDtypeStruct((_V_ROWS, n_tok), BF16) for o in outs]
    return pl.pallas_call(
        _proj_kernel,
        grid=(n_tok // tm,),
        in_specs=[pl.BlockSpec((tm, d_model), row),
                  pl.BlockSpec(w1.shape, const),
                  pl.BlockSpec(w2.shape, const),
                  pl.BlockSpec((tm, LANES), pos),
                  pl.BlockSpec((tm, LANES), pos),
                  pl.BlockSpec((1, KV_RANK), const)],
        out_specs=out_specs,
        out_shape=out_shape,
        compiler_params=_cparams(("parallel",)),
        name="proj",
    )(x2, w1, w2, cos128, sin128, kv_norm_g.reshape(1, KV_RANK).astype(F32))


def _dsa_kernel(qa_ref, iq_ref, iw_ref, ik_ref, ka_ref, vat_ref, o_ref,
                key_ref, hi_ref, lo_ref, m_ref, acc_ref, *, tq, kc, top):
    qi = pl.program_id(1)
    row0 = qi * tq
    n_kc = (row0 + tq + kc - 1) // kc
    qpos = row0 + lax.broadcasted_iota(I32, (1, tq), 1)

    def key_fold(v):
        return jnp.sum(v.reshape(kc // SUBLANES, SUBLANES, tq), axis=0)

    def head_rows(x, n_heads, width):
        return jnp.concatenate([x[:, h * width:(h + 1) * width] for h in range(n_heads)], axis=0)

    iq_rows = head_rows(iq_ref[...], IDX_HEADS, IDX_DIM)
    iw_t = iw_ref[...].T

    n_pairs = (n_kc + 1) // 2
    kc2 = 2 * kc

    def score_body(c, carry):
        k0 = pl.multiple_of(c * kc2, kc2)
        d = _dot_nt(ik_ref[pl.ds(k0, kc2), :], iq_rows)
        s = jnp.zeros((kc2, tq), F32)
        for h in range(IDX_HEADS):
            s = s + iw_t[h:h + 1, :] * jnp.maximum(d[:, h * tq:(h + 1) * tq], 0.0)
        kidx = k0 + lax.broadcasted_iota(I32, (kc2, tq), 0)
        s = jnp.where(s == 0.0, 0.0, s)
        s = jnp.where(kidx <= qpos, s, NEG_INF)
        bits = pltpu.bitcast(s, I32)
        key = jnp.where(bits >= 0, bits, bits ^ INT_MAX)
        key_ref[pl.ds(k0, kc2), :] = key
        hi_ref[pl.ds(k0, kc2), :] = lax.shift_right_arithmetic(key, HALF_BITS).astype(I16)
        lo_ref[pl.ds(k0, kc2), :] = ((key & (HALF_SPAN - 1)) + I16_MIN).astype(I16)
        return carry

    lax.fori_loop(0, n_pairs, score_body, 0)

    k_eff = jnp.minimum(top, qpos + 1).astype(F32)

    one_b = jnp.ones((), BF16)
    zero_b = jnp.zeros((), BF16)

    def count16(ref, pred):
        def body(c, acc):
            k0 = pl.multiple_of(c * (2 * kc), 2 * kc)
            for j in range(2):
                hit = jnp.where(pred(ref[pl.ds(k0 + j * kc, kc), :]), one_b, zero_b)
                hit = hit.reshape(kc // PACKED_ROWS, PACKED_ROWS, tq)
                parts = [hit[r] for r in range(kc // PACKED_ROWS)]
                while len(parts) > 1:
                    parts = [a + b for a, b in zip(parts[0::2], parts[1::2])]
                acc = acc + parts[0]
            return acc
        acc = lax.fori_loop(0, n_pairs, body, jnp.zeros((PACKED_ROWS, tq), BF16))
        return jnp.sum(acc.astype(F32), axis=0, keepdims=True)

    def search16(ref, start, n_bits, k_want):
        def bit_body(i, t):
            cand = t + lax.shift_left(jnp.int32(1), n_bits - 1 - i)
            c16 = cand.astype(I16)
            return jnp.where(count16(ref, lambda v: v >= c16) >= k_want, cand, t)
        return lax.fori_loop(0, n_bits, bit_body, start)

    hi0 = jnp.where(count16(hi_ref, lambda v: v >= 0) >= k_eff, 0, I16_MIN).astype(I32)
    thr_hi = search16(hi_ref, hi0, HALF_BITS - 1, k_eff)
    thr_hi16 = thr_hi.astype(I16)
    k_low = k_eff - count16(hi_ref, lambda v: v > thr_hi16)

    def band_body(c, carry):
        k0 = pl.multiple_of(c * kc2, kc2)
        lo_ref[pl.ds(k0, kc2), :] = jnp.where(hi_ref[pl.ds(k0, kc2), :] == thr_hi16,
                                              lo_ref[pl.ds(k0, kc2), :], I16_MIN)
        return carry

    lax.fori_loop(0, n_pairs, band_body, 0)
    thr_lo = search16(lo_ref, jnp.full((1, tq), I16_MIN, I32), HALF_BITS, k_low)
    thr_lo16 = thr_lo.astype(I16)
    thr = thr_hi * HALF_SPAN + (thr_lo - I16_MIN)
    n_tie = k_low - count16(lo_ref, lambda v: v > thr_lo16)

    q_rows = head_rows((qa_ref[...].astype(F32) * (HEAD_DIM ** -0.5)).astype(BF16),
                       ATTN_HEADS, HEAD_DIM)
    m_ref[...] = jnp.full(m_ref.shape, NEG_INF, F32)
    acc_ref[...] = jnp.zeros(acc_ref.shape, F32)
    earlier = jnp.where(lax.broadcasted_iota(I32, (kc, kc), 1) < lax.broadcasted_iota(I32, (kc, kc), 0),
                        1.0, 0.0).astype(BF16)

    def chunk_bias(k0, tie_seen):
        kk = key_ref[pl.ds(k0, kc), :]
        kidx = k0 + lax.broadcasted_iota(I32, (kc, tq), 0)
        tie = kk == thr
        tie_f = jnp.where(tie, 1.0, 0.0)
        tie_rank = jnp.dot(earlier, tie_f.astype(BF16), preferred_element_type=F32) + tie_seen
        bias = jnp.where(kk > thr, 0.0,
                         jnp.where(tie, jnp.where(tie_rank < n_tie, 0.0, NEG_INF), NEG_INF))
        bias = jnp.where(kidx <= qpos, bias, NEG_INF)
        return bias.astype(BF16), tie_seen + jnp.sum(key_fold(tie_f), axis=0, keepdims=True)

    def attn_body(c, tie_seen):
        k0 = pl.multiple_of(c * (2 * kc), 2 * kc)
        bias_0, tie_seen = chunk_bias(k0, tie_seen)
        bias_1, tie_seen = chunk_bias(k0 + kc, tie_seen)
        bias_b = jnp.concatenate([bias_0, bias_1], axis=0)
        lg_all = _dot_nt(ka_ref[pl.ds(k0, 2 * kc), :], q_rows).astype(BF16)
        lgs = [lg_all[:, h * tq:(h + 1) * tq] + bias_b for h in range(ATTN_HEADS)]
        m_prev = m_ref[...]
        m_new = jnp.maximum(m_prev, jnp.concatenate(
            [jnp.max(lg, axis=0, keepdims=True) for lg in lgs], axis=1).astype(F32))
        m_b = m_new[0:1, :].astype(BF16)
        p = jnp.concatenate([jnp.exp(lgs[h] - m_b[:, h * tq:(h + 1) * tq])
                             for h in range(ATTN_HEADS)], axis=1)
        alpha = jnp.exp(m_prev - m_new)
        acc_ref[...] = alpha[0:1, :] * acc_ref[...] + jnp.dot(
            vat_ref[:, pl.ds(k0, 2 * kc)], p, preferred_element_type=F32)
        m_ref[...] = m_new
        return tie_seen

    lax.fori_loop(0, n_pairs, attn_body, jnp.zeros((1, tq), F32))
    o_t = acc_ref[0:HEAD_DIM, :] / acc_ref[HEAD_DIM:HEAD_DIM + 1, :]
    o_ref[...] = jnp.concatenate(
        [o_t[:, h * tq:(h + 1) * tq].T for h in range(ATTN_HEADS)], axis=1)


def _dsa(qa, iq, iw, ik, ka, va_t, bsz, seq, tq=256):
    tq = min(tq, seq)
    kc = tq
    top = min(INDEX_TOPK, seq // 4)
    n_q = seq // tq
    assert seq // PACKED_ROWS <= 256 and (seq // kc) % 2 == 0
    row = lambda b, i: (b * n_q + i, 0)
    per_b = lambda b, i: (b, 0, 0)
    kern = functools.partial(_dsa_kernel, tq=tq, kc=kc, top=top)
    return pl.pallas_call(
        kern,
        grid=(bsz, n_q),
        in_specs=[pl.BlockSpec((tq, A_WIDTH), row),
                  pl.BlockSpec((tq, IDX_HEADS * IDX_DIM), row),
                  pl.BlockSpec((tq, LANES), row),
                  pl.BlockSpec((None, seq, IDX_DIM), per_b),
                  pl.BlockSpec((None, seq, HEAD_DIM), per_b),
                  pl.BlockSpec((_V_ROWS, seq), lambda b, i: (0, b))],
        out_specs=pl.BlockSpec((tq, A_WIDTH), row),
        out_shape=jax.ShapeDtypeStruct((bsz * seq, A_WIDTH), F32),
        scratch_shapes=[pltpu.VMEM((seq, tq), I32),
                        pltpu.VMEM((seq, tq), I16),
                        pltpu.VMEM((seq, tq), I16),
                        pltpu.VMEM((SUBLANES, ATTN_HEADS * tq), F32),
                        pltpu.VMEM((_V_ROWS, ATTN_HEADS * tq), F32)],
        compiler_params=_cparams(("parallel", "arbitrary")),
        name="dsa",
    )(qa, iq, iw, ik.reshape(bsz, seq, IDX_DIM), ka.reshape(bsz, seq, HEAD_DIM), va_t)


_HALO = SUBLANES


def _bmm(a, b):
    return lax.dot_general(a.astype(BF16), b.astype(BF16), (((2,), (1,)), ((0,), (0,))),
                           preferred_element_type=F32)


def _bmm_nt(a, b):
    return lax.dot_general(a.astype(BF16), b.astype(BF16), (((2,), (2,)), ((0,), (0,))),
                           preferred_element_type=F32)


def _gdn_kernel(qkv_ref, ab_ref, z_ref, cw_ref, alog_ref, dtb_ref, gn_ref, o_ref,
                ext_ref, state_ref, *, tr, nb):
    ti = pl.program_id(1)
    width = 3 * GDN_WIDTH
    n_c = tr // CHUNK
    n_p = GDN_HEADS // 2

    @pl.when(ti == 0)
    def _():
        ext_ref[:, 0:_HALO, :] = jnp.zeros((nb, _HALO, width), F32)
        state_ref[...] = jnp.zeros(state_ref.shape, F32)

    lane = lax.broadcasted_iota(I32, (CHUNK, LANES), 1)
    row_i = lax.broadcasted_iota(I32, (CHUNK, LANES), 0)
    left = lane < HEAD_DIM
    incl = (row_i >= lane % HEAD_DIM)[None]
    strict = (row_i > lane % HEAD_DIM)[None]
    left_t = (lax.broadcasted_iota(I32, (tr, LANES), 1) < HEAD_DIM)
    tri = jnp.where(lax.broadcasted_iota(I32, (CHUNK, CHUNK), 0) >= lax.broadcasted_iota(I32, (CHUNK, CHUNK), 1),
                    1.0, 0.0).astype(BF16)

    def blk(x):
        return jnp.concatenate([jnp.where(left[None], x, 0.0), jnp.where(left[None], 0.0, x)], axis=1)

    def head_scale(x, eps, scale):
        sq = x * x
        keep = left_t if x.shape[0] == tr else left
        s_l = jnp.sum(jnp.where(keep, sq, 0.0), axis=-1, keepdims=True)
        s_r = jnp.sum(jnp.where(keep, 0.0, sq), axis=-1, keepdims=True)
        return x * jnp.where(keep, lax.rsqrt(s_l * scale + eps), lax.rsqrt(s_r * scale + eps))

    def rows(c):
        return slice(c * CHUNK, (c + 1) * CHUNK)

    qkvs, betas, q_n, k_n, gc_cols, gc_rows = [], [], [], [], [], []
    for s in range(nb):
        ext_ref[s, _HALO:_HALO + tr, :] = qkv_ref[s]
        conv = cw_ref[CONV_WIDTH - 1:CONV_WIDTH, :] * ext_ref[s, _HALO:_HALO + tr, :]
        for j in range(CONV_WIDTH - 1):
            off = _HALO - (CONV_WIDTH - 1) + j
            conv = conv + cw_ref[j:j + 1, :] * ext_ref[s, off:off + tr, :]
        ext_ref[s, 0:_HALO, :] = ext_ref[s, tr:tr + _HALO, :]
        qkv = _silu(conv)
        qkvs.append(qkv)

        ab = ab_ref[s]
        sp_in = ab + dtb_ref[...]
        sp = jnp.maximum(sp_in, 0.0) + jnp.log1p(jnp.exp(-jnp.abs(sp_in)))
        g_all = -jnp.exp(alog_ref[...]) * sp
        betas.append(1.0 / (1.0 + jnp.exp(-ab)))

        cols, rws = [], []
        for c in range(n_c):
            g_c = g_all[rows(c), :]
            g_hi = g_c.astype(BF16)
            g_r1 = g_c - g_hi.astype(F32)
            g_mid = g_r1.astype(BF16)
            g_lo = (g_r1 - g_mid.astype(F32)).astype(BF16)
            gc = (jnp.dot(tri, g_hi, preferred_element_type=F32)
                  + jnp.dot(tri, g_mid, preferred_element_type=F32)
                  + jnp.dot(tri, g_lo, preferred_element_type=F32))
            cols.append(gc)
            rws.append(gc.T)
        gc_cols.append(cols)
        gc_rows.append(rws)
        q_n.append([head_scale(qkv[:, p * LANES:(p + 1) * LANES], RMS_EPS, 1.0) * (HEAD_DIM ** -0.5)
                    for p in range(n_p)])
        k_n.append([head_scale(qkv[:, GDN_WIDTH + p * LANES:GDN_WIDTH + (p + 1) * LANES], RMS_EPS, 1.0)
                    for p in range(n_p)])

    def stack(fn):
        return jnp.stack([fn(c, s, p) for c in range(n_c) for s in range(nb) for p in range(n_p)], axis=0)

    def pair_cols(x, c, p, base):
        return jnp.where(left, x[rows(c), base + 2 * p:base + 2 * p + 1],
                         x[rows(c), base + 2 * p + 1:base + 2 * p + 2])

    qs = stack(lambda c, s, p: q_n[s][p][rows(c), :])
    ks = stack(lambda c, s, p: k_n[s][p][rows(c), :])
    vs = stack(lambda c, s, p: qkvs[s][rows(c), 2 * GDN_WIDTH + p * LANES:2 * GDN_WIDTH + (p + 1) * LANES])
    beta = stack(lambda c, s, p: pair_cols(betas[s], c, p, GDN_HEADS))
    gcc = stack(lambda c, s, p: pair_cols(gc_cols[s][c], 0, p, 0))
    gcr = stack(lambda c, s, p: jnp.concatenate(
        [gc_rows[s][c][2 * p:2 * p + 1, :], gc_rows[s][c][2 * p + 1:2 * p + 2, :]], axis=1))
    decay = jnp.where(incl, jnp.exp(jnp.where(incl, gcc - gcr, 0.0)), 0.0)
    kb = ks * beta
    k_blk = blk(ks)
    low = jnp.where(strict, _bmm_nt(kb, k_blk) * decay, 0.0)
    e_gc = jnp.exp(gcc)
    u = vs * beta
    w = kb * e_gc
    u = u - _bmm(low, blk(u))
    w = w - _bmm(low, blk(w))
    pw = low
    for _ in range(5):
        pw = _bmm(pw, blk(pw))
        u = u + _bmm(pw, blk(u))
        w = w + _bmm(pw, blk(w))
    a_intra = jnp.where(incl, _bmm_nt(qs, k_blk) * decay, 0.0)
    q_dec = qs * e_gc
    gc_last = gcc[:, CHUNK - 1:CHUNK, :]
    kd_blk = blk(ks * jnp.exp(gc_last - gcc))
    kd_blk_t = jnp.stack([kd_blk[i].T for i in range(n_c * nb * n_p)], axis=0)
    g_last = jnp.exp(gc_last)

    gn = gn_ref[...]
    st = state_ref[...]
    per_c = nb * n_p
    for c in range(n_c):
        sl = slice(c * per_c, (c + 1) * per_c)
        v_new = blk(u[sl] - _bmm(w[sl], st))
        o = _bmm(q_dec[sl], st) + _bmm(a_intra[sl], v_new)
        st = st * g_last[sl] + _bmm(kd_blk_t[sl], v_new)
        for s in range(nb):
            for p in range(n_p):
                y = head_scale(o[s * n_p + p], RMS_EPS, 1.0 / HEAD_DIM) * gn
                o_ref[s, rows(c), p * LANES:(p + 1) * LANES] = y * _silu(
                    z_ref[s, rows(c), p * LANES:(p + 1) * LANES])
    state_ref[...] = st


def _gdn(qkvb, ab, zb, conv_w, a_log, dt_bias, gdn_norm_g, bsz, seq, tr=256):
    tr = min(tr, seq)
    n_t = seq // tr
    nb = 2 if bsz % 2 == 0 else 1
    blk3 = lambda b, i: (b, i, 0)
    const = lambda b, i: (0, 0)
    alog = _pad_cols(a_log.reshape(1, GDN_HEADS).astype(F32), LANES)
    dtb = _pad_cols(dt_bias.reshape(1, GDN_HEADS).astype(F32), LANES)
    out = pl.pallas_call(
        functools.partial(_gdn_kernel, tr=tr, nb=nb),
        grid=(bsz // nb, n_t),
        in_specs=[pl.BlockSpec((nb, tr, 3 * GDN_WIDTH), blk3),
                  pl.BlockSpec((nb, tr, LANES), blk3),
                  pl.BlockSpec((nb, tr, GDN_WIDTH), blk3),
                  pl.BlockSpec((CONV_WIDTH, 3 * GDN_WIDTH), const),
                  pl.BlockSpec((1, LANES), const),
                  pl.BlockSpec((1, LANES), const),
                  pl.BlockSpec((1, LANES), const)],
        out_specs=pl.BlockSpec((nb, tr, GDN_WIDTH), blk3),
        out_shape=jax.ShapeDtypeStruct((bsz, seq, GDN_WIDTH), F32),
        scratch_shapes=[pltpu.VMEM((nb, tr + _HALO, 3 * GDN_WIDTH), F32),
                        pltpu.VMEM((nb * (GDN_HEADS // 2), LANES, LANES), F32)],
        compiler_params=_cparams(("parallel", "arbitrary")),
        name="gdn",
    )(qkvb.reshape(bsz, seq, 3 * GDN_WIDTH), ab.reshape(bsz, seq, LANES), zb.reshape(bsz, seq, GDN_WIDTH),
      conv_w.astype(F32), alog, dtb,
      jnp.tile(gdn_norm_g.reshape(1, HEAD_DIM).astype(F32), (1, LANES // HEAD_DIM)))
    return out.reshape(bsz * seq, GDN_WIDTH)


def _memkv_kernel(m_ref, w_ref, k_ref, v_ref):
    r = jnp.dot(m_ref[...].astype(BF16), w_ref[...], preferred_element_type=F32)
    k_ref[...] = r[:, :MEM_WIDTH].astype(BF16)
    v_ref[...] = r[:, MEM_WIDTH:].astype(BF16)


def _memkv(mem2, w_mem_kv, n_mem):
    n_rows, d_model = mem2.shape
    row = lambda i: (i, 0)
    return pl.pallas_call(
        _memkv_kernel,
        grid=(n_rows // n_mem,),
        in_specs=[pl.BlockSpec((n_mem, d_model), row),
                  pl.BlockSpec((d_model, 2 * MEM_WIDTH), lambda i: (0, 0))],
        out_specs=[pl.BlockSpec((n_mem, MEM_WIDTH), row)] * 2,
        out_shape=[jax.ShapeDtypeStruct((n_rows, MEM_WIDTH), BF16)] * 2,
        compiler_params=_cparams(("parallel",)),
        name="memkv",
    )(mem2, w_mem_kv.astype(BF16))


_ROUTE_ROWS = 128


def _post_kernel(x_ref, oa_ref, ob_ref, qc_ref, mk_ref, mv_ref, wout_ref, ag_ref, mg_ref,
                 lg_ref, lb_ref, wr_ref, br_ref, x1_ref, ids_ref, gate_ref, *, alpha):
    tm = x_ref.shape[0]
    oa = oa_ref[...]
    oa = oa * lax.rsqrt(jnp.mean(oa * oa, axis=-1, keepdims=True) + RMS_EPS) * ag_ref[...]

    qc = qc_ref[...]
    mk = mk_ref[...]
    mv = mv_ref[...]
    cols = [slice(h * HEAD_DIM, (h + 1) * HEAD_DIM) for h in range(MEM_HEADS)]
    lgs = [_dot_nt(qc[:, c], mk[:, c]) * (HEAD_DIM ** -0.5) for c in cols]
    es = [jnp.exp(lg - jnp.max(lg, axis=-1, keepdims=True)) for lg in lgs]
    ps = [e / jnp.sum(e, axis=-1, keepdims=True) for e in es]
    oc = jnp.concatenate([jnp.dot(p.astype(BF16), mv[:, c], preferred_element_type=F32)
                          for p, c in zip(ps, cols)], axis=1)
    oc = oc * lax.rsqrt(jnp.mean(oc * oc, axis=-1, keepdims=True) + RMS_EPS) * mg_ref[...]

    mix = (jnp.dot(oa.astype(BF16), wout_ref[0:A_WIDTH, :], preferred_element_type=F32)
           + jnp.dot(ob_ref[...].astype(BF16), wout_ref[A_WIDTH:A_WIDTH + GDN_WIDTH, :],
                     preferred_element_type=F32)
           + jnp.dot(oc.astype(BF16), wout_ref[A_WIDTH + GDN_WIDTH:, :], preferred_element_type=F32))
    hres = alpha * x_ref[...] + mix
    mu = jnp.mean(hres, axis=-1, keepdims=True)
    var = jnp.mean(jnp.square(hres - mu), axis=-1, keepdims=True)
    x1 = (hres - mu) * lax.rsqrt(var + LN_EPS) * lg_ref[...] + lb_ref[...]
    x1_ref[...] = x1

    lt = _dot_nt(wr_ref[...], x1) + br_ref[...]
    sub = lax.broadcasted_iota(I32, (EXPERTS_PER_GROUP, tm), 0)
    gl = lt[0:N_GROUPS, :]
    gmax = jnp.max(gl, axis=0, keepdims=True)
    gprob = jnp.exp(gl - gmax) / jnp.sum(jnp.exp(gl - gmax), axis=0, keepdims=True)
    p_grp = jnp.max(gprob, axis=0, keepdims=True)
    grp = jnp.min(jnp.where(gprob == p_grp, sub, N_GROUPS), axis=0, keepdims=True)
    el = jnp.zeros((EXPERTS_PER_GROUP, tm), F32)
    for g in range(N_GROUPS):
        r0 = N_GROUPS + g * EXPERTS_PER_GROUP
        el = el + jnp.where(grp == g, lt[r0:r0 + EXPERTS_PER_GROUP, :], 0.0)
    ee = jnp.exp(el - jnp.max(el, axis=0, keepdims=True))
    pe = ee / jnp.sum(ee, axis=0, keepdims=True)
    p1 = jnp.max(pe, axis=0, keepdims=True)
    i1 = jnp.min(jnp.where(pe == p1, sub, EXPERTS_PER_GROUP), axis=0, keepdims=True)
    rest = jnp.where(sub == i1, -1.0, pe)
    p2 = jnp.max(rest, axis=0, keepdims=True)
    i2 = jnp.min(jnp.where(rest == p2, sub, EXPERTS_PER_GROUP), axis=0, keepdims=True)
    psum = p1 + p2
    g1 = p_grp * p1 / psum
    g2 = p_grp * p2 / psum
    e1 = grp * EXPERTS_PER_GROUP + i1
    e2 = grp * EXPERTS_PER_GROUP + i2
    ids_ref[...] = jnp.where(sub == 0, e1, jnp.where(sub == 1, e2, 0))
    gate_ref[...] = jnp.where(sub == 0, g1, jnp.where(sub == 1, g2, 0.0))


def _post(x2, oa, ob, qc, mk, mv, w_out, attn_g, mem_g, ln_g, ln_b, w_group, b_group,
          w_router, b_router, seq, n_mem, alpha, tm=512):
    n_tok, d_model = x2.shape
    n_t = seq // tm
    row = lambda i: (i, 0)
    const = lambda i: (0, 0)
    per_b = lambda i: (i // n_t, 0, 0)
    lane = lambda i: (0, i)
    bsz = n_tok // seq
    wr = jnp.pad(jnp.concatenate([w_group, w_router], axis=1).T,
                 ((0, _ROUTE_ROWS - N_GROUPS - N_EXPERTS), (0, 0))).astype(BF16)
    br = jnp.pad(jnp.concatenate([b_group, b_router]),
                 (0, _ROUTE_ROWS - N_GROUPS - N_EXPERTS)).reshape(_ROUTE_ROWS, 1).astype(F32)
    vec = lambda v: v.reshape(1, -1).astype(F32)
    return pl.pallas_call(
        functools.partial(_post_kernel, alpha=alpha),
        grid=(n_tok // tm,),
        in_specs=[pl.BlockSpec((tm, d_model), row),
                  pl.BlockSpec((tm, A_WIDTH), row),
                  pl.BlockSpec((tm, GDN_WIDTH), row),
                  pl.BlockSpec((tm, MEM_WIDTH), row),
                  pl.BlockSpec((None, n_mem, MEM_WIDTH), per_b),
                  pl.BlockSpec((None, n_mem, MEM_WIDTH), per_b),
                  pl.BlockSpec(w_out.shape, const),
                  pl.BlockSpec((1, A_WIDTH), const),
                  pl.BlockSpec((1, MEM_WIDTH), const),
                  pl.BlockSpec((1, d_model), const),
                  pl.BlockSpec((1, d_model), const),
                  pl.BlockSpec((_ROUTE_ROWS, d_model), const),
                  pl.BlockSpec((_ROUTE_ROWS, 1), const)],
        out_specs=[pl.BlockSpec((tm, d_model), row),
                   pl.BlockSpec((EXPERTS_PER_GROUP, tm), lane),
                   pl.BlockSpec((EXPERTS_PER_GROUP, tm), lane)],
        out_shape=[jax.ShapeDtypeStruct((n_tok, d_model), F32),
                   jax.ShapeDtypeStruct((EXPERTS_PER_GROUP, n_tok), I32),
                   jax.ShapeDtypeStruct((EXPERTS_PER_GROUP, n_tok), F32)],
        compiler_params=_cparams(("parallel",)),
        name="post",
    )(x2, oa, ob, qc, mk.reshape(bsz, n_mem, MEM_WIDTH), mv.reshape(bsz, n_mem, MEM_WIDTH),
      w_out.astype(BF16), vec(attn_g), vec(mem_g), vec(ln_g), vec(ln_b), wr, br)


def _rank_kernel(ids_ref, rank_ref, cnt_ref, carry_ref):
    i = pl.program_id(0)
    tm = ids_ref.shape[1]

    @pl.when(i == 0)
    def _():
        carry_ref[...] = jnp.zeros(carry_ref.shape, F32)

    ids = ids_ref[...]
    eio = lax.broadcasted_iota(I32, (N_EXPERTS, tm), 0)
    oh0 = jnp.where(eio == ids[0:1, :], 1.0, 0.0)
    oh1 = jnp.where(eio == ids[1:2, :], 1.0, 0.0)
    cnt = oh0 + oh1
    before = (lax.broadcasted_iota(I32, (tm, tm), 0) < lax.broadcasted_iota(I32, (tm, tm), 1))
    prefix = jnp.dot(cnt.astype(BF16), jnp.where(before, 1.0, 0.0).astype(BF16),
                     preferred_element_type=F32) + carry_ref[:, 0:1]
    r0 = jnp.sum(oh0 * prefix, axis=0, keepdims=True)
    r1 = jnp.sum(oh1 * prefix, axis=0, keepdims=True)
    sub = lax.broadcasted_iota(I32, (EXPERTS_PER_GROUP, tm), 0)
    rank_ref[...] = jnp.where(sub == 0, r0, jnp.where(sub == 1, r1, 0.0)).astype(I32)
    carry_ref[...] = carry_ref[...] + jnp.sum(cnt, axis=1, keepdims=True)
    cnt_ref[...] = carry_ref[...].astype(I32)


def _rank(ids, tm=512):
    n_tok = ids.shape[1]
    lane = lambda i: (0, i)
    return pl.pallas_call(
        _rank_kernel,
        grid=(n_tok // tm,),
        in_specs=[pl.BlockSpec((EXPERTS_PER_GROUP, tm), lane)],
        out_specs=[pl.BlockSpec((EXPERTS_PER_GROUP, tm), lane),
                   pl.BlockSpec((N_EXPERTS, LANES), lambda i: (0, 0))],
        out_shape=[jax.ShapeDtypeStruct((EXPERTS_PER_GROUP, n_tok), I32),
                   jax.ShapeDtypeStruct((N_EXPERTS, LANES), I32)],
        scratch_shapes=[pltpu.VMEM((N_EXPERTS, LANES), F32)],
        compiler_params=_cparams(("arbitrary",)),
        name="rank",
    )(ids)


def _dest_kernel(ids_ref, rank_ref, ps_ref, dest_ref):
    ids = ids_ref[...]
    tm = ids.shape[1]
    eio = lax.broadcasted_iota(I32, (N_EXPERTS, tm), 0)
    ps = ps_ref[...]
    d0 = jnp.sum(jnp.where(eio == ids[0:1, :], ps, 0.0), axis=0, keepdims=True)
    d1 = jnp.sum(jnp.where(eio == ids[1:2, :], ps, 0.0), axis=0, keepdims=True)
    sub = lax.broadcasted_iota(I32, (EXPERTS_PER_GROUP, tm), 0)
    dest_ref[...] = rank_ref[...] + jnp.where(sub == 0, d0, jnp.where(sub == 1, d1, 0.0)).astype(I32)


def _dest(ids, rank, pad_start, tm=2048):
    n_tok = ids.shape[1]
    tm = min(tm, n_tok)
    lane = lambda i: (0, i)
    return pl.pallas_call(
        _dest_kernel,
        grid=(n_tok // tm,),
        in_specs=[pl.BlockSpec((EXPERTS_PER_GROUP, tm), lane),
                  pl.BlockSpec((EXPERTS_PER_GROUP, tm), lane),
                  pl.BlockSpec((N_EXPERTS, 1), lambda i: (0, 0))],
        out_specs=pl.BlockSpec((EXPERTS_PER_GROUP, tm), lane),
        out_shape=jax.ShapeDtypeStruct((EXPERTS_PER_GROUP, n_tok), I32),
        compiler_params=_cparams(("parallel",)),
        name="dest",
    )(ids, rank, pad_start.astype(F32).reshape(N_EXPERTS, 1))


def _sc_mesh():
    return plsc.VectorSubcoreMesh(core_axis_name="c", subcore_axis_name="s",
                                  num_cores=SC_CORES, num_subcores=SC_SUBCORES)


def _sc_scatter_rows(x1, dest, pad_rows, cap):
    n_tok, d = x1.shape
    n_pad = pad_rows.shape[0]
    n_workers = SC_CORES * SC_SUBCORES
    tok_per_worker = n_tok // n_workers
    pad_per_worker = n_pad // n_workers
    n_win = tok_per_worker // SC_ROWS
    assert n_tok % (n_workers * SC_ROWS) == 0 and n_pad % (n_workers * SC_ROWS) == 0 and TOP_K == 2
    zero_rows = jnp.zeros((SC_ROWS, d), x1.dtype)

    @functools.partial(
        pl.kernel, mesh=_sc_mesh(), out_type=jax.ShapeDtypeStruct((cap, d), x1.dtype),
        scratch_types=[pltpu.VMEM((tok_per_worker,), I32), pltpu.VMEM((tok_per_worker,), I32),
                       pltpu.VMEM((pad_per_worker,), I32), pltpu.VMEM((2, SC_ROWS, d), x1.dtype),
                       pltpu.SemaphoreType.DMA((2,)), pltpu.SemaphoreType.DMA((2,))],
        name="sc_scatter")
    def scatter(x_hbm, dest_hbm, pad_hbm, zero_hbm, out_hbm, idx0_v, idx1_v, pad_v, rows_v, lsem, ssem):
        worker = lax.axis_index("s") * SC_CORES + lax.axis_index("c")
        base = worker * tok_per_worker
        pltpu.sync_copy(dest_hbm.at[pl.ds(base, tok_per_worker)], idx0_v)
        pltpu.sync_copy(dest_hbm.at[pl.ds(n_tok + base, tok_per_worker)], idx1_v)
        pltpu.sync_copy(pad_hbm.at[pl.ds(worker * pad_per_worker, pad_per_worker)], pad_v)

        def load(i, slot):
            return pltpu.make_async_copy(x_hbm.at[pl.ds(base + i * SC_ROWS, SC_ROWS)],
                                         rows_v.at[slot], lsem.at[slot])

        def store(idx_v, i, slot):
            return pltpu.make_async_copy(rows_v.at[slot],
                                         out_hbm.at[idx_v.at[pl.ds(i * SC_ROWS, SC_ROWS)]], ssem.at[slot])

        load(0, 0).start()

        @pl.loop(0, n_win)
        def _(i):
            slot = i % 2
            load(i, slot).wait()
            store(idx0_v, i, slot).start()
            store(idx1_v, i, slot).start()

            @pl.when(i >= 1)
            def _():
                store(idx0_v, i - 1, 1 - slot).wait()
                store(idx1_v, i - 1, 1 - slot).wait()

            @pl.when(i + 1 < n_win)
            def _():
                load(i + 1, 1 - slot).start()

        last = (n_win - 1) % 2
        store(idx0_v, n_win - 1, last).wait()
        store(idx1_v, n_win - 1, last).wait()

        pltpu.sync_copy(zero_hbm, rows_v.at[0])

        @pl.loop(0, pad_per_worker // SC_ROWS)
        def _(i):
            pltpu.async_copy(rows_v.at[0], out_hbm.at[pad_v.at[pl.ds(i * SC_ROWS, SC_ROWS)]],
                             ssem.at[0]).wait()

    return scatter(x1, dest, pad_rows, zero_rows)


def _mlp_kernel(be_ref, nused_ref, x_ref, wg_ref, wu_ref, wd_ref, y_ref):
    i = pl.program_id(0)

    @pl.when(i < nused_ref[0])
    def _():
        xb = x_ref[...].astype(BF16)
        hg = jnp.dot(xb, wg_ref[...].astype(BF16), preferred_element_type=F32)
        hu = jnp.dot(xb, wu_ref[...].astype(BF16), preferred_element_type=F32)
        y_ref[...] = jnp.dot((_silu(hg) * hu).astype(BF16), wd_ref[...].astype(BF16),
                             preferred_element_type=F32)

    @pl.when(i >= nused_ref[0])
    def _():
        y_ref[...] = jnp.zeros(y_ref.shape, F32)


def _mlp(blk_expert, n_used, xbuf, w_gate, w_up, w_down):
    cap, d_model = xbuf.shape
    d_exp = w_gate.shape[2]
    blk = lambda i, be, nu: (i, 0)
    used_blk = lambda i, be, nu: (jnp.minimum(i, nu[0] - 1), 0)
    wsel = lambda i, be, nu: (be[i], 0, 0)
    return pl.pallas_call(
        _mlp_kernel,
        grid_spec=pltpu.PrefetchScalarGridSpec(
            num_scalar_prefetch=2,
            grid=(cap // MOE_BLOCK,),
            in_specs=[pl.BlockSpec((MOE_BLOCK, d_model), used_blk),
                      pl.BlockSpec((None, d_model, d_exp), wsel),
                      pl.BlockSpec((None, d_model, d_exp), wsel),
                      pl.BlockSpec((None, d_exp, d_model), wsel)],
            out_specs=pl.BlockSpec((MOE_BLOCK, d_model), blk)),
        out_shape=jax.ShapeDtypeStruct((cap, d_model), F32),
        compiler_params=_cparams(("arbitrary",)),
        name="mlp",
    )(blk_expert, n_used, xbuf, w_gate, w_up, w_down)


def _sc_gather_rows(table, idx):
    n = idx.shape[0]
    d = table.shape[1]
    n_workers = SC_CORES * SC_SUBCORES
    per_worker = n // n_workers
    n_win = per_worker // SC_ROWS
    assert n % (n_workers * SC_ROWS) == 0

    @functools.partial(
        pl.kernel, mesh=_sc_mesh(), out_type=jax.ShapeDtypeStruct((n, d), table.dtype),
        scratch_types=[pltpu.VMEM((per_worker,), I32), pltpu.VMEM((2, SC_ROWS, d), table.dtype),
                       pltpu.SemaphoreType.DMA((2,)), pltpu.SemaphoreType.DMA((2,))],
        name="sc_gather")
    def gather(table_hbm, idx_hbm, out_hbm, idx_v, rows_v, gsem, wsem):
        worker = lax.axis_index("s") * SC_CORES + lax.axis_index("c")
        base = worker * per_worker
        pltpu.sync_copy(idx_hbm.at[pl.ds(base, per_worker)], idx_v)

        def fetch(i, slot):
            return pltpu.make_async_copy(table_hbm.at[idx_v.at[pl.ds(i * SC_ROWS, SC_ROWS)]],
                                         rows_v.at[slot], gsem.at[slot])

        def write(i, slot):
            return pltpu.make_async_copy(rows_v.at[slot],
                                         out_hbm.at[pl.ds(base + i * SC_ROWS, SC_ROWS)], wsem.at[slot])

        fetch(0, 0).start()

        @pl.loop(0, n_win)
        def _(i):
            slot = i % 2
            fetch(i, slot).wait()
            write(i, slot).start()

            @pl.when(i >= 1)
            def _():
                write(i - 1, 1 - slot).wait()

            @pl.when(i + 1 < n_win)
            def _():
                fetch(i + 1, 1 - slot).start()

        write(n_win - 1, (n_win - 1) % 2).wait()

    return gather(table, idx)


def _combine_kernel(x1_ref, y0_ref, y1_ref, gt_ref, lg_ref, lb_ref, o_ref, *, alpha):
    gt = gt_ref[...]
    ffn = y0_ref[...] * gt[:, 0:1] + y1_ref[...] * gt[:, 1:2]
    hres = alpha * x1_ref[...] + ffn
    mu = jnp.mean(hres, axis=-1, keepdims=True)
    var = jnp.mean(jnp.square(hres - mu), axis=-1, keepdims=True)
    o_ref[...] = (hres - mu) * lax.rsqrt(var + LN_EPS) * lg_ref[...] + lb_ref[...]


def _combine(dest, x1, gates_t, ln_g, ln_b, ybuf, alpha, tc=512):
    n_tok, d_model = x1.shape
    n_tiles = n_tok // tc
    yrows = _sc_gather_rows(ybuf, dest.reshape(-1))
    row = lambda i: (i, 0)
    const = lambda i: (0, 0)
    vec = lambda v: v.reshape(1, -1).astype(F32)
    return pl.pallas_call(
        functools.partial(_combine_kernel, alpha=alpha),
        grid=(n_tiles,),
        in_specs=[pl.BlockSpec((tc, d_model), row),
                  pl.BlockSpec((tc, d_model), row),
                  pl.BlockSpec((tc, d_model), lambda i: (i + n_tiles, 0)),
                  pl.BlockSpec((tc, EXPERTS_PER_GROUP), row),
                  pl.BlockSpec((1, d_model), const),
                  pl.BlockSpec((1, d_model), const)],
        out_specs=pl.BlockSpec((tc, d_model), row),
        out_shape=jax.ShapeDtypeStruct((n_tok, d_model), F32),
        compiler_params=_cparams(("parallel",)),
        name="combine",
    )(x1, yrows, yrows, gates_t, vec(ln_g), vec(ln_b))


def _moe(x1, ids, gates, w_gate, w_up, w_down, ln_g, ln_b, alpha):
    n_tok, d_model = x1.shape
    rank, counts = _rank(ids)
    counts = counts[:, 0]
    padded = (counts + MOE_BLOCK - 1) // MOE_BLOCK * MOE_BLOCK
    pad_ends = jnp.cumsum(padded)
    pad_start = (pad_ends - padded).astype(I32)
    n_asg = n_tok * TOP_K
    cap = (n_asg + MOE_BLOCK - 1) // MOE_BLOCK * MOE_BLOCK + N_EXPERTS * MOE_BLOCK
    n_blk = cap // MOE_BLOCK
    blk_pos = jnp.arange(n_blk, dtype=I32) * MOE_BLOCK
    blk_expert = jnp.minimum(
        jnp.sum((pad_ends[None, :] <= blk_pos[:, None]).astype(I32), axis=1), N_EXPERTS - 1)
    n_used = (pad_ends[-1:] // MOE_BLOCK).astype(I32)
    dest = _dest(ids, rank, pad_start)[0:TOP_K]
    slot = jnp.arange(MOE_BLOCK, dtype=I32)[None, :]
    spare = cap - 1 - (jnp.arange(N_EXPERTS * MOE_BLOCK, dtype=I32).reshape(N_EXPERTS, MOE_BLOCK)
                       % N_EXPERTS)
    pad_rows = jnp.where(slot < (padded - counts)[:, None], (pad_start + counts)[:, None] + slot,
                         spare).reshape(-1)
    xbuf = _sc_scatter_rows(x1, dest.reshape(-1), pad_rows, cap)
    ybuf = _mlp(blk_expert, n_used, xbuf, w_gate, w_up, w_down)
    return _combine(dest, x1, gates.T, ln_g, ln_b, ybuf, alpha)


def kernel(x, mem, w_in, kv_norm_g, w_k_up, w_v_up, conv_w, A_log, dt_bias, gdn_norm_g, attn_norm_g, mem_norm_g, w_mem_kv, w_out, ln1_g, ln1_b, w_group, b_group, w_router, b_router, w_gate, w_up, w_down, ln2_g, ln2_b):
    bsz, seq, d_model = x.shape
    n_mem = mem.shape[1]
    depth = w_in.shape[0]
    alpha = (2 * depth) ** 0.25
    inv_freq = 1.0 / (ROPE_THETA ** (jnp.arange(0, HEAD_DIM, 2, dtype=F32) / HEAD_DIM))
    ang = jnp.arange(seq, dtype=F32)[:, None] * inv_freq[None, :]
    cos128 = jnp.tile(jnp.cos(ang), (1, LANES // (HEAD_DIM // 2)))
    sin128 = jnp.tile(jnp.sin(ang), (1, LANES // (HEAD_DIM // 2)))
    x2 = x.reshape(bsz * seq, d_model)
    mem2 = mem.reshape(bsz * n_mem, d_model)
    for l in range(depth):
        qa, iq, ik, ka, va, iw, qkvb, zb, ab, qc = _proj(
            x2, w_in[l], w_k_up[l], w_v_up[l], kv_norm_g[l], cos128, sin128, seq)
        oa = _dsa(qa, iq, iw, ik, ka, va, bsz, seq)
        ob = _gdn(qkvb, ab, zb, conv_w[l], A_log[l], dt_bias[l], gdn_norm_g[l], bsz, seq)
        mk, mv = _memkv(mem2, w_mem_kv[l], n_mem)
        x1, ids, gates = _post(x2, oa, ob, qc, mk, mv, w_out[l], attn_norm_g[l], mem_norm_g[l],
                               ln1_g[l], ln1_b[l], w_group[l], b_group[l], w_router[l],
                               b_router[l], seq, n_mem, alpha)
        x2 = _moe(x1, ids, gates, w_gate[l], w_up[l], w_down[l], ln2_g[l], ln2_b[l], alpha)
    return x2.reshape(bsz, seq, d_model)
```

```python
import functools

import numpy as np
import jax
import jax.numpy as jnp
from jax import lax
from jax.experimental import pallas as pl
from jax.experimental.pallas import tpu as pltpu
from jax.experimental.pallas import tpu_sc as plsc

F32 = jnp.float32
BF16 = jnp.bfloat16
I32 = jnp.int32
I16 = jnp.int16

HEAD_DIM = 64
ATTN_HEADS = 6
A_WIDTH = ATTN_HEADS * HEAD_DIM
KV_RANK = 128
IDX_HEADS = 4
IDX_DIM = 64
INDEX_TOPK = 256
GDN_HEADS = 6
GDN_WIDTH = GDN_HEADS * HEAD_DIM
CONV_WIDTH = 4
CHUNK = 64
MEM_HEADS = 4
MEM_WIDTH = MEM_HEADS * HEAD_DIM
SPLIT_SIZES = (A_WIDTH, KV_RANK, IDX_HEADS * IDX_DIM, IDX_DIM, IDX_HEADS,
               GDN_WIDTH, GDN_WIDTH, GDN_WIDTH, GDN_WIDTH, GDN_HEADS, GDN_HEADS,
               MEM_WIDTH)
ROPE_THETA = 10000.0
N_GROUPS = 8
EXPERTS_PER_GROUP = 8
N_EXPERTS = N_GROUPS * EXPERTS_PER_GROUP
TOP_K = 2
MOE_BLOCK = 512
LN_EPS = 1e-5
RMS_EPS = 1e-6
NEG_INF = -1e30
INT_MIN = -2 ** 31
I16_MIN = -2 ** 15

LANES = 128
SUBLANES = 8
PACKED_ROWS = 16
HALF_BITS = 16
HALF_SPAN = 1 << HALF_BITS
INT_MAX = 2 ** 31 - 1
SC_CORES = 2
SC_SUBCORES = 16
SC_ROWS = 32
VMEM_LIMIT = 56 * 1024 * 1024


def _cparams(sem):
    return pltpu.CompilerParams(dimension_semantics=sem, vmem_limit_bytes=VMEM_LIMIT)


def _dot(a, b):
    return jnp.dot(a.astype(BF16), b.astype(BF16), preferred_element_type=F32)


def _dot_nt(a, b):
    return lax.dot_general(a.astype(BF16), b.astype(BF16), (((1,), (1,)), ((), ())),
                           preferred_element_type=F32)


def _silu(t):
    return t * (1.0 / (1.0 + jnp.exp(-t)))


_P_QA = (0, 384)
_P_IQ = (384, 640)
_P_CKV = (640, 768)
_P_IK = (768, 896)
_P_IW = (896, 1024)
_P_QKVB = (1024, 2176)
_P_ZB = (2176, 2560)
_P_AB = (2560, 2688)
_P_QC = (2688, 2944)
_P_TOTAL = 2944
_V_ROWS = HEAD_DIM + PACKED_ROWS


def _rot_cols(w, n_heads):
    k = w.shape[0]
    w4 = w.reshape(k, n_heads, 2, HEAD_DIM // 2)
    return jnp.concatenate([-w4[:, :, 1:2], w4[:, :, 0:1]], axis=2).reshape(k, n_heads * HEAD_DIM)


def _pad_cols(w, width):
    return jnp.pad(w, ((0, 0), (0, width - w.shape[1])))


def _proj_kernel(x_ref, w_ref, w2_ref, cos_ref, sin_ref, kvg_ref,
                 qa_ref, iq_ref, ik_ref, ka_ref, va_ref, iw_ref, qkvb_ref, zb_ref, ab_ref, qc_ref):
    xb = x_ref[...].astype(BF16)

    def mm(slab):
        return jnp.dot(xb, w_ref[:, slab[0]:slab[1]], preferred_element_type=F32)

    cos = cos_ref[...]
    sin = sin_ref[...]
    cos3 = jnp.concatenate([cos] * 3, axis=1)
    sin3 = jnp.concatenate([sin] * 3, axis=1)
    cos2 = jnp.concatenate([cos] * 2, axis=1)
    sin2 = jnp.concatenate([sin] * 2, axis=1)
    cos64 = cos[:, :HEAD_DIM]
    sin64 = sin[:, :HEAD_DIM]

    def rotate_half(t):
        slabs = []
        for j in range(t.shape[1] // LANES):
            ts = t[:, j * LANES:(j + 1) * LANES]
            first = lax.broadcasted_iota(I32, ts.shape, 1) % HEAD_DIM < HEAD_DIM // 2
            slabs.append(jnp.where(first, -pltpu.roll(ts, LANES - HEAD_DIM // 2, 1),
                                   pltpu.roll(ts, HEAD_DIM // 2, 1)))
        return jnp.concatenate(slabs, axis=1)

    qa = mm(_P_QA)
    qa_ref[...] = (qa * cos3 + rotate_half(qa) * sin3).astype(BF16)
    iq = mm(_P_IQ)
    iq_ref[...] = (iq * cos2 + rotate_half(iq) * sin2).astype(BF16)

    ckv = mm(_P_CKV)
    cn = ckv * lax.rsqrt(jnp.mean(ckv * ckv, axis=-1, keepdims=True) + RMS_EPS) * kvg_ref[...]
    r = jnp.dot(cn.astype(BF16), w2_ref[...], preferred_element_type=F32)
    ka_ref[...] = (r[:, 0:64] * cos64 + r[:, 64:128] * sin64).astype(BF16)
    va_ref[...] = jnp.concatenate(
        [r[:, 128:256].T[0:HEAD_DIM, :], jnp.ones((_V_ROWS - HEAD_DIM, r.shape[0]), F32)],
        axis=0).astype(BF16)

    ikk = mm(_P_IK)
    ik_ref[...] = (ikk[:, 0:64] * cos64 + ikk[:, 64:128] * sin64).astype(BF16)
    iw_ref[...] = mm(_P_IW) * (IDX_HEADS ** -0.5 * IDX_DIM ** -0.5)
    qkvb_ref[...] = mm(_P_QKVB)
    zb_ref[...] = mm(_P_ZB)
    ab_ref[...] = mm(_P_AB)
    qc_ref[...] = mm(_P_QC).astype(BF16)


def _proj(x2, w_in, w_k_up, w_v_up, kv_norm_g, cos128, sin128, seq, tm=256):
    n_tok, d_model = x2.shape
    offs = np.cumsum(SPLIT_SIZES)[:-1].tolist()
    (w_qa, w_ckv, w_iq, w_ik, w_iw, w_qb, w_kb, w_vb, w_zb, w_a, w_b, w_qc) = jnp.split(w_in, offs, axis=1)
    w1 = jnp.concatenate([
        w_qa, w_iq, w_ckv,
        w_ik, _rot_cols(w_ik, 1), _pad_cols(w_iw, LANES),
        w_qb, w_kb, w_vb, w_zb, _pad_cols(jnp.concatenate([w_a, w_b], axis=1), LANES), w_qc,
    ], axis=1).astype(BF16)
    assert w1.shape[1] == _P_TOTAL
    w2 = _pad_cols(jnp.concatenate([w_k_up, _rot_cols(w_k_up, 1), w_v_up], axis=1), 2 * LANES).astype(BF16)
    n_pos = seq // tm
    row = lambda i: (i, 0)
    const = lambda i: (0, 0)
    pos = lambda i: (i % n_pos, 0)
    outs = [(A_WIDTH, BF16), (IDX_HEADS * IDX_DIM, BF16), (IDX_DIM, BF16), (HEAD_DIM, BF16),
            None, (LANES, F32), (3 * GDN_WIDTH, F32), (GDN_WIDTH, F32), (LANES, F32),
            (MEM_WIDTH, BF16)]
    out_specs = [pl.BlockSpec((tm, o[0]), row) if o else pl.BlockSpec((_V_ROWS, tm), lambda i: (0, i))
                 for o in outs]
    out_shape = [jax.ShapeDtypeStruct((n_tok, o[0]), o[1]) if o
                 else jax.ShapeDtypeStruct((_V_ROWS, n_tok), BF16) for o in outs]
    return pl.pallas_call(
        _proj_kernel,
        grid=(n_tok // tm,),
        in_specs=[pl.BlockSpec((tm, d_model), row),
                  pl.BlockSpec(w1.shape, const),
                  pl.BlockSpec(w2.shape, const),
                  pl.BlockSpec((tm, LANES), pos),
                  pl.BlockSpec((tm, LANES), pos),
                  pl.BlockSpec((1, KV_RANK), const)],
        out_specs=out_specs,
        out_shape=out_shape,
        compiler_params=_cparams(("parallel",)),
        name="proj",
    )(x2, w1, w2, cos128, sin128, kv_norm_g.reshape(1, KV_RANK).astype(F32))


def _dsa_kernel(qa_ref, iq_ref, iw_ref, ik_ref, ka_ref, vat_ref, o_ref,
                key_ref, hi_ref, lo_ref, m_ref, acc_ref, *, tq, kc, top):
    qi = pl.program_id(1)
    row0 = qi * tq
    n_kc = (row0 + tq + kc - 1) // kc
    qpos = row0 + lax.broadcasted_iota(I32, (1, tq), 1)

    def key_fold(v):
        return jnp.sum(v.reshape(kc // SUBLANES, SUBLANES, tq), axis=0)

    def head_rows(x, n_heads, width):
        return jnp.concatenate([x[:, h * width:(h + 1) * width] for h in range(n_heads)], axis=0)

    iq_rows = head_rows(iq_ref[...], IDX_HEADS, IDX_DIM)
    iw_t = iw_ref[...].T

    n_pairs = (n_kc + 1) // 2
    kc2 = 2 * kc

    def score_body(c, carry):
        k0 = pl.multiple_of(c * kc2, kc2)
        d = _dot_nt(ik_ref[pl.ds(k0, kc2), :], iq_rows)
        s = jnp.zeros((kc2, tq), F32)
        for h in range(IDX_HEADS):
            s = s + iw_t[h:h + 1, :] * jnp.maximum(d[:, h * tq:(h + 1) * tq], 0.0)
        kidx = k0 + lax.broadcasted_iota(I32, (kc2, tq), 0)
        s = jnp.where(s == 0.0, 0.0, s)
        s = jnp.where(kidx <= qpos, s, NEG_INF)
        bits = pltpu.bitcast(s, I32)
        key = jnp.where(bits >= 0, bits, bits ^ INT_MAX)
        key_ref[pl.ds(k0, kc2), :] = key
        hi_ref[pl.ds(k0, kc2), :] = lax.shift_right_arithmetic(key, HALF_BITS).astype(I16)
        lo_ref[pl.ds(k0, kc2), :] = ((key & (HALF_SPAN - 1)) + I16_MIN).astype(I16)
        return carry

    lax.fori_loop(0, n_pairs, score_body, 0)

    k_eff = jnp.minimum(top, qpos + 1).astype(F32)

    one_b = jnp.ones((), BF16)
    zero_b = jnp.zeros((), BF16)

    def count16(ref, pred):
        def body(c, acc):
            k0 = pl.multiple_of(c * (2 * kc), 2 * kc)
            for j in range(2):
                hit = jnp.where(pred(ref[pl.ds(k0 + j * kc, kc), :]), one_b, zero_b)
                hit = hit.reshape(kc // PACKED_ROWS, PACKED_ROWS, tq)
                parts = [hit[r] for r in range(kc // PACKED_ROWS)]
                while len(parts) > 1:
                    parts = [a + b for a, b in zip(parts[0::2], parts[1::2])]
                acc = acc + parts[0]
            return acc
        acc = lax.fori_loop(0, n_pairs, body, jnp.zeros((PACKED_ROWS, tq), BF16))
        return jnp.sum(acc.astype(F32), axis=0, keepdims=True)

    def search16(ref, start, n_bits, k_want):
        def bit_body(i, t):
            cand = t + lax.shift_left(jnp.int32(1), n_bits - 1 - i)
            c16 = cand.astype(I16)
            return jnp.where(count16(ref, lambda v: v >= c16) >= k_want, cand, t)
        return lax.fori_loop(0, n_bits, bit_body, start)

    hi0 = jnp.where(count16(hi_ref, lambda v: v >= 0) >= k_eff, 0, I16_MIN).astype(I32)
    thr_hi = search16(hi_ref, hi0, HALF_BITS - 1, k_eff)
    thr_hi16 = thr_hi.astype(I16)
    k_low = k_eff - count16(hi_ref, lambda v: v > thr_hi16)

    def band_body(c, carry):
        k0 = pl.multiple_of(c * kc2, kc2)
        lo_ref[pl.ds(k0, kc2), :] = jnp.where(hi_ref[pl.ds(k0, kc2), :] == thr_hi16,
                                              lo_ref[pl.ds(k0, kc2), :], I16_MIN)
        return carry

    lax.fori_loop(0, n_pairs, band_body, 0)
    thr_lo = search16(lo_ref, jnp.full((1, tq), I16_MIN, I32), HALF_BITS, k_low)
    thr_lo16 = thr_lo.astype(I16)
    thr = thr_hi * HALF_SPAN + (thr_lo - I16_MIN)
    n_tie = k_low - count16(lo_ref, lambda v: v > thr_lo16)

    q_rows = head_rows((qa_ref[...].astype(F32) * (HEAD_DIM ** -0.5)).astype(BF16),
                       ATTN_HEADS, HEAD_DIM)
    m_ref[...] = jnp.full(m_ref.shape, NEG_INF, F32)
    acc_ref[...] = jnp.zeros(acc_ref.shape, F32)
    earlier = jnp.where(lax.broadcasted_iota(I32, (kc, kc), 1) < lax.broadcasted_iota(I32, (kc, kc), 0),
                        1.0, 0.0).astype(BF16)

    def chunk_bias(k0, tie_seen):
        kk = key_ref[pl.ds(k0, kc), :]
        kidx = k0 + lax.broadcasted_iota(I32, (kc, tq), 0)
        tie = kk == thr
        tie_f = jnp.where(tie, 1.0, 0.0)
        tie_rank = jnp.dot(earlier, tie_f.astype(BF16), preferred_element_type=F32) + tie_seen
        bias = jnp.where(kk > thr, 0.0,
                         jnp.where(tie, jnp.where(tie_rank < n_tie, 0.0, NEG_INF), NEG_INF))
        bias = jnp.where(kidx <= qpos, bias, NEG_INF)
        return bias.astype(BF16), tie_seen + jnp.sum(key_fold(tie_f), axis=0, keepdims=True)

    def attn_body(c, tie_seen):
        k0 = pl.multiple_of(c * (2 * kc), 2 * kc)
        bias_0, tie_seen = chunk_bias(k0, tie_seen)
        bias_1, tie_seen = chunk_bias(k0 + kc, tie_seen)
        bias_b = jnp.concatenate([bias_0, bias_1], axis=0)
        lg_all = _dot_nt(ka_ref[pl.ds(k0, 2 * kc), :], q_rows).astype(BF16)
        lgs = [lg_all[:, h * tq:(h + 1) * tq] + bias_b for h in range(ATTN_HEADS)]
        m_prev = m_ref[...]
        m_new = jnp.maximum(m_prev, jnp.concatenate(
            [jnp.max(lg, axis=0, keepdims=True) for lg in lgs], axis=1).astype(F32))
        m_b = m_new[0:1, :].astype(BF16)
        p = jnp.concatenate([jnp.exp(lgs[h] - m_b[:, h * tq:(h + 1) * tq])
                             for h in range(ATTN_HEADS)], axis=1)
        alpha = jnp.exp(m_prev - m_new)
        acc_ref[...] = alpha[0:1, :] * acc_ref[...] + jnp.dot(
            vat_ref[:, pl.ds(k0, 2 * kc)], p, preferred_element_type=F32)
        m_ref[...] = m_new
        return tie_seen

    lax.fori_loop(0, n_pairs, attn_body, jnp.zeros((1, tq), F32))
    o_t = acc_ref[0:HEAD_DIM, :] / acc_ref[HEAD_DIM:HEAD_DIM + 1, :]
    o_ref[...] = jnp.concatenate(
        [o_t[:, h * tq:(h + 1) * tq].T for h in range(ATTN_HEADS)], axis=1)


def _dsa(qa, iq, iw, ik, ka, va_t, bsz, seq, tq=256):
    tq = min(tq, seq)
    kc = tq
    top = min(INDEX_TOPK, seq // 4)
    n_q = seq // tq
    assert seq // PACKED_ROWS <= 256 and (seq // kc) % 2 == 0
    row = lambda b, i: (b * n_q + i, 0)
    per_b = lambda b, i: (b, 0, 0)
    kern = functools.partial(_dsa_kernel, tq=tq, kc=kc, top=top)
    return pl.pallas_call(
        kern,
        grid=(bsz, n_q),
        in_specs=[pl.BlockSpec((tq, A_WIDTH), row),
                  pl.BlockSpec((tq, IDX_HEADS * IDX_DIM), row),
                  pl.BlockSpec((tq, LANES), row),
                  pl.BlockSpec((None, seq, IDX_DIM), per_b),
                  pl.BlockSpec((None, seq, HEAD_DIM), per_b),
                  pl.BlockSpec((_V_ROWS, seq), lambda b, i: (0, b))],
        out_specs=pl.BlockSpec((tq, A_WIDTH), row),
        out_shape=jax.ShapeDtypeStruct((bsz * seq, A_WIDTH), F32),
        scratch_shapes=[pltpu.VMEM((seq, tq), I32),
                        pltpu.VMEM((seq, tq), I16),
                        pltpu.VMEM((seq, tq), I16),
                        pltpu.VMEM((SUBLANES, ATTN_HEADS * tq), F32),
                        pltpu.VMEM((_V_ROWS, ATTN_HEADS * tq), F32)],
        compiler_params=_cparams(("parallel", "arbitrary")),
        name="dsa",
    )(qa, iq, iw, ik.reshape(bsz, seq, IDX_DIM), ka.reshape(bsz, seq, HEAD_DIM), va_t)


_HALO = SUBLANES


def _bmm(a, b):
    return lax.dot_general(a.astype(BF16), b.astype(BF16), (((2,), (1,)), ((0,), (0,))),
                           preferred_element_type=F32)


def _bmm_nt(a, b):
    return lax.dot_general(a.astype(BF16), b.astype(BF16), (((2,), (2,)), ((0,), (0,))),
                           preferred_element_type=F32)


def _gdn_kernel(qkv_ref, ab_ref, z_ref, cw_ref, alog_ref, dtb_ref, gn_ref, o_ref,
                ext_ref, state_ref, *, tr, nb):
    ti = pl.program_id(1)
    width = 3 * GDN_WIDTH
    n_c = tr // CHUNK
    n_p = GDN_HEADS // 2

    @pl.when(ti == 0)
    def _():
        ext_ref[:, 0:_HALO, :] = jnp.zeros((nb, _HALO, width), F32)
        state_ref[...] = jnp.zeros(state_ref.shape, F32)

    lane = lax.broadcasted_iota(I32, (CHUNK, LANES), 1)
    row_i = lax.broadcasted_iota(I32, (CHUNK, LANES), 0)
    left = lane < HEAD_DIM
    incl = (row_i >= lane % HEAD_DIM)[None]
    strict = (row_i > lane % HEAD_DIM)[None]
    left_t = (lax.broadcasted_iota(I32, (tr, LANES), 1) < HEAD_DIM)
    tri = jnp.where(lax.broadcasted_iota(I32, (CHUNK, CHUNK), 0) >= lax.broadcasted_iota(I32, (CHUNK, CHUNK), 1),
                    1.0, 0.0).astype(BF16)

    def blk(x):
        return jnp.concatenate([jnp.where(left[None], x, 0.0), jnp.where(left[None], 0.0, x)], axis=1)

    def head_scale(x, eps, scale):
        sq = x * x
        keep = left_t if x.shape[0] == tr else left
        s_l = jnp.sum(jnp.where(keep, sq, 0.0), axis=-1, keepdims=True)
        s_r = jnp.sum(jnp.where(keep, 0.0, sq), axis=-1, keepdims=True)
        return x * jnp.where(keep, lax.rsqrt(s_l * scale + eps), lax.rsqrt(s_r * scale + eps))

    def rows(c):
        return slice(c * CHUNK, (c + 1) * CHUNK)

    qkvs, betas, q_n, k_n, gc_cols, gc_rows = [], [], [], [], [], []
    for s in range(nb):
        ext_ref[s, _HALO:_HALO + tr, :] = qkv_ref[s]
        conv = cw_ref[CONV_WIDTH - 1:CONV_WIDTH, :] * ext_ref[s, _HALO:_HALO + tr, :]
        for j in range(CONV_WIDTH - 1):
            off = _HALO - (CONV_WIDTH - 1) + j
            conv = conv + cw_ref[j:j + 1, :] * ext_ref[s, off:off + tr, :]
        ext_ref[s, 0:_HALO, :] = ext_ref[s, tr:tr + _HALO, :]
        qkv = _silu(conv)
        qkvs.append(qkv)

        ab = ab_ref[s]
        sp_in = ab + dtb_ref[...]
        sp = jnp.maximum(sp_in, 0.0) + jnp.log1p(jnp.exp(-jnp.abs(sp_in)))
        g_all = -jnp.exp(alog_ref[...]) * sp
        betas.append(1.0 / (1.0 + jnp.exp(-ab)))

        cols, rws = [], []
        for c in range(n_c):
            g_c = g_all[rows(c), :]
            g_hi = g_c.astype(BF16)
            g_r1 = g_c - g_hi.astype(F32)
            g_mid = g_r1.astype(BF16)
            g_lo = (g_r1 - g_mid.astype(F32)).astype(BF16)
            gc = (jnp.dot(tri, g_hi, preferred_element_type=F32)
                  + jnp.dot(tri, g_mid, preferred_element_type=F32)
                  + jnp.dot(tri, g_lo, preferred_element_type=F32))
            cols.append(gc)
            rws.append(gc.T)
        gc_cols.append(cols)
        gc_rows.append(rws)
        q_n.append([head_scale(qkv[:, p * LANES:(p + 1) * LANES], RMS_EPS, 1.0) * (HEAD_DIM ** -0.5)
                    for p in range(n_p)])
        k_n.append([head_scale(qkv[:, GDN_WIDTH + p * LANES:GDN_WIDTH + (p + 1) * LANES], RMS_EPS, 1.0)
                    for p in range(n_p)])

    def stack(fn):
        return jnp.stack([fn(c, s, p) for c in range(n_c) for s in range(nb) for p in range(n_p)], axis=0)

    def pair_cols(x, c, p, base):
        return jnp.where(left, x[rows(c), base + 2 * p:base + 2 * p + 1],
                         x[rows(c), base + 2 * p + 1:base + 2 * p + 2])

    qs = stack(lambda c, s, p: q_n[s][p][rows(c), :])
    ks = stack(lambda c, s, p: k_n[s][p][rows(c), :])
    vs = stack(lambda c, s, p: qkvs[s][rows(c), 2 * GDN_WIDTH + p * LANES:2 * GDN_WIDTH + (p + 1) * LANES])
    beta = stack(lambda c, s, p: pair_cols(betas[s], c, p, GDN_HEADS))
    gcc = stack(lambda c, s, p: pair_cols(gc_cols[s][c], 0, p, 0))
    gcr = stack(lambda c, s, p: jnp.concatenate(
        [gc_rows[s][c][2 * p:2 * p + 1, :], gc_rows[s][c][2 * p + 1:2 * p + 2, :]], axis=1))
    decay = jnp.where(incl, jnp.exp(jnp.where(incl, gcc - gcr, 0.0)), 0.0)
    kb = ks * beta
    k_blk = blk(ks)
    low = jnp.where(strict, _bmm_nt(kb, k_blk) * decay, 0.0)
    e_gc = jnp.exp(gcc)
    u = vs * beta
    w = kb * e_gc
    u = u - _bmm(low, blk(u))
    w = w - _bmm(low, blk(w))
    pw = low
    for _ in range(5):
        pw = _bmm(pw, blk(pw))
        u = u + _bmm(pw, blk(u))
        w = w + _bmm(pw, blk(w))
    a_intra = jnp.where(incl, _bmm_nt(qs, k_blk) * decay, 0.0)
    q_dec = qs * e_gc
    gc_last = gcc[:, CHUNK - 1:CHUNK, :]
    kd_blk = blk(ks * jnp.exp(gc_last - gcc))
    kd_blk_t = jnp.stack([kd_blk[i].T for i in range(n_c * nb * n_p)], axis=0)
    g_last = jnp.exp(gc_last)

    gn = gn_ref[...]
    st = state_ref[...]
    per_c = nb * n_p
    for c in range(n_c):
        sl = slice(c * per_c, (c + 1) * per_c)
        v_new = blk(u[sl] - _bmm(w[sl], st))
        o = _bmm(q_dec[sl], st) + _bmm(a_intra[sl], v_new)
        st = st * g_last[sl] + _bmm(kd_blk_t[sl], v_new)
        for s in range(nb):
            for p in range(n_p):
                y = head_scale(o[s * n_p + p], RMS_EPS, 1.0 / HEAD_DIM) * gn
                o_ref[s, rows(c), p * LANES:(p + 1) * LANES] = y * _silu(
                    z_ref[s, rows(c), p * LANES:(p + 1) * LANES])
    state_ref[...] = st


def _gdn(qkvb, ab, zb, conv_w, a_log, dt_bias, gdn_norm_g, bsz, seq, tr=256):
    tr = min(tr, seq)
    n_t = seq // tr
    nb = 2 if bsz % 2 == 0 else 1
    blk3 = lambda b, i: (b, i, 0)
    const = lambda b, i: (0, 0)
    alog = _pad_cols(a_log.reshape(1, GDN_HEADS).astype(F32), LANES)
    dtb = _pad_cols(dt_bias.reshape(1, GDN_HEADS).astype(F32), LANES)
    out = pl.pallas_call(
        functools.partial(_gdn_kernel, tr=tr, nb=nb),
        grid=(bsz // nb, n_t),
        in_specs=[pl.BlockSpec((nb, tr, 3 * GDN_WIDTH), blk3),
                  pl.BlockSpec((nb, tr, LANES), blk3),
                  pl.BlockSpec((nb, tr, GDN_WIDTH), blk3),
                  pl.BlockSpec((CONV_WIDTH, 3 * GDN_WIDTH), const),
                  pl.BlockSpec((1, LANES), const),
                  pl.BlockSpec((1, LANES), const),
                  pl.BlockSpec((1, LANES), const)],
        out_specs=pl.BlockSpec((nb, tr, GDN_WIDTH), blk3),
        out_shape=jax.ShapeDtypeStruct((bsz, seq, GDN_WIDTH), F32),
        scratch_shapes=[pltpu.VMEM((nb, tr + _HALO, 3 * GDN_WIDTH), F32),
                        pltpu.VMEM((nb * (GDN_HEADS // 2), LANES, LANES), F32)],
        compiler_params=_cparams(("parallel", "arbitrary")),
        name="gdn",
    )(qkvb.reshape(bsz, seq, 3 * GDN_WIDTH), ab.reshape(bsz, seq, LANES), zb.reshape(bsz, seq, GDN_WIDTH),
      conv_w.astype(F32), alog, dtb,
      jnp.tile(gdn_norm_g.reshape(1, HEAD_DIM).astype(F32), (1, LANES // HEAD_DIM)))
    return out.reshape(bsz * seq, GDN_WIDTH)


def _memkv_kernel(m_ref, w_ref, k_ref, v_ref):
    r = jnp.dot(m_ref[...].astype(BF16), w_ref[...], preferred_element_type=F32)
    k_ref[...] = r[:, :MEM_WIDTH].astype(BF16)
    v_ref[...] = r[:, MEM_WIDTH:].astype(BF16)


def _memkv(mem2, w_mem_kv, n_mem):
    n_rows, d_model = mem2.shape
    row = lambda i: (i, 0)
    return pl.pallas_call(
        _memkv_kernel,
        grid=(n_rows // n_mem,),
        in_specs=[pl.BlockSpec((n_mem, d_model), row),
                  pl.BlockSpec((d_model, 2 * MEM_WIDTH), lambda i: (0, 0))],
        out_specs=[pl.BlockSpec((n_mem, MEM_WIDTH), row)] * 2,
        out_shape=[jax.ShapeDtypeStruct((n_rows, MEM_WIDTH), BF16)] * 2,
        compiler_params=_cparams(("parallel",)),
        name="memkv",
    )(mem2, w_mem_kv.astype(BF16))


_ROUTE_ROWS = 128


def _post_kernel(x_ref, oa_ref, ob_ref, qc_ref, mk_ref, mv_ref, wout_ref, ag_ref, mg_ref,
                 lg_ref, lb_ref, wr_ref, br_ref, x1_ref, ids_ref, gate_ref, *, alpha):
    tm = x_ref.shape[0]
    oa = oa_ref[...]
    oa = oa * lax.rsqrt(jnp.mean(oa * oa, axis=-1, keepdims=True) + RMS_EPS) * ag_ref[...]

    qc = qc_ref[...]
    mk = mk_ref[...]
    mv = mv_ref[...]
    cols = [slice(h * HEAD_DIM, (h + 1) * HEAD_DIM) for h in range(MEM_HEADS)]
    lgs = [_dot_nt(qc[:, c], mk[:, c]) * (HEAD_DIM ** -0.5) for c in cols]
    es = [jnp.exp(lg - jnp.max(lg, axis=-1, keepdims=True)) for lg in lgs]
    ps = [e / jnp.sum(e, axis=-1, keepdims=True) for e in es]
    oc = jnp.concatenate([jnp.dot(p.astype(BF16), mv[:, c], preferred_element_type=F32)
                          for p, c in zip(ps, cols)], axis=1)
    oc = oc * lax.rsqrt(jnp.mean(oc * oc, axis=-1, keepdims=True) + RMS_EPS) * mg_ref[...]

    mix = (jnp.dot(oa.astype(BF16), wout_ref[0:A_WIDTH, :], preferred_element_type=F32)
           + jnp.dot(ob_ref[...].astype(BF16), wout_ref[A_WIDTH:A_WIDTH + GDN_WIDTH, :],
                     preferred_element_type=F32)
           + jnp.dot(oc.astype(BF16), wout_ref[A_WIDTH + GDN_WIDTH:, :], preferred_element_type=F32))
    hres = alpha * x_ref[...] + mix
    mu = jnp.mean(hres, axis=-1, keepdims=True)
    var = jnp.mean(jnp.square(hres - mu), axis=-1, keepdims=True)
    x1 = (hres - mu) * lax.rsqrt(var + LN_EPS) * lg_ref[...] + lb_ref[...]
    x1_ref[...] = x1

    lt = _dot_nt(wr_ref[...], x1) + br_ref[...]
    sub = lax.broadcasted_iota(I32, (EXPERTS_PER_GROUP, tm), 0)
    gl = lt[0:N_GROUPS, :]
    gmax = jnp.max(gl, axis=0, keepdims=True)
    gprob = jnp.exp(gl - gmax) / jnp.sum(jnp.exp(gl - gmax), axis=0, keepdims=True)
    p_grp = jnp.max(gprob, axis=0, keepdims=True)
    grp = jnp.min(jnp.where(gprob == p_grp, sub, N_GROUPS), axis=0, keepdims=True)
    el = jnp.zeros((EXPERTS_PER_GROUP, tm), F32)
    for g in range(N_GROUPS):
        r0 = N_GROUPS + g * EXPERTS_PER_GROUP
        el = el + jnp.where(grp == g, lt[r0:r0 + EXPERTS_PER_GROUP, :], 0.0)
    ee = jnp.exp(el - jnp.max(el, axis=0, keepdims=True))
    pe = ee / jnp.sum(ee, axis=0, keepdims=True)
    p1 = jnp.max(pe, axis=0, keepdims=True)
    i1 = jnp.min(jnp.where(pe == p1, sub, EXPERTS_PER_GROUP), axis=0, keepdims=True)
    rest = jnp.where(sub == i1, -1.0, pe)
    p2 = jnp.max(rest, axis=0, keepdims=True)
    i2 = jnp.min(jnp.where(rest == p2, sub, EXPERTS_PER_GROUP), axis=0, keepdims=True)
    psum = p1 + p2
    g1 = p_grp * p1 / psum
    g2 = p_grp * p2 / psum
    e1 = grp * EXPERTS_PER_GROUP + i1
    e2 = grp * EXPERTS_PER_GROUP + i2
    ids_ref[...] = jnp.where(sub == 0, e1, jnp.where(sub == 1, e2, 0))
    gate_ref[...] = jnp.where(sub == 0, g1, jnp.where(sub == 1, g2, 0.0))


def _post(x2, oa, ob, qc, mk, mv, w_out, attn_g, mem_g, ln_g, ln_b, w_group, b_group,
          w_router, b_router, seq, n_mem, alpha, tm=512):
    n_tok, d_model = x2.shape
    n_t = seq // tm
    row = lambda i: (i, 0)
    const = lambda i: (0, 0)
    per_b = lambda i: (i // n_t, 0, 0)
    lane = lambda i: (0, i)
    bsz = n_tok // seq
    wr = jnp.pad(jnp.concatenate([w_group, w_router], axis=1).T,
                 ((0, _ROUTE_ROWS - N_GROUPS - N_EXPERTS), (0, 0))).astype(BF16)
    br = jnp.pad(jnp.concatenate([b_group, b_router]),
                 (0, _ROUTE_ROWS - N_GROUPS - N_EXPERTS)).reshape(_ROUTE_ROWS, 1).astype(F32)
    vec = lambda v: v.reshape(1, -1).astype(F32)
    return pl.pallas_call(
        functools.partial(_post_kernel, alpha=alpha),
        grid=(n_tok // tm,),
        in_specs=[pl.BlockSpec((tm, d_model), row),
                  pl.BlockSpec((tm, A_WIDTH), row),
                  pl.BlockSpec((tm, GDN_WIDTH), row),
                  pl.BlockSpec((tm, MEM_WIDTH), row),
                  pl.BlockSpec((None, n_mem, MEM_WIDTH), per_b),
                  pl.BlockSpec((None, n_mem, MEM_WIDTH), per_b),
                  pl.BlockSpec(w_out.shape, const),
                  pl.BlockSpec((1, A_WIDTH), const),
                  pl.BlockSpec((1, MEM_WIDTH), const),
                  pl.BlockSpec((1, d_model), const),
                  pl.BlockSpec((1, d_model), const),
                  pl.BlockSpec((_ROUTE_ROWS, d_model), const),
                  pl.BlockSpec((_ROUTE_ROWS, 1), const)],
        out_specs=[pl.BlockSpec((tm, d_model), row),
                   pl.BlockSpec((EXPERTS_PER_GROUP, tm), lane),
                   pl.BlockSpec((EXPERTS_PER_GROUP, tm), lane)],
        out_shape=[jax.ShapeDtypeStruct((n_tok, d_model), F32),
                   jax.ShapeDtypeStruct((EXPERTS_PER_GROUP, n_tok), I32),
                   jax.ShapeDtypeStruct((EXPERTS_PER_GROUP, n_tok), F32)],
        compiler_params=_cparams(("parallel",)),
        name="post",
    )(x2, oa, ob, qc, mk.reshape(bsz, n_mem, MEM_WIDTH), mv.reshape(bsz, n_mem, MEM_WIDTH),
      w_out.astype(BF16), vec(attn_g), vec(mem_g), vec(ln_g), vec(ln_b), wr, br)


def _rank_kernel(ids_ref, rank_ref, cnt_ref, carry_ref):
    i = pl.program_id(0)
    tm = ids_ref.shape[1]

    @pl.when(i == 0)
    def _():
        carry_ref[...] = jnp.zeros(carry_ref.shape, F32)

    ids = ids_ref[...]
    eio = lax.broadcasted_iota(I32, (N_EXPERTS, tm), 0)
    oh0 = jnp.where(eio == ids[0:1, :], 1.0, 0.0)
    oh1 = jnp.where(eio == ids[1:2, :], 1.0, 0.0)
    cnt = oh0 + oh1
    before = (lax.broadcasted_iota(I32, (tm, tm), 0) < lax.broadcasted_iota(I32, (tm, tm), 1))
    prefix = jnp.dot(cnt.astype(BF16), jnp.where(before, 1.0, 0.0).astype(BF16),
                     preferred_element_type=F32) + carry_ref[:, 0:1]
    r0 = jnp.sum(oh0 * prefix, axis=0, keepdims=True)
    r1 = jnp.sum(oh1 * prefix, axis=0, keepdims=True)
    sub = lax.broadcasted_iota(I32, (EXPERTS_PER_GROUP, tm), 0)
    rank_ref[...] = jnp.where(sub == 0, r0, jnp.where(sub == 1, r1, 0.0)).astype(I32)
    carry_ref[...] = carry_ref[...] + jnp.sum(cnt, axis=1, keepdims=True)
    cnt_ref[...] = carry_ref[...].astype(I32)


def _rank(ids, tm=512):
    n_tok = ids.shape[1]
    lane = lambda i: (0, i)
    return pl.pallas_call(
        _rank_kernel,
        grid=(n_tok // tm,),
        in_specs=[pl.BlockSpec((EXPERTS_PER_GROUP, tm), lane)],
        out_specs=[pl.BlockSpec((EXPERTS_PER_GROUP, tm), lane),
                   pl.BlockSpec((N_EXPERTS, LANES), lambda i: (0, 0))],
        out_shape=[jax.ShapeDtypeStruct((EXPERTS_PER_GROUP, n_tok), I32),
                   jax.ShapeDtypeStruct((N_EXPERTS, LANES), I32)],
        scratch_shapes=[pltpu.VMEM((N_EXPERTS, LANES), F32)],
        compiler_params=_cparams(("arbitrary",)),
        name="rank",
    )(ids)


def _dest_kernel(ids_ref, rank_ref, ps_ref, dest_ref):
    ids = ids_ref[...]
    tm = ids.shape[1]
    eio = lax.broadcasted_iota(I32, (N_EXPERTS, tm), 0)
    ps = ps_ref[...]
    d0 = jnp.sum(jnp.where(eio == ids[0:1, :], ps, 0.0), axis=0, keepdims=True)
    d1 = jnp.sum(jnp.where(eio == ids[1:2, :], ps, 0.0), axis=0, keepdims=True)
    sub = lax.broadcasted_iota(I32, (EXPERTS_PER_GROUP, tm), 0)
    dest_ref[...] = rank_ref[...] + jnp.where(sub == 0, d0, jnp.where(sub == 1, d1, 0.0)).astype(I32)


def _dest(ids, rank, pad_start, tm=2048):
    n_tok = ids.shape[1]
    tm = min(tm, n_tok)
    lane = lambda i: (0, i)
    return pl.pallas_call(
        _dest_kernel,
        grid=(n_tok // tm,),
        in_specs=[pl.BlockSpec((EXPERTS_PER_GROUP, tm), lane),
                  pl.BlockSpec((EXPERTS_PER_GROUP, tm), lane),
                  pl.BlockSpec((N_EXPERTS, 1), lambda i: (0, 0))],
        out_specs=pl.BlockSpec((EXPERTS_PER_GROUP, tm), lane),
        out_shape=jax.ShapeDtypeStruct((EXPERTS_PER_GROUP, n_tok), I32),
        compiler_params=_cparams(("parallel",)),
        name="dest",
    )(ids, rank, pad_start.astype(F32).reshape(N_EXPERTS, 1))


def _sc_mesh():
    return plsc.VectorSubcoreMesh(core_axis_name="c", subcore_axis_name="s",
                                  num_cores=SC_CORES, num_subcores=SC_SUBCORES)


def _sc_scatter_rows(x1, dest, pad_rows, cap):
    n_tok, d = x1.shape
    n_pad = pad_rows.shape[0]
    n_workers = SC_CORES * SC_SUBCORES
    tok_per_worker = n_tok // n_workers
    pad_per_worker = n_pad // n_workers
    n_win = tok_per_worker // SC_ROWS
    assert n_tok % (n_workers * SC_ROWS) == 0 and n_pad % (n_workers * SC_ROWS) == 0 and TOP_K == 2
    zero_rows = jnp.zeros((SC_ROWS, d), x1.dtype)

    @functools.partial(
        pl.kernel, mesh=_sc_mesh(), out_type=jax.ShapeDtypeStruct((cap, d), x1.dtype),
        scratch_types=[pltpu.VMEM((tok_per_worker,), I32), pltpu.VMEM((tok_per_worker,), I32),
                       pltpu.VMEM((pad_per_worker,), I32), pltpu.VMEM((2, SC_ROWS, d), x1.dtype),
                       pltpu.SemaphoreType.DMA((2,)), pltpu.SemaphoreType.DMA((2,))],
        name="sc_scatter")
    def scatter(x_hbm, dest_hbm, pad_hbm, zero_hbm, out_hbm, idx0_v, idx1_v, pad_v, rows_v, lsem, ssem):
        worker = lax.axis_index("s") * SC_CORES + lax.axis_index("c")
        base = worker * tok_per_worker
        pltpu.sync_copy(dest_hbm.at[pl.ds(base, tok_per_worker)], idx0_v)
        pltpu.sync_copy(dest_hbm.at[pl.ds(n_tok + base, tok_per_worker)], idx1_v)
        pltpu.sync_copy(pad_hbm.at[pl.ds(worker * pad_per_worker, pad_per_worker)], pad_v)

        def load(i, slot):
            return pltpu.make_async_copy(x_hbm.at[pl.ds(base + i * SC_ROWS, SC_ROWS)],
                                         rows_v.at[slot], lsem.at[slot])

        def store(idx_v, i, slot):
            return pltpu.make_async_copy(rows_v.at[slot],
                                         out_hbm.at[idx_v.at[pl.ds(i * SC_ROWS, SC_ROWS)]], ssem.at[slot])

        load(0, 0).start()

        @pl.loop(0, n_win)
        def _(i):
            slot = i % 2
            load(i, slot).wait()
            store(idx0_v, i, slot).start()
            store(idx1_v, i, slot).start()

            @pl.when(i >= 1)
            def _():
                store(idx0_v, i - 1, 1 - slot).wait()
                store(idx1_v, i - 1, 1 - slot).wait()

            @pl.when(i + 1 < n_win)
            def _():
                load(i + 1, 1 - slot).start()

        last = (n_win - 1) % 2
        store(idx0_v, n_win - 1, last).wait()
        store(idx1_v, n_win - 1, last).wait()

        pltpu.sync_copy(zero_hbm, rows_v.at[0])

        @pl.loop(0, pad_per_worker // SC_ROWS)
        def _(i):
            pltpu.async_copy(rows_v.at[0], out_hbm.at[pad_v.at[pl.ds(i * SC_ROWS, SC_ROWS)]],
                             ssem.at[0]).wait()

    return scatter(x1, dest, pad_rows, zero_rows)


def _mlp_kernel(be_ref, nused_ref, x_ref, wg_ref, wu_ref, wd_ref, y_ref):
    i = pl.program_id(0)

    @pl.when(i < nused_ref[0])
    def _():
        xb = x_ref[...].astype(BF16)
        hg = jnp.dot(xb, wg_ref[...].astype(BF16), preferred_element_type=F32)
        hu = jnp.dot(xb, wu_ref[...].astype(BF16), preferred_element_type=F32)
        y_ref[...] = jnp.dot((_silu(hg) * hu).astype(BF16), wd_ref[...].astype(BF16),
                             preferred_element_type=F32)

    @pl.when(i >= nused_ref[0])
    def _():
        y_ref[...] = jnp.zeros(y_ref.shape, F32)


def _mlp(blk_expert, n_used, xbuf, w_gate, w_up, w_down):
    cap, d_model = xbuf.shape
    d_exp = w_gate.shape[2]
    blk = lambda i, be, nu: (i, 0)
    used_blk = lambda i, be, nu: (jnp.minimum(i, nu[0] - 1), 0)
    wsel = lambda i, be, nu: (be[i], 0, 0)
    return pl.pallas_call(
        _mlp_kernel,
        grid_spec=pltpu.PrefetchScalarGridSpec(
            num_scalar_prefetch=2,
            grid=(cap // MOE_BLOCK,),
            in_specs=[pl.BlockSpec((MOE_BLOCK, d_model), used_blk),
                      pl.BlockSpec((None, d_model, d_exp), wsel),
                      pl.BlockSpec((None, d_model, d_exp), wsel),
                      pl.BlockSpec((None, d_exp, d_model), wsel)],
            out_specs=pl.BlockSpec((MOE_BLOCK, d_model), blk)),
        out_shape=jax.ShapeDtypeStruct((cap, d_model), F32),
        compiler_params=_cparams(("arbitrary",)),
        name="mlp",
    )(blk_expert, n_used, xbuf, w_gate, w_up, w_down)


def _sc_gather_rows(table, idx):
    n = idx.shape[0]
    d = table.shape[1]
    n_workers = SC_CORES * SC_SUBCORES
    per_worker = n // n_workers
    n_win = per_worker // SC_ROWS
    assert n % (n_workers * SC_ROWS) == 0

    @functools.partial(
        pl.kernel, mesh=_sc_mesh(), out_type=jax.ShapeDtypeStruct((n, d), table.dtype),
        scratch_types=[pltpu.VMEM((per_worker,), I32), pltpu.VMEM((2, SC_ROWS, d), table.dtype),
                       pltpu.SemaphoreType.DMA((2,)), pltpu.SemaphoreType.DMA((2,))],
        name="sc_gather")
    def gather(table_hbm, idx_hbm, out_hbm, idx_v, rows_v, gsem, wsem):
        worker = lax.axis_index("s") * SC_CORES + lax.axis_index("c")
        base = worker * per_worker
        pltpu.sync_copy(idx_hbm.at[pl.ds(base, per_worker)], idx_v)

        def fetch(i, slot):
            return pltpu.make_async_copy(table_hbm.at[idx_v.at[pl.ds(i * SC_ROWS, SC_ROWS)]],
                                         rows_v.at[slot], gsem.at[slot])

        def write(i, slot):
            return pltpu.make_async_copy(rows_v.at[slot],
                                         out_hbm.at[pl.ds(base + i * SC_ROWS, SC_ROWS)], wsem.at[slot])

        fetch(0, 0).start()

        @pl.loop(0, n_win)
        def _(i):
            slot = i % 2
            fetch(i, slot).wait()
            write(i, slot).start()

            @pl.when(i >= 1)
            def _():
                write(i - 1, 1 - slot).wait()

            @pl.when(i + 1 < n_win)
            def _():
                fetch(i + 1, 1 - slot).start()

        write(n_win - 1, (n_win - 1) % 2).wait()

    return gather(table, idx)


def _combine_kernel(x1_ref, y0_ref, y1_ref, gt_ref, lg_ref, lb_ref, o_ref, *, alpha):
    gt = gt_ref[...]
    ffn = y0_ref[...] * gt[:, 0:1] + y1_ref[...] * gt[:, 1:2]
    hres = alpha * x1_ref[...] + ffn
    mu = jnp.mean(hres, axis=-1, keepdims=True)
    var = jnp.mean(jnp.square(hres - mu), axis=-1, keepdims=True)
    o_ref[...] = (hres - mu) * lax.rsqrt(var + LN_EPS) * lg_ref[...] + lb_ref[...]


def _combine(dest, x1, gates_t, ln_g, ln_b, ybuf, alpha, tc=512):
    n_tok, d_model = x1.shape
    n_tiles = n_tok // tc
    yrows = _sc_gather_rows(ybuf, dest.reshape(-1))
    row = lambda i: (i, 0)
    const = lambda i: (0, 0)
    vec = lambda v: v.reshape(1, -1).astype(F32)
    return pl.pallas_call(
        functools.partial(_combine_kernel, alpha=alpha),
        grid=(n_tiles,),
        in_specs=[pl.BlockSpec((tc, d_model), row),
                  pl.BlockSpec((tc, d_model), row),
                  pl.BlockSpec((tc, d_model), lambda i: (i + n_tiles, 0)),
                  pl.BlockSpec((tc, EXPERTS_PER_GROUP), row),
                  pl.BlockSpec((1, d_model), const),
                  pl.BlockSpec((1, d_model), const)],
        out_specs=pl.BlockSpec((tc, d_model), row),
        out_shape=jax.ShapeDtypeStruct((n_tok, d_model), F32),
        compiler_params=_cparams(("parallel",)),
        name="combine",
    )(x1, yrows, yrows, gates_t, vec(ln_g), vec(ln_b))


def _moe(x1, ids, gates, w_gate, w_up, w_down, ln_g, ln_b, alpha):
    n_tok, d_model = x1.shape
    rank, counts = _rank(ids)
    counts = counts[:, 0]
    padded = (counts + MOE_BLOCK - 1) // MOE_BLOCK * MOE_BLOCK
    pad_ends = jnp.cumsum(padded)
    pad_start = (pad_ends - padded).astype(I32)
    n_asg = n_tok * TOP_K
    cap = (n_asg + MOE_BLOCK - 1) // MOE_BLOCK * MOE_BLOCK + N_EXPERTS * MOE_BLOCK
    n_blk = cap // MOE_BLOCK
    blk_pos = jnp.arange(n_blk, dtype=I32) * MOE_BLOCK
    blk_expert = jnp.minimum(
        jnp.sum((pad_ends[None, :] <= blk_pos[:, None]).astype(I32), axis=1), N_EXPERTS - 1)
    n_used = (pad_ends[-1:] // MOE_BLOCK).astype(I32)
    dest = _dest(ids, rank, pad_start)[0:TOP_K]
    slot = jnp.arange(MOE_BLOCK, dtype=I32)[None, :]
    spare = cap - 1 - (jnp.arange(N_EXPERTS * MOE_BLOCK, dtype=I32).reshape(N_EXPERTS, MOE_BLOCK)
                       % N_EXPERTS)
    pad_rows = jnp.where(slot < (padded - counts)[:, None], (pad_start + counts)[:, None] + slot,
                         spare).reshape(-1)
    xbuf = _sc_scatter_rows(x1, dest.reshape(-1), pad_rows, cap)
    ybuf = _mlp(blk_expert, n_used, xbuf, w_gate, w_up, w_down)
    return _combine(dest, x1, gates.T, ln_g, ln_b, ybuf, alpha)


def kernel(x, mem, w_in, kv_norm_g, w_k_up, w_v_up, conv_w, A_log, dt_bias, gdn_norm_g, attn_norm_g, mem_norm_g, w_mem_kv, w_out, ln1_g, ln1_b, w_group, b_group, w_router, b_router, w_gate, w_up, w_down, ln2_g, ln2_b):
    bsz, seq, d_model = x.shape
    n_mem = mem.shape[1]
    depth = w_in.shape[0]
    alpha = (2 * depth) ** 0.25
    inv_freq = 1.0 / (ROPE_THETA ** (jnp.arange(0, HEAD_DIM, 2, dtype=F32) / HEAD_DIM))
    ang = jnp.arange(seq, dtype=F32)[:, None] * inv_freq[None, :]
    cos128 = jnp.tile(jnp.cos(ang), (1, LANES // (HEAD_DIM // 2)))
    sin128 = jnp.tile(jnp.sin(ang), (1, LANES // (HEAD_DIM // 2)))
    x2 = x.reshape(bsz * seq, d_model)
    mem2 = mem.reshape(bsz * n_mem, d_model)
    for l in range(depth):
        qa, iq, ik, ka, va, iw, qkvb, zb, ab, qc = _proj(
            x2, w_in[l], w_k_up[l], w_v_up[l], kv_norm_g[l], cos128, sin128, seq)
        oa = _dsa(qa, iq, iw, ik, ka, va, bsz, seq)
        ob = _gdn(qkvb, ab, zb, conv_w[l], A_log[l], dt_bias[l], gdn_norm_g[l], bsz, seq)
        mk, mv = _memkv(mem2, w_mem_kv[l], n_mem)
        x1, ids, gates = _post(x2, oa, ob, qc, mk, mv, w_out[l], attn_norm_g[l], mem_norm_g[l],
                               ln1_g[l], ln1_b[l], w_group[l], b_group[l], w_router[l],
                               b_router[l], seq, n_mem, alpha)
        x2 = _moe(x1, ids, gates, w_gate[l], w_up[l], w_down[l], ln2_g[l], ln2_b[l], alpha)
    return x2.reshape(bsz, seq, d_model)
```

```python
import functools

import numpy as np
import jax
import jax.numpy as jnp
from jax import lax
from jax.experimental import pallas as pl
from jax.experimental.pallas import tpu as pltpu
from jax.experimental.pallas import tpu_sc as plsc

F32 = jnp.float32
BF16 = jnp.bfloat16
I32 = jnp.int32
I16 = jnp.int16

HEAD_DIM = 64
ATTN_HEADS = 6
A_WIDTH = ATTN_HEADS * HEAD_DIM
KV_RANK = 128
IDX_HEADS = 4
IDX_DIM = 64
INDEX_TOPK = 256
GDN_HEADS = 6
GDN_WIDTH = GDN_HEADS * HEAD_DIM
CONV_WIDTH = 4
CHUNK = 64
MEM_HEADS = 4
MEM_WIDTH = MEM_HEADS * HEAD_DIM
SPLIT_SIZES = (A_WIDTH, KV_RANK, IDX_HEADS * IDX_DIM, IDX_DIM, IDX_HEADS,
               GDN_WIDTH, GDN_WIDTH, GDN_WIDTH, GDN_WIDTH, GDN_HEADS, GDN_HEADS,
               MEM_WIDTH)
ROPE_THETA = 10000.0
N_GROUPS = 8
EXPERTS_PER_GROUP = 8
N_EXPERTS = N_GROUPS * EXPERTS_PER_GROUP
TOP_K = 2
MOE_BLOCK = 512
LN_EPS = 1e-5
RMS_EPS = 1e-6
NEG_INF = -1e30
INT_MIN = -2 ** 31
I16_MIN = -2 ** 15

LANES = 128
SUBLANES = 8
PACKED_ROWS = 16
HALF_BITS = 16
HALF_SPAN = 1 << HALF_BITS
INT_MAX = 2 ** 31 - 1
SC_CORES = 2
SC_SUBCORES = 16
SC_ROWS = 32
VMEM_LIMIT = 56 * 1024 * 1024


def _cparams(sem):
    return pltpu.CompilerParams(dimension_semantics=sem, vmem_limit_bytes=VMEM_LIMIT)


def _dot(a, b):
    return jnp.dot(a.astype(BF16), b.astype(BF16), preferred_element_type=F32)


def _dot_nt(a, b):
    return lax.dot_general(a.astype(BF16), b.astype(BF16), (((1,), (1,)), ((), ())),
                           preferred_element_type=F32)


def _silu(t):
    return t * (1.0 / (1.0 + jnp.exp(-t)))


_P_QA = (0, 384)
_P_IQ = (384, 640)
_P_CKV = (640, 768)
_P_IK = (768, 896)
_P_IW = (896, 1024)
_P_QKVB = (1024, 2176)
_P_ZB = (2176, 2560)
_P_AB = (2560, 2688)
_P_QC = (2688, 2944)
_P_TOTAL = 2944
_V_ROWS = HEAD_DIM + PACKED_ROWS


def _rot_cols(w, n_heads):
    k = w.shape[0]
    w4 = w.reshape(k, n_heads, 2, HEAD_DIM // 2)
    return jnp.concatenate([-w4[:, :, 1:2], w4[:, :, 0:1]], axis=2).reshape(k, n_heads * HEAD_DIM)


def _pad_cols(w, width):
    return jnp.pad(w, ((0, 0), (0, width - w.shape[1])))


def _proj_kernel(x_ref, w_ref, w2_ref, cos_ref, sin_ref, kvg_ref,
                 qa_ref, iq_ref, ik_ref, ka_ref, va_ref, iw_ref, qkvb_ref, zb_ref, ab_ref, qc_ref):
    xb = x_ref[...].astype(BF16)

    def mm(slab):
        return jnp.dot(xb, w_ref[:, slab[0]:slab[1]], preferred_element_type=F32)

    cos = cos_ref[...]
    sin = sin_ref[...]
    cos3 = jnp.concatenate([cos] * 3, axis=1)
    sin3 = jnp.concatenate([sin] * 3, axis=1)
    cos2 = jnp.concatenate([cos] * 2, axis=1)
    sin2 = jnp.concatenate([sin] * 2, axis=1)
    cos64 = cos[:, :HEAD_DIM]
    sin64 = sin[:, :HEAD_DIM]

    def rotate_half(t):
        slabs = []
        for j in range(t.shape[1] // LANES):
            ts = t[:, j * LANES:(j + 1) * LANES]
            first = lax.broadcasted_iota(I32, ts.shape, 1) % HEAD_DIM < HEAD_DIM // 2
            slabs.append(jnp.where(first, -pltpu.roll(ts, LANES - HEAD_DIM // 2, 1),
                                   pltpu.roll(ts, HEAD_DIM // 2, 1)))
        return jnp.concatenate(slabs, axis=1)

    qa = mm(_P_QA)
    qa_ref[...] = (qa * cos3 + rotate_half(qa) * sin3).astype(BF16)
    iq = mm(_P_IQ)
    iq_ref[...] = (iq * cos2 + rotate_half(iq) * sin2).astype(BF16)

    ckv = mm(_P_CKV)
    cn = ckv * lax.rsqrt(jnp.mean(ckv * ckv, axis=-1, keepdims=True) + RMS_EPS) * kvg_ref[...]
    r = jnp.dot(cn.astype(BF16), w2_ref[...], preferred_element_type=F32)
    ka_ref[...] = (r[:, 0:64] * cos64 + r[:, 64:128] * sin64).astype(BF16)
    va_ref[...] = jnp.concatenate(
        [r[:, 128:256].T[0:HEAD_DIM, :], jnp.ones((_V_ROWS - HEAD_DIM, r.shape[0]), F32)],
        axis=0).astype(BF16)

    ikk = mm(_P_IK)
    ik_ref[...] = (ikk[:, 0:64] * cos64 + ikk[:, 64:128] * sin64).astype(BF16)
    iw_ref[...] = mm(_P_IW) * (IDX_HEADS ** -0.5 * IDX_DIM ** -0.5)
    qkvb_ref[...] = mm(_P_QKVB)
    zb_ref[...] = mm(_P_ZB)
    ab_ref[...] = mm(_P_AB)
    qc_ref[...] = mm(_P_QC).astype(BF16)


def _proj(x2, w_in, w_k_up, w_v_up, kv_norm_g, cos128, sin128, seq, tm=256):
    n_tok, d_model = x2.shape
    offs = np.cumsum(SPLIT_SIZES)[:-1].tolist()
    (w_qa, w_ckv, w_iq, w_ik, w_iw, w_qb, w_kb, w_vb, w_zb, w_a, w_b, w_qc) = jnp.split(w_in, offs, axis=1)
    w1 = jnp.concatenate([
        w_qa, w_iq, w_ckv,
        w_ik, _rot_cols(w_ik, 1), _pad_cols(w_iw, LANES),
        w_qb, w_kb, w_vb, w_zb, _pad_cols(jnp.concatenate([w_a, w_b], axis=1), LANES), w_qc,
    ], axis=1).astype(BF16)
    assert w1.shape[1] == _P_TOTAL
    w2 = _pad_cols(jnp.concatenate([w_k_up, _rot_cols(w_k_up, 1), w_v_up], axis=1), 2 * LANES).astype(BF16)
    n_pos = seq // tm
    row = lambda i: (i, 0)
    const = lambda i: (0, 0)
    pos = lambda i: (i % n_pos, 0)
    outs = [(A_WIDTH, BF16), (IDX_HEADS * IDX_DIM, BF16), (IDX_DIM, BF16), (HEAD_DIM, BF16),
            None, (LANES, F32), (3 * GDN_WIDTH, F32), (GDN_WIDTH, F32), (LANES, F32),
            (MEM_WIDTH, BF16)]
    out_specs = [pl.BlockSpec((tm, o[0]), row) if o else pl.BlockSpec((_V_ROWS, tm), lambda i: (0, i))
                 for o in outs]
    out_shape = [jax.ShapeDtypeStruct((n_tok, o[0]), o[1]) if o
                 else jax.ShapeDtypeStruct((_V_ROWS, n_tok), BF16) for o in outs]
    return pl.pallas_call(
        _proj_kernel,
        grid=(n_tok // tm,),
        in_specs=[pl.BlockSpec((tm, d_model), row),
                  pl.BlockSpec(w1.shape, const),
                  pl.BlockSpec(w2.shape, const),
                  pl.BlockSpec((tm, LANES), pos),
                  pl.BlockSpec((tm, LANES), pos),
                  pl.BlockSpec((1, KV_RANK), const)],
        out_specs=out_specs,
        out_shape=out_shape,
        compiler_params=_cparams(("parallel",)),
        name="proj",
    )(x2, w1, w2, cos128, sin128, kv_norm_g.reshape(1, KV_RANK).astype(F32))


def _dsa_kernel(qa_ref, iq_ref, iw_ref, ik_ref, ka_ref, vat_ref, o_ref,
                key_ref, hi_ref, lo_ref, m_ref, acc_ref, *, tq, kc, top):
    qi = pl.program_id(1)
    row0 = qi * tq
    n_kc = (row0 + tq + kc - 1) // kc
    qpos = row0 + lax.broadcasted_iota(I32, (1, tq), 1)

    def key_fold(v):
        return jnp.sum(v.reshape(kc // SUBLANES, SUBLANES, tq), axis=0)

    def head_rows(x, n_heads, width):
        return jnp.concatenate([x[:, h * width:(h + 1) * width] for h in range(n_heads)], axis=0)

    iq_rows = head_rows(iq_ref[...], IDX_HEADS, IDX_DIM)
    iw_t = iw_ref[...].T

    n_pairs = (n_kc + 1) // 2
    kc2 = 2 * kc

    def score_body(c, carry):
        k0 = pl.multiple_of(c * kc2, kc2)
        d = _dot_nt(ik_ref[pl.ds(k0, kc2), :], iq_rows)
        s = jnp.zeros((kc2, tq), F32)
        for h in range(IDX_HEADS):
            s = s + iw_t[h:h + 1, :] * jnp.maximum(d[:, h * tq:(h + 1) * tq], 0.0)
        kidx = k0 + lax.broadcasted_iota(I32, (kc2, tq), 0)
        s = jnp.where(s == 0.0, 0.0, s)
        s = jnp.where(kidx <= qpos, s, NEG_INF)
        bits = pltpu.bitcast(s, I32)
        key = jnp.where(bits >= 0, bits, bits ^ INT_MAX)
        key_ref[pl.ds(k0, kc2), :] = key
        hi_ref[pl.ds(k0, kc2), :] = lax.shift_right_arithmetic(key, HALF_BITS).astype(I16)
        lo_ref[pl.ds(k0, kc2), :] = ((key & (HALF_SPAN - 1)) + I16_MIN).astype(I16)
        return carry

    lax.fori_loop(0, n_pairs, score_body, 0)

    k_eff = jnp.minimum(top, qpos + 1).astype(F32)

    one_b = jnp.ones((), BF16)
    zero_b = jnp.zeros((), BF16)

    def count16(ref, pred):
        def body(c, acc):
            k0 = pl.multiple_of(c * (2 * kc), 2 * kc)
            for j in range(2):
                hit = jnp.where(pred(ref[pl.ds(k0 + j * kc, kc), :]), one_b, zero_b)
                hit = hit.reshape(kc // PACKED_ROWS, PACKED_ROWS, tq)
                parts = [hit[r] for r in range(kc // PACKED_ROWS)]
                while len(parts) > 1:
                    parts = [a + b for a, b in zip(parts[0::2], parts[1::2])]
                acc = acc + parts[0]
            return acc
        acc = lax.fori_loop(0, n_pairs, body, jnp.zeros((PACKED_ROWS, tq), BF16))
        return jnp.sum(acc.astype(F32), axis=0, keepdims=True)

    def search16(ref, start, n_bits, k_want):
        def bit_body(i, t):
            cand = t + lax.shift_left(jnp.int32(1), n_bits - 1 - i)
            c16 = cand.astype(I16)
            return jnp.where(count16(ref, lambda v: v >= c16) >= k_want, cand, t)
        return lax.fori_loop(0, n_bits, bit_body, start)

    hi0 = jnp.where(count16(hi_ref, lambda v: v >= 0) >= k_eff, 0, I16_MIN).astype(I32)
    thr_hi = search16(hi_ref, hi0, HALF_BITS - 1, k_eff)
    thr_hi16 = thr_hi.astype(I16)
    k_low = k_eff - count16(hi_ref, lambda v: v > thr_hi16)

    def band_body(c, carry):
        k0 = pl.multiple_of(c * kc2, kc2)
        lo_ref[pl.ds(k0, kc2), :] = jnp.where(hi_ref[pl.ds(k0, kc2), :] == thr_hi16,
                                              lo_ref[pl.ds(k0, kc2), :], I16_MIN)
        return carry

    lax.fori_loop(0, n_pairs, band_body, 0)
    thr_lo = search16(lo_ref, jnp.full((1, tq), I16_MIN, I32), HALF_BITS, k_low)
    thr_lo16 = thr_lo.astype(I16)
    thr = thr_hi * HALF_SPAN + (thr_lo - I16_MIN)
    n_tie = k_low - count16(lo_ref, lambda v: v > thr_lo16)

    q_rows = head_rows((qa_ref[...].astype(F32) * (HEAD_DIM ** -0.5)).astype(BF16),
                       ATTN_HEADS, HEAD_DIM)
    m_ref[...] = jnp.full(m_ref.shape, NEG_INF, F32)
    acc_ref[...] = jnp.zeros(acc_ref.shape, F32)
    earlier = jnp.where(lax.broadcasted_iota(I32, (kc, kc), 1) < lax.broadcasted_iota(I32, (kc, kc), 0),
                        1.0, 0.0).astype(BF16)

    def chunk_bias(k0, tie_seen):
        kk = key_ref[pl.ds(k0, kc), :]
        kidx = k0 + lax.broadcasted_iota(I32, (kc, tq), 0)
        tie = kk == thr
        tie_f = jnp.where(tie, 1.0, 0.0)
        tie_rank = jnp.dot(earlier, tie_f.astype(BF16), preferred_element_type=F32) + tie_seen
        bias = jnp.where(kk > thr, 0.0,
                         jnp.where(tie, jnp.where(tie_rank < n_tie, 0.0, NEG_INF), NEG_INF))
        bias = jnp.where(kidx <= qpos, bias, NEG_INF)
        return bias.astype(BF16), tie_seen + jnp.sum(key_fold(tie_f), axis=0, keepdims=True)

    def attn_body(c, tie_seen):
        k0 = pl.multiple_of(c * (2 * kc), 2 * kc)
        bias_0, tie_seen = chunk_bias(k0, tie_seen)
        bias_1, tie_seen = chunk_bias(k0 + kc, tie_seen)
        bias_b = jnp.concatenate([bias_0, bias_1], axis=0)
        lg_all = _dot_nt(ka_ref[pl.ds(k0, 2 * kc), :], q_rows).astype(BF16)
        lgs = [lg_all[:, h * tq:(h + 1) * tq] + bias_b for h in range(ATTN_HEADS)]
        m_prev = m_ref[...]
        m_new = jnp.maximum(m_prev, jnp.concatenate(
            [jnp.max(lg, axis=0, keepdims=True) for lg in lgs], axis=1).astype(F32))
        m_b = m_new[0:1, :].astype(BF16)
        p = jnp.concatenate([jnp.exp(lgs[h] - m_b[:, h * tq:(h + 1) * tq])
                             for h in range(ATTN_HEADS)], axis=1)
        alpha = jnp.exp(m_prev - m_new)
        acc_ref[...] = alpha[0:1, :] * acc_ref[...] + jnp.dot(
            vat_ref[:, pl.ds(k0, 2 * kc)], p, preferred_element_type=F32)
        m_ref[...] = m_new
        return tie_seen

    lax.fori_loop(0, n_pairs, attn_body, jnp.zeros((1, tq), F32))
    o_t = acc_ref[0:HEAD_DIM, :] / acc_ref[HEAD_DIM:HEAD_DIM + 1, :]
    o_ref[...] = jnp.concatenate(
        [o_t[:, h * tq:(h + 1) * tq].T for h in range(ATTN_HEADS)], axis=1)


def _dsa(qa, iq, iw, ik, ka, va_t, bsz, seq, tq=256):
    tq = min(tq, seq)
    kc = tq
    top = min(INDEX_TOPK, seq // 4)
    n_q = seq // tq
    assert seq // PACKED_ROWS <= 256 and (seq // kc) % 2 == 0
    row = lambda b, i: (b * n_q + i, 0)
    per_b = lambda b, i: (b, 0, 0)
    kern = functools.partial(_dsa_kernel, tq=tq, kc=kc, top=top)
    return pl.pallas_call(
        kern,
        grid=(bsz, n_q),
        in_specs=[pl.BlockSpec((tq, A_WIDTH), row),
                  pl.BlockSpec((tq, IDX_HEADS * IDX_DIM), row),
                  pl.BlockSpec((tq, LANES), row),
                  pl.BlockSpec((None, seq, IDX_DIM), per_b),
                  pl.BlockSpec((None, seq, HEAD_DIM), per_b),
                  pl.BlockSpec((_V_ROWS, seq), lambda b, i: (0, b))],
        out_specs=pl.BlockSpec((tq, A_WIDTH), row),
        out_shape=jax.ShapeDtypeStruct((bsz * seq, A_WIDTH), F32),
        scratch_shapes=[pltpu.VMEM((seq, tq), I32),
                        pltpu.VMEM((seq, tq), I16),
                        pltpu.VMEM((seq, tq), I16),
                        pltpu.VMEM((SUBLANES, ATTN_HEADS * tq), F32),
                        pltpu.VMEM((_V_ROWS, ATTN_HEADS * tq), F32)],
        compiler_params=_cparams(("parallel", "arbitrary")),
        name="dsa",
    )(qa, iq, iw, ik.reshape(bsz, seq, IDX_DIM), ka.reshape(bsz, seq, HEAD_DIM), va_t)


_HALO = SUBLANES


def _bmm(a, b):
    return lax.dot_general(a.astype(BF16), b.astype(BF16), (((2,), (1,)), ((0,), (0,))),
                           preferred_element_type=F32)


def _bmm_nt(a, b):
    return lax.dot_general(a.astype(BF16), b.astype(BF16), (((2,), (2,)), ((0,), (0,))),
                           preferred_element_type=F32)


def _gdn_kernel(qkv_ref, ab_ref, z_ref, cw_ref, alog_ref, dtb_ref, gn_ref, o_ref,
                ext_ref, state_ref, *, tr, nb):
    ti = pl.program_id(1)
    width = 3 * GDN_WIDTH
    n_c = tr // CHUNK
    n_p = GDN_HEADS // 2

    @pl.when(ti == 0)
    def _():
        ext_ref[:, 0:_HALO, :] = jnp.zeros((nb, _HALO, width), F32)
        state_ref[...] = jnp.zeros(state_ref.shape, F32)

    lane = lax.broadcasted_iota(I32, (CHUNK, LANES), 1)
    row_i = lax.broadcasted_iota(I32, (CHUNK, LANES), 0)
    left = lane < HEAD_DIM
    incl = (row_i >= lane % HEAD_DIM)[None]
    strict = (row_i > lane % HEAD_DIM)[None]
    left_t = (lax.broadcasted_iota(I32, (tr, LANES), 1) < HEAD_DIM)
    tri = jnp.where(lax.broadcasted_iota(I32, (CHUNK, CHUNK), 0) >= lax.broadcasted_iota(I32, (CHUNK, CHUNK), 1),
                    1.0, 0.0).astype(BF16)

    def blk(x):
        return jnp.concatenate([jnp.where(left[None], x, 0.0), jnp.where(left[None], 0.0, x)], axis=1)

    def head_scale(x, eps, scale):
        sq = x * x
        keep = left_t if x.shape[0] == tr else left
        s_l = jnp.sum(jnp.where(keep, sq, 0.0), axis=-1, keepdims=True)
        s_r = jnp.sum(jnp.where(keep, 0.0, sq), axis=-1, keepdims=True)
        return x * jnp.where(keep, lax.rsqrt(s_l * scale + eps), lax.rsqrt(s_r * scale + eps))

    def rows(c):
        return slice(c * CHUNK, (c + 1) * CHUNK)

    qkvs, betas, q_n, k_n, gc_cols, gc_rows = [], [], [], [], [], []
    for s in range(nb):
        ext_ref[s, _HALO:_HALO + tr, :] = qkv_ref[s]
        conv = cw_ref[CONV_WIDTH - 1:CONV_WIDTH, :] * ext_ref[s, _HALO:_HALO + tr, :]
        for j in range(CONV_WIDTH - 1):
            off = _HALO - (CONV_WIDTH - 1) + j
            conv = conv + cw_ref[j:j + 1, :] * ext_ref[s, off:off + tr, :]
        ext_ref[s, 0:_HALO, :] = ext_ref[s, tr:tr + _HALO, :]
        qkv = _silu(conv)
        qkvs.append(qkv)

        ab = ab_ref[s]
        sp_in = ab + dtb_ref[...]
        sp = jnp.maximum(sp_in, 0.0) + jnp.log1p(jnp.exp(-jnp.abs(sp_in)))
        g_all = -jnp.exp(alog_ref[...]) * sp
        betas.append(1.0 / (1.0 + jnp.exp(-ab)))

        cols, rws = [], []
        for c in range(n_c):
            g_c = g_all[rows(c), :]
            g_hi = g_c.astype(BF16)
            g_r1 = g_c - g_hi.astype(F32)
            g_mid = g_r1.astype(BF16)
            g_lo = (g_r1 - g_mid.astype(F32)).astype(BF16)
            gc = (jnp.dot(tri, g_hi, preferred_element_type=F32)
                  + jnp.dot(tri, g_mid, preferred_element_type=F32)
                  + jnp.dot(tri, g_lo, preferred_element_type=F32))
            cols.append(gc)
            rws.append(gc.T)
        gc_cols.append(cols)
        gc_rows.append(rws)
        q_n.append([head_scale(qkv[:, p * LANES:(p + 1) * LANES], RMS_EPS, 1.0) * (HEAD_DIM ** -0.5)
                    for p in range(n_p)])
        k_n.append([head_scale(qkv[:, GDN_WIDTH + p * LANES:GDN_WIDTH + (p + 1) * LANES], RMS_EPS, 1.0)
                    for p in range(n_p)])

    def stack(fn):
        return jnp.stack([fn(c, s, p) for c in range(n_c) for s in range(nb) for p in range(n_p)], axis=0)

    def pair_cols(x, c, p, base):
        return jnp.where(left, x[rows(c), base + 2 * p:base + 2 * p + 1],
                         x[rows(c), base + 2 * p + 1:base + 2 * p + 2])

    qs = stack(lambda c, s, p: q_n[s][p][rows(c), :])
    ks = stack(lambda c, s, p: k_n[s][p][rows(c), :])
    vs = stack(lambda c, s, p: qkvs[s][rows(c), 2 * GDN_WIDTH + p * LANES:2 * GDN_WIDTH + (p + 1) * LANES])
    beta = stack(lambda c, s, p: pair_cols(betas[s], c, p, GDN_HEADS))
    gcc = stack(lambda c, s, p: pair_cols(gc_cols[s][c], 0, p, 0))
    gcr = stack(lambda c, s, p: jnp.concatenate(
        [gc_rows[s][c][2 * p:2 * p + 1, :], gc_rows[s][c][2 * p + 1:2 * p + 2, :]], axis=1))
    decay = jnp.where(incl, jnp.exp(jnp.where(incl, gcc - gcr, 0.0)), 0.0)
    kb = ks * beta
    k_blk = blk(ks)
    low = jnp.where(strict, _bmm_nt(kb, k_blk) * decay, 0.0)
    e_gc = jnp.exp(gcc)
    u = vs * beta
    w = kb * e_gc
    u = u - _bmm(low, blk(u))
    w = w - _bmm(low, blk(w))
    pw = low
    for _ in range(5):
        pw = _bmm(pw, blk(pw))
        u = u + _bmm(pw, blk(u))
        w = w + _bmm(pw, blk(w))
    a_intra = jnp.where(incl, _bmm_nt(qs, k_blk) * decay, 0.0)
    q_dec = qs * e_gc
    gc_last = gcc[:, CHUNK - 1:CHUNK, :]
    kd_blk = blk(ks * jnp.exp(gc_last - gcc))
    kd_blk_t = jnp.stack([kd_blk[i].T for i in range(n_c * nb * n_p)], axis=0)
    g_last = jnp.exp(gc_last)

    gn = gn_ref[...]
    st = state_ref[...]
    per_c = nb * n_p
    for c in range(n_c):
        sl = slice(c * per_c, (c + 1) * per_c)
        v_new = blk(u[sl] - _bmm(w[sl], st))
        o = _bmm(q_dec[sl], st) + _bmm(a_intra[sl], v_new)
        st = st * g_last[sl] + _bmm(kd_blk_t[sl], v_new)
        for s in range(nb):
            for p in range(n_p):
                y = head_scale(o[s * n_p + p], RMS_EPS, 1.0 / HEAD_DIM) * gn
                o_ref[s, rows(c), p * LANES:(p + 1) * LANES] = y * _silu(
                    z_ref[s, rows(c), p * LANES:(p + 1) * LANES])
    state_ref[...] = st


def _gdn(qkvb, ab, zb, conv_w, a_log, dt_bias, gdn_norm_g, bsz, seq, tr=256):
    tr = min(tr, seq)
    n_t = seq // tr
    nb = 2 if bsz % 2 == 0 else 1
    blk3 = lambda b, i: (b, i, 0)
    const = lambda b, i: (0, 0)
    alog = _pad_cols(a_log.reshape(1, GDN_HEADS).astype(F32), LANES)
    dtb = _pad_cols(dt_bias.reshape(1, GDN_HEADS).astype(F32), LANES)
    out = pl.pallas_call(
        functools.partial(_gdn_kernel, tr=tr, nb=nb),
        grid=(bsz // nb, n_t),
        in_specs=[pl.BlockSpec((nb, tr, 3 * GDN_WIDTH), blk3),
                  pl.BlockSpec((nb, tr, LANES), blk3),
                  pl.BlockSpec((nb, tr, GDN_WIDTH), blk3),
                  pl.BlockSpec((CONV_WIDTH, 3 * GDN_WIDTH), const),
                  pl.BlockSpec((1, LANES), const),
                  pl.BlockSpec((1, LANES), const),
                  pl.BlockSpec((1, LANES), const)],
        out_specs=pl.BlockSpec((nb, tr, GDN_WIDTH), blk3),
        out_shape=jax.ShapeDtypeStruct((bsz, seq, GDN_WIDTH), F32),
        scratch_shapes=[pltpu.VMEM((nb, tr + _HALO, 3 * GDN_WIDTH), F32),
                        pltpu.VMEM((nb * (GDN_HEADS // 2), LANES, LANES), F32)],
        compiler_params=_cparams(("parallel", "arbitrary")),
        name="gdn",
    )(qkvb.reshape(bsz, seq, 3 * GDN_WIDTH), ab.reshape(bsz, seq, LANES), zb.reshape(bsz, seq, GDN_WIDTH),
      conv_w.astype(F32), alog, dtb,
      jnp.tile(gdn_norm_g.reshape(1, HEAD_DIM).astype(F32), (1, LANES // HEAD_DIM)))
    return out.reshape(bsz * seq, GDN_WIDTH)


def _memkv_kernel(m_ref, w_ref, k_ref, v_ref):
    r = jnp.dot(m_ref[...].astype(BF16), w_ref[...], preferred_element_type=F32)
    k_ref[...] = r[:, :MEM_WIDTH].astype(BF16)
    v_ref[...] = r[:, MEM_WIDTH:].astype(BF16)


def _memkv(mem2, w_mem_kv, n_mem):
    n_rows, d_model = mem2.shape
    row = lambda i: (i, 0)
    return pl.pallas_call(
        _memkv_kernel,
        grid=(n_rows // n_mem,),
        in_specs=[pl.BlockSpec((n_mem, d_model), row),
                  pl.BlockSpec((d_model, 2 * MEM_WIDTH), lambda i: (0, 0))],
        out_specs=[pl.BlockSpec((n_mem, MEM_WIDTH), row)] * 2,
        out_shape=[jax.ShapeDtypeStruct((n_rows, MEM_WIDTH), BF16)] * 2,
        compiler_params=_cparams(("parallel",)),
        name="memkv",
    )(mem2, w_mem_kv.astype(BF16))


_ROUTE_ROWS = 128


def _post_kernel(x_ref, oa_ref, ob_ref, qc_ref, mk_ref, mv_ref, wout_ref, ag_ref, mg_ref,
                 lg_ref, lb_ref, wr_ref, br_ref, x1_ref, ids_ref, gate_ref, *, alpha):
    tm = x_ref.shape[0]
    oa = oa_ref[...]
    oa = oa * lax.rsqrt(jnp.mean(oa * oa, axis=-1, keepdims=True) + RMS_EPS) * ag_ref[...]

    qc = qc_ref[...]
    mk = mk_ref[...]
    mv = mv_ref[...]
    cols = [slice(h * HEAD_DIM, (h + 1) * HEAD_DIM) for h in range(MEM_HEADS)]
    lgs = [_dot_nt(qc[:, c], mk[:, c]) * (HEAD_DIM ** -0.5) for c in cols]
    es = [jnp.exp(lg - jnp.max(lg, axis=-1, keepdims=True)) for lg in lgs]
    ps = [e / jnp.sum(e, axis=-1, keepdims=True) for e in es]
    oc = jnp.concatenate([jnp.dot(p.astype(BF16), mv[:, c], preferred_element_type=F32)
                          for p, c in zip(ps, cols)], axis=1)
    oc = oc * lax.rsqrt(jnp.mean(oc * oc, axis=-1, keepdims=True) + RMS_EPS) * mg_ref[...]

    mix = (jnp.dot(oa.astype(BF16), wout_ref[0:A_WIDTH, :], preferred_element_type=F32)
           + jnp.dot(ob_ref[...].astype(BF16), wout_ref[A_WIDTH:A_WIDTH + GDN_WIDTH, :],
                     preferred_element_type=F32)
           + jnp.dot(oc.astype(BF16), wout_ref[A_WIDTH + GDN_WIDTH:, :], preferred_element_type=F32))
    hres = alpha * x_ref[...] + mix
    mu = jnp.mean(hres, axis=-1, keepdims=True)
    var = jnp.mean(jnp.square(hres - mu), axis=-1, keepdims=True)
    x1 = (hres - mu) * lax.rsqrt(var + LN_EPS) * lg_ref[...] + lb_ref[...]
    x1_ref[...] = x1

    lt = _dot_nt(wr_ref[...], x1) + br_ref[...]
    sub = lax.broadcasted_iota(I32, (EXPERTS_PER_GROUP, tm), 0)
    gl = lt[0:N_GROUPS, :]
    gmax = jnp.max(gl, axis=0, keepdims=True)
    gprob = jnp.exp(gl - gmax) / jnp.sum(jnp.exp(gl - gmax), axis=0, keepdims=True)
    p_grp = jnp.max(gprob, axis=0, keepdims=True)
    grp = jnp.min(jnp.where(gprob == p_grp, sub, N_GROUPS), axis=0, keepdims=True)
    el = jnp.zeros((EXPERTS_PER_GROUP, tm), F32)
    for g in range(N_GROUPS):
        r0 = N_GROUPS + g * EXPERTS_PER_GROUP
        el = el + jnp.where(grp == g, lt[r0:r0 + EXPERTS_PER_GROUP, :], 0.0)
    ee = jnp.exp(el - jnp.max(el, axis=0, keepdims=True))
    pe = ee / jnp.sum(ee, axis=0, keepdims=True)
    p1 = jnp.max(pe, axis=0, keepdims=True)
    i1 = jnp.min(jnp.where(pe == p1, sub, EXPERTS_PER_GROUP), axis=0, keepdims=True)
    rest = jnp.where(sub == i1, -1.0, pe)
    p2 = jnp.max(rest, axis=0, keepdims=True)
    i2 = jnp.min(jnp.where(rest == p2, sub, EXPERTS_PER_GROUP), axis=0, keepdims=True)
    psum = p1 + p2
    g1 = p_grp * p1 / psum
    g2 = p_grp * p2 / psum
    e1 = grp * EXPERTS_PER_GROUP + i1
    e2 = grp * EXPERTS_PER_GROUP + i2
    ids_ref[...] = jnp.where(sub == 0, e1, jnp.where(sub == 1, e2, 0))
    gate_ref[...] = jnp.where(sub == 0, g1, jnp.where(sub == 1, g2, 0.0))


def _post(x2, oa, ob, qc, mk, mv, w_out, attn_g, mem_g, ln_g, ln_b, w_group, b_group,
          w_router, b_router, seq, n_mem, alpha, tm=512):
    n_tok, d_model = x2.shape
    n_t = seq // tm
    row = lambda i: (i, 0)
    const = lambda i: (0, 0)
    per_b = lambda i: (i // n_t, 0, 0)
    lane = lambda i: (0, i)
    bsz = n_tok // seq
    wr = jnp.pad(jnp.concatenate([w_group, w_router], axis=1).T,
                 ((0, _ROUTE_ROWS - N_GROUPS - N_EXPERTS), (0, 0))).astype(BF16)
    br = jnp.pad(jnp.concatenate([b_group, b_router]),
                 (0, _ROUTE_ROWS - N_GROUPS - N_EXPERTS)).reshape(_ROUTE_ROWS, 1).astype(F32)
    vec = lambda v: v.reshape(1, -1).astype(F32)
    return pl.pallas_call(
        functools.partial(_post_kernel, alpha=alpha),
        grid=(n_tok // tm,),
        in_specs=[pl.BlockSpec((tm, d_model), row),
                  pl.BlockSpec((tm, A_WIDTH), row),
                  pl.BlockSpec((tm, GDN_WIDTH), row),
                  pl.BlockSpec((tm, MEM_WIDTH), row),
                  pl.BlockSpec((None, n_mem, MEM_WIDTH), per_b),
                  pl.BlockSpec((None, n_mem, MEM_WIDTH), per_b),
                  pl.BlockSpec(w_out.shape, const),
                  pl.BlockSpec((1, A_WIDTH), const),
                  pl.BlockSpec((1, MEM_WIDTH), const),
                  pl.BlockSpec((1, d_model), const),
                  pl.BlockSpec((1, d_model), const),
                  pl.BlockSpec((_ROUTE_ROWS, d_model), const),
                  pl.BlockSpec((_ROUTE_ROWS, 1), const)],
        out_specs=[pl.BlockSpec((tm, d_model), row),
                   pl.BlockSpec((EXPERTS_PER_GROUP, tm), lane),
                   pl.BlockSpec((EXPERTS_PER_GROUP, tm), lane)],
        out_shape=[jax.ShapeDtypeStruct((n_tok, d_model), F32),
                   jax.ShapeDtypeStruct((EXPERTS_PER_GROUP, n_tok), I32),
                   jax.ShapeDtypeStruct((EXPERTS_PER_GROUP, n_tok), F32)],
        compiler_params=_cparams(("parallel",)),
        name="post",
    )(x2, oa, ob, qc, mk.reshape(bsz, n_mem, MEM_WIDTH), mv.reshape(bsz, n_mem, MEM_WIDTH),
      w_out.astype(BF16), vec(attn_g), vec(mem_g), vec(ln_g), vec(ln_b), wr, br)


def _rank_kernel(ids_ref, rank_ref, cnt_ref, carry_ref):
    i = pl.program_id(0)
    tm = ids_ref.shape[1]

    @pl.when(i == 0)
    def _():
        carry_ref[...] = jnp.zeros(carry_ref.shape, F32)

    ids = ids_ref[...]
    eio = lax.broadcasted_iota(I32, (N_EXPERTS, tm), 0)
    oh0 = jnp.where(eio == ids[0:1, :], 1.0, 0.0)
    oh1 = jnp.where(eio == ids[1:2, :], 1.0, 0.0)
    cnt = oh0 + oh1
    before = (lax.broadcasted_iota(I32, (tm, tm), 0) < lax.broadcasted_iota(I32, (tm, tm), 1))
    prefix = jnp.dot(cnt.astype(BF16), jnp.where(before, 1.0, 0.0).astype(BF16),
                     preferred_element_type=F32) + carry_ref[:, 0:1]
    r0 = jnp.sum(oh0 * prefix, axis=0, keepdims=True)
    r1 = jnp.sum(oh1 * prefix, axis=0, keepdims=True)
    sub = lax.broadcasted_iota(I32, (EXPERTS_PER_GROUP, tm), 0)
    rank_ref[...] = jnp.where(sub == 0, r0, jnp.where(sub == 1, r1, 0.0)).astype(I32)
    carry_ref[...] = carry_ref[...] + jnp.sum(cnt, axis=1, keepdims=True)
    cnt_ref[...] = carry_ref[...].astype(I32)


def _rank(ids, tm=512):
    n_tok = ids.shape[1]
    lane = lambda i: (0, i)
    return pl.pallas_call(
        _rank_kernel,
        grid=(n_tok // tm,),
        in_specs=[pl.BlockSpec((EXPERTS_PER_GROUP, tm), lane)],
        out_specs=[pl.BlockSpec((EXPERTS_PER_GROUP, tm), lane),
                   pl.BlockSpec((N_EXPERTS, LANES), lambda i: (0, 0))],
        out_shape=[jax.ShapeDtypeStruct((EXPERTS_PER_GROUP, n_tok), I32),
                   jax.ShapeDtypeStruct((N_EXPERTS, LANES), I32)],
        scratch_shapes=[pltpu.VMEM((N_EXPERTS, LANES), F32)],
        compiler_params=_cparams(("arbitrary",)),
        name="rank",
    )(ids)


def _dest_kernel(ids_ref, rank_ref, ps_ref, dest_ref):
    ids = ids_ref[...]
    tm = ids.shape[1]
    eio = lax.broadcasted_iota(I32, (N_EXPERTS, tm), 0)
    ps = ps_ref[...]
    d0 = jnp.sum(jnp.where(eio == ids[0:1, :], ps, 0.0), axis=0, keepdims=True)
    d1 = jnp.sum(jnp.where(eio == ids[1:2, :], ps, 0.0), axis=0, keepdims=True)
    sub = lax.broadcasted_iota(I32, (EXPERTS_PER_GROUP, tm), 0)
    dest_ref[...] = rank_ref[...] + jnp.where(sub == 0, d0, jnp.where(sub == 1, d1, 0.0)).astype(I32)


def _dest(ids, rank, pad_start, tm=2048):
    n_tok = ids.shape[1]
    tm = min(tm, n_tok)
    lane = lambda i: (0, i)
    return pl.pallas_call(
        _dest_kernel,
        grid=(n_tok // tm,),
        in_specs=[pl.BlockSpec((EXPERTS_PER_GROUP, tm), lane),
                  pl.BlockSpec((EXPERTS_PER_GROUP, tm), lane),
                  pl.BlockSpec((N_EXPERTS, 1), lambda i: (0, 0))],
        out_specs=pl.BlockSpec((EXPERTS_PER_GROUP, tm), lane),
        out_shape=jax.ShapeDtypeStruct((EXPERTS_PER_GROUP, n_tok), I32),
        compiler_params=_cparams(("parallel",)),
        name="dest",
    )(ids, rank, pad_start.astype(F32).reshape(N_EXPERTS, 1))


def _sc_mesh():
    return plsc.VectorSubcoreMesh(core_axis_name="c", subcore_axis_name="s",
                                  num_cores=SC_CORES, num_subcores=SC_SUBCORES)


def _sc_scatter_rows(x1, dest, pad_rows, cap):
    n_tok, d = x1.shape
    n_pad = pad_rows.shape[0]
    n_workers = SC_CORES * SC_SUBCORES
    tok_per_worker = n_tok // n_workers
    pad_per_worker = n_pad // n_workers
    n_win = tok_per_worker // SC_ROWS
    assert n_tok % (n_workers * SC_ROWS) == 0 and n_pad % (n_workers * SC_ROWS) == 0 and TOP_K == 2
    zero_rows = jnp.zeros((SC_ROWS, d), x1.dtype)

    @functools.partial(
        pl.kernel, mesh=_sc_mesh(), out_type=jax.ShapeDtypeStruct((cap, d), x1.dtype),
        scratch_types=[pltpu.VMEM((tok_per_worker,), I32), pltpu.VMEM((tok_per_worker,), I32),
                       pltpu.VMEM((pad_per_worker,), I32), pltpu.VMEM((2, SC_ROWS, d), x1.dtype),
                       pltpu.SemaphoreType.DMA((2,)), pltpu.SemaphoreType.DMA((2,))],
        name="sc_scatter")
    def scatter(x_hbm, dest_hbm, pad_hbm, zero_hbm, out_hbm, idx0_v, idx1_v, pad_v, rows_v, lsem, ssem):
        worker = lax.axis_index("s") * SC_CORES + lax.axis_index("c")
        base = worker * tok_per_worker
        pltpu.sync_copy(dest_hbm.at[pl.ds(base, tok_per_worker)], idx0_v)
        pltpu.sync_copy(dest_hbm.at[pl.ds(n_tok + base, tok_per_worker)], idx1_v)
        pltpu.sync_copy(pad_hbm.at[pl.ds(worker * pad_per_worker, pad_per_worker)], pad_v)

        def load(i, slot):
            return pltpu.make_async_copy(x_hbm.at[pl.ds(base + i * SC_ROWS, SC_ROWS)],
                                         rows_v.at[slot], lsem.at[slot])

        def store(idx_v, i, slot):
            return pltpu.make_async_copy(rows_v.at[slot],
                                         out_hbm.at[idx_v.at[pl.ds(i * SC_ROWS, SC_ROWS)]], ssem.at[slot])

        load(0, 0).start()

        @pl.loop(0, n_win)
        def _(i):
            slot = i % 2
            load(i, slot).wait()
            store(idx0_v, i, slot).start()
            store(idx1_v, i, slot).start()

            @pl.when(i >= 1)
            def _():
                store(idx0_v, i - 1, 1 - slot).wait()
                store(idx1_v, i - 1, 1 - slot).wait()

            @pl.when(i + 1 < n_win)
            def _():
                load(i + 1, 1 - slot).start()

        last = (n_win - 1) % 2
        store(idx0_v, n_win - 1, last).wait()
        store(idx1_v, n_win - 1, last).wait()

        pltpu.sync_copy(zero_hbm, rows_v.at[0])

        @pl.loop(0, pad_per_worker // SC_ROWS)
        def _(i):
            pltpu.async_copy(rows_v.at[0], out_hbm.at[pad_v.at[pl.ds(i * SC_ROWS, SC_ROWS)]],
                             ssem.at[0]).wait()

    return scatter(x1, dest, pad_rows, zero_rows)


def _mlp_kernel(be_ref, nused_ref, x_ref, wg_ref, wu_ref, wd_ref, y_ref):
    i = pl.program_id(0)

    @pl.when(i < nused_ref[0])
    def _():
        xb = x_ref[...].astype(BF16)
        hg = jnp.dot(xb, wg_ref[...].astype(BF16), preferred_element_type=F32)
        hu = jnp.dot(xb, wu_ref[...].astype(BF16), preferred_element_type=F32)
        y_ref[...] = jnp.dot((_silu(hg) * hu).astype(BF16), wd_ref[...].astype(BF16),
                             preferred_element_type=F32)

    @pl.when(i >= nused_ref[0])
    def _():
        y_ref[...] = jnp.zeros(y_ref.shape, F32)


def _mlp(blk_expert, n_used, xbuf, w_gate, w_up, w_down):
    cap, d_model = xbuf.shape
    d_exp = w_gate.shape[2]
    blk = lambda i, be, nu: (i, 0)
    used_blk = lambda i, be, nu: (jnp.minimum(i, nu[0] - 1), 0)
    wsel = lambda i, be, nu: (be[i], 0, 0)
    return pl.pallas_call(
        _mlp_kernel,
        grid_spec=pltpu.PrefetchScalarGridSpec(
            num_scalar_prefetch=2,
            grid=(cap // MOE_BLOCK,),
            in_specs=[pl.BlockSpec((MOE_BLOCK, d_model), used_blk),
                      pl.BlockSpec((None, d_model, d_exp), wsel),
                      pl.BlockSpec((None, d_model, d_exp), wsel),
                      pl.BlockSpec((None, d_exp, d_model), wsel)],
            out_specs=pl.BlockSpec((MOE_BLOCK, d_model), blk)),
        out_shape=jax.ShapeDtypeStruct((cap, d_model), F32),
        compiler_params=_cparams(("arbitrary",)),
        name="mlp",
    )(blk_expert, n_used, xbuf, w_gate, w_up, w_down)


def _sc_gather_rows(table, idx):
    n = idx.shape[0]
    d = table.shape[1]
    n_workers = SC_CORES * SC_SUBCORES
    per_worker = n // n_workers
    n_win = per_worker // SC_ROWS
    assert n % (n_workers * SC_ROWS) == 0

    @functools.partial(
        pl.kernel, mesh=_sc_mesh(), out_type=jax.ShapeDtypeStruct((n, d), table.dtype),
        scratch_types=[pltpu.VMEM((per_worker,), I32), pltpu.VMEM((2, SC_ROWS, d), table.dtype),
                       pltpu.SemaphoreType.DMA((2,)), pltpu.SemaphoreType.DMA((2,))],
        name="sc_gather")
    def gather(table_hbm, idx_hbm, out_hbm, idx_v, rows_v, gsem, wsem):
        worker = lax.axis_index("s") * SC_CORES + lax.axis_index("c")
        base = worker * per_worker
        pltpu.sync_copy(idx_hbm.at[pl.ds(base, per_worker)], idx_v)

        def fetch(i, slot):
            return pltpu.make_async_copy(table_hbm.at[idx_v.at[pl.ds(i * SC_ROWS, SC_ROWS)]],
                                         rows_v.at[slot], gsem.at[slot])

        def write(i, slot):
            return pltpu.make_async_copy(rows_v.at[slot],
                                         out_hbm.at[pl.ds(base + i * SC_ROWS, SC_ROWS)], wsem.at[slot])

        fetch(0, 0).start()

        @pl.loop(0, n_win)
        def _(i):
            slot = i % 2
            fetch(i, slot).wait()
            write(i, slot).start()

            @pl.when(i >= 1)
            def _():
                write(i - 1, 1 - slot).wait()

            @pl.when(i + 1 < n_win)
            def _():
                fetch(i + 1, 1 - slot).start()

        write(n_win - 1, (n_win - 1) % 2).wait()

    return gather(table, idx)


def _combine_kernel(x1_ref, y0_ref, y1_ref, gt_ref, lg_ref, lb_ref, o_ref, *, alpha):
    gt = gt_ref[...]
    ffn = y0_ref[...] * gt[:, 0:1] + y1_ref[...] * gt[:, 1:2]
    hres = alpha * x1_ref[...] + ffn
    mu = jnp.mean(hres, axis=-1, keepdims=True)
    var = jnp.mean(jnp.square(hres - mu), axis=-1, keepdims=True)
    o_ref[...] = (hres - mu) * lax.rsqrt(var + LN_EPS) * lg_ref[...] + lb_ref[...]


def _combine(dest, x1, gates_t, ln_g, ln_b, ybuf, alpha, tc=512):
    n_tok, d_model = x1.shape
    n_tiles = n_tok // tc
    yrows = _sc_gather_rows(ybuf, dest.reshape(-1))
    row = lambda i: (i, 0)
    const = lambda i: (0, 0)
    vec = lambda v: v.reshape(1, -1).astype(F32)
    return pl.pallas_call(
        functools.partial(_combine_kernel, alpha=alpha),
        grid=(n_tiles,),
        in_specs=[pl.BlockSpec((tc, d_model), row),
                  pl.BlockSpec((tc, d_model), row),
                  pl.BlockSpec((tc, d_model), lambda i: (i + n_tiles, 0)),
                  pl.BlockSpec((tc, EXPERTS_PER_GROUP), row),
                  pl.BlockSpec((1, d_model), const),
                  pl.BlockSpec((1, d_model), const)],
        out_specs=pl.BlockSpec((tc, d_model), row),
        out_shape=jax.ShapeDtypeStruct((n_tok, d_model), F32),
        compiler_params=_cparams(("parallel",)),
        name="combine",
    )(x1, yrows, yrows, gates_t, vec(ln_g), vec(ln_b))


def _moe(x1, ids, gates, w_gate, w_up, w_down, ln_g, ln_b, alpha):
    n_tok, d_model = x1.shape
    rank, counts = _rank(ids)
    counts = counts[:, 0]
    padded = (counts + MOE_BLOCK - 1) // MOE_BLOCK * MOE_BLOCK
    pad_ends = jnp.cumsum(padded)
    pad_start = (pad_ends - padded).astype(I32)
    n_asg = n_tok * TOP_K
    cap = (n_asg + MOE_BLOCK - 1) // MOE_BLOCK * MOE_BLOCK + N_EXPERTS * MOE_BLOCK
    n_blk = cap // MOE_BLOCK
    blk_pos = jnp.arange(n_blk, dtype=I32) * MOE_BLOCK
    blk_expert = jnp.minimum(
        jnp.sum((pad_ends[None, :] <= blk_pos[:, None]).astype(I32), axis=1), N_EXPERTS - 1)
    n_used = (pad_ends[-1:] // MOE_BLOCK).astype(I32)
    dest = _dest(ids, rank, pad_start)[0:TOP_K]
    slot = jnp.arange(MOE_BLOCK, dtype=I32)[None, :]
    n_padding = (padded - counts)[:, None]
    spare = cap - 1 - (jnp.arange(N_EXPERTS * MOE_BLOCK, dtype=I32).reshape(N_EXPERTS, MOE_BLOCK)
                       % N_EXPERTS)
    pad_rows = jnp.where(n_padding > 0,
                         (pad_start + counts)[:, None] + slot % jnp.maximum(n_padding, 1),
                         spare).reshape(-1)
    xbuf = _sc_scatter_rows(x1, dest.reshape(-1), pad_rows, cap)
    ybuf = _mlp(blk_expert, n_used, xbuf, w_gate, w_up, w_down)
    return _combine(dest, x1, gates.T, ln_g, ln_b, ybuf, alpha)


def kernel(x, mem, w_in, kv_norm_g, w_k_up, w_v_up, conv_w, A_log, dt_bias, gdn_norm_g, attn_norm_g, mem_norm_g, w_mem_kv, w_out, ln1_g, ln1_b, w_group, b_group, w_router, b_router, w_gate, w_up, w_down, ln2_g, ln2_b):
    bsz, seq, d_model = x.shape
    n_mem = mem.shape[1]
    depth = w_in.shape[0]
    alpha = (2 * depth) ** 0.25
    inv_freq = 1.0 / (ROPE_THETA ** (jnp.arange(0, HEAD_DIM, 2, dtype=F32) / HEAD_DIM))
    ang = jnp.arange(seq, dtype=F32)[:, None] * inv_freq[None, :]
    cos128 = jnp.tile(jnp.cos(ang), (1, LANES // (HEAD_DIM // 2)))
    sin128 = jnp.tile(jnp.sin(ang), (1, LANES // (HEAD_DIM // 2)))
    x2 = x.reshape(bsz * seq, d_model)
    mem2 = mem.reshape(bsz * n_mem, d_model)
    for l in range(depth):
        qa, iq, ik, ka, va, iw, qkvb, zb, ab, qc = _proj(
            x2, w_in[l], w_k_up[l], w_v_up[l], kv_norm_g[l], cos128, sin128, seq)
        oa = _dsa(qa, iq, iw, ik, ka, va, bsz, seq)
        ob = _gdn(qkvb, ab, zb, conv_w[l], A_log[l], dt_bias[l], gdn_norm_g[l], bsz, seq)
        mk, mv = _memkv(mem2, w_mem_kv[l], n_mem)
        x1, ids, gates = _post(x2, oa, ob, qc, mk, mv, w_out[l], attn_norm_g[l], mem_norm_g[l],
                               ln1_g[l], ln1_b[l], w_group[l], b_group[l], w_router[l],
                               b_router[l], seq, n_mem, alpha)
        x2 = _moe(x1, ids, gates, w_gate[l], w_up[l], w_down[l], ln2_g[l], ln2_b[l], alpha)
    return x2.reshape(bsz, seq, d_model)
```

```python
import functools

import numpy as np
import jax
import jax.numpy as jnp
from jax import lax
from jax.experimental import pallas as pl
from jax.experimental.pallas import tpu as pltpu
from jax.experimental.pallas import tpu_sc as plsc

F32 = jnp.float32
BF16 = jnp.bfloat16
I32 = jnp.int32
I16 = jnp.int16

HEAD_DIM = 64
ATTN_HEADS = 6
A_WIDTH = ATTN_HEADS * HEAD_DIM
KV_RANK = 128
IDX_HEADS = 4
IDX_DIM = 64
INDEX_TOPK = 256
GDN_HEADS = 6
GDN_WIDTH = GDN_HEADS * HEAD_DIM
CONV_WIDTH = 4
CHUNK = 64
MEM_HEADS = 4
MEM_WIDTH = MEM_HEADS * HEAD_DIM
SPLIT_SIZES = (A_WIDTH, KV_RANK, IDX_HEADS * IDX_DIM, IDX_DIM, IDX_HEADS,
               GDN_WIDTH, GDN_WIDTH, GDN_WIDTH, GDN_WIDTH, GDN_HEADS, GDN_HEADS,
               MEM_WIDTH)
ROPE_THETA = 10000.0
N_GROUPS = 8
EXPERTS_PER_GROUP = 8
N_EXPERTS = N_GROUPS * EXPERTS_PER_GROUP
TOP_K = 2
MOE_BLOCK = 512
LN_EPS = 1e-5
RMS_EPS = 1e-6
NEG_INF = -1e30
INT_MIN = -2 ** 31
I16_MIN = -2 ** 15

LANES = 128
SUBLANES = 8
PACKED_ROWS = 16
HALF_BITS = 16
HALF_SPAN = 1 << HALF_BITS
INT_MAX = 2 ** 31 - 1
SC_CORES = 2
SC_SUBCORES = 16
SC_ROWS = 32
VMEM_LIMIT = 56 * 1024 * 1024


def _cparams(sem):
    return pltpu.CompilerParams(dimension_semantics=sem, vmem_limit_bytes=VMEM_LIMIT)


def _dot(a, b):
    return jnp.dot(a.astype(BF16), b.astype(BF16), preferred_element_type=F32)


def _dot_nt(a, b):
    return lax.dot_general(a.astype(BF16), b.astype(BF16), (((1,), (1,)), ((), ())),
                           preferred_element_type=F32)


def _silu(t):
    return t * (1.0 / (1.0 + jnp.exp(-t)))


_P_QA = (0, 384)
_P_IQ = (384, 640)
_P_CKV = (640, 768)
_P_IK = (768, 896)
_P_IW = (896, 1024)
_P_QKVB = (1024, 2176)
_P_ZB = (2176, 2560)
_P_AB = (2560, 2688)
_P_QC = (2688, 2944)
_P_TOTAL = 2944
_V_ROWS = HEAD_DIM + PACKED_ROWS


def _rot_cols(w, n_heads):
    k = w.shape[0]
    w4 = w.reshape(k, n_heads, 2, HEAD_DIM // 2)
    return jnp.concatenate([-w4[:, :, 1:2], w4[:, :, 0:1]], axis=2).reshape(k, n_heads * HEAD_DIM)


def _pad_cols(w, width):
    return jnp.pad(w, ((0, 0), (0, width - w.shape[1])))


def _proj_kernel(x_ref, w_ref, w2_ref, cos_ref, sin_ref, kvg_ref,
                 qa_ref, iq_ref, ik_ref, ka_ref, va_ref, iw_ref, qkvb_ref, zb_ref, ab_ref, qc_ref):
    xb = x_ref[...].astype(BF16)

    def mm(slab):
        return jnp.dot(xb, w_ref[:, slab[0]:slab[1]], preferred_element_type=F32)

    cos = cos_ref[...]
    sin = sin_ref[...]
    cos3 = jnp.concatenate([cos] * 3, axis=1)
    sin3 = jnp.concatenate([sin] * 3, axis=1)
    cos2 = jnp.concatenate([cos] * 2, axis=1)
    sin2 = jnp.concatenate([sin] * 2, axis=1)
    cos64 = cos[:, :HEAD_DIM]
    sin64 = sin[:, :HEAD_DIM]

    def rotate_half(t):
        slabs = []
        for j in range(t.shape[1] // LANES):
            ts = t[:, j * LANES:(j + 1) * LANES]
            first = lax.broadcasted_iota(I32, ts.shape, 1) % HEAD_DIM < HEAD_DIM // 2
            slabs.append(jnp.where(first, -pltpu.roll(ts, LANES - HEAD_DIM // 2, 1),
                                   pltpu.roll(ts, HEAD_DIM // 2, 1)))
        return jnp.concatenate(slabs, axis=1)

    qa = mm(_P_QA)
    qa_ref[...] = (qa * cos3 + rotate_half(qa) * sin3).astype(BF16)
    iq = mm(_P_IQ)
    iq_ref[...] = (iq * cos2 + rotate_half(iq) * sin2).astype(BF16)

    ckv = mm(_P_CKV)
    cn = ckv * lax.rsqrt(jnp.mean(ckv * ckv, axis=-1, keepdims=True) + RMS_EPS) * kvg_ref[...]
    r = jnp.dot(cn.astype(BF16), w2_ref[...], preferred_element_type=F32)
    ka_ref[...] = (r[:, 0:64] * cos64 + r[:, 64:128] * sin64).astype(BF16)
    va_ref[...] = jnp.concatenate(
        [r[:, 128:256].T[0:HEAD_DIM, :], jnp.ones((_V_ROWS - HEAD_DIM, r.shape[0]), F32)],
        axis=0).astype(BF16)

    ikk = mm(_P_IK)
    ik_ref[...] = (ikk[:, 0:64] * cos64 + ikk[:, 64:128] * sin64).astype(BF16)
    iw_ref[...] = mm(_P_IW) * (IDX_HEADS ** -0.5 * IDX_DIM ** -0.5)
    qkvb_ref[...] = mm(_P_QKVB)
    zb_ref[...] = mm(_P_ZB)
    ab_ref[...] = mm(_P_AB)
    qc_ref[...] = mm(_P_QC).astype(BF16)


def _proj(x2, w_in, w_k_up, w_v_up, kv_norm_g, cos128, sin128, seq, tm=256):
    n_tok, d_model = x2.shape
    offs = np.cumsum(SPLIT_SIZES)[:-1].tolist()
    (w_qa, w_ckv, w_iq, w_ik, w_iw, w_qb, w_kb, w_vb, w_zb, w_a, w_b, w_qc) = jnp.split(w_in, offs, axis=1)
    w1 = jnp.concatenate([
        w_qa, w_iq, w_ckv,
        w_ik, _rot_cols(w_ik, 1), _pad_cols(w_iw, LANES),
        w_qb, w_kb, w_vb, w_zb, _pad_cols(jnp.concatenate([w_a, w_b], axis=1), LANES), w_qc,
    ], axis=1).astype(BF16)
    assert w1.shape[1] == _P_TOTAL
    w2 = _pad_cols(jnp.concatenate([w_k_up, _rot_cols(w_k_up, 1), w_v_up], axis=1), 2 * LANES).astype(BF16)
    n_pos = seq // tm
    row = lambda i: (i, 0)
    const = lambda i: (0, 0)
    pos = lambda i: (i % n_pos, 0)
    outs = [(A_WIDTH, BF16), (IDX_HEADS * IDX_DIM, BF16), (IDX_DIM, BF16), (HEAD_DIM, BF16),
            None, (LANES, F32), (3 * GDN_WIDTH, F32), (GDN_WIDTH, F32), (LANES, F32),
            (MEM_WIDTH, BF16)]
    out_specs = [pl.BlockSpec((tm, o[0]), row) if o else pl.BlockSpec((_V_ROWS, tm), lambda i: (0, i))
                 for o in outs]
    out_shape = [jax.ShapeDtypeStruct((n_tok, o[0]), o[1]) if o
                 else jax.ShapeDtypeStruct((_V_ROWS, n_tok), BF16) for o in outs]
    return pl.pallas_call(
        _proj_kernel,
        grid=(n_tok // tm,),
        in_specs=[pl.BlockSpec((tm, d_model), row),
                  pl.BlockSpec(w1.shape, const),
                  pl.BlockSpec(w2.shape, const),
                  pl.BlockSpec((tm, LANES), pos),
                  pl.BlockSpec((tm, LANES), pos),
                  pl.BlockSpec((1, KV_RANK), const)],
        out_specs=out_specs,
        out_shape=out_shape,
        compiler_params=_cparams(("parallel",)),
        name="proj",
    )(x2, w1, w2, cos128, sin128, kv_norm_g.reshape(1, KV_RANK).astype(F32))


def _dsa_kernel(qa_ref, iq_ref, iw_ref, ik_ref, ka_ref, vat_ref, o_ref,
                key_ref, hi_ref, lo_ref, m_ref, acc_ref, *, tq, kc, top):
    qi = pl.program_id(1)
    row0 = qi * tq
    n_kc = (row0 + tq + kc - 1) // kc
    qpos = row0 + lax.broadcasted_iota(I32, (1, tq), 1)

    def key_fold(v):
        return jnp.sum(v.reshape(kc // SUBLANES, SUBLANES, tq), axis=0)

    def head_rows(x, n_heads, width):
        return jnp.concatenate([x[:, h * width:(h + 1) * width] for h in range(n_heads)], axis=0)

    iq_rows = head_rows(iq_ref[...], IDX_HEADS, IDX_DIM)
    iw_t = iw_ref[...].T

    n_pairs = (n_kc + 1) // 2
    kc2 = 2 * kc

    def score_body(c, carry):
        k0 = pl.multiple_of(c * kc2, kc2)
        d = _dot_nt(ik_ref[pl.ds(k0, kc2), :], iq_rows)
        s = jnp.zeros((kc2, tq), F32)
        for h in range(IDX_HEADS):
            s = s + iw_t[h:h + 1, :] * jnp.maximum(d[:, h * tq:(h + 1) * tq], 0.0)
        kidx = k0 + lax.broadcasted_iota(I32, (kc2, tq), 0)
        s = jnp.where(s == 0.0, 0.0, s)
        s = jnp.where(kidx <= qpos, s, NEG_INF)
        bits = pltpu.bitcast(s, I32)
        key = jnp.where(bits >= 0, bits, bits ^ INT_MAX)
        key_ref[pl.ds(k0, kc2), :] = key
        hi_ref[pl.ds(k0, kc2), :] = lax.shift_right_arithmetic(key, HALF_BITS).astype(I16)
        lo_ref[pl.ds(k0, kc2), :] = ((key & (HALF_SPAN - 1)) + I16_MIN).astype(I16)
        return carry

    lax.fori_loop(0, n_pairs, score_body, 0)

    k_eff = jnp.minimum(top, qpos + 1).astype(F32)

    one_b = jnp.ones((), BF16)
    zero_b = jnp.zeros((), BF16)

    def count16(ref, pred):
        def body(c, acc):
            k0 = pl.multiple_of(c * (2 * kc), 2 * kc)
            for j in range(2):
                hit = jnp.where(pred(ref[pl.ds(k0 + j * kc, kc), :]), one_b, zero_b)
                hit = hit.reshape(kc // PACKED_ROWS, PACKED_ROWS, tq)
                parts = [hit[r] for r in range(kc // PACKED_ROWS)]
                while len(parts) > 1:
                    parts = [a + b for a, b in zip(parts[0::2], parts[1::2])]
                acc = acc + parts[0]
            return acc
        acc = lax.fori_loop(0, n_pairs, body, jnp.zeros((PACKED_ROWS, tq), BF16))
        return jnp.sum(acc.astype(F32), axis=0, keepdims=True)

    def search16(ref, start, n_bits, k_want):
        def bit_body(i, t):
            cand = t + lax.shift_left(jnp.int32(1), n_bits - 1 - i)
            c16 = cand.astype(I16)
            return jnp.where(count16(ref, lambda v: v >= c16) >= k_want, cand, t)
        return lax.fori_loop(0, n_bits, bit_body, start)

    hi0 = jnp.where(count16(hi_ref, lambda v: v >= 0) >= k_eff, 0, I16_MIN).astype(I32)
    thr_hi = search16(hi_ref, hi0, HALF_BITS - 1, k_eff)
    thr_hi16 = thr_hi.astype(I16)
    k_low = k_eff - count16(hi_ref, lambda v: v > thr_hi16)

    def band_body(c, carry):
        k0 = pl.multiple_of(c * kc2, kc2)
        lo_ref[pl.ds(k0, kc2), :] = jnp.where(hi_ref[pl.ds(k0, kc2), :] == thr_hi16,
                                              lo_ref[pl.ds(k0, kc2), :], I16_MIN)
        return carry

    lax.fori_loop(0, n_pairs, band_body, 0)
    thr_lo = search16(lo_ref, jnp.full((1, tq), I16_MIN, I32), HALF_BITS, k_low)
    thr_lo16 = thr_lo.astype(I16)
    thr = thr_hi * HALF_SPAN + (thr_lo - I16_MIN)
    n_tie = k_low - count16(lo_ref, lambda v: v > thr_lo16)

    q_rows = head_rows((qa_ref[...].astype(F32) * (HEAD_DIM ** -0.5)).astype(BF16),
                       ATTN_HEADS, HEAD_DIM)
    m_ref[...] = jnp.full(m_ref.shape, NEG_INF, F32)
    acc_ref[...] = jnp.zeros(acc_ref.shape, F32)
    earlier = jnp.where(lax.broadcasted_iota(I32, (kc, kc), 1) < lax.broadcasted_iota(I32, (kc, kc), 0),
                        1.0, 0.0).astype(BF16)

    def chunk_bias(k0, tie_seen):
        kk = key_ref[pl.ds(k0, kc), :]
        kidx = k0 + lax.broadcasted_iota(I32, (kc, tq), 0)
        tie = kk == thr
        tie_f = jnp.where(tie, 1.0, 0.0)
        tie_rank = jnp.dot(earlier, tie_f.astype(BF16), preferred_element_type=F32) + tie_seen
        bias = jnp.where(kk > thr, 0.0,
                         jnp.where(tie, jnp.where(tie_rank < n_tie, 0.0, NEG_INF), NEG_INF))
        bias = jnp.where(kidx <= qpos, bias, NEG_INF)
        return bias.astype(BF16), tie_seen + jnp.sum(key_fold(tie_f), axis=0, keepdims=True)

    def attn_body(c, tie_seen):
        k0 = pl.multiple_of(c * (2 * kc), 2 * kc)
        bias_0, tie_seen = chunk_bias(k0, tie_seen)
        bias_1, tie_seen = chunk_bias(k0 + kc, tie_seen)
        bias_b = jnp.concatenate([bias_0, bias_1], axis=0)
        lg_all = _dot_nt(ka_ref[pl.ds(k0, 2 * kc), :], q_rows).astype(BF16)
        lgs = [lg_all[:, h * tq:(h + 1) * tq] + bias_b for h in range(ATTN_HEADS)]
        m_prev = m_ref[...]
        m_new = jnp.maximum(m_prev, jnp.concatenate(
            [jnp.max(lg, axis=0, keepdims=True) for lg in lgs], axis=1).astype(F32))
        m_b = m_new[0:1, :].astype(BF16)
        p = jnp.concatenate([jnp.exp(lgs[h] - m_b[:, h * tq:(h + 1) * tq])
                             for h in range(ATTN_HEADS)], axis=1)
        alpha = jnp.exp(m_prev - m_new)
        acc_ref[...] = alpha[0:1, :] * acc_ref[...] + jnp.dot(
            vat_ref[:, pl.ds(k0, 2 * kc)], p, preferred_element_type=F32)
        m_ref[...] = m_new
        return tie_seen

    lax.fori_loop(0, n_pairs, attn_body, jnp.zeros((1, tq), F32))
    o_t = acc_ref[0:HEAD_DIM, :] / acc_ref[HEAD_DIM:HEAD_DIM + 1, :]
    o_ref[...] = jnp.concatenate(
        [o_t[:, h * tq:(h + 1) * tq].T for h in range(ATTN_HEADS)], axis=1)


def _dsa(qa, iq, iw, ik, ka, va_t, bsz, seq, tq=256):
    tq = min(tq, seq)
    kc = tq
    top = min(INDEX_TOPK, seq // 4)
    n_q = seq // tq
    assert seq // PACKED_ROWS <= 256 and (seq // kc) % 2 == 0
    row = lambda b, i: (b * n_q + i, 0)
    per_b = lambda b, i: (b, 0, 0)
    kern = functools.partial(_dsa_kernel, tq=tq, kc=kc, top=top)
    return pl.pallas_call(
        kern,
        grid=(bsz, n_q),
        in_specs=[pl.BlockSpec((tq, A_WIDTH), row),
                  pl.BlockSpec((tq, IDX_HEADS * IDX_DIM), row),
                  pl.BlockSpec((tq, LANES), row),
                  pl.BlockSpec((None, seq, IDX_DIM), per_b),
                  pl.BlockSpec((None, seq, HEAD_DIM), per_b),
                  pl.BlockSpec((_V_ROWS, seq), lambda b, i: (0, b))],
        out_specs=pl.BlockSpec((tq, A_WIDTH), row),
        out_shape=jax.ShapeDtypeStruct((bsz * seq, A_WIDTH), F32),
        scratch_shapes=[pltpu.VMEM((seq, tq), I32),
                        pltpu.VMEM((seq, tq), I16),
                        pltpu.VMEM((seq, tq), I16),
                        pltpu.VMEM((SUBLANES, ATTN_HEADS * tq), F32),
                        pltpu.VMEM((_V_ROWS, ATTN_HEADS * tq), F32)],
        compiler_params=_cparams(("parallel", "arbitrary")),
        name="dsa",
    )(qa, iq, iw, ik.reshape(bsz, seq, IDX_DIM), ka.reshape(bsz, seq, HEAD_DIM), va_t)


_HALO = SUBLANES


def _bmm(a, b):
    return lax.dot_general(a.astype(BF16), b.astype(BF16), (((2,), (1,)), ((0,), (0,))),
                           preferred_element_type=F32)


def _bmm_nt(a, b):
    return lax.dot_general(a.astype(BF16), b.astype(BF16), (((2,), (2,)), ((0,), (0,))),
                           preferred_element_type=F32)


def _gdn_kernel(qkv_ref, ab_ref, z_ref, cw_ref, alog_ref, dtb_ref, gn_ref, o_ref,
                ext_ref, state_ref, *, tr, nb):
    ti = pl.program_id(1)
    width = 3 * GDN_WIDTH
    n_c = tr // CHUNK
    n_p = GDN_HEADS // 2

    @pl.when(ti == 0)
    def _():
        ext_ref[:, 0:_HALO, :] = jnp.zeros((nb, _HALO, width), F32)
        state_ref[...] = jnp.zeros(state_ref.shape, F32)

    lane = lax.broadcasted_iota(I32, (CHUNK, LANES), 1)
    row_i = lax.broadcasted_iota(I32, (CHUNK, LANES), 0)
    left = lane < HEAD_DIM
    incl = (row_i >= lane % HEAD_DIM)[None]
    strict = (row_i > lane % HEAD_DIM)[None]
    left_t = (lax.broadcasted_iota(I32, (tr, LANES), 1) < HEAD_DIM)
    tri = jnp.where(lax.broadcasted_iota(I32, (CHUNK, CHUNK), 0) >= lax.broadcasted_iota(I32, (CHUNK, CHUNK), 1),
                    1.0, 0.0).astype(BF16)

    def blk(x):
        return jnp.concatenate([jnp.where(left[None], x, 0.0), jnp.where(left[None], 0.0, x)], axis=1)

    def head_scale(x, eps, scale):
        sq = x * x
        keep = left_t if x.shape[0] == tr else left
        s_l = jnp.sum(jnp.where(keep, sq, 0.0), axis=-1, keepdims=True)
        s_r = jnp.sum(jnp.where(keep, 0.0, sq), axis=-1, keepdims=True)
        return x * jnp.where(keep, lax.rsqrt(s_l * scale + eps), lax.rsqrt(s_r * scale + eps))

    def rows(c):
        return slice(c * CHUNK, (c + 1) * CHUNK)

    qkvs, betas, q_n, k_n, gc_cols, gc_rows = [], [], [], [], [], []
    for s in range(nb):
        ext_ref[s, _HALO:_HALO + tr, :] = qkv_ref[s]
        conv = cw_ref[CONV_WIDTH - 1:CONV_WIDTH, :] * ext_ref[s, _HALO:_HALO + tr, :]
        for j in range(CONV_WIDTH - 1):
            off = _HALO - (CONV_WIDTH - 1) + j
            conv = conv + cw_ref[j:j + 1, :] * ext_ref[s, off:off + tr, :]
        ext_ref[s, 0:_HALO, :] = ext_ref[s, tr:tr + _HALO, :]
        qkv = _silu(conv)
        qkvs.append(qkv)

        ab = ab_ref[s]
        sp_in = ab + dtb_ref[...]
        sp = jnp.maximum(sp_in, 0.0) + jnp.log1p(jnp.exp(-jnp.abs(sp_in)))
        g_all = -jnp.exp(alog_ref[...]) * sp
        betas.append(1.0 / (1.0 + jnp.exp(-ab)))

        cols, rws = [], []
        for c in range(n_c):
            g_c = g_all[rows(c), :]
            g_hi = g_c.astype(BF16)
            g_r1 = g_c - g_hi.astype(F32)
            g_mid = g_r1.astype(BF16)
            g_lo = (g_r1 - g_mid.astype(F32)).astype(BF16)
            gc = (jnp.dot(tri, g_hi, preferred_element_type=F32)
                  + jnp.dot(tri, g_mid, preferred_element_type=F32)
                  + jnp.dot(tri, g_lo, preferred_element_type=F32))
            cols.append(gc)
            rws.append(gc.T)
        gc_cols.append(cols)
        gc_rows.append(rws)
        q_n.append([head_scale(qkv[:, p * LANES:(p + 1) * LANES], RMS_EPS, 1.0) * (HEAD_DIM ** -0.5)
                    for p in range(n_p)])
        k_n.append([head_scale(qkv[:, GDN_WIDTH + p * LANES:GDN_WIDTH + (p + 1) * LANES], RMS_EPS, 1.0)
                    for p in range(n_p)])

    def stack(fn):
        return jnp.stack([fn(c, s, p) for c in range(n_c) for s in range(nb) for p in range(n_p)], axis=0)

    def pair_cols(x, c, p, base):
        return jnp.where(left, x[rows(c), base + 2 * p:base + 2 * p + 1],
                         x[rows(c), base + 2 * p + 1:base + 2 * p + 2])

    qs = stack(lambda c, s, p: q_n[s][p][rows(c), :])
    ks = stack(lambda c, s, p: k_n[s][p][rows(c), :])
    vs = stack(lambda c, s, p: qkvs[s][rows(c), 2 * GDN_WIDTH + p * LANES:2 * GDN_WIDTH + (p + 1) * LANES])
    beta = stack(lambda c, s, p: pair_cols(betas[s], c, p, GDN_HEADS))
    gcc = stack(lambda c, s, p: pair_cols(gc_cols[s][c], 0, p, 0))
    gcr = stack(lambda c, s, p: jnp.concatenate(
        [gc_rows[s][c][2 * p:2 * p + 1, :], gc_rows[s][c][2 * p + 1:2 * p + 2, :]], axis=1))
    decay = jnp.where(incl, jnp.exp(jnp.where(incl, gcc - gcr, 0.0)), 0.0)
    kb = ks * beta
    k_blk = blk(ks)
    low = jnp.where(strict, _bmm_nt(kb, k_blk) * decay, 0.0)
    e_gc = jnp.exp(gcc)
    u = vs * beta
    w = kb * e_gc
    u = u - _bmm(low, blk(u))
    w = w - _bmm(low, blk(w))
    pw = low
    for _ in range(5):
        pw = _bmm(pw, blk(pw))
        u = u + _bmm(pw, blk(u))
        w = w + _bmm(pw, blk(w))
    a_intra = jnp.where(incl, _bmm_nt(qs, k_blk) * decay, 0.0)
    q_dec = qs * e_gc
    gc_last = gcc[:, CHUNK - 1:CHUNK, :]
    kd_blk = blk(ks * jnp.exp(gc_last - gcc))
    kd_blk_t = jnp.stack([kd_blk[i].T for i in range(n_c * nb * n_p)], axis=0)
    g_last = jnp.exp(gc_last)

    gn = gn_ref[...]
    st = state_ref[...]
    per_c = nb * n_p
    for c in range(n_c):
        sl = slice(c * per_c, (c + 1) * per_c)
        v_new = blk(u[sl] - _bmm(w[sl], st))
        o = _bmm(q_dec[sl], st) + _bmm(a_intra[sl], v_new)
        st = st * g_last[sl] + _bmm(kd_blk_t[sl], v_new)
        for s in range(nb):
            for p in range(n_p):
                y = head_scale(o[s * n_p + p], RMS_EPS, 1.0 / HEAD_DIM) * gn
                o_ref[s, rows(c), p * LANES:(p + 1) * LANES] = y * _silu(
                    z_ref[s, rows(c), p * LANES:(p + 1) * LANES])
    state_ref[...] = st


def _gdn(qkvb, ab, zb, conv_w, a_log, dt_bias, gdn_norm_g, bsz, seq, tr=256):
    tr = min(tr, seq)
    n_t = seq // tr
    nb = 2 if bsz % 2 == 0 else 1
    blk3 = lambda b, i: (b, i, 0)
    const = lambda b, i: (0, 0)
    alog = _pad_cols(a_log.reshape(1, GDN_HEADS).astype(F32), LANES)
    dtb = _pad_cols(dt_bias.reshape(1, GDN_HEADS).astype(F32), LANES)
    out = pl.pallas_call(
        functools.partial(_gdn_kernel, tr=tr, nb=nb),
        grid=(bsz // nb, n_t),
        in_specs=[pl.BlockSpec((nb, tr, 3 * GDN_WIDTH), blk3),
                  pl.BlockSpec((nb, tr, LANES), blk3),
                  pl.BlockSpec((nb, tr, GDN_WIDTH), blk3),
                  pl.BlockSpec((CONV_WIDTH, 3 * GDN_WIDTH), const),
                  pl.BlockSpec((1, LANES), const),
                  pl.BlockSpec((1, LANES), const),
                  pl.BlockSpec((1, LANES), const)],
        out_specs=pl.BlockSpec((nb, tr, GDN_WIDTH), blk3),
        out_shape=jax.ShapeDtypeStruct((bsz, seq, GDN_WIDTH), F32),
        scratch_shapes=[pltpu.VMEM((nb, tr + _HALO, 3 * GDN_WIDTH), F32),
                        pltpu.VMEM((nb * (GDN_HEADS // 2), LANES, LANES), F32)],
        compiler_params=_cparams(("parallel", "arbitrary")),
        name="gdn",
    )(qkvb.reshape(bsz, seq, 3 * GDN_WIDTH), ab.reshape(bsz, seq, LANES), zb.reshape(bsz, seq, GDN_WIDTH),
      conv_w.astype(F32), alog, dtb,
      jnp.tile(gdn_norm_g.reshape(1, HEAD_DIM).astype(F32), (1, LANES // HEAD_DIM)))
    return out.reshape(bsz * seq, GDN_WIDTH)


def _memkv_kernel(m_ref, w_ref, k_ref, v_ref):
    r = jnp.dot(m_ref[...].astype(BF16), w_ref[...], preferred_element_type=F32)
    k_ref[...] = r[:, :MEM_WIDTH].astype(BF16)
    v_ref[...] = r[:, MEM_WIDTH:].astype(BF16)


def _memkv(mem2, w_mem_kv, n_mem):
    n_rows, d_model = mem2.shape
    row = lambda i: (i, 0)
    return pl.pallas_call(
        _memkv_kernel,
        grid=(n_rows // n_mem,),
        in_specs=[pl.BlockSpec((n_mem, d_model), row),
                  pl.BlockSpec((d_model, 2 * MEM_WIDTH), lambda i: (0, 0))],
        out_specs=[pl.BlockSpec((n_mem, MEM_WIDTH), row)] * 2,
        out_shape=[jax.ShapeDtypeStruct((n_rows, MEM_WIDTH), BF16)] * 2,
        compiler_params=_cparams(("parallel",)),
        name="memkv",
    )(mem2, w_mem_kv.astype(BF16))


_ROUTE_ROWS = 128


def _post_kernel(x_ref, oa_ref, ob_ref, qc_ref, mk_ref, mv_ref, wout_ref, ag_ref, mg_ref,
                 lg_ref, lb_ref, wr_ref, br_ref, x1_ref, ids_ref, gate_ref, *, alpha):
    tm = x_ref.shape[0]
    oa = oa_ref[...]
    oa = oa * lax.rsqrt(jnp.mean(oa * oa, axis=-1, keepdims=True) + RMS_EPS) * ag_ref[...]

    qc = qc_ref[...]
    mk = mk_ref[...]
    mv = mv_ref[...]
    cols = [slice(h * HEAD_DIM, (h + 1) * HEAD_DIM) for h in range(MEM_HEADS)]
    lgs = [_dot_nt(qc[:, c], mk[:, c]) * (HEAD_DIM ** -0.5) for c in cols]
    es = [jnp.exp(lg - jnp.max(lg, axis=-1, keepdims=True)) for lg in lgs]
    ps = [e / jnp.sum(e, axis=-1, keepdims=True) for e in es]
    oc = jnp.concatenate([jnp.dot(p.astype(BF16), mv[:, c], preferred_element_type=F32)
                          for p, c in zip(ps, cols)], axis=1)
    oc = oc * lax.rsqrt(jnp.mean(oc * oc, axis=-1, keepdims=True) + RMS_EPS) * mg_ref[...]

    mix = (jnp.dot(oa.astype(BF16), wout_ref[0:A_WIDTH, :], preferred_element_type=F32)
           + jnp.dot(ob_ref[...].astype(BF16), wout_ref[A_WIDTH:A_WIDTH + GDN_WIDTH, :],
                     preferred_element_type=F32)
           + jnp.dot(oc.astype(BF16), wout_ref[A_WIDTH + GDN_WIDTH:, :], preferred_element_type=F32))
    hres = alpha * x_ref[...] + mix
    mu = jnp.mean(hres, axis=-1, keepdims=True)
    var = jnp.mean(jnp.square(hres - mu), axis=-1, keepdims=True)
    x1 = (hres - mu) * lax.rsqrt(var + LN_EPS) * lg_ref[...] + lb_ref[...]
    x1_ref[...] = x1

    lt = _dot_nt(wr_ref[...], x1) + br_ref[...]
    sub = lax.broadcasted_iota(I32, (EXPERTS_PER_GROUP, tm), 0)
    gl = lt[0:N_GROUPS, :]
    gmax = jnp.max(gl, axis=0, keepdims=True)
    gprob = jnp.exp(gl - gmax) / jnp.sum(jnp.exp(gl - gmax), axis=0, keepdims=True)
    p_grp = jnp.max(gprob, axis=0, keepdims=True)
    grp = jnp.min(jnp.where(gprob == p_grp, sub, N_GROUPS), axis=0, keepdims=True)
    el = jnp.zeros((EXPERTS_PER_GROUP, tm), F32)
    for g in range(N_GROUPS):
        r0 = N_GROUPS + g * EXPERTS_PER_GROUP
        el = el + jnp.where(grp == g, lt[r0:r0 + EXPERTS_PER_GROUP, :], 0.0)
    ee = jnp.exp(el - jnp.max(el, axis=0, keepdims=True))
    pe = ee / jnp.sum(ee, axis=0, keepdims=True)
    p1 = jnp.max(pe, axis=0, keepdims=True)
    i1 = jnp.min(jnp.where(pe == p1, sub, EXPERTS_PER_GROUP), axis=0, keepdims=True)
    rest = jnp.where(sub == i1, -1.0, pe)
    p2 = jnp.max(rest, axis=0, keepdims=True)
    i2 = jnp.min(jnp.where(rest == p2, sub, EXPERTS_PER_GROUP), axis=0, keepdims=True)
    psum = p1 + p2
    g1 = p_grp * p1 / psum
    g2 = p_grp * p2 / psum
    e1 = grp * EXPERTS_PER_GROUP + i1
    e2 = grp * EXPERTS_PER_GROUP + i2
    ids_ref[...] = jnp.where(sub == 0, e1, jnp.where(sub == 1, e2, 0))
    gate_ref[...] = jnp.where(sub == 0, g1, jnp.where(sub == 1, g2, 0.0))


def _post(x2, oa, ob, qc, mk, mv, w_out, attn_g, mem_g, ln_g, ln_b, w_group, b_group,
          w_router, b_router, seq, n_mem, alpha, tm=512):
    n_tok, d_model = x2.shape
    n_t = seq // tm
    row = lambda i: (i, 0)
    const = lambda i: (0, 0)
    per_b = lambda i: (i // n_t, 0, 0)
    lane = lambda i: (0, i)
    bsz = n_tok // seq
    wr = jnp.pad(jnp.concatenate([w_group, w_router], axis=1).T,
                 ((0, _ROUTE_ROWS - N_GROUPS - N_EXPERTS), (0, 0))).astype(BF16)
    br = jnp.pad(jnp.concatenate([b_group, b_router]),
                 (0, _ROUTE_ROWS - N_GROUPS - N_EXPERTS)).reshape(_ROUTE_ROWS, 1).astype(F32)
    vec = lambda v: v.reshape(1, -1).astype(F32)
    return pl.pallas_call(
        functools.partial(_post_kernel, alpha=alpha),
        grid=(n_tok // tm,),
        in_specs=[pl.BlockSpec((tm, d_model), row),
                  pl.BlockSpec((tm, A_WIDTH), row),
                  pl.BlockSpec((tm, GDN_WIDTH), row),
                  pl.BlockSpec((tm, MEM_WIDTH), row),
                  pl.BlockSpec((None, n_mem, MEM_WIDTH), per_b),
                  pl.BlockSpec((None, n_mem, MEM_WIDTH), per_b),
                  pl.BlockSpec(w_out.shape, const),
                  pl.BlockSpec((1, A_WIDTH), const),
                  pl.BlockSpec((1, MEM_WIDTH), const),
                  pl.BlockSpec((1, d_model), const),
                  pl.BlockSpec((1, d_model), const),
                  pl.BlockSpec((_ROUTE_ROWS, d_model), const),
                  pl.BlockSpec((_ROUTE_ROWS, 1), const)],
        out_specs=[pl.BlockSpec((tm, d_model), row),
                   pl.BlockSpec((EXPERTS_PER_GROUP, tm), lane),
                   pl.BlockSpec((EXPERTS_PER_GROUP, tm), lane)],
        out_shape=[jax.ShapeDtypeStruct((n_tok, d_model), F32),
                   jax.ShapeDtypeStruct((EXPERTS_PER_GROUP, n_tok), I32),
                   jax.ShapeDtypeStruct((EXPERTS_PER_GROUP, n_tok), F32)],
        compiler_params=_cparams(("parallel",)),
        name="post",
    )(x2, oa, ob, qc, mk.reshape(bsz, n_mem, MEM_WIDTH), mv.reshape(bsz, n_mem, MEM_WIDTH),
      w_out.astype(BF16), vec(attn_g), vec(mem_g), vec(ln_g), vec(ln_b), wr, br)


def _rank_kernel(ids_ref, rank_ref, cnt_ref, carry_ref):
    i = pl.program_id(0)
    tm = ids_ref.shape[1]

    @pl.when(i == 0)
    def _():
        carry_ref[...] = jnp.zeros(carry_ref.shape, F32)

    ids = ids_ref[...]
    eio = lax.broadcasted_iota(I32, (N_EXPERTS, tm), 0)
    oh0 = jnp.where(eio == ids[0:1, :], 1.0, 0.0)
    oh1 = jnp.where(eio == ids[1:2, :], 1.0, 0.0)
    cnt = oh0 + oh1
    before = (lax.broadcasted_iota(I32, (tm, tm), 0) < lax.broadcasted_iota(I32, (tm, tm), 1))
    prefix = jnp.dot(cnt.astype(BF16), jnp.where(before, 1.0, 0.0).astype(BF16),
                     preferred_element_type=F32) + carry_ref[:, 0:1]
    r0 = jnp.sum(oh0 * prefix, axis=0, keepdims=True)
    r1 = jnp.sum(oh1 * prefix, axis=0, keepdims=True)
    sub = lax.broadcasted_iota(I32, (EXPERTS_PER_GROUP, tm), 0)
    rank_ref[...] = jnp.where(sub == 0, r0, jnp.where(sub == 1, r1, 0.0)).astype(I32)
    carry_ref[...] = carry_ref[...] + jnp.sum(cnt, axis=1, keepdims=True)
    cnt_ref[...] = carry_ref[...].astype(I32)


def _rank(ids, tm=512):
    n_tok = ids.shape[1]
    lane = lambda i: (0, i)
    return pl.pallas_call(
        _rank_kernel,
        grid=(n_tok // tm,),
        in_specs=[pl.BlockSpec((EXPERTS_PER_GROUP, tm), lane)],
        out_specs=[pl.BlockSpec((EXPERTS_PER_GROUP, tm), lane),
                   pl.BlockSpec((N_EXPERTS, LANES), lambda i: (0, 0))],
        out_shape=[jax.ShapeDtypeStruct((EXPERTS_PER_GROUP, n_tok), I32),
                   jax.ShapeDtypeStruct((N_EXPERTS, LANES), I32)],
        scratch_shapes=[pltpu.VMEM((N_EXPERTS, LANES), F32)],
        compiler_params=_cparams(("arbitrary",)),
        name="rank",
    )(ids)


def _dest_kernel(ids_ref, rank_ref, ps_ref, dest_ref):
    ids = ids_ref[...]
    tm = ids.shape[1]
    eio = lax.broadcasted_iota(I32, (N_EXPERTS, tm), 0)
    ps = ps_ref[...]
    d0 = jnp.sum(jnp.where(eio == ids[0:1, :], ps, 0.0), axis=0, keepdims=True)
    d1 = jnp.sum(jnp.where(eio == ids[1:2, :], ps, 0.0), axis=0, keepdims=True)
    sub = lax.broadcasted_iota(I32, (EXPERTS_PER_GROUP, tm), 0)
    dest_ref[...] = rank_ref[...] + jnp.where(sub == 0, d0, jnp.where(sub == 1, d1, 0.0)).astype(I32)


def _dest(ids, rank, pad_start, tm=2048):
    n_tok = ids.shape[1]
    tm = min(tm, n_tok)
    lane = lambda i: (0, i)
    return pl.pallas_call(
        _dest_kernel,
        grid=(n_tok // tm,),
        in_specs=[pl.BlockSpec((EXPERTS_PER_GROUP, tm), lane),
                  pl.BlockSpec((EXPERTS_PER_GROUP, tm), lane),
                  pl.BlockSpec((N_EXPERTS, 1), lambda i: (0, 0))],
        out_specs=pl.BlockSpec((EXPERTS_PER_GROUP, tm), lane),
        out_shape=jax.ShapeDtypeStruct((EXPERTS_PER_GROUP, n_tok), I32),
        compiler_params=_cparams(("parallel",)),
        name="dest",
    )(ids, rank, pad_start.astype(F32).reshape(N_EXPERTS, 1))


def _sc_mesh():
    return plsc.VectorSubcoreMesh(core_axis_name="c", subcore_axis_name="s",
                                  num_cores=SC_CORES, num_subcores=SC_SUBCORES)


def _sc_scatter_rows(x1, dest, pad_rows, cap):
    n_tok, d = x1.shape
    n_pad = pad_rows.shape[0]
    n_workers = SC_CORES * SC_SUBCORES
    tok_per_worker = n_tok // n_workers
    pad_per_worker = n_pad // n_workers
    n_win = tok_per_worker // SC_ROWS
    assert n_tok % (n_workers * SC_ROWS) == 0 and n_pad % (n_workers * SC_ROWS) == 0 and TOP_K == 2
    zero_rows = jnp.zeros((SC_ROWS, d), x1.dtype)

    @functools.partial(
        pl.kernel, mesh=_sc_mesh(), out_type=jax.ShapeDtypeStruct((cap, d), x1.dtype),
        scratch_types=[pltpu.VMEM((tok_per_worker,), I32), pltpu.VMEM((tok_per_worker,), I32),
                       pltpu.VMEM((pad_per_worker,), I32), pltpu.VMEM((2, SC_ROWS, d), x1.dtype),
                       pltpu.SemaphoreType.DMA((2,)), pltpu.SemaphoreType.DMA((2,))],
        name="sc_scatter")
    def scatter(x_hbm, dest_hbm, pad_hbm, zero_hbm, out_hbm, idx0_v, idx1_v, pad_v, rows_v, lsem, ssem):
        worker = lax.axis_index("s") * SC_CORES + lax.axis_index("c")
        base = worker * tok_per_worker
        pltpu.sync_copy(dest_hbm.at[pl.ds(base, tok_per_worker)], idx0_v)
        pltpu.sync_copy(dest_hbm.at[pl.ds(n_tok + base, tok_per_worker)], idx1_v)
        pltpu.sync_copy(pad_hbm.at[pl.ds(worker * pad_per_worker, pad_per_worker)], pad_v)

        def load(i, slot):
            return pltpu.make_async_copy(x_hbm.at[pl.ds(base + i * SC_ROWS, SC_ROWS)],
                                         rows_v.at[slot], lsem.at[slot])

        def store(idx_v, i, slot):
            return pltpu.make_async_copy(rows_v.at[slot],
                                         out_hbm.at[idx_v.at[pl.ds(i * SC_ROWS, SC_ROWS)]], ssem.at[slot])

        load(0, 0).start()

        @pl.loop(0, n_win)
        def _(i):
            slot = i % 2
            load(i, slot).wait()
            store(idx0_v, i, slot).start()
            store(idx1_v, i, slot).start()

            @pl.when(i >= 1)
            def _():
                store(idx0_v, i - 1, 1 - slot).wait()
                store(idx1_v, i - 1, 1 - slot).wait()

            @pl.when(i + 1 < n_win)
            def _():
                load(i + 1, 1 - slot).start()

        last = (n_win - 1) % 2
        store(idx0_v, n_win - 1, last).wait()
        store(idx1_v, n_win - 1, last).wait()

        pltpu.sync_copy(zero_hbm, rows_v.at[0])

        @pl.loop(0, pad_per_worker // SC_ROWS)
        def _(i):
            pltpu.async_copy(rows_v.at[0], out_hbm.at[pad_v.at[pl.ds(i * SC_ROWS, SC_ROWS)]],
                             ssem.at[0]).wait()

    return scatter(x1, dest, pad_rows, zero_rows)


def _mlp_kernel(be_ref, nused_ref, x_ref, wg_ref, wu_ref, wd_ref, y_ref):
    i = pl.program_id(0)

    @pl.when(i < nused_ref[0])
    def _():
        xb = x_ref[...].astype(BF16)
        hg = jnp.dot(xb, wg_ref[...].astype(BF16), preferred_element_type=F32)
        hu = jnp.dot(xb, wu_ref[...].astype(BF16), preferred_element_type=F32)
        y_ref[...] = jnp.dot((_silu(hg) * hu).astype(BF16), wd_ref[...].astype(BF16),
                             preferred_element_type=F32)

    @pl.when(i >= nused_ref[0])
    def _():
        y_ref[...] = jnp.zeros(y_ref.shape, F32)


def _mlp(blk_expert, n_used, xbuf, w_gate, w_up, w_down):
    cap, d_model = xbuf.shape
    d_exp = w_gate.shape[2]
    blk = lambda i, be, nu: (i, 0)
    used_blk = lambda i, be, nu: (jnp.minimum(i, nu[0] - 1), 0)
    wsel = lambda i, be, nu: (be[i], 0, 0)
    return pl.pallas_call(
        _mlp_kernel,
        grid_spec=pltpu.PrefetchScalarGridSpec(
            num_scalar_prefetch=2,
            grid=(cap // MOE_BLOCK,),
            in_specs=[pl.BlockSpec((MOE_BLOCK, d_model), used_blk),
                      pl.BlockSpec((None, d_model, d_exp), wsel),
                      pl.BlockSpec((None, d_model, d_exp), wsel),
                      pl.BlockSpec((None, d_exp, d_model), wsel)],
            out_specs=pl.BlockSpec((MOE_BLOCK, d_model), blk)),
        out_shape=jax.ShapeDtypeStruct((cap, d_model), F32),
        compiler_params=_cparams(("arbitrary",)),
        name="mlp",
    )(blk_expert, n_used, xbuf, w_gate, w_up, w_down)


def _sc_gather_rows(table, idx):
    n = idx.shape[0]
    d = table.shape[1]
    n_workers = SC_CORES * SC_SUBCORES
    per_worker = n // n_workers
    n_win = per_worker // SC_ROWS
    assert n % (n_workers * SC_ROWS) == 0

    @functools.partial(
        pl.kernel, mesh=_sc_mesh(), out_type=jax.ShapeDtypeStruct((n, d), table.dtype),
        scratch_types=[pltpu.VMEM((per_worker,), I32), pltpu.VMEM((2, SC_ROWS, d), table.dtype),
                       pltpu.SemaphoreType.DMA((2,)), pltpu.SemaphoreType.DMA((2,))],
        name="sc_gather")
    def gather(table_hbm, idx_hbm, out_hbm, idx_v, rows_v, gsem, wsem):
        worker = lax.axis_index("s") * SC_CORES + lax.axis_index("c")
        base = worker * per_worker
        pltpu.sync_copy(idx_hbm.at[pl.ds(base, per_worker)], idx_v)

        def fetch(i, slot):
            return pltpu.make_async_copy(table_hbm.at[idx_v.at[pl.ds(i * SC_ROWS, SC_ROWS)]],
                                         rows_v.at[slot], gsem.at[slot])

        def write(i, slot):
            return pltpu.make_async_copy(rows_v.at[slot],
                                         out_hbm.at[pl.ds(base + i * SC_ROWS, SC_ROWS)], wsem.at[slot])

        fetch(0, 0).start()

        @pl.loop(0, n_win)
        def _(i):
            slot = i % 2
            fetch(i, slot).wait()
            write(i, slot).start()

            @pl.when(i >= 1)
            def _():
                write(i - 1, 1 - slot).wait()

            @pl.when(i + 1 < n_win)
            def _():
                fetch(i + 1, 1 - slot).start()

        write(n_win - 1, (n_win - 1) % 2).wait()

    return gather(table, idx)


def _combine_kernel(x1_ref, y0_ref, y1_ref, gt_ref, lg_ref, lb_ref, o_ref, *, alpha):
    gt = gt_ref[...].T
    ffn = y0_ref[...] * gt[:, 0:1] + y1_ref[...] * gt[:, 1:2]
    hres = alpha * x1_ref[...] + ffn
    mu = jnp.mean(hres, axis=-1, keepdims=True)
    var = jnp.mean(jnp.square(hres - mu), axis=-1, keepdims=True)
    o_ref[...] = (hres - mu) * lax.rsqrt(var + LN_EPS) * lg_ref[...] + lb_ref[...]


def _combine(dest, x1, gates_t, ln_g, ln_b, ybuf, alpha, tc=512):
    n_tok, d_model = x1.shape
    n_tiles = n_tok // tc
    yrows = _sc_gather_rows(ybuf, dest.reshape(-1))
    row = lambda i: (i, 0)
    const = lambda i: (0, 0)
    vec = lambda v: v.reshape(1, -1).astype(F32)
    return pl.pallas_call(
        functools.partial(_combine_kernel, alpha=alpha),
        grid=(n_tiles,),
        in_specs=[pl.BlockSpec((tc, d_model), row),
                  pl.BlockSpec((tc, d_model), row),
                  pl.BlockSpec((tc, d_model), lambda i: (i + n_tiles, 0)),
                  pl.BlockSpec((EXPERTS_PER_GROUP, tc), lambda i: (0, i)),
                  pl.BlockSpec((1, d_model), const),
                  pl.BlockSpec((1, d_model), const)],
        out_specs=pl.BlockSpec((tc, d_model), row),
        out_shape=jax.ShapeDtypeStruct((n_tok, d_model), F32),
        compiler_params=_cparams(("parallel",)),
        name="combine",
    )(x1, yrows, yrows, gates_t, vec(ln_g), vec(ln_b))


def _moe(x1, ids, gates, w_gate, w_up, w_down, ln_g, ln_b, alpha):
    n_tok, d_model = x1.shape
    rank, counts = _rank(ids)
    counts = counts[:, 0]
    padded = (counts + MOE_BLOCK - 1) // MOE_BLOCK * MOE_BLOCK
    pad_ends = jnp.cumsum(padded)
    pad_start = (pad_ends - padded).astype(I32)
    n_asg = n_tok * TOP_K
    cap = (n_asg + MOE_BLOCK - 1) // MOE_BLOCK * MOE_BLOCK + N_EXPERTS * MOE_BLOCK
    n_blk = cap // MOE_BLOCK
    blk_pos = jnp.arange(n_blk, dtype=I32) * MOE_BLOCK
    blk_expert = jnp.minimum(
        jnp.sum((pad_ends[None, :] <= blk_pos[:, None]).astype(I32), axis=1), N_EXPERTS - 1)
    n_used = (pad_ends[-1:] // MOE_BLOCK).astype(I32)
    dest = _dest(ids, rank, pad_start)[0:TOP_K]
    slot = jnp.arange(MOE_BLOCK, dtype=I32)[None, :]
    n_padding = (padded - counts)[:, None]
    spare = cap - 1 - (jnp.arange(N_EXPERTS * MOE_BLOCK, dtype=I32).reshape(N_EXPERTS, MOE_BLOCK)
                       % N_EXPERTS)
    pad_rows = jnp.where(n_padding > 0,
                         (pad_start + counts)[:, None] + slot % jnp.maximum(n_padding, 1),
                         spare).reshape(-1)
    xbuf = _sc_scatter_rows(x1, dest.reshape(-1), pad_rows, cap)
    ybuf = _mlp(blk_expert, n_used, xbuf, w_gate, w_up, w_down)
    return _combine(dest, x1, gates, ln_g, ln_b, ybuf, alpha)


def kernel(x, mem, w_in, kv_norm_g, w_k_up, w_v_up, conv_w, A_log, dt_bias, gdn_norm_g, attn_norm_g, mem_norm_g, w_mem_kv, w_out, ln1_g, ln1_b, w_group, b_group, w_router, b_router, w_gate, w_up, w_down, ln2_g, ln2_b):
    bsz, seq, d_model = x.shape
    n_mem = mem.shape[1]
    depth = w_in.shape[0]
    alpha = (2 * depth) ** 0.25
    inv_freq = 1.0 / (ROPE_THETA ** (jnp.arange(0, HEAD_DIM, 2, dtype=F32) / HEAD_DIM))
    ang = jnp.arange(seq, dtype=F32)[:, None] * inv_freq[None, :]
    cos128 = jnp.tile(jnp.cos(ang), (1, LANES // (HEAD_DIM // 2)))
    sin128 = jnp.tile(jnp.sin(ang), (1, LANES // (HEAD_DIM // 2)))
    x2 = x.reshape(bsz * seq, d_model)
    mem2 = mem.reshape(bsz * n_mem, d_model)
    for l in range(depth):
        qa, iq, ik, ka, va, iw, qkvb, zb, ab, qc = _proj(
            x2, w_in[l], w_k_up[l], w_v_up[l], kv_norm_g[l], cos128, sin128, seq)
        oa = _dsa(qa, iq, iw, ik, ka, va, bsz, seq)
        ob = _gdn(qkvb, ab, zb, conv_w[l], A_log[l], dt_bias[l], gdn_norm_g[l], bsz, seq)
        mk, mv = _memkv(mem2, w_mem_kv[l], n_mem)
        x1, ids, gates = _post(x2, oa, ob, qc, mk, mv, w_out[l], attn_norm_g[l], mem_norm_g[l],
                               ln1_g[l], ln1_b[l], w_group[l], b_group[l], w_router[l],
                               b_router[l], seq, n_mem, alpha)
        x2 = _moe(x1, ids, gates, w_gate[l], w_up[l], w_down[l], ln2_g[l], ln2_b[l], alpha)
    return x2.reshape(bsz, seq, d_model)
```
